```python
import math
import jax, jax.numpy as jnp
from jax import lax
import numpy as np

D_MODEL = 1024
BATCH = 8
SEQ = 4096
DEPTH = 2

GRID_W = 64
CTX_LEN = 256

SSD_EXPAND = 2
SSD_INNER = SSD_EXPAND * D_MODEL
SSD_HEAD_DIM = 64
SSD_HEADS = SSD_INNER // SSD_HEAD_DIM
SSD_GROUPS = 4
SSD_HPG = SSD_HEADS // SSD_GROUPS
SSD_STATE = 128
SSD_CHUNK = 64
SSD_CONV = 3
XBC_WIDTH = SSD_INNER + 2 * SSD_GROUPS * SSD_STATE

SC_WIDTH = D_MODEL
SC_CONV = 3

N_BRANCH = 2
N_MOD = 6
IN_WIDTH = SSD_INNER + XBC_WIDTH + 2 * SSD_HEADS + 3 * SC_WIDTH + N_BRANCH * D_MODEL

FFN_DENSE = 2816
N_EXPERTS = 8
TOP_K = 2
FFN_EXPERT = 3584
N_DENSE = (DEPTH + 1) // 2
N_MOE = DEPTH // 2

EPS = 1e-6

kernel_name = "hybrid_ssd_shortconv_moe_diffusion_block"


def rmsnorm(x, g):
    xf = x.astype(jnp.float32)
    y = xf * lax.rsqrt(jnp.mean(xf * xf, axis=-1, keepdims=True) + EPS)
    return (y * g.astype(jnp.float32)).astype(x.dtype)


def modulate(h, shift, scale):
    return h * (1.0 + scale) + shift


def dwconv_centred(u, w, axis):
    k = w.shape[0]
    p = k // 2
    n = u.shape[axis]
    pad = [(0, 0)] * u.ndim
    pad[axis] = (p, p)
    up = jnp.pad(u, pad)
    return sum(lax.slice_in_dim(up, j, j + n, axis=axis) * w[j] for j in range(k))


def split_proj(proj):
    offs = np.cumsum([SSD_INNER, XBC_WIDTH, 2 * SSD_HEADS, 3 * SC_WIDTH]).tolist()
    return jnp.split(proj, offs, axis=-1)


def ssd_inputs(xbc, conv_w, conv_b):
    xbc = jax.nn.silu(dwconv_centred(xbc, conv_w, axis=1) + conv_b)
    xs, bm, cm = jnp.split(xbc, [SSD_INNER, SSD_INNER + SSD_GROUPS * SSD_STATE], axis=-1)
    b, n = xs.shape[:2]
    return (xs.reshape(b, n, SSD_HEADS, SSD_HEAD_DIM),
            bm.reshape(b, n, SSD_GROUPS, SSD_STATE),
            cm.reshape(b, n, SSD_GROUPS, SSD_STATE))


def ssd_dt(dt_raw, dt_bias, a_log):
    dt = jax.nn.softplus(dt_raw.astype(jnp.float32) + dt_bias.reshape(-1).astype(jnp.float32))
    a = -jnp.exp(a_log.astype(jnp.float32))
    return dt[..., :SSD_HEADS], dt[..., SSD_HEADS:], a[0], a[1]


def ssd_scan(x, dt, a, bm, cm, s0):
    b, n = x.shape[:2]
    nc = n // SSD_CHUNK
    xc = (x * dt[..., None]).reshape(b, nc, SSD_CHUNK, SSD_GROUPS, SSD_HPG, SSD_HEAD_DIM)
    a_cs = jnp.cumsum((dt * a).reshape(b, nc, SSD_CHUNK, SSD_GROUPS, SSD_HPG), axis=2)
    bc = bm.reshape(b, nc, SSD_CHUNK, SSD_GROUPS, SSD_STATE)
    cc = cm.reshape(b, nc, SSD_CHUNK, SSD_GROUPS, SSD_STATE)
    seg = a_cs[:, :, :, None] - a_cs[:, :, None, :]
    lower = jnp.tril(jnp.ones((SSD_CHUNK, SSD_CHUNK), dtype=bool))[:, :, None, None]
    decay_ls = jnp.exp(jnp.where(lower, seg, -jnp.inf))
    scores = jnp.einsum("bclgn,bcsgn->bclsg", cc, bc)
    y_diag = jnp.einsum("bclsgh,bcsghp->bclghp", scores[..., None] * decay_ls, xc)
    decay_to_end = jnp.exp(a_cs[:, :, -1:] - a_cs)
    chunk_states = jnp.einsum("bclgn,bclghp->bcghpn", bc, xc * decay_to_end[..., None])
    chunk_decay = jnp.exp(a_cs[:, :, -1])

    def step(s, inp):
        st, dec = inp
        return s * dec[..., None, None] + st, s

    s_final, s_enter = lax.scan(step, s0, (jnp.moveaxis(chunk_states, 1, 0),
                                           jnp.moveaxis(chunk_decay, 1, 0)))
    s_enter = jnp.moveaxis(s_enter, 0, 1)
    y_off = jnp.einsum("bclgn,bcghpn->bclghp", cc, s_enter) * jnp.exp(a_cs)[..., None]
    y = (y_diag + y_off).reshape(b, n, SSD_HEADS, SSD_HEAD_DIM)
    return y, s_final


def ssd_final_state(x, dt, a, bm):
    b, n = x.shape[:2]
    a_cs = jnp.cumsum(dt * a, axis=1)
    w = jnp.exp(a_cs[:, -1:] - a_cs) * dt
    xw = (x * w[..., None]).reshape(b, n, SSD_GROUPS, SSD_HPG, SSD_HEAD_DIM)
    return jnp.einsum("blgn,blghp->bghpn", bm, xw)


def flip(t):
    return jnp.flip(t, axis=1)


def short_conv(sc, conv_w, grid_rows):
    gb, gc, hv = jnp.split(sc, 3, axis=-1)
    u = gc * hv
    if grid_rows is None:
        v = dwconv_centred(u, conv_w, axis=1)
    else:
        b, n, ch = u.shape
        v = dwconv_centred(u.reshape(b, grid_rows, GRID_W, ch), conv_w, axis=2).reshape(b, n, ch)
    return gb * v


def token_mixer(h, s0_f, s0_b, grid_rows, w_in, b_gate, conv_w, conv_b, dt_bias, a_log,
                d_skip, ssd_norm_g, w_ssd_out, sc_conv_w, w_sc_out, w_o):
    b, n, _ = h.shape
    z, xbc, dt_raw, sc, gl = split_proj(h @ w_in)
    xs, bm, cm = ssd_inputs(xbc, conv_w, conv_b)
    dt_f, dt_b, a_f, a_b = ssd_dt(dt_raw, dt_bias, a_log)
    y_f, s_f = ssd_scan(xs, dt_f, a_f, bm, cm, s0_f)
    y_b, s_b = ssd_scan(flip(xs), flip(dt_b), a_b, flip(bm), flip(cm), s0_b)
    y = y_f + flip(y_b) + d_skip.astype(jnp.float32)[:, None] * xs.astype(jnp.float32)
    y = y.reshape(b, n, SSD_INNER) * jax.nn.silu(z.astype(jnp.float32))
    y_ssd = rmsnorm(y, ssd_norm_g).astype(h.dtype) @ w_ssd_out
    y_sc = short_conv(sc, sc_conv_w, grid_rows) @ w_sc_out
    g = jax.nn.sigmoid((gl + b_gate).astype(jnp.float32)).reshape(b, n, N_BRANCH, D_MODEL).astype(h.dtype)
    out = (g[:, :, 0] * y_ssd + g[:, :, 1] * y_sc) @ w_o
    return out, s_f, s_b


def context_states(hc, w_in, conv_w, conv_b, dt_bias, a_log):
    cols = hc @ w_in[:, SSD_INNER:SSD_INNER + XBC_WIDTH + 2 * SSD_HEADS]
    xbc, dt_raw = jnp.split(cols, [XBC_WIDTH], axis=-1)
    xs, bm, _ = ssd_inputs(xbc, conv_w, conv_b)
    dt_f, dt_b, a_f, a_b = ssd_dt(dt_raw, dt_bias, a_log)
    s_f = ssd_final_state(xs, dt_f, a_f, bm)
    s_b = ssd_final_state(flip(xs), flip(dt_b), a_b, flip(bm))
    return s_f, s_b


def swiglu(h, w1, w3, w2):
    return (jax.nn.silu(h @ w1) * (h @ w3)) @ w2


def moe_swiglu(h, router_w, w1, w3, w2):
    shp = h.shape
    t = h.reshape(-1, shp[-1])
    logits = (t @ router_w).astype(jnp.float32)
    top_v, top_i = lax.top_k(logits, TOP_K)
    top_w = jax.nn.softmax(top_v, axis=-1)
    gates = jnp.sum(jax.nn.one_hot(top_i, N_EXPERTS, dtype=jnp.float32) * top_w[..., None], axis=1)
    out = jnp.zeros_like(t)
    for e in range(N_EXPERTS):
        out = out + gates[:, e:e + 1].astype(t.dtype) * swiglu(t, w1[e], w3[e], w2[e])
    return out.reshape(shp)


def channel_mixer(h, i, ffn_w1, ffn_w3, ffn_w2, router_w, moe_w1, moe_w3, moe_w2):
    j = i // 2
    if i % 2 == 0:
        return swiglu(h, ffn_w1[j], ffn_w3[j], ffn_w2[j])
    return moe_swiglu(h, router_w[j], moe_w1[j], moe_w3[j], moe_w2[j])


def setup_inputs(seed: int = 0) -> dict:
    key = jax.random.key(seed)
    ks = iter(jax.random.split(key, 40))
    D = D_MODEL

    def nrm(shape, scale):
        return jax.random.normal(next(ks), shape, jnp.float32) * scale

    dt0 = jnp.exp(jax.random.uniform(next(ks), (DEPTH, 2, SSD_HEADS), jnp.float32,
                                     minval=math.log(1e-3), maxval=math.log(1e-1)))
    dt_bias = dt0 + jnp.log(-jnp.expm1(-dt0))
    a_log = jnp.log(jax.random.uniform(next(ks), (DEPTH, 2, SSD_HEADS), jnp.float32, minval=1.0, maxval=16.0))
    return {
        "x": nrm((BATCH, SEQ, D), 1.0),
        "c": nrm((BATCH, D), 1.0),
        "ctx": nrm((BATCH, CTX_LEN, D), 1.0),
        "c_ctx": nrm((D,), 1.0),
        "w_mod": nrm((DEPTH, D, N_MOD * D), 0.5 * D ** -0.5),
        "b_mod": nrm((DEPTH, N_MOD * D), 0.02),
        "norm1_g": 1.0 + nrm((DEPTH, D), 0.1),
        "norm2_g": 1.0 + nrm((DEPTH, D), 0.1),
        "w_in": nrm((DEPTH, D, IN_WIDTH), D ** -0.5),
        "b_gate": nrm((DEPTH, N_BRANCH * D), 0.1),
        "ssd_conv_w": nrm((DEPTH, SSD_CONV, XBC_WIDTH), SSD_CONV ** -0.5),
        "ssd_conv_b": nrm((DEPTH, XBC_WIDTH), 0.02),
        "ssd_dt_bias": dt_bias,
        "ssd_a_log": a_log,
        "ssd_d": 1.0 + nrm((DEPTH, SSD_HEADS), 0.1),
        "ssd_norm_g": 1.0 + nrm((DEPTH, SSD_INNER), 0.1),
        "w_ssd_out": nrm((DEPTH, SSD_INNER, D), SSD_INNER ** -0.5),
        "sc_conv_w": nrm((DEPTH, SC_CONV, SC_WIDTH), SC_CONV ** -0.5),
        "w_sc_out": nrm((DEPTH, SC_WIDTH, D), SC_WIDTH ** -0.5),
        "w_o": nrm((DEPTH, D, D), D ** -0.5),
        "ffn_w1": nrm((N_DENSE, D, FFN_DENSE), D ** -0.5),
        "ffn_w3": nrm((N_DENSE, D, FFN_DENSE), D ** -0.5),
        "ffn_w2": nrm((N_DENSE, FFN_DENSE, D), FFN_DENSE ** -0.5),
        "router_w": nrm((N_MOE, D, N_EXPERTS), D ** -0.5),
        "moe_w1": nrm((N_MOE, N_EXPERTS, D, FFN_EXPERT), D ** -0.5),
        "moe_w3": nrm((N_MOE, N_EXPERTS, D, FFN_EXPERT), D ** -0.5),
        "moe_w2": nrm((N_MOE, N_EXPERTS, FFN_EXPERT, D), FFN_EXPERT ** -0.5),
        "final_g": 1.0 + nrm((D,), 0.1),
    }


def reference(x, c, ctx, c_ctx, w_mod, b_mod, norm1_g, norm2_g, w_in, b_gate, ssd_conv_w,
              ssd_conv_b, ssd_dt_bias, ssd_a_log, ssd_d, ssd_norm_g, w_ssd_out, sc_conv_w,
              w_sc_out, w_o, ffn_w1, ffn_w3, ffn_w2, router_w, moe_w1, moe_w3, moe_w2, final_g):
    b = x.shape[0]
    rows = x.shape[1] // GRID_W
    for i in range(DEPTH):
        last = i == DEPTH - 1
        mx = (jax.nn.silu(c) @ w_mod[i] + b_mod[i]).reshape(b, N_MOD, 1, D_MODEL)
        mc = (jax.nn.silu(c_ctx) @ w_mod[i] + b_mod[i]).reshape(N_MOD, D_MODEL)
        mix_p = (w_in[i], b_gate[i], ssd_conv_w[i], ssd_conv_b[i], ssd_dt_bias[i], ssd_a_log[i],
                 ssd_d[i], ssd_norm_g[i], w_ssd_out[i], sc_conv_w[i], w_sc_out[i], w_o[i])

        hc = modulate(rmsnorm(ctx, norm1_g[i]), mc[0], mc[1])
        hx = modulate(rmsnorm(x, norm1_g[i]), mx[:, 0], mx[:, 1])
        if last:
            s_f, s_b = context_states(hc, w_in[i], ssd_conv_w[i], ssd_conv_b[i], ssd_dt_bias[i], ssd_a_log[i])
        else:
            zeros = jnp.zeros((b, SSD_GROUPS, SSD_HPG, SSD_HEAD_DIM, SSD_STATE), jnp.float32)
            yc, s_f, s_b = token_mixer(hc, zeros, zeros, None, *mix_p)
            ctx = ctx + mc[2] * yc
        yx, _, _ = token_mixer(hx, s_f, s_b, rows, *mix_p)
        x = x + mx[:, 2] * yx

        hx = modulate(rmsnorm(x, norm2_g[i]), mx[:, 3], mx[:, 4])
        x = x + mx[:, 5] * channel_mixer(hx, i, ffn_w1, ffn_w3, ffn_w2, router_w, moe_w1, moe_w3, moe_w2)
        if not last:
            hc = modulate(rmsnorm(ctx, norm2_g[i]), mc[3], mc[4])
            ctx = ctx + mc[5] * channel_mixer(hc, i, ffn_w1, ffn_w3, ffn_w2, router_w, moe_w1, moe_w3, moe_w2)
    return rmsnorm(x, final_g)
```

```python
import functools

import numpy as np
import jax
import jax.numpy as jnp
from jax import lax
from jax.experimental import pallas as pl
from jax.experimental.pallas import tpu as pltpu

F32 = jnp.float32
BF16 = jnp.bfloat16

D_MODEL = 1024
GRID_W = 64
SSD_INNER = 2048
SSD_HEADS = 32
SSD_GROUPS = 4
SSD_HPG = 8
SSD_HEAD_DIM = 64
SSD_STATE = 128
GROUP_W = SSD_HPG * SSD_HEAD_DIM
XBC_WIDTH = SSD_INNER + 2 * SSD_GROUPS * SSD_STATE
SC_WIDTH = 1024
N_MOD = 6
N_EXPERTS = 8
EPS = 1e-6

LANES = 128
SSD_Q = 128
MOE_TM = 512
VMEM_LIMIT = 56 * 1024 * 1024


def _dot(a, b):
    return jnp.dot(a, b, preferred_element_type=F32)


def _sigmoid(v):
    return 1.0 / (1.0 + jnp.exp(-v))


def _silu(v):
    return v * _sigmoid(v)


def _split2(a):
    hi = a.astype(BF16)
    lo = (a - hi.astype(F32)).astype(BF16)
    return hi, lo


def _split3(a):
    hi = a.astype(BF16)
    r = a - hi.astype(F32)
    mid = r.astype(BF16)
    lo = (r - mid.astype(F32)).astype(BF16)
    return hi, mid, lo


def _norm_mod(x, g, shift, scale):
    ms = jnp.mean(x * x, axis=-1, keepdims=True)
    return (x * lax.rsqrt(ms + EPS) * g) * (1.0 + scale) + shift


def _cparams(sem, vmem=VMEM_LIMIT):
    return pltpu.CompilerParams(dimension_semantics=sem, vmem_limit_bytes=vmem)


def _mod_kernel(c_ref, w_ref, b_ref, o_ref):
    a_hi, a_lo = _split2(_silu(c_ref[...]))
    w_hi, w_lo = _split2(w_ref[0])
    o_ref[0] = _dot(a_hi, w_hi) + _dot(a_lo, w_hi) + _dot(a_hi, w_lo) + b_ref[0]


def modulation(cc, w_mod, b_mod):
    depth, d, n = w_mod.shape
    tn = 1536
    return pl.pallas_call(
        _mod_kernel,
        grid=(depth, n // tn),
        in_specs=[pl.BlockSpec((16, d), lambda i, j: (0, 0)),
                  pl.BlockSpec((1, d, tn), lambda i, j: (i, 0, j)),
                  pl.BlockSpec((1, 1, tn), lambda i, j: (i, 0, j))],
        out_specs=pl.BlockSpec((1, 16, tn), lambda i, j: (i, 0, j)),
        out_shape=jax.ShapeDtypeStruct((depth, 16, n), F32),
        compiler_params=_cparams(("parallel", "parallel")),
        name="modulation",
    )(cc, w_mod, b_mod.reshape(depth, 1, n))


def _inproj_kernel(x_ref, g_ref, sh_ref, sc_ref, w_ref, wdt_ref, o_ref, dt_ref, h_ref):
    @pl.when(pl.program_id(2) == 0)
    def _():
        hb = _norm_mod(x_ref[0], g_ref[...], sh_ref[0], sc_ref[0]).astype(BF16)
        h_ref[...] = hb
        dt_ref[0] = _dot(hb, wdt_ref[...])

    o_ref[0] = _dot(h_ref[...], w_ref[...]).astype(o_ref.dtype)


def in_proj(x, g, shift, scale, w, wdt, tm, tn):
    b, l, d = x.shape
    n = w.shape[1]
    return pl.pallas_call(
        _inproj_kernel,
        grid=(b, l // tm, n // tn),
        in_specs=[pl.BlockSpec((1, tm, d), lambda i, m, j: (i, m, 0)),
                  pl.BlockSpec((1, d), lambda i, m, j: (0, 0)),
                  pl.BlockSpec((1, 1, d), lambda i, m, j: (i, 0, 0)),
                  pl.BlockSpec((1, 1, d), lambda i, m, j: (i, 0, 0)),
                  pl.BlockSpec((d, tn), lambda i, m, j: (0, j)),
                  pl.BlockSpec((d, LANES), lambda i, m, j: (0, 0))],
        out_specs=[pl.BlockSpec((1, tm, tn), lambda i, m, j: (i, m, j)),
                   pl.BlockSpec((1, tm, LANES), lambda i, m, j: (i, m, 0))],
        out_shape=[jax.ShapeDtypeStruct((b, l, n), BF16),
                   jax.ShapeDtypeStruct((b, l, LANES), F32)],
        scratch_shapes=[pltpu.VMEM((tm, d), BF16)],
        compiler_params=_cparams(("parallel", "parallel", "arbitrary")),
        name="in_proj",
    )(x, g.reshape(1, d), shift, scale, w, wdt)


def _dt_kernel(raw_ref, bias_ref, a_ref, cs_ref, e_ref, whi_ref, wlo_ref, rt_ref, dtt_ref, edge_ref,
               *, q, ch):
    row = lax.broadcasted_iota(jnp.int32, (q, q), 0)
    col = lax.broadcasted_iota(jnp.int32, (q, q), 1)
    tri_l = jnp.where(col <= row, 1.0, 0.0).astype(BF16)
    tri_u = jnp.where(col >= row, 1.0, 0.0).astype(BF16)
    lane = lax.broadcasted_iota(jnp.int32, (q, LANES), 1)
    fwd = lane < SSD_HEADS
    for k in range(ch):
        v = raw_ref[0, k * q:(k + 1) * q, :] + bias_ref[...]
        dt = jnp.maximum(v, 0.0) + jnp.log1p(jnp.exp(-jnp.abs(v)))
        d1, d2, d3 = _split3(dt * a_ref[...])
        cs_f = _dot(tri_l, d1) + _dot(tri_l, d2) + _dot(tri_l, d3)
        cs_b = _dot(tri_u, d1) + _dot(tri_u, d2) + _dot(tri_u, d3)
        cs = jnp.where(fwd, cs_f, cs_b)
        tot = jnp.where(fwd[0:1], cs_f[q - 1:q, :], cs_b[0:1, :])
        w = dt * jnp.exp(tot - cs)
        w_hi = w.astype(BF16)
        sl = slice(k * q, (k + 1) * q)
        cs_ref[0, sl, :] = cs
        e_ref[0, sl, :] = jnp.exp(cs)
        whi_ref[0, sl, :] = w_hi
        wlo_ref[0, sl, :] = (w - w_hi.astype(F32)).astype(BF16)
        rt_ref[0, k] = (cs - jnp.log(dt)).T
        dtt_ref[0, k] = dt.T
        edge_ref[0, k] = jnp.exp(tot)


def dt_prep(raw, bias, a, q):
    b, l, _ = raw.shape
    nc = l // q
    ch = min(8, nc)
    col = jax.ShapeDtypeStruct((b, l, LANES), F32)
    colb = jax.ShapeDtypeStruct((b, l, LANES), BF16)
    rowt = jax.ShapeDtypeStruct((b, nc, LANES, q), F32)
    cspec = pl.BlockSpec((1, ch * q, LANES), lambda i, c: (i, c, 0))
    rspec = pl.BlockSpec((1, ch, LANES, q), lambda i, c: (i, c, 0, 0))
    vspec = pl.BlockSpec((1, LANES), lambda i, c: (0, 0))
    return pl.pallas_call(
        functools.partial(_dt_kernel, q=q, ch=ch),
        grid=(b, nc // ch),
        in_specs=[cspec, vspec, vspec],
        out_specs=[cspec, cspec, cspec, cspec, rspec, rspec,
                   pl.BlockSpec((1, ch, 1, LANES), lambda i, c: (i, c, 0, 0))],
        out_shape=[col, col, colb, colb, rowt, rowt, jax.ShapeDtypeStruct((b, nc, 1, LANES), F32)],
        compiler_params=_cparams(("parallel", "parallel")),
        name="dt_prep",
    )(raw, bias, a)


def _ssd_kernel(xp_ref, bp_ref, cp_ref, cwx_ref, cwb_ref, cwc_ref, cbx_ref, cbb_ref, cbc_ref,
                col_ref, row_ref, ws_ref, e2f_ref, e2b_ref, edge_ref, dexp_ref, s0_ref,
                y_ref, sfin_ref,
                xs_ref, cc_ref, bt_ref, sbe_ref, sf_ref, sb_ref, *, l, q):
    nc = l // q
    rows = lax.broadcasted_iota(jnp.int32, (q, 1), 0)

    def conv_silu(src_ref, w_ref, b_ref, k):
        r0 = pl.multiple_of(k * q, q)
        blk = src_ref[0, pl.ds(r0, q), :].astype(F32)
        p0 = pl.multiple_of(jnp.maximum(r0 - 16, 0), 16)
        n0 = pl.multiple_of(jnp.minimum(r0 + q, l - 16), 16)
        prev = src_ref[0, pl.ds(p0, 16), :][15:16, :].astype(F32)
        nxt = src_ref[0, pl.ds(n0, 16), :][0:1, :].astype(F32)
        prev = jnp.where(k > 0, prev, jnp.zeros_like(prev))
        nxt = jnp.where(k < nc - 1, nxt, jnp.zeros_like(nxt))
        x_prev = jnp.where(rows == 0, prev, pltpu.roll(blk, 1, 0))
        x_next = jnp.where(rows == q - 1, nxt, pltpu.roll(blk, q - 1, 0))
        w = w_ref[...]
        return _silu(w[0:1] * x_prev + w[1:2] * blk + w[2:3] * x_next + b_ref[...])

    def prep(k, carry):
        r0 = pl.multiple_of(k * q, q)
        xs_ref[pl.ds(r0, q), :] = conv_silu(xp_ref, cwx_ref, cbx_ref, k).astype(BF16)
        cc_ref[pl.ds(r0, q), :] = conv_silu(cp_ref, cwc_ref, cbc_ref, k).astype(BF16)
        bt_ref[k] = conv_silu(bp_ref, cwb_ref, cbb_ref, k).T.astype(BF16)
        return carry

    lax.fori_loop(0, nc, prep, 0)

    def state_step(s_ref, c, r0, e2_ref, d):
        w_exp = _dot(ws_ref[0, pl.ds(r0, q), :], e2_ref[0, 0])
        xw = (xs_ref[pl.ds(r0, q), :].astype(F32) * w_exp).astype(BF16)
        s_ref[...] = s_ref[...] * edge_ref[0, 0, d, pl.ds(c, 1), :] + _dot(bt_ref[c], xw)

    sb_ref[...] = s0_ref[0, 0, 1]

    def bstep(i, carry):
        c = nc - 1 - i
        r0 = pl.multiple_of(c * q, q)
        sbe_ref[c] = sb_ref[...].astype(BF16)
        state_step(sb_ref, c, r0, e2b_ref, 1)
        return carry

    lax.fori_loop(0, nc, bstep, 0)
    sfin_ref[0, 0, 1] = sb_ref[...]

    sf_ref[...] = s0_ref[0, 0, 0]
    li = lax.broadcasted_iota(jnp.int32, (q, q), 0)
    si = lax.broadcasted_iota(jnp.int32, (q, q), 1)
    lower = si <= li
    eye = si == li
    left = lax.broadcasted_iota(jnp.int32, (q, LANES), 1) < SSD_HEAD_DIM

    def fstep(c, carry):
        r0 = pl.multiple_of(c * q, q)
        sf_b = sf_ref[...].astype(BF16)
        sb_b = sbe_ref[c]
        cc = cc_ref[pl.ds(r0, q), :]
        ccf = cc.astype(F32)
        g = _dot(cc, bt_ref[c])
        col = col_ref[0, 0, pl.ds(r0, q), :]
        row = row_ref[0, 0, c]
        for j in range(SSD_HPG // 2):
            lanes = slice(j * LANES, (j + 1) * LANES)
            x_pair = xs_ref[pl.ds(r0, q), lanes]
            outs = []
            for h in (2 * j, 2 * j + 1):
                arg = jnp.where(lower,
                                col[:, h:h + 1] - row[h:h + 1, :],
                                col[:, 8 + h:9 + h] - row[8 + h:9 + h, :])
                p = jnp.exp(arg) + jnp.where(eye, row[16 + h:17 + h, :], 0.0)
                m = (g * p).astype(BF16)
                c_f = (ccf * col[:, 16 + h:17 + h]).astype(BF16)
                c_b = (ccf * col[:, 24 + h:25 + h]).astype(BF16)
                outs.append(_dot(m, x_pair) + _dot(c_f, sf_b[:, lanes]) + _dot(c_b, sb_b[:, lanes]))
            y = jnp.where(left, outs[0], outs[1]) + dexp_ref[0, :, lanes] * x_pair.astype(F32)
            y_ref[0, pl.ds(r0, q), lanes] = y.astype(y_ref.dtype)
        state_step(sf_ref, c, r0, e2f_ref, 0)
        return carry

    lax.fori_loop(0, nc, fstep, 0)
    sfin_ref[0, 0, 0] = sf_ref[...]


def _expanders():
    e = np.zeros((2, SSD_GROUPS, LANES, GROUP_W), np.float32)
    for d in range(2):
        for g in range(SSD_GROUPS):
            for s in range(2):
                for j in range(SSD_HPG):
                    e[d, g, (d * SSD_GROUPS + g) * 16 + s * 8 + j, j * SSD_HEAD_DIM:(j + 1) * SSD_HEAD_DIM] = 1.0
    return jnp.asarray(e, BF16)


def ssd_mixer(proj, xbc_col0, dt_raw, conv_w, conv_b, dt_bias, a_log, d_skip, s0, q):
    b, l, _ = proj.shape
    nc = l // q
    gn = SSD_GROUPS
    bias = jnp.zeros((1, LANES), F32).at[0, :2 * SSD_HEADS].set(dt_bias.reshape(-1))
    a = jnp.zeros((1, LANES), F32).at[0, :2 * SSD_HEADS].set(-jnp.exp(a_log.reshape(-1)))
    cs, e, w_hi, w_lo, rt, dtt, edge = dt_prep(dt_raw, bias, a, q)

    def heads(t):
        return t[..., :2 * SSD_HEADS].reshape(t.shape[:-1] + (2, gn, SSD_HPG))

    colg = jnp.stack([heads(cs), heads(e)], axis=2)
    colg = jnp.transpose(colg, (0, 4, 1, 2, 3, 5)).reshape(b, gn, l, 32)
    wsplit = jnp.stack([heads(w_hi), heads(w_lo)], axis=-2).reshape(b, l, LANES)
    rt = rt[:, :, :2 * SSD_HEADS].reshape(b, nc, 2, gn, SSD_HPG, q)
    dtb = dtt[:, :, SSD_HEADS:2 * SSD_HEADS].reshape(b, nc, 1, gn, SSD_HPG, q)
    rowg = jnp.concatenate([rt, dtb], axis=2)
    rowg = jnp.transpose(rowg, (0, 3, 1, 2, 4, 5)).reshape(b, gn, nc, 24, q)
    edge = heads(edge[:, :, 0, :])
    edge = jnp.repeat(jnp.transpose(edge, (0, 3, 2, 1, 4)), SSD_HEAD_DIM, axis=-1)
    dexp = jnp.repeat(d_skip.astype(F32), SSD_HEAD_DIM).reshape(gn, 1, GROUP_W)
    e2 = _expanders()

    cw = conv_w.astype(F32)
    cb = conv_b.astype(F32).reshape(1, -1)
    nb = SSD_INNER
    cwx, cwb, cwc = cw[:, :nb], cw[:, nb:nb + gn * SSD_STATE], cw[:, nb + gn * SSD_STATE:]
    cbx, cbb, cbc = cb[:, :nb], cb[:, nb:nb + gn * SSD_STATE], cb[:, nb + gn * SSD_STATE:]

    xo = xbc_col0 // GROUP_W
    bo = (xbc_col0 + SSD_INNER) // SSD_STATE
    co = bo + gn
    st_spec = pl.BlockSpec((1, 1, 2, SSD_STATE, GROUP_W), lambda i, g: (i, g, 0, 0, 0))
    y, sfin = pl.pallas_call(
        functools.partial(_ssd_kernel, l=l, q=q),
        grid=(b, gn),
        in_specs=[pl.BlockSpec((1, l, GROUP_W), lambda i, g: (i, 0, xo + g)),
                  pl.BlockSpec((1, l, SSD_STATE), lambda i, g: (i, 0, bo + g)),
                  pl.BlockSpec((1, l, SSD_STATE), lambda i, g: (i, 0, co + g)),
                  pl.BlockSpec((3, GROUP_W), lambda i, g: (0, g)),
                  pl.BlockSpec((3, SSD_STATE), lambda i, g: (0, g)),
                  pl.BlockSpec((3, SSD_STATE), lambda i, g: (0, g)),
                  pl.BlockSpec((1, GROUP_W), lambda i, g: (0, g)),
                  pl.BlockSpec((1, SSD_STATE), lambda i, g: (0, g)),
                  pl.BlockSpec((1, SSD_STATE), lambda i, g: (0, g)),
                  pl.BlockSpec((1, 1, l, 32), lambda i, g: (i, g, 0, 0)),
                  pl.BlockSpec((1, 1, nc, 24, q), lambda i, g: (i, g, 0, 0, 0)),
                  pl.BlockSpec((1, l, LANES), lambda i, g: (i, 0, 0)),
                  pl.BlockSpec((1, 1, LANES, GROUP_W), lambda i, g: (0, g, 0, 0)),
                  pl.BlockSpec((1, 1, LANES, GROUP_W), lambda i, g: (1, g, 0, 0)),
                  pl.BlockSpec((1, 1, 2, nc, GROUP_W), lambda i, g: (i, g, 0, 0, 0)),
                  pl.BlockSpec((1, 1, GROUP_W), lambda i, g: (g, 0, 0)),
                  st_spec],
        out_specs=[pl.BlockSpec((1, l, GROUP_W), lambda i, g: (i, 0, g)), st_spec],
        out_shape=[jax.ShapeDtypeStruct((b, l, SSD_INNER), BF16),
                   jax.ShapeDtypeStruct((b, gn, 2, SSD_STATE, GROUP_W), F32)],
        scratch_shapes=[pltpu.VMEM((l, GROUP_W), BF16),
                        pltpu.VMEM((l, SSD_STATE), BF16),
                        pltpu.VMEM((nc, SSD_STATE, q), BF16),
                        pltpu.VMEM((nc, SSD_STATE, GROUP_W), BF16),
                        pltpu.VMEM((SSD_STATE, GROUP_W), F32),
                        pltpu.VMEM((SSD_STATE, GROUP_W), F32)],
        compiler_params=_cparams(("parallel", "parallel")),
        name="ssd",
    )(proj, proj, proj, cwx, cwb, cwc, cbx, cbb, cbc, colg, rowg, wsplit, e2, e2, edge, dexp, s0)
    return y, sfin


def _mixout_kernel(y_ref, z_ref, gb_ref, gc_ref, hv_ref, g0_ref, g1_ref, x_ref, gate_ref,
                   ng_ref, bg_ref, scw_ref, wssd_ref, wsc_ref, wo_ref, o_ref, *, tm, period):
    yz = y_ref[0].astype(F32) * _silu(z_ref[0].astype(F32))
    ms = jnp.mean(yz * yz, axis=-1, keepdims=True)
    yn = (yz * lax.rsqrt(ms + EPS) * ng_ref[...]).astype(BF16)
    y_ssd = _dot(yn, wssd_ref[...])

    u = gc_ref[0].astype(F32) * hv_ref[0].astype(F32)
    pos = lax.broadcasted_iota(jnp.int32, (tm, 1), 0) % period
    u_prev = jnp.where(pos == 0, 0.0, pltpu.roll(u, 1, 0))
    u_next = jnp.where(pos == period - 1, 0.0, pltpu.roll(u, tm - 1, 0))
    w = scw_ref[...]
    v = w[0:1] * u_prev + w[1:2] * u + w[2:3] * u_next
    y_sc = _dot((gb_ref[0].astype(F32) * v).astype(BF16), wsc_ref[...])

    bg = bg_ref[...]
    g0 = _sigmoid(g0_ref[0].astype(F32) + bg[:, :D_MODEL])
    g1 = _sigmoid(g1_ref[0].astype(F32) + bg[:, D_MODEL:])
    out = _dot((g0 * y_ssd + g1 * y_sc).astype(BF16), wo_ref[...])
    o_ref[0] = x_ref[0] + gate_ref[0] * out


def mixer_out(y, proj, x, gate, norm_g, b_gate, sc_conv_w, w_ssd, w_sc, w_o, tm, period):
    b, l, d = x.shape
    pc = lambda k: pl.BlockSpec((1, tm, d), lambda i, m, k=k: (i, m, k))
    full = lambda shp: pl.BlockSpec(shp, lambda i, m: (0,) * len(shp))
    return pl.pallas_call(
        functools.partial(_mixout_kernel, tm=tm, period=period),
        grid=(b, l // tm),
        in_specs=[pl.BlockSpec((1, tm, SSD_INNER), lambda i, m: (i, m, 0)),
                  pl.BlockSpec((1, tm, SSD_INNER), lambda i, m: (i, m, 0)),
                  pc(5), pc(6), pc(7), pc(8), pc(9),
                  pl.BlockSpec((1, tm, d), lambda i, m: (i, m, 0)),
                  pl.BlockSpec((1, 1, d), lambda i, m: (i, 0, 0)),
                  full((1, SSD_INNER)), full((1, 2 * d)), full((3, SC_WIDTH)),
                  full((SSD_INNER, d)), full((SC_WIDTH, d)), full((d, d))],
        out_specs=pl.BlockSpec((1, tm, d), lambda i, m: (i, m, 0)),
        out_shape=jax.ShapeDtypeStruct((b, l, d), F32),
        compiler_params=_cparams(("parallel", "parallel")),
        name="mixer_out",
    )(y, proj, proj, proj, proj, proj, proj, x, gate,
      norm_g.reshape(1, -1), b_gate.reshape(1, -1), sc_conv_w, w_ssd, w_sc, w_o)


def _ffn_kernel(x_ref, g_ref, sh_ref, sc_ref, gate_ref, w1_ref, w3_ref, w2_ref, o_ref, *, nf):
    x = x_ref[0]
    hb = _norm_mod(x, g_ref[...], sh_ref[0], sc_ref[0]).astype(BF16)
    fw = w1_ref.shape[1] // nf
    acc = None
    for k in range(nf):
        a = _dot(hb, w1_ref[:, k * fw:(k + 1) * fw])
        bb = _dot(hb, w3_ref[:, k * fw:(k + 1) * fw])
        part = _dot((_silu(a) * bb).astype(BF16), w2_ref[k * fw:(k + 1) * fw, :])
        acc = part if acc is None else acc + part
    o_ref[0] = x + gate_ref[0] * acc


def ffn_dense(x, g, shift, scale, gate, w1, w3, w2, tm):
    b, l, d = x.shape
    f = w1.shape[1]
    vec = pl.BlockSpec((1, 1, d), lambda i, m: (i, 0, 0))
    const = lambda shp: pl.BlockSpec(shp, lambda i, m: (0, 0), pipeline_mode=pl.Buffered(1))
    return pl.pallas_call(
        functools.partial(_ffn_kernel, nf=2),
        grid=(b, l // tm),
        in_specs=[pl.BlockSpec((1, tm, d), lambda i, m: (i, m, 0)),
                  pl.BlockSpec((1, d), lambda i, m: (0, 0)),
                  vec, vec, vec, const((d, f)), const((d, f)), const((f, d))],
        out_specs=pl.BlockSpec((1, tm, d), lambda i, m: (i, m, 0)),
        out_shape=jax.ShapeDtypeStruct((b, l, d), F32),
        compiler_params=_cparams(("parallel", "parallel")),
        name="ffn_dense",
    )(x, g.reshape(1, d), shift, scale, gate, w1, w3, w2)


def _router_kernel(x_ref, g_ref, sh_ref, sc_ref, rw_ref, h_ref, route_ref, cnt_ref, run_ref, *, tm):
    @pl.when((pl.program_id(0) == 0) & (pl.program_id(1) == 0))
    def _():
        run_ref[...] = jnp.zeros_like(run_ref)

    h = _norm_mod(x_ref[0], g_ref[...], sh_ref[0], sc_ref[0])
    h_hi, h_lo = _split2(h)
    h_ref[0] = h_hi
    w_hi, w_lo = _split2(rw_ref[...])
    logits = _dot(h_hi, w_hi) + _dot(h_lo, w_hi) + _dot(h_hi, w_lo)
    lane = lax.broadcasted_iota(jnp.int32, (tm, LANES), 1)
    ninf = float("-inf")
    lg = jnp.where(lane < N_EXPERTS, logits, ninf)
    m1 = jnp.max(lg, axis=1, keepdims=True)
    i1 = jnp.min(jnp.where(lg == m1, lane, LANES), axis=1, keepdims=True)
    lg2 = jnp.where(lane == i1, ninf, lg)
    m2 = jnp.max(lg2, axis=1, keepdims=True)
    i2 = jnp.min(jnp.where(lg2 == m2, lane, LANES), axis=1, keepdims=True)
    e2 = jnp.exp(m2 - m1)
    den = 1.0 + e2
    sel1 = jnp.where(lane == i1, 1.0, 0.0)
    sel2 = jnp.where(lane == i2, 1.0, 0.0)
    cnt = sel1 + sel2
    r = lax.broadcasted_iota(jnp.int32, (tm, tm), 0)
    c = lax.broadcasted_iota(jnp.int32, (tm, tm), 1)
    tri = jnp.where(c < r, 1.0, 0.0).astype(BF16)
    base = _dot(tri, cnt.astype(BF16)) + run_ref[0:1, :]
    r1 = jnp.sum(sel1 * base, axis=1, keepdims=True)
    r2 = jnp.sum(sel2 * base, axis=1, keepdims=True)
    vals = (i1.astype(F32), i2.astype(F32), 1.0 / den, e2 / den, r1, r2)
    out = jnp.zeros((tm, LANES), F32)
    for k, v in enumerate(vals):
        out = jnp.where(lane == k, v, out)
    route_ref[0] = out
    new_run = run_ref[...] + jnp.sum(cnt, axis=0, keepdims=True)
    run_ref[...] = new_run
    cnt_ref[...] = new_run


def router(x, g, shift, scale, router_w, tm):
    b, l, d = x.shape
    rw = jnp.zeros((d, LANES), F32).at[:, :N_EXPERTS].set(router_w)
    vec = pl.BlockSpec((1, 1, d), lambda i, m: (i, 0, 0))
    return pl.pallas_call(
        functools.partial(_router_kernel, tm=tm),
        grid=(b, l // tm),
        in_specs=[pl.BlockSpec((1, tm, d), lambda i, m: (i, m, 0)),
                  pl.BlockSpec((1, d), lambda i, m: (0, 0)),
                  vec, vec, pl.BlockSpec((d, LANES), lambda i, m: (0, 0))],
        out_specs=[pl.BlockSpec((1, tm, d), lambda i, m: (i, m, 0)),
                   pl.BlockSpec((1, tm, LANES), lambda i, m: (i, m, 0)),
                   pl.BlockSpec((8, LANES), lambda i, m: (0, 0))],
        out_shape=[jax.ShapeDtypeStruct((b, l, d), BF16),
                   jax.ShapeDtypeStruct((b, l, LANES), F32),
                   jax.ShapeDtypeStruct((8, LANES), F32)],
        scratch_shapes=[pltpu.VMEM((8, LANES), F32)],
        compiler_params=_cparams(("arbitrary", "arbitrary")),
        name="router",
    )(x, g.reshape(1, d), shift, scale, rw)


def _gffn_kernel(te_ref, nv_ref, x_ref, w1_ref, w3_ref, w2_ref, o_ref, acc_ref, *, nf):
    i = pl.program_id(0)
    f = pl.program_id(1)

    @pl.when(i < nv_ref[0])
    def _():
        x = x_ref[...]
        a = _dot(x, w1_ref[0])
        bb = _dot(x, w3_ref[0])
        part = _dot((_silu(a) * bb).astype(BF16), w2_ref[0])

        @pl.when(f == 0)
        def _():
            acc_ref[...] = part

        @pl.when(f > 0)
        def _():
            acc_ref[...] += part

        @pl.when(f == nf - 1)
        def _():
            o_ref[...] = acc_ref[...].astype(o_ref.dtype)

    @pl.when((i >= nv_ref[0]) & (f == nf - 1))
    def _():
        o_ref[...] = jnp.zeros_like(o_ref)


def grouped_ffn(xs, tile_expert, n_valid, w1, w3, w2, tm, nf):
    rows, d = xs.shape
    nt = rows // tm
    f = w1.shape[2]
    fw = f // nf
    grid_spec = pltpu.PrefetchScalarGridSpec(
        num_scalar_prefetch=2,
        grid=(nt, nf),
        in_specs=[pl.BlockSpec((tm, d), lambda i, k, te, nv: (i, 0)),
                  pl.BlockSpec((1, d, fw), lambda i, k, te, nv: (te[i], 0, k)),
                  pl.BlockSpec((1, d, fw), lambda i, k, te, nv: (te[i], 0, k)),
                  pl.BlockSpec((1, fw, d), lambda i, k, te, nv: (te[i], k, 0))],
        out_specs=pl.BlockSpec((tm, d), lambda i, k, te, nv: (i, 0)),
        scratch_shapes=[pltpu.VMEM((tm, d), F32)],
    )
    return pl.pallas_call(
        functools.partial(_gffn_kernel, nf=nf),
        grid_spec=grid_spec,
        out_shape=jax.ShapeDtypeStruct((rows, d), BF16),
        compiler_params=_cparams(("parallel", "arbitrary")),
        name="grouped_ffn",
    )(tile_expert, n_valid, xs, w1, w3, w2)


def _combine_kernel(x_ref, yg_ref, route_ref, gate_ref, fg_ref, o_ref):
    r = route_ref[0]
    yg = yg_ref[0]
    moe = r[:, 2:3] * yg[:, :D_MODEL].astype(F32) + r[:, 3:4] * yg[:, D_MODEL:].astype(F32)
    xn = x_ref[0] + gate_ref[0] * moe
    ms = jnp.mean(xn * xn, axis=-1, keepdims=True)
    o_ref[0] = xn * lax.rsqrt(ms + EPS) * fg_ref[...]


def combine_final(x, yg, route, gate, final_g, tm):
    b, l, d = x.shape
    return pl.pallas_call(
        _combine_kernel,
        grid=(b, l // tm),
        in_specs=[pl.BlockSpec((1, tm, d), lambda i, m: (i, m, 0)),
                  pl.BlockSpec((1, tm, 2 * d), lambda i, m: (i, m, 0)),
                  pl.BlockSpec((1, tm, LANES), lambda i, m: (i, m, 0)),
                  pl.BlockSpec((1, 1, d), lambda i, m: (i, 0, 0)),
                  pl.BlockSpec((1, d), lambda i, m: (0, 0))],
        out_specs=pl.BlockSpec((1, tm, d), lambda i, m: (i, m, 0)),
        out_shape=jax.ShapeDtypeStruct((b, l, d), F32),
        compiler_params=_cparams(("parallel", "parallel")),
        name="combine_final",
    )(x, yg, route, gate, final_g.reshape(1, d))


def dispatch_rows(h, pos, n_rows):
    t, d = h.shape
    xs = jnp.zeros((n_rows, d), h.dtype)
    return xs.at[pos.reshape(-1)].set(jnp.repeat(h, 2, axis=0))


def return_rows(ys, pos):
    return jnp.take(ys, pos.reshape(-1), axis=0)


def moe_block(x, g, shift, scale, gate, router_w, w1, w3, w2, final_g):
    b, l, d = x.shape
    t = b * l
    tm = MOE_TM
    h, route, counts = router(x, g, shift, scale, router_w, min(512, l))
    rt = route.reshape(t, LANES)
    e = rt[:, 0:2].astype(jnp.int32)
    rank = rt[:, 4:6].astype(jnp.int32)
    cnt = counts[0, :N_EXPERTS].astype(jnp.int32)
    gs = ((cnt + tm - 1) // tm) * tm
    ends = jnp.cumsum(gs)
    offs = ends - gs
    pos = offs[e] + rank
    nt = (2 * t) // tm + N_EXPERTS
    n_valid = (ends[-1] // tm).astype(jnp.int32).reshape(1)
    tile = jnp.minimum(jnp.arange(nt, dtype=jnp.int32), n_valid[0] - 1)
    tile_expert = jnp.sum((tile[:, None] >= (ends // tm)[None, :]).astype(jnp.int32), axis=1)
    xs = dispatch_rows(h.reshape(t, d), pos, nt * tm)
    ys = grouped_ffn(xs, tile_expert, n_valid, w1, w3, w2, tm, 2)
    yg = return_rows(ys, pos).reshape(b, l, 2 * d)
    return combine_final(x, yg, route, gate, final_g, min(512, l))


def _in_weights(w_in):
    o1 = SSD_INNER
    o2 = o1 + XBC_WIDTH
    o3 = o2 + 2 * SSD_HEADS
    w_main = jnp.concatenate([w_in[:, :o2], w_in[:, o3:]], axis=1).astype(BF16)
    w_dt = jnp.pad(w_in[:, o2:o3], ((0, 0), (0, LANES - 2 * SSD_HEADS))).astype(BF16)
    return w_main, w_dt


def kernel(x, c, ctx, c_ctx, w_mod, b_mod, norm1_g, norm2_g, w_in, b_gate, ssd_conv_w, ssd_conv_b, ssd_dt_bias, ssd_a_log, ssd_d, ssd_norm_g, w_ssd_out, sc_conv_w, w_sc_out, w_o, ffn_w1, ffn_w3, ffn_w2, router_w, moe_w1, moe_w3, moe_w2, final_g):
    b, l, d = x.shape
    lc = ctx.shape[1]
    depth = w_mod.shape[0]
    cc = jnp.zeros((16, d), F32).at[:b].set(c).at[b].set(c_ctx)
    mod = modulation(cc, w_mod, b_mod)
    zeros_state = jnp.zeros((b, SSD_GROUPS, 2, SSD_STATE, GROUP_W), F32)

    for i in range(depth):
        last = i == depth - 1
        mx = mod[i, :b].reshape(b, N_MOD, 1, d)
        mc = jnp.broadcast_to(mod[i, b].reshape(1, N_MOD, 1, d), (b, N_MOD, 1, d))
        w_main, w_dt = _in_weights(w_in[i])
        ssd_p = (ssd_conv_w[i], ssd_conv_b[i], ssd_dt_bias[i], ssd_a_log[i], ssd_d[i])
        out_p = (ssd_norm_g[i], b_gate[i], sc_conv_w[i], w_ssd_out[i].astype(BF16),
                 w_sc_out[i].astype(BF16), w_o[i].astype(BF16))

        if last:
            w_xbc = w_main[:, SSD_INNER:SSD_INNER + XBC_WIDTH]
            proj_c, dt_c = in_proj(ctx, norm1_g[i], mc[:, 0], mc[:, 1], w_xbc, w_dt, lc, 1024)
            _, s_ctx = ssd_mixer(proj_c, 0, dt_c, *ssd_p, zeros_state, SSD_Q)
        else:
            proj_c, dt_c = in_proj(ctx, norm1_g[i], mc[:, 0], mc[:, 1], w_main, w_dt, lc, 1024)
            y_c, s_ctx = ssd_mixer(proj_c, SSD_INNER, dt_c, *ssd_p, zeros_state, SSD_Q)
            ctx = mixer_out(y_c, proj_c, ctx, mc[:, 2], *out_p, lc, lc)

        proj_x, dt_x = in_proj(x, norm1_g[i], mx[:, 0], mx[:, 1], w_main, w_dt, min(1024, l), 1024)
        y_x, _ = ssd_mixer(proj_x, SSD_INNER, dt_x, *ssd_p, s_ctx, SSD_Q)
        x = mixer_out(y_x, proj_x, x, mx[:, 2], *out_p, min(512, l), GRID_W)

        j = i // 2
        if i % 2 == 0:
            w1, w3, w2 = ffn_w1[j].astype(BF16), ffn_w3[j].astype(BF16), ffn_w2[j].astype(BF16)
            x = ffn_dense(x, norm2_g[i], mx[:, 3], mx[:, 4], mx[:, 5], w1, w3, w2, min(512, l))
            if not last:
                ctx = ffn_dense(ctx, norm2_g[i], mc[:, 3], mc[:, 4], mc[:, 5], w1, w3, w2, lc)
        else:
            assert last, "the routed channel mixer is fused with the final norm"
            w1, w3, w2 = moe_w1[j].astype(BF16), moe_w3[j].astype(BF16), moe_w2[j].astype(BF16)
            x = moe_block(x, norm2_g[i], mx[:, 3], mx[:, 4], mx[:, 5], router_w[j], w1, w3, w2, final_g)
    return x
```

```python
import functools

import numpy as np
import jax
import jax.numpy as jnp
from jax import lax
from jax.experimental import pallas as pl
from jax.experimental.pallas import tpu as pltpu

F32 = jnp.float32
BF16 = jnp.bfloat16

D_MODEL = 1024
GRID_W = 64
SSD_INNER = 2048
SSD_HEADS = 32
SSD_GROUPS = 4
SSD_HPG = 8
SSD_HEAD_DIM = 64
SSD_STATE = 128
GROUP_W = SSD_HPG * SSD_HEAD_DIM
XBC_WIDTH = SSD_INNER + 2 * SSD_GROUPS * SSD_STATE
SC_WIDTH = 1024
N_MOD = 6
N_EXPERTS = 8
EPS = 1e-6

LANES = 128
SSD_Q = 128
MOE_TM = 512
VMEM_LIMIT = 56 * 1024 * 1024


def _dot(a, b):
    return jnp.dot(a, b, preferred_element_type=F32)


def _sigmoid(v):
    return 1.0 / (1.0 + jnp.exp(-v))


def _silu(v):
    return v * _sigmoid(v)


def _split2(a):
    hi = a.astype(BF16)
    lo = (a - hi.astype(F32)).astype(BF16)
    return hi, lo


def _split3(a):
    hi = a.astype(BF16)
    r = a - hi.astype(F32)
    mid = r.astype(BF16)
    lo = (r - mid.astype(F32)).astype(BF16)
    return hi, mid, lo


def _norm_mod(x, g, shift, scale):
    ms = jnp.mean(x * x, axis=-1, keepdims=True)
    return (x * lax.rsqrt(ms + EPS) * g) * (1.0 + scale) + shift


def _cparams(sem, vmem=VMEM_LIMIT):
    return pltpu.CompilerParams(dimension_semantics=sem, vmem_limit_bytes=vmem)


def _mod_kernel(c_ref, w_ref, b_ref, o_ref):
    a_hi, a_lo = _split2(_silu(c_ref[...]))
    w_hi, w_lo = _split2(w_ref[0])
    o_ref[0] = _dot(a_hi, w_hi) + _dot(a_lo, w_hi) + _dot(a_hi, w_lo) + b_ref[0]


def modulation(cc, w_mod, b_mod):
    depth, d, n = w_mod.shape
    tn = 1536
    return pl.pallas_call(
        _mod_kernel,
        grid=(depth, n // tn),
        in_specs=[pl.BlockSpec((16, d), lambda i, j: (0, 0)),
                  pl.BlockSpec((1, d, tn), lambda i, j: (i, 0, j)),
                  pl.BlockSpec((1, 1, tn), lambda i, j: (i, 0, j))],
        out_specs=pl.BlockSpec((1, 16, tn), lambda i, j: (i, 0, j)),
        out_shape=jax.ShapeDtypeStruct((depth, 16, n), F32),
        compiler_params=_cparams(("parallel", "parallel")),
        name="modulation",
    )(cc, w_mod, b_mod.reshape(depth, 1, n))


def _inproj_kernel(x_ref, g_ref, sh_ref, sc_ref, w_ref, wdt_ref, o_ref, dt_ref, h_ref):
    @pl.when(pl.program_id(2) == 0)
    def _():
        hb = _norm_mod(x_ref[0], g_ref[...], sh_ref[0], sc_ref[0]).astype(BF16)
        h_ref[...] = hb
        dt_ref[0] = _dot(hb, wdt_ref[...])

    o_ref[0] = _dot(h_ref[...], w_ref[...]).astype(o_ref.dtype)


def in_proj(x, g, shift, scale, w, wdt, tm, tn):
    b, l, d = x.shape
    n = w.shape[1]
    return pl.pallas_call(
        _inproj_kernel,
        grid=(b, l // tm, n // tn),
        in_specs=[pl.BlockSpec((1, tm, d), lambda i, m, j: (i, m, 0)),
                  pl.BlockSpec((1, d), lambda i, m, j: (0, 0)),
                  pl.BlockSpec((1, 1, d), lambda i, m, j: (i, 0, 0)),
                  pl.BlockSpec((1, 1, d), lambda i, m, j: (i, 0, 0)),
                  pl.BlockSpec((d, tn), lambda i, m, j: (0, j)),
                  pl.BlockSpec((d, LANES), lambda i, m, j: (0, 0))],
        out_specs=[pl.BlockSpec((1, tm, tn), lambda i, m, j: (i, m, j)),
                   pl.BlockSpec((1, tm, LANES), lambda i, m, j: (i, m, 0))],
        out_shape=[jax.ShapeDtypeStruct((b, l, n), BF16),
                   jax.ShapeDtypeStruct((b, l, LANES), F32)],
        scratch_shapes=[pltpu.VMEM((tm, d), BF16)],
        compiler_params=_cparams(("parallel", "parallel", "arbitrary")),
        name="in_proj",
    )(x, g.reshape(1, d), shift, scale, w, wdt)


def _dt_kernel(raw_ref, bias_ref, a_ref, pc_ref, ph_ref, plo_ref, col_ref, row_ref, sp_ref, *, q, ch):
    row = lax.broadcasted_iota(jnp.int32, (q, q), 0)
    col = lax.broadcasted_iota(jnp.int32, (q, q), 1)
    tri_l = jnp.where(col <= row, 1.0, 0.0).astype(BF16)
    tri_u = jnp.where(col >= row, 1.0, 0.0).astype(BF16)
    lane = lax.broadcasted_iota(jnp.int32, (q, LANES), 1)
    fwd = lane < SSD_HEADS
    pc, ph, plo = pc_ref[...], ph_ref[...], plo_ref[...]

    def place_split(t):
        t_hi, t_lo = _split2(t)
        return (_dot(t_hi, ph) + _dot(t_lo, plo)).astype(BF16)

    for k in range(ch):
        v = raw_ref[0, k * q:(k + 1) * q, :] + bias_ref[...]
        dt = jnp.maximum(v, 0.0) + jnp.log1p(jnp.exp(-jnp.abs(v)))
        d1, d2, d3 = _split3(dt * a_ref[...])
        cs_f = _dot(tri_l, d1) + _dot(tri_l, d2) + _dot(tri_l, d3)
        cs_b = _dot(tri_u, d1) + _dot(tri_u, d2) + _dot(tri_u, d3)
        cs = jnp.where(fwd, cs_f, cs_b)
        tot = jnp.where(fwd[0:1], cs_f[q - 1:q, :], cs_b[0:1, :])
        sl = slice(k * q, (k + 1) * q)
        c1, c2, c3 = _split3(cs)
        colsel = _dot(c1, pc) + _dot(c2, pc) + _dot(c3, pc)
        r_t = (cs - jnp.log(dt)).T
        dt_t = dt.T
        for g in range(SSD_GROUPS):
            col_ref[0, g, sl, :] = colsel[:, g * 16:(g + 1) * 16]
            lo = g * SSD_HPG
            row_ref[0, g, k, 0:8, :] = r_t[lo:lo + 8, :]
            row_ref[0, g, k, 8:16, :] = r_t[SSD_HEADS + lo:SSD_HEADS + lo + 8, :]
            row_ref[0, g, k, 16:24, :] = dt_t[SSD_HEADS + lo:SSD_HEADS + lo + 8, :]
        sp_ref[0, sl, 0:LANES] = place_split(dt * jnp.exp(tot - cs))
        sp_ref[0, sl, LANES:2 * LANES] = place_split(jnp.exp(cs))


def _placements():
    pc = np.zeros((LANES, LANES), np.float32)
    ph = np.zeros((LANES, LANES), np.float32)
    plo = np.zeros((LANES, LANES), np.float32)
    for d in range(2):
        for g in range(SSD_GROUPS):
            for j in range(SSD_HPG):
                src = d * SSD_HEADS + g * SSD_HPG + j
                pc[src, g * 16 + d * 8 + j] = 1.0
                ph[src, (d * SSD_GROUPS + g) * 16 + j] = 1.0
                plo[src, (d * SSD_GROUPS + g) * 16 + 8 + j] = 1.0
    return jnp.asarray(pc, BF16), jnp.asarray(ph, BF16), jnp.asarray(plo, BF16)


def dt_prep(raw, bias, a, q):
    b, l, _ = raw.shape
    nc = l // q
    ch = min(8, nc)
    gn = SSD_GROUPS
    vspec = pl.BlockSpec((1, LANES), lambda i, c: (0, 0))
    pspec = pl.BlockSpec((LANES, LANES), lambda i, c: (0, 0))
    return pl.pallas_call(
        functools.partial(_dt_kernel, q=q, ch=ch),
        grid=(b, nc // ch),
        in_specs=[pl.BlockSpec((1, ch * q, LANES), lambda i, c: (i, c, 0)), vspec, vspec, pspec, pspec, pspec],
        out_specs=[pl.BlockSpec((1, gn, ch * q, 16), lambda i, c: (i, 0, c, 0)),
                   pl.BlockSpec((1, gn, ch, 24, q), lambda i, c: (i, 0, c, 0, 0)),
                   pl.BlockSpec((1, ch * q, 2 * LANES), lambda i, c: (i, c, 0))],
        out_shape=[jax.ShapeDtypeStruct((b, gn, l, 16), F32),
                   jax.ShapeDtypeStruct((b, gn, nc, 24, q), F32),
                   jax.ShapeDtypeStruct((b, l, 2 * LANES), BF16)],
        compiler_params=_cparams(("parallel", "parallel")),
        name="dt_prep",
    )(raw, bias, a, *_placements())


def _ssd_kernel(xp_ref, bp_ref, cp_ref, cwx_ref, cwb_ref, cwc_ref, cbx_ref, cbb_ref, cbc_ref,
                col_ref, row_ref, sp_ref, e2f_ref, e2b_ref, dexp_ref, s0_ref,
                y_ref, sfin_ref,
                xs_ref, cc_ref, bt_ref, xwf_ref, xwb_ref, sbe_ref, edge_ref, sf_ref, sb_ref, *, l, q):
    nc = l // q
    rows = lax.broadcasted_iota(jnp.int32, (q, 1), 0)

    def conv_silu(src_ref, w_ref, b_ref, k):
        r0 = pl.multiple_of(k * q, q)
        blk = src_ref[0, pl.ds(r0, q), :].astype(F32)
        p0 = pl.multiple_of(jnp.maximum(r0 - 16, 0), 16)
        n0 = pl.multiple_of(jnp.minimum(r0 + q, l - 16), 16)
        prev = src_ref[0, pl.ds(p0, 16), :][15:16, :].astype(F32)
        nxt = src_ref[0, pl.ds(n0, 16), :][0:1, :].astype(F32)
        prev = jnp.where(k > 0, prev, jnp.zeros_like(prev))
        nxt = jnp.where(k < nc - 1, nxt, jnp.zeros_like(nxt))
        x_prev = jnp.where(rows == 0, prev, pltpu.roll(blk, 1, 0))
        x_next = jnp.where(rows == q - 1, nxt, pltpu.roll(blk, q - 1, 0))
        w = w_ref[...]
        return _silu(w[0:1] * x_prev + w[1:2] * blk + w[2:3] * x_next + b_ref[...])

    def prep(k, carry):
        r0 = pl.multiple_of(k * q, q)
        x = conv_silu(xp_ref, cwx_ref, cbx_ref, k)
        xs_ref[pl.ds(r0, q), :] = x.astype(BF16)
        cc_ref[pl.ds(r0, q), :] = conv_silu(cp_ref, cwc_ref, cbc_ref, k).astype(BF16)
        bt_ref[k] = conv_silu(bp_ref, cwb_ref, cbb_ref, k).T.astype(BF16)
        sp_w = sp_ref[0, pl.ds(r0, q), 0:LANES]
        xwf_ref[pl.ds(r0, q), :] = (x * _dot(sp_w, e2f_ref[0, 0])).astype(BF16)
        xwb_ref[pl.ds(r0, q), :] = (x * _dot(sp_w, e2b_ref[0, 0])).astype(BF16)
        e_last = sp_ref[0, pl.ds(pl.multiple_of(r0 + q - 16, 16), 16), LANES:2 * LANES]
        e_first = sp_ref[0, pl.ds(r0, 16), LANES:2 * LANES]
        edge_ref[k, 0:1, :] = _dot(e_last, e2f_ref[0, 0])[15:16, :]
        edge_ref[k, 1:2, :] = _dot(e_first, e2b_ref[0, 0])[0:1, :]
        return carry

    lax.fori_loop(0, nc, prep, 0)

    def state_step(s_ref, c, xw_ref, d):
        r0 = pl.multiple_of(c * q, q)
        s_ref[...] = s_ref[...] * edge_ref[c, d:d + 1, :] + _dot(bt_ref[c], xw_ref[pl.ds(r0, q), :])

    sb_ref[...] = s0_ref[0, 0, 1]

    def bstep(i, carry):
        c = nc - 1 - i
        sbe_ref[c] = sb_ref[...].astype(BF16)
        state_step(sb_ref, c, xwb_ref, 1)
        return carry

    lax.fori_loop(0, nc, bstep, 0, unroll=2)
    sfin_ref[0, 0, 1] = sb_ref[...]

    sf_ref[...] = s0_ref[0, 0, 0]
    li = lax.broadcasted_iota(jnp.int32, (q, q), 0)
    si = lax.broadcasted_iota(jnp.int32, (q, q), 1)
    lower = si <= li
    eye = si == li
    left = lax.broadcasted_iota(jnp.int32, (q, LANES), 1) < SSD_HEAD_DIM

    def fstep(c, carry):
        r0 = pl.multiple_of(c * q, q)
        cc = cc_ref[pl.ds(r0, q), :]
        sp_e = sp_ref[0, pl.ds(r0, q), LANES:2 * LANES]
        y_off = (_dot(sp_e, e2f_ref[0, 0]) * _dot(cc, sf_ref[...].astype(BF16))
                 + _dot(sp_e, e2b_ref[0, 0]) * _dot(cc, sbe_ref[c]))
        state_step(sf_ref, c, xwf_ref, 0)
        g = _dot(cc, bt_ref[c])
        col = col_ref[0, 0, pl.ds(r0, q), :]
        row = row_ref[0, 0, c]
        for j in range(SSD_HPG // 2):
            lanes = slice(j * LANES, (j + 1) * LANES)
            x_pair = xs_ref[pl.ds(r0, q), lanes]
            outs = []
            for h in (2 * j, 2 * j + 1):
                arg = jnp.where(lower,
                                col[:, h:h + 1] - row[h:h + 1, :],
                                col[:, 8 + h:9 + h] - row[8 + h:9 + h, :])
                p = jnp.exp(arg) + jnp.where(eye, row[16 + h:17 + h, :], 0.0)
                outs.append(_dot((g * p).astype(BF16), x_pair))
            y = (jnp.where(left, outs[0], outs[1]) + y_off[:, lanes]
                 + dexp_ref[0, :, lanes] * x_pair.astype(F32))
            y_ref[0, pl.ds(r0, q), lanes] = y.astype(y_ref.dtype)
        return carry

    lax.fori_loop(0, nc, fstep, 0)
    sfin_ref[0, 0, 0] = sf_ref[...]


def _expanders():
    e = np.zeros((2, SSD_GROUPS, LANES, GROUP_W), np.float32)
    for d in range(2):
        for g in range(SSD_GROUPS):
            for s in range(2):
                for j in range(SSD_HPG):
                    e[d, g, (d * SSD_GROUPS + g) * 16 + s * 8 + j, j * SSD_HEAD_DIM:(j + 1) * SSD_HEAD_DIM] = 1.0
    return jnp.asarray(e, BF16)


def ssd_mixer(proj, xbc_col0, dt_raw, conv_w, conv_b, dt_bias, a_log, d_skip, s0, q):
    b, l, _ = proj.shape
    nc = l // q
    gn = SSD_GROUPS
    bias = jnp.zeros((1, LANES), F32).at[0, :2 * SSD_HEADS].set(dt_bias.reshape(-1))
    a = jnp.zeros((1, LANES), F32).at[0, :2 * SSD_HEADS].set(-jnp.exp(a_log.reshape(-1)))
    colg, rowg, sp = dt_prep(dt_raw, bias, a, q)
    dexp = jnp.repeat(d_skip.astype(F32), SSD_HEAD_DIM).reshape(gn, 1, GROUP_W)
    e2 = _expanders()

    cw = conv_w.astype(F32)
    cb = conv_b.astype(F32).reshape(1, -1)
    nb = SSD_INNER
    cwx, cwb, cwc = cw[:, :nb], cw[:, nb:nb + gn * SSD_STATE], cw[:, nb + gn * SSD_STATE:]
    cbx, cbb, cbc = cb[:, :nb], cb[:, nb:nb + gn * SSD_STATE], cb[:, nb + gn * SSD_STATE:]

    xo = xbc_col0 // GROUP_W
    bo = (xbc_col0 + SSD_INNER) // SSD_STATE
    co = bo + gn
    st_spec = pl.BlockSpec((1, 1, 2, SSD_STATE, GROUP_W), lambda i, g: (i, g, 0, 0, 0))
    y, sfin = pl.pallas_call(
        functools.partial(_ssd_kernel, l=l, q=q),
        grid=(b, gn),
        in_specs=[pl.BlockSpec((1, l, GROUP_W), lambda i, g: (i, 0, xo + g)),
                  pl.BlockSpec((1, l, SSD_STATE), lambda i, g: (i, 0, bo + g)),
                  pl.BlockSpec((1, l, SSD_STATE), lambda i, g: (i, 0, co + g)),
                  pl.BlockSpec((3, GROUP_W), lambda i, g: (0, g)),
                  pl.BlockSpec((3, SSD_STATE), lambda i, g: (0, g)),
                  pl.BlockSpec((3, SSD_STATE), lambda i, g: (0, g)),
                  pl.BlockSpec((1, GROUP_W), lambda i, g: (0, g)),
                  pl.BlockSpec((1, SSD_STATE), lambda i, g: (0, g)),
                  pl.BlockSpec((1, SSD_STATE), lambda i, g: (0, g)),
                  pl.BlockSpec((1, 1, l, 16), lambda i, g: (i, g, 0, 0)),
                  pl.BlockSpec((1, 1, nc, 24, q), lambda i, g: (i, g, 0, 0, 0)),
                  pl.BlockSpec((1, l, 2 * LANES), lambda i, g: (i, 0, 0)),
                  pl.BlockSpec((1, 1, LANES, GROUP_W), lambda i, g: (0, g, 0, 0)),
                  pl.BlockSpec((1, 1, LANES, GROUP_W), lambda i, g: (1, g, 0, 0)),
                  pl.BlockSpec((1, 1, GROUP_W), lambda i, g: (g, 0, 0)),
                  st_spec],
        out_specs=[pl.BlockSpec((1, l, GROUP_W), lambda i, g: (i, 0, g)), st_spec],
        out_shape=[jax.ShapeDtypeStruct((b, l, SSD_INNER), BF16),
                   jax.ShapeDtypeStruct((b, gn, 2, SSD_STATE, GROUP_W), F32)],
        scratch_shapes=[pltpu.VMEM((l, GROUP_W), BF16),
                        pltpu.VMEM((l, SSD_STATE), BF16),
                        pltpu.VMEM((nc, SSD_STATE, q), BF16),
                        pltpu.VMEM((l, GROUP_W), BF16),
                        pltpu.VMEM((l, GROUP_W), BF16),
                        pltpu.VMEM((nc, SSD_STATE, GROUP_W), BF16),
                        pltpu.VMEM((nc, 2, GROUP_W), F32),
                        pltpu.VMEM((SSD_STATE, GROUP_W), F32),
                        pltpu.VMEM((SSD_STATE, GROUP_W), F32)],
        compiler_params=_cparams(("parallel", "parallel")),
        name="ssd",
    )(proj, proj, proj, cwx, cwb, cwc, cbx, cbb, cbc, colg, rowg, sp, e2, e2, dexp, s0)
    return y, sfin


def _mixout_kernel(y_ref, z_ref, gb_ref, gc_ref, hv_ref, g0_ref, g1_ref, x_ref, gate_ref,
                   ng_ref, bg_ref, scw_ref, wssd_ref, wsc_ref, wo_ref, o_ref, *, tm, period):
    yz = y_ref[0].astype(F32) * _silu(z_ref[0].astype(F32))
    ms = jnp.mean(yz * yz, axis=-1, keepdims=True)
    yn = (yz * lax.rsqrt(ms + EPS) * ng_ref[...]).astype(BF16)
    y_ssd = _dot(yn, wssd_ref[...])

    u = gc_ref[0].astype(F32) * hv_ref[0].astype(F32)
    pos = lax.broadcasted_iota(jnp.int32, (tm, 1), 0) % period
    u_prev = jnp.where(pos == 0, 0.0, pltpu.roll(u, 1, 0))
    u_next = jnp.where(pos == period - 1, 0.0, pltpu.roll(u, tm - 1, 0))
    w = scw_ref[...]
    v = w[0:1] * u_prev + w[1:2] * u + w[2:3] * u_next
    y_sc = _dot((gb_ref[0].astype(F32) * v).astype(BF16), wsc_ref[...])

    bg = bg_ref[...]
    g0 = _sigmoid(g0_ref[0].astype(F32) + bg[:, :D_MODEL])
    g1 = _sigmoid(g1_ref[0].astype(F32) + bg[:, D_MODEL:])
    out = _dot((g0 * y_ssd + g1 * y_sc).astype(BF16), wo_ref[...])
    o_ref[0] = x_ref[0] + gate_ref[0] * out


def mixer_out(y, proj, x, gate, norm_g, b_gate, sc_conv_w, w_ssd, w_sc, w_o, tm, period):
    b, l, d = x.shape
    pc = lambda k: pl.BlockSpec((1, tm, d), lambda i, m, k=k: (i, m, k))
    full = lambda shp: pl.BlockSpec(shp, lambda i, m: (0,) * len(shp))
    return pl.pallas_call(
        functools.partial(_mixout_kernel, tm=tm, period=period),
        grid=(b, l // tm),
        in_specs=[pl.BlockSpec((1, tm, SSD_INNER), lambda i, m: (i, m, 0)),
                  pl.BlockSpec((1, tm, SSD_INNER), lambda i, m: (i, m, 0)),
                  pc(5), pc(6), pc(7), pc(8), pc(9),
                  pl.BlockSpec((1, tm, d), lambda i, m: (i, m, 0)),
                  pl.BlockSpec((1, 1, d), lambda i, m: (i, 0, 0)),
                  full((1, SSD_INNER)), full((1, 2 * d)), full((3, SC_WIDTH)),
                  full((SSD_INNER, d)), full((SC_WIDTH, d)), full((d, d))],
        out_specs=pl.BlockSpec((1, tm, d), lambda i, m: (i, m, 0)),
        out_shape=jax.ShapeDtypeStruct((b, l, d), F32),
        compiler_params=_cparams(("parallel", "parallel")),
        name="mixer_out",
    )(y, proj, proj, proj, proj, proj, proj, x, gate,
      norm_g.reshape(1, -1), b_gate.reshape(1, -1), sc_conv_w, w_ssd, w_sc, w_o)


def _ffn_kernel(x_ref, g_ref, sh_ref, sc_ref, gate_ref, w1_ref, w3_ref, w2_ref, o_ref, *, nf):
    x = x_ref[0]
    hb = _norm_mod(x, g_ref[...], sh_ref[0], sc_ref[0]).astype(BF16)
    fw = w1_ref.shape[1] // nf
    acc = None
    for k in range(nf):
        a = _dot(hb, w1_ref[:, k * fw:(k + 1) * fw])
        bb = _dot(hb, w3_ref[:, k * fw:(k + 1) * fw])
        part = _dot((_silu(a) * bb).astype(BF16), w2_ref[k * fw:(k + 1) * fw, :])
        acc = part if acc is None else acc + part
    o_ref[0] = x + gate_ref[0] * acc


def ffn_dense(x, g, shift, scale, gate, w1, w3, w2, tm):
    b, l, d = x.shape
    f = w1.shape[1]
    vec = pl.BlockSpec((1, 1, d), lambda i, m: (i, 0, 0))
    const = lambda shp: pl.BlockSpec(shp, lambda i, m: (0, 0), pipeline_mode=pl.Buffered(1))
    return pl.pallas_call(
        functools.partial(_ffn_kernel, nf=2),
        grid=(b, l // tm),
        in_specs=[pl.BlockSpec((1, tm, d), lambda i, m: (i, m, 0)),
                  pl.BlockSpec((1, d), lambda i, m: (0, 0)),
                  vec, vec, vec, const((d, f)), const((d, f)), const((f, d))],
        out_specs=pl.BlockSpec((1, tm, d), lambda i, m: (i, m, 0)),
        out_shape=jax.ShapeDtypeStruct((b, l, d), F32),
        compiler_params=_cparams(("parallel", "parallel")),
        name="ffn_dense",
    )(x, g.reshape(1, d), shift, scale, gate, w1, w3, w2)


def _router_kernel(x_ref, g_ref, sh_ref, sc_ref, rw_ref, h_ref, route_ref, cnt_ref, run_ref, *, tm):
    @pl.when((pl.program_id(0) == 0) & (pl.program_id(1) == 0))
    def _():
        run_ref[...] = jnp.zeros_like(run_ref)

    h = _norm_mod(x_ref[0], g_ref[...], sh_ref[0], sc_ref[0])
    h_hi, h_lo = _split2(h)
    h_ref[0] = h_hi
    w_hi, w_lo = _split2(rw_ref[...])
    logits = _dot(h_hi, w_hi) + _dot(h_lo, w_hi) + _dot(h_hi, w_lo)
    lane = lax.broadcasted_iota(jnp.int32, (tm, LANES), 1)
    ninf = float("-inf")
    lg = jnp.where(lane < N_EXPERTS, logits, ninf)
    m1 = jnp.max(lg, axis=1, keepdims=True)
    i1 = jnp.min(jnp.where(lg == m1, lane, LANES), axis=1, keepdims=True)
    lg2 = jnp.where(lane == i1, ninf, lg)
    m2 = jnp.max(lg2, axis=1, keepdims=True)
    i2 = jnp.min(jnp.where(lg2 == m2, lane, LANES), axis=1, keepdims=True)
    e2 = jnp.exp(m2 - m1)
    den = 1.0 + e2
    sel1 = jnp.where(lane == i1, 1.0, 0.0)
    sel2 = jnp.where(lane == i2, 1.0, 0.0)
    cnt = sel1 + sel2
    r = lax.broadcasted_iota(jnp.int32, (tm, tm), 0)
    c = lax.broadcasted_iota(jnp.int32, (tm, tm), 1)
    tri = jnp.where(c < r, 1.0, 0.0).astype(BF16)
    base = _dot(tri, cnt.astype(BF16)) + run_ref[0:1, :]
    r1 = jnp.sum(sel1 * base, axis=1, keepdims=True)
    r2 = jnp.sum(sel2 * base, axis=1, keepdims=True)
    vals = (i1.astype(F32), i2.astype(F32), 1.0 / den, e2 / den, r1, r2)
    out = jnp.zeros((tm, LANES), F32)
    for k, v in enumerate(vals):
        out = jnp.where(lane == k, v, out)
    route_ref[0] = out
    new_run = run_ref[...] + jnp.sum(cnt, axis=0, keepdims=True)
    run_ref[...] = new_run
    cnt_ref[...] = new_run


def router(x, g, shift, scale, router_w, tm):
    b, l, d = x.shape
    rw = jnp.zeros((d, LANES), F32).at[:, :N_EXPERTS].set(router_w)
    vec = pl.BlockSpec((1, 1, d), lambda i, m: (i, 0, 0))
    return pl.pallas_call(
        functools.partial(_router_kernel, tm=tm),
        grid=(b, l // tm),
        in_specs=[pl.BlockSpec((1, tm, d), lambda i, m: (i, m, 0)),
                  pl.BlockSpec((1, d), lambda i, m: (0, 0)),
                  vec, vec, pl.BlockSpec((d, LANES), lambda i, m: (0, 0))],
        out_specs=[pl.BlockSpec((1, tm, d), lambda i, m: (i, m, 0)),
                   pl.BlockSpec((1, tm, LANES), lambda i, m: (i, m, 0)),
                   pl.BlockSpec((8, LANES), lambda i, m: (0, 0))],
        out_shape=[jax.ShapeDtypeStruct((b, l, d), BF16),
                   jax.ShapeDtypeStruct((b, l, LANES), F32),
                   jax.ShapeDtypeStruct((8, LANES), F32)],
        scratch_shapes=[pltpu.VMEM((8, LANES), F32)],
        compiler_params=_cparams(("arbitrary", "arbitrary")),
        name="router",
    )(x, g.reshape(1, d), shift, scale, rw)


def _gffn_kernel(te_ref, nv_ref, x_ref, w1_ref, w3_ref, w2_ref, o_ref, acc_ref, *, nf):
    i = pl.program_id(0)
    f = pl.program_id(1)

    @pl.when(i < nv_ref[0])
    def _():
        x = x_ref[...]
        a = _dot(x, w1_ref[0])
        bb = _dot(x, w3_ref[0])
        part = _dot((_silu(a) * bb).astype(BF16), w2_ref[0])

        @pl.when(f == 0)
        def _():
            acc_ref[...] = part

        @pl.when(f > 0)
        def _():
            acc_ref[...] += part

        @pl.when(f == nf - 1)
        def _():
            o_ref[...] = acc_ref[...].astype(o_ref.dtype)

    @pl.when((i >= nv_ref[0]) & (f == nf - 1))
    def _():
        o_ref[...] = jnp.zeros_like(o_ref)


def grouped_ffn(xs, tile_expert, n_valid, w1, w3, w2, tm, nf):
    rows, d = xs.shape
    nt = rows // tm
    f = w1.shape[2]
    fw = f // nf
    grid_spec = pltpu.PrefetchScalarGridSpec(
        num_scalar_prefetch=2,
        grid=(nt, nf),
        in_specs=[pl.BlockSpec((tm, d), lambda i, k, te, nv: (i, 0)),
                  pl.BlockSpec((1, d, fw), lambda i, k, te, nv: (te[i], 0, k)),
                  pl.BlockSpec((1, d, fw), lambda i, k, te, nv: (te[i], 0, k)),
                  pl.BlockSpec((1, fw, d), lambda i, k, te, nv: (te[i], k, 0))],
        out_specs=pl.BlockSpec((tm, d), lambda i, k, te, nv: (i, 0)),
        scratch_shapes=[pltpu.VMEM((tm, d), F32)],
    )
    return pl.pallas_call(
        functools.partial(_gffn_kernel, nf=nf),
        grid_spec=grid_spec,
        out_shape=jax.ShapeDtypeStruct((rows, d), BF16),
        compiler_params=_cparams(("parallel", "arbitrary")),
        name="grouped_ffn",
    )(tile_expert, n_valid, xs, w1, w3, w2)


def _combine_kernel(x_ref, yg_ref, route_ref, gate_ref, fg_ref, o_ref):
    r = route_ref[0]
    yg = yg_ref[0]
    moe = r[:, 2:3] * yg[:, :D_MODEL].astype(F32) + r[:, 3:4] * yg[:, D_MODEL:].astype(F32)
    xn = x_ref[0] + gate_ref[0] * moe
    ms = jnp.mean(xn * xn, axis=-1, keepdims=True)
    o_ref[0] = xn * lax.rsqrt(ms + EPS) * fg_ref[...]


def combine_final(x, yg, route, gate, final_g, tm):
    b, l, d = x.shape
    return pl.pallas_call(
        _combine_kernel,
        grid=(b, l // tm),
        in_specs=[pl.BlockSpec((1, tm, d), lambda i, m: (i, m, 0)),
                  pl.BlockSpec((1, tm, 2 * d), lambda i, m: (i, m, 0)),
                  pl.BlockSpec((1, tm, LANES), lambda i, m: (i, m, 0)),
                  pl.BlockSpec((1, 1, d), lambda i, m: (i, 0, 0)),
                  pl.BlockSpec((1, d), lambda i, m: (0, 0))],
        out_specs=pl.BlockSpec((1, tm, d), lambda i, m: (i, m, 0)),
        out_shape=jax.ShapeDtypeStruct((b, l, d), F32),
        compiler_params=_cparams(("parallel", "parallel")),
        name="combine_final",
    )(x, yg, route, gate, final_g.reshape(1, d))


def dispatch_rows(h, pos, n_rows):
    t, d = h.shape
    xs = jnp.zeros((n_rows, d), h.dtype)
    return xs.at[pos.reshape(-1)].set(jnp.repeat(h, 2, axis=0))


def return_rows(ys, pos):
    return jnp.take(ys, pos.reshape(-1), axis=0)


def moe_block(x, g, shift, scale, gate, router_w, w1, w3, w2, final_g):
    b, l, d = x.shape
    t = b * l
    tm = MOE_TM
    h, route, counts = router(x, g, shift, scale, router_w, min(512, l))
    rt = route.reshape(t, LANES)
    e = rt[:, 0:2].astype(jnp.int32)
    rank = rt[:, 4:6].astype(jnp.int32)
    cnt = counts[0, :N_EXPERTS].astype(jnp.int32)
    gs = ((cnt + tm - 1) // tm) * tm
    ends = jnp.cumsum(gs)
    offs = ends - gs
    pos = offs[e] + rank
    nt = (2 * t) // tm + N_EXPERTS
    n_valid = (ends[-1] // tm).astype(jnp.int32).reshape(1)
    tile = jnp.minimum(jnp.arange(nt, dtype=jnp.int32), n_valid[0] - 1)
    tile_expert = jnp.sum((tile[:, None] >= (ends // tm)[None, :]).astype(jnp.int32), axis=1)
    xs = dispatch_rows(h.reshape(t, d), pos, nt * tm)
    ys = grouped_ffn(xs, tile_expert, n_valid, w1, w3, w2, tm, 2)
    yg = return_rows(ys, pos).reshape(b, l, 2 * d)
    return combine_final(x, yg, route, gate, final_g, min(512, l))


def _in_weights(w_in):
    o1 = SSD_INNER
    o2 = o1 + XBC_WIDTH
    o3 = o2 + 2 * SSD_HEADS
    w_main = jnp.concatenate([w_in[:, :o2], w_in[:, o3:]], axis=1).astype(BF16)
    w_dt = jnp.pad(w_in[:, o2:o3], ((0, 0), (0, LANES - 2 * SSD_HEADS))).astype(BF16)
    return w_main, w_dt


def kernel(x, c, ctx, c_ctx, w_mod, b_mod, norm1_g, norm2_g, w_in, b_gate, ssd_conv_w, ssd_conv_b, ssd_dt_bias, ssd_a_log, ssd_d, ssd_norm_g, w_ssd_out, sc_conv_w, w_sc_out, w_o, ffn_w1, ffn_w3, ffn_w2, router_w, moe_w1, moe_w3, moe_w2, final_g):
    b, l, d = x.shape
    lc = ctx.shape[1]
    depth = w_mod.shape[0]
    cc = jnp.zeros((16, d), F32).at[:b].set(c).at[b].set(c_ctx)
    mod = modulation(cc, w_mod, b_mod)
    zeros_state = jnp.zeros((b, SSD_GROUPS, 2, SSD_STATE, GROUP_W), F32)

    for i in range(depth):
        last = i == depth - 1
        mx = mod[i, :b].reshape(b, N_MOD, 1, d)
        mc = jnp.broadcast_to(mod[i, b].reshape(1, N_MOD, 1, d), (b, N_MOD, 1, d))
        w_main, w_dt = _in_weights(w_in[i])
        ssd_p = (ssd_conv_w[i], ssd_conv_b[i], ssd_dt_bias[i], ssd_a_log[i], ssd_d[i])
        out_p = (ssd_norm_g[i], b_gate[i], sc_conv_w[i], w_ssd_out[i].astype(BF16),
                 w_sc_out[i].astype(BF16), w_o[i].astype(BF16))

        if last:
            w_xbc = w_main[:, SSD_INNER:SSD_INNER + XBC_WIDTH]
            proj_c, dt_c = in_proj(ctx, norm1_g[i], mc[:, 0], mc[:, 1], w_xbc, w_dt, lc, 1024)
            _, s_ctx = ssd_mixer(proj_c, 0, dt_c, *ssd_p, zeros_state, SSD_Q)
        else:
            proj_c, dt_c = in_proj(ctx, norm1_g[i], mc[:, 0], mc[:, 1], w_main, w_dt, lc, 1024)
            y_c, s_ctx = ssd_mixer(proj_c, SSD_INNER, dt_c, *ssd_p, zeros_state, SSD_Q)
            ctx = mixer_out(y_c, proj_c, ctx, mc[:, 2], *out_p, lc, lc)

        proj_x, dt_x = in_proj(x, norm1_g[i], mx[:, 0], mx[:, 1], w_main, w_dt, min(1024, l), 1024)
        y_x, _ = ssd_mixer(proj_x, SSD_INNER, dt_x, *ssd_p, s_ctx, SSD_Q)
        x = mixer_out(y_x, proj_x, x, mx[:, 2], *out_p, min(512, l), GRID_W)

        j = i // 2
        if i % 2 == 0:
            w1, w3, w2 = ffn_w1[j].astype(BF16), ffn_w3[j].astype(BF16), ffn_w2[j].astype(BF16)
            x = ffn_dense(x, norm2_g[i], mx[:, 3], mx[:, 4], mx[:, 5], w1, w3, w2, min(512, l))
            if not last:
                ctx = ffn_dense(ctx, norm2_g[i], mc[:, 3], mc[:, 4], mc[:, 5], w1, w3, w2, lc)
        else:
            assert last, "the routed channel mixer is fused with the final norm"
            w1, w3, w2 = moe_w1[j].astype(BF16), moe_w3[j].astype(BF16), moe_w2[j].astype(BF16)
            x = moe_block(x, norm2_g[i], mx[:, 3], mx[:, 4], mx[:, 5], router_w[j], w1, w3, w2, final_g)
    return x
```

```python
import functools

import numpy as np
import jax
import jax.numpy as jnp
from jax import lax
from jax.experimental import pallas as pl
from jax.experimental.pallas import tpu as pltpu
from jax.experimental.pallas import tpu_sc as plsc

F32 = jnp.float32
BF16 = jnp.bfloat16

D_MODEL = 1024
GRID_W = 64
SSD_INNER = 2048
SSD_HEADS = 32
SSD_GROUPS = 4
SSD_HPG = 8
SSD_HEAD_DIM = 64
SSD_STATE = 128
GROUP_W = SSD_HPG * SSD_HEAD_DIM
XBC_WIDTH = SSD_INNER + 2 * SSD_GROUPS * SSD_STATE
SC_WIDTH = 1024
N_MOD = 6
N_EXPERTS = 8
EPS = 1e-6

LANES = 128
SSD_Q = 128
MOE_TM = 512
VMEM_LIMIT = 56 * 1024 * 1024


def _dot(a, b):
    return jnp.dot(a, b, preferred_element_type=F32)


def _sigmoid(v):
    return 1.0 / (1.0 + jnp.exp(-v))


def _silu(v):
    return v * _sigmoid(v)


def _split2(a):
    hi = a.astype(BF16)
    lo = (a - hi.astype(F32)).astype(BF16)
    return hi, lo


def _split3(a):
    hi = a.astype(BF16)
    r = a - hi.astype(F32)
    mid = r.astype(BF16)
    lo = (r - mid.astype(F32)).astype(BF16)
    return hi, mid, lo


def _norm_mod(x, g, shift, scale):
    ms = jnp.mean(x * x, axis=-1, keepdims=True)
    return (x * lax.rsqrt(ms + EPS) * g) * (1.0 + scale) + shift


def _cparams(sem, vmem=VMEM_LIMIT):
    return pltpu.CompilerParams(dimension_semantics=sem, vmem_limit_bytes=vmem)


def _mod_kernel(c_ref, w_ref, b_ref, o_ref):
    a_hi, a_lo = _split2(_silu(c_ref[...]))
    w_hi, w_lo = _split2(w_ref[0])
    o_ref[0] = _dot(a_hi, w_hi) + _dot(a_lo, w_hi) + _dot(a_hi, w_lo) + b_ref[0]


def modulation(cc, w_mod, b_mod):
    depth, d, n = w_mod.shape
    tn = 1536
    return pl.pallas_call(
        _mod_kernel,
        grid=(depth, n // tn),
        in_specs=[pl.BlockSpec((16, d), lambda i, j: (0, 0)),
                  pl.BlockSpec((1, d, tn), lambda i, j: (i, 0, j)),
                  pl.BlockSpec((1, 1, tn), lambda i, j: (i, 0, j))],
        out_specs=pl.BlockSpec((1, 16, tn), lambda i, j: (i, 0, j)),
        out_shape=jax.ShapeDtypeStruct((depth, 16, n), F32),
        compiler_params=_cparams(("parallel", "parallel")),
        name="modulation",
    )(cc, w_mod, b_mod.reshape(depth, 1, n))


def _inproj_kernel(x_ref, g_ref, sh_ref, sc_ref, w_ref, wdt_ref, o_ref, dt_ref, h_ref):
    @pl.when(pl.program_id(2) == 0)
    def _():
        hb = _norm_mod(x_ref[0], g_ref[...], sh_ref[0], sc_ref[0]).astype(BF16)
        h_ref[...] = hb
        dt_ref[0] = _dot(hb, wdt_ref[...])

    o_ref[0] = _dot(h_ref[...], w_ref[...]).astype(o_ref.dtype)


def in_proj(x, g, shift, scale, w, wdt, tm, tn):
    b, l, d = x.shape
    n = w.shape[1]
    return pl.pallas_call(
        _inproj_kernel,
        grid=(b, l // tm, n // tn),
        in_specs=[pl.BlockSpec((1, tm, d), lambda i, m, j: (i, m, 0)),
                  pl.BlockSpec((1, d), lambda i, m, j: (0, 0)),
                  pl.BlockSpec((1, 1, d), lambda i, m, j: (i, 0, 0)),
                  pl.BlockSpec((1, 1, d), lambda i, m, j: (i, 0, 0)),
                  pl.BlockSpec((d, tn), lambda i, m, j: (0, j)),
                  pl.BlockSpec((d, LANES), lambda i, m, j: (0, 0))],
        out_specs=[pl.BlockSpec((1, tm, tn), lambda i, m, j: (i, m, j)),
                   pl.BlockSpec((1, tm, LANES), lambda i, m, j: (i, m, 0))],
        out_shape=[jax.ShapeDtypeStruct((b, l, n), BF16),
                   jax.ShapeDtypeStruct((b, l, LANES), F32)],
        scratch_shapes=[pltpu.VMEM((tm, d), BF16)],
        compiler_params=_cparams(("parallel", "parallel", "arbitrary")),
        name="in_proj",
    )(x, g.reshape(1, d), shift, scale, w, wdt)


def _dt_kernel(raw_ref, bias_ref, a_ref, pc_ref, ph_ref, plo_ref, col_ref, row_ref, sp_ref, *, q, ch):
    row = lax.broadcasted_iota(jnp.int32, (q, q), 0)
    col = lax.broadcasted_iota(jnp.int32, (q, q), 1)
    tri_l = jnp.where(col <= row, 1.0, 0.0).astype(BF16)
    tri_u = jnp.where(col >= row, 1.0, 0.0).astype(BF16)
    lane = lax.broadcasted_iota(jnp.int32, (q, LANES), 1)
    fwd = lane < SSD_HEADS
    pc, ph, plo = pc_ref[...], ph_ref[...], plo_ref[...]

    def place_split(t):
        t_hi, t_lo = _split2(t)
        return (_dot(t_hi, ph) + _dot(t_lo, plo)).astype(BF16)

    for k in range(ch):
        v = raw_ref[0, k * q:(k + 1) * q, :] + bias_ref[...]
        dt = jnp.maximum(v, 0.0) + jnp.log1p(jnp.exp(-jnp.abs(v)))
        d1, d2, d3 = _split3(dt * a_ref[...])
        cs_f = _dot(tri_l, d1) + _dot(tri_l, d2) + _dot(tri_l, d3)
        cs_b = _dot(tri_u, d1) + _dot(tri_u, d2) + _dot(tri_u, d3)
        cs = jnp.where(fwd, cs_f, cs_b)
        tot = jnp.where(fwd[0:1], cs_f[q - 1:q, :], cs_b[0:1, :])
        sl = slice(k * q, (k + 1) * q)
        c1, c2, c3 = _split3(cs)
        colsel = _dot(c1, pc) + _dot(c2, pc) + _dot(c3, pc)
        r_t = (cs - jnp.log(dt)).T
        dt_t = dt.T
        for g in range(SSD_GROUPS):
            col_ref[0, g, sl, :] = colsel[:, g * 16:(g + 1) * 16]
            lo = g * SSD_HPG
            row_ref[0, g, k, 0:8, :] = r_t[lo:lo + 8, :]
            row_ref[0, g, k, 8:16, :] = r_t[SSD_HEADS + lo:SSD_HEADS + lo + 8, :]
            row_ref[0, g, k, 16:24, :] = dt_t[SSD_HEADS + lo:SSD_HEADS + lo + 8, :]
        sp_ref[0, sl, 0:LANES] = place_split(dt * jnp.exp(tot - cs))
        sp_ref[0, sl, LANES:2 * LANES] = place_split(jnp.exp(cs))


def _placements():
    pc = np.zeros((LANES, LANES), np.float32)
    ph = np.zeros((LANES, LANES), np.float32)
    plo = np.zeros((LANES, LANES), np.float32)
    for d in range(2):
        for g in range(SSD_GROUPS):
            for j in range(SSD_HPG):
                src = d * SSD_HEADS + g * SSD_HPG + j
                pc[src, g * 16 + d * 8 + j] = 1.0
                ph[src, (d * SSD_GROUPS + g) * 16 + j] = 1.0
                plo[src, (d * SSD_GROUPS + g) * 16 + 8 + j] = 1.0
    return jnp.asarray(pc, BF16), jnp.asarray(ph, BF16), jnp.asarray(plo, BF16)


def dt_prep(raw, bias, a, q):
    b, l, _ = raw.shape
    nc = l // q
    ch = min(8, nc)
    gn = SSD_GROUPS
    vspec = pl.BlockSpec((1, LANES), lambda i, c: (0, 0))
    pspec = pl.BlockSpec((LANES, LANES), lambda i, c: (0, 0))
    return pl.pallas_call(
        functools.partial(_dt_kernel, q=q, ch=ch),
        grid=(b, nc // ch),
        in_specs=[pl.BlockSpec((1, ch * q, LANES), lambda i, c: (i, c, 0)), vspec, vspec, pspec, pspec, pspec],
        out_specs=[pl.BlockSpec((1, gn, ch * q, 16), lambda i, c: (i, 0, c, 0)),
                   pl.BlockSpec((1, gn, ch, 24, q), lambda i, c: (i, 0, c, 0, 0)),
                   pl.BlockSpec((1, ch * q, 2 * LANES), lambda i, c: (i, c, 0))],
        out_shape=[jax.ShapeDtypeStruct((b, gn, l, 16), F32),
                   jax.ShapeDtypeStruct((b, gn, nc, 24, q), F32),
                   jax.ShapeDtypeStruct((b, l, 2 * LANES), BF16)],
        compiler_params=_cparams(("parallel", "parallel")),
        name="dt_prep",
    )(raw, bias, a, *_placements())


def _ssd_kernel(xp_ref, bp_ref, cp_ref, cwx_ref, cwb_ref, cwc_ref, cbx_ref, cbb_ref, cbc_ref,
                col_ref, row_ref, sp_ref, e2f_ref, e2b_ref, dexp_ref, s0_ref,
                y_ref, sfin_ref,
                xs_ref, cc_ref, bt_ref, xwf_ref, xwb_ref, sbe_ref, edge_ref, sf_ref, sb_ref, *, l, q):
    nc = l // q
    rows = lax.broadcasted_iota(jnp.int32, (q, 1), 0)

    def conv_silu(src_ref, w_ref, b_ref, k):
        r0 = pl.multiple_of(k * q, q)
        blk = src_ref[0, pl.ds(r0, q), :].astype(F32)
        p0 = pl.multiple_of(jnp.maximum(r0 - 16, 0), 16)
        n0 = pl.multiple_of(jnp.minimum(r0 + q, l - 16), 16)
        prev = src_ref[0, pl.ds(p0, 16), :][15:16, :].astype(F32)
        nxt = src_ref[0, pl.ds(n0, 16), :][0:1, :].astype(F32)
        prev = jnp.where(k > 0, prev, jnp.zeros_like(prev))
        nxt = jnp.where(k < nc - 1, nxt, jnp.zeros_like(nxt))
        x_prev = jnp.where(rows == 0, prev, pltpu.roll(blk, 1, 0))
        x_next = jnp.where(rows == q - 1, nxt, pltpu.roll(blk, q - 1, 0))
        w = w_ref[...]
        return _silu(w[0:1] * x_prev + w[1:2] * blk + w[2:3] * x_next + b_ref[...])

    def prep(k, carry):
        r0 = pl.multiple_of(k * q, q)
        x = conv_silu(xp_ref, cwx_ref, cbx_ref, k)
        xs_ref[pl.ds(r0, q), :] = x.astype(BF16)
        cc_ref[pl.ds(r0, q), :] = conv_silu(cp_ref, cwc_ref, cbc_ref, k).astype(BF16)
        bt_ref[k] = conv_silu(bp_ref, cwb_ref, cbb_ref, k).T.astype(BF16)
        sp_w = sp_ref[0, pl.ds(r0, q), 0:LANES]
        xwf_ref[pl.ds(r0, q), :] = (x * _dot(sp_w, e2f_ref[0, 0])).astype(BF16)
        xwb_ref[pl.ds(r0, q), :] = (x * _dot(sp_w, e2b_ref[0, 0])).astype(BF16)
        e_last = sp_ref[0, pl.ds(pl.multiple_of(r0 + q - 16, 16), 16), LANES:2 * LANES]
        e_first = sp_ref[0, pl.ds(r0, 16), LANES:2 * LANES]
        edge_ref[k, 0:1, :] = _dot(e_last, e2f_ref[0, 0])[15:16, :]
        edge_ref[k, 1:2, :] = _dot(e_first, e2b_ref[0, 0])[0:1, :]
        return carry

    lax.fori_loop(0, nc, prep, 0)

    def state_step(s_ref, c, xw_ref, d):
        r0 = pl.multiple_of(c * q, q)
        s_ref[...] = s_ref[...] * edge_ref[c, d:d + 1, :] + _dot(bt_ref[c], xw_ref[pl.ds(r0, q), :])

    sb_ref[...] = s0_ref[0, 0, 1]

    def bstep(i, carry):
        c = nc - 1 - i
        sbe_ref[c] = sb_ref[...].astype(BF16)
        state_step(sb_ref, c, xwb_ref, 1)
        return carry

    lax.fori_loop(0, nc, bstep, 0, unroll=2)
    sfin_ref[0, 0, 1] = sb_ref[...]

    sf_ref[...] = s0_ref[0, 0, 0]
    li = lax.broadcasted_iota(jnp.int32, (q, q), 0)
    si = lax.broadcasted_iota(jnp.int32, (q, q), 1)
    lower = si <= li
    eye = si == li
    left = lax.broadcasted_iota(jnp.int32, (q, LANES), 1) < SSD_HEAD_DIM

    def fstep(c, carry):
        r0 = pl.multiple_of(c * q, q)
        cc = cc_ref[pl.ds(r0, q), :]
        sp_e = sp_ref[0, pl.ds(r0, q), LANES:2 * LANES]
        y_off = (_dot(sp_e, e2f_ref[0, 0]) * _dot(cc, sf_ref[...].astype(BF16))
                 + _dot(sp_e, e2b_ref[0, 0]) * _dot(cc, sbe_ref[c]))
        state_step(sf_ref, c, xwf_ref, 0)
        g = _dot(cc, bt_ref[c])
        col = col_ref[0, 0, pl.ds(r0, q), :]
        row = row_ref[0, 0, c]
        for j in range(SSD_HPG // 2):
            lanes = slice(j * LANES, (j + 1) * LANES)
            x_pair = xs_ref[pl.ds(r0, q), lanes]
            outs = []
            for h in (2 * j, 2 * j + 1):
                arg = jnp.where(lower,
                                col[:, h:h + 1] - row[h:h + 1, :],
                                col[:, 8 + h:9 + h] - row[8 + h:9 + h, :])
                p = jnp.exp(arg) + jnp.where(eye, row[16 + h:17 + h, :], 0.0)
                outs.append(_dot((g * p).astype(BF16), x_pair))
            y = (jnp.where(left, outs[0], outs[1]) + y_off[:, lanes]
                 + dexp_ref[0, :, lanes] * x_pair.astype(F32))
            y_ref[0, pl.ds(r0, q), lanes] = y.astype(y_ref.dtype)
        return carry

    lax.fori_loop(0, nc, fstep, 0)
    sfin_ref[0, 0, 0] = sf_ref[...]


def _expanders():
    e = np.zeros((2, SSD_GROUPS, LANES, GROUP_W), np.float32)
    for d in range(2):
        for g in range(SSD_GROUPS):
            for s in range(2):
                for j in range(SSD_HPG):
                    e[d, g, (d * SSD_GROUPS + g) * 16 + s * 8 + j, j * SSD_HEAD_DIM:(j + 1) * SSD_HEAD_DIM] = 1.0
    return jnp.asarray(e, BF16)


def ssd_mixer(proj, xbc_col0, dt_raw, conv_w, conv_b, dt_bias, a_log, d_skip, s0, q):
    b, l, _ = proj.shape
    nc = l // q
    gn = SSD_GROUPS
    bias = jnp.zeros((1, LANES), F32).at[0, :2 * SSD_HEADS].set(dt_bias.reshape(-1))
    a = jnp.zeros((1, LANES), F32).at[0, :2 * SSD_HEADS].set(-jnp.exp(a_log.reshape(-1)))
    colg, rowg, sp = dt_prep(dt_raw, bias, a, q)
    dexp = jnp.repeat(d_skip.astype(F32), SSD_HEAD_DIM).reshape(gn, 1, GROUP_W)
    e2 = _expanders()

    cw = conv_w.astype(F32)
    cb = conv_b.astype(F32).reshape(1, -1)
    nb = SSD_INNER
    cwx, cwb, cwc = cw[:, :nb], cw[:, nb:nb + gn * SSD_STATE], cw[:, nb + gn * SSD_STATE:]
    cbx, cbb, cbc = cb[:, :nb], cb[:, nb:nb + gn * SSD_STATE], cb[:, nb + gn * SSD_STATE:]

    xo = xbc_col0 // GROUP_W
    bo = (xbc_col0 + SSD_INNER) // SSD_STATE
    co = bo + gn
    st_spec = pl.BlockSpec((1, 1, 2, SSD_STATE, GROUP_W), lambda i, g: (i, g, 0, 0, 0))
    y, sfin = pl.pallas_call(
        functools.partial(_ssd_kernel, l=l, q=q),
        grid=(b, gn),
        in_specs=[pl.BlockSpec((1, l, GROUP_W), lambda i, g: (i, 0, xo + g)),
                  pl.BlockSpec((1, l, SSD_STATE), lambda i, g: (i, 0, bo + g)),
                  pl.BlockSpec((1, l, SSD_STATE), lambda i, g: (i, 0, co + g)),
                  pl.BlockSpec((3, GROUP_W), lambda i, g: (0, g)),
                  pl.BlockSpec((3, SSD_STATE), lambda i, g: (0, g)),
                  pl.BlockSpec((3, SSD_STATE), lambda i, g: (0, g)),
                  pl.BlockSpec((1, GROUP_W), lambda i, g: (0, g)),
                  pl.BlockSpec((1, SSD_STATE), lambda i, g: (0, g)),
                  pl.BlockSpec((1, SSD_STATE), lambda i, g: (0, g)),
                  pl.BlockSpec((1, 1, l, 16), lambda i, g: (i, g, 0, 0)),
                  pl.BlockSpec((1, 1, nc, 24, q), lambda i, g: (i, g, 0, 0, 0)),
                  pl.BlockSpec((1, l, 2 * LANES), lambda i, g: (i, 0, 0)),
                  pl.BlockSpec((1, 1, LANES, GROUP_W), lambda i, g: (0, g, 0, 0)),
                  pl.BlockSpec((1, 1, LANES, GROUP_W), lambda i, g: (1, g, 0, 0)),
                  pl.BlockSpec((1, 1, GROUP_W), lambda i, g: (g, 0, 0)),
                  st_spec],
        out_specs=[pl.BlockSpec((1, l, GROUP_W), lambda i, g: (i, 0, g)), st_spec],
        out_shape=[jax.ShapeDtypeStruct((b, l, SSD_INNER), BF16),
                   jax.ShapeDtypeStruct((b, gn, 2, SSD_STATE, GROUP_W), F32)],
        scratch_shapes=[pltpu.VMEM((l, GROUP_W), BF16),
                        pltpu.VMEM((l, SSD_STATE), BF16),
                        pltpu.VMEM((nc, SSD_STATE, q), BF16),
                        pltpu.VMEM((l, GROUP_W), BF16),
                        pltpu.VMEM((l, GROUP_W), BF16),
                        pltpu.VMEM((nc, SSD_STATE, GROUP_W), BF16),
                        pltpu.VMEM((nc, 2, GROUP_W), F32),
                        pltpu.VMEM((SSD_STATE, GROUP_W), F32),
                        pltpu.VMEM((SSD_STATE, GROUP_W), F32)],
        compiler_params=_cparams(("parallel", "parallel")),
        name="ssd",
    )(proj, proj, proj, cwx, cwb, cwc, cbx, cbb, cbc, colg, rowg, sp, e2, e2, dexp, s0)
    return y, sfin


def _mixout_kernel(y_ref, z_ref, gb_ref, gc_ref, hv_ref, g0_ref, g1_ref, x_ref, gate_ref,
                   ng_ref, bg_ref, scw_ref, wssd_ref, wsc_ref, wo_ref, o_ref, *, tm, period):
    yz = y_ref[0].astype(F32) * _silu(z_ref[0].astype(F32))
    ms = jnp.mean(yz * yz, axis=-1, keepdims=True)
    yn = (yz * lax.rsqrt(ms + EPS) * ng_ref[...]).astype(BF16)
    y_ssd = _dot(yn, wssd_ref[...])

    u = gc_ref[0].astype(F32) * hv_ref[0].astype(F32)
    pos = lax.broadcasted_iota(jnp.int32, (tm, 1), 0) % period
    u_prev = jnp.where(pos == 0, 0.0, pltpu.roll(u, 1, 0))
    u_next = jnp.where(pos == period - 1, 0.0, pltpu.roll(u, tm - 1, 0))
    w = scw_ref[...]
    v = w[0:1] * u_prev + w[1:2] * u + w[2:3] * u_next
    y_sc = _dot((gb_ref[0].astype(F32) * v).astype(BF16), wsc_ref[...])

    bg = bg_ref[...]
    g0 = _sigmoid(g0_ref[0].astype(F32) + bg[:, :D_MODEL])
    g1 = _sigmoid(g1_ref[0].astype(F32) + bg[:, D_MODEL:])
    out = _dot((g0 * y_ssd + g1 * y_sc).astype(BF16), wo_ref[...])
    o_ref[0] = x_ref[0] + gate_ref[0] * out


def mixer_out(y, proj, x, gate, norm_g, b_gate, sc_conv_w, w_ssd, w_sc, w_o, tm, period):
    b, l, d = x.shape
    pc = lambda k: pl.BlockSpec((1, tm, d), lambda i, m, k=k: (i, m, k))
    full = lambda shp: pl.BlockSpec(shp, lambda i, m: (0,) * len(shp))
    return pl.pallas_call(
        functools.partial(_mixout_kernel, tm=tm, period=period),
        grid=(b, l // tm),
        in_specs=[pl.BlockSpec((1, tm, SSD_INNER), lambda i, m: (i, m, 0)),
                  pl.BlockSpec((1, tm, SSD_INNER), lambda i, m: (i, m, 0)),
                  pc(5), pc(6), pc(7), pc(8), pc(9),
                  pl.BlockSpec((1, tm, d), lambda i, m: (i, m, 0)),
                  pl.BlockSpec((1, 1, d), lambda i, m: (i, 0, 0)),
                  full((1, SSD_INNER)), full((1, 2 * d)), full((3, SC_WIDTH)),
                  full((SSD_INNER, d)), full((SC_WIDTH, d)), full((d, d))],
        out_specs=pl.BlockSpec((1, tm, d), lambda i, m: (i, m, 0)),
        out_shape=jax.ShapeDtypeStruct((b, l, d), F32),
        compiler_params=_cparams(("parallel", "parallel")),
        name="mixer_out",
    )(y, proj, proj, proj, proj, proj, proj, x, gate,
      norm_g.reshape(1, -1), b_gate.reshape(1, -1), sc_conv_w, w_ssd, w_sc, w_o)


def _ffn_kernel(x_ref, g_ref, sh_ref, sc_ref, gate_ref, w1_ref, w3_ref, w2_ref, o_ref, *, nf):
    x = x_ref[0]
    hb = _norm_mod(x, g_ref[...], sh_ref[0], sc_ref[0]).astype(BF16)
    fw = w1_ref.shape[1] // nf
    acc = None
    for k in range(nf):
        a = _dot(hb, w1_ref[:, k * fw:(k + 1) * fw])
        bb = _dot(hb, w3_ref[:, k * fw:(k + 1) * fw])
        part = _dot((_silu(a) * bb).astype(BF16), w2_ref[k * fw:(k + 1) * fw, :])
        acc = part if acc is None else acc + part
    o_ref[0] = x + gate_ref[0] * acc


def ffn_dense(x, g, shift, scale, gate, w1, w3, w2, tm):
    b, l, d = x.shape
    f = w1.shape[1]
    vec = pl.BlockSpec((1, 1, d), lambda i, m: (i, 0, 0))
    const = lambda shp: pl.BlockSpec(shp, lambda i, m: (0, 0), pipeline_mode=pl.Buffered(1))
    return pl.pallas_call(
        functools.partial(_ffn_kernel, nf=2),
        grid=(b, l // tm),
        in_specs=[pl.BlockSpec((1, tm, d), lambda i, m: (i, m, 0)),
                  pl.BlockSpec((1, d), lambda i, m: (0, 0)),
                  vec, vec, vec, const((d, f)), const((d, f)), const((f, d))],
        out_specs=pl.BlockSpec((1, tm, d), lambda i, m: (i, m, 0)),
        out_shape=jax.ShapeDtypeStruct((b, l, d), F32),
        compiler_params=_cparams(("parallel", "parallel")),
        name="ffn_dense",
    )(x, g.reshape(1, d), shift, scale, gate, w1, w3, w2)


def _router_kernel(x_ref, g_ref, sh_ref, sc_ref, rw_ref, h_ref, route_ref, cnt_ref, run_ref, *, tm):
    @pl.when((pl.program_id(0) == 0) & (pl.program_id(1) == 0))
    def _():
        run_ref[...] = jnp.zeros_like(run_ref)

    h = _norm_mod(x_ref[0], g_ref[...], sh_ref[0], sc_ref[0])
    h_hi, h_lo = _split2(h)
    h_ref[0] = h
    w_hi, w_lo = _split2(rw_ref[...])
    logits = _dot(h_hi, w_hi) + _dot(h_lo, w_hi) + _dot(h_hi, w_lo)
    lane = lax.broadcasted_iota(jnp.int32, (tm, LANES), 1)
    ninf = float("-inf")
    lg = jnp.where(lane < N_EXPERTS, logits, ninf)
    m1 = jnp.max(lg, axis=1, keepdims=True)
    i1 = jnp.min(jnp.where(lg == m1, lane, LANES), axis=1, keepdims=True)
    lg2 = jnp.where(lane == i1, ninf, lg)
    m2 = jnp.max(lg2, axis=1, keepdims=True)
    i2 = jnp.min(jnp.where(lg2 == m2, lane, LANES), axis=1, keepdims=True)
    e2 = jnp.exp(m2 - m1)
    den = 1.0 + e2
    sel1 = jnp.where(lane == i1, 1.0, 0.0)
    sel2 = jnp.where(lane == i2, 1.0, 0.0)
    cnt = sel1 + sel2
    r = lax.broadcasted_iota(jnp.int32, (tm, tm), 0)
    c = lax.broadcasted_iota(jnp.int32, (tm, tm), 1)
    tri = jnp.where(c < r, 1.0, 0.0).astype(BF16)
    base = _dot(tri, cnt.astype(BF16)) + run_ref[0:1, :]
    r1 = jnp.sum(sel1 * base, axis=1, keepdims=True)
    r2 = jnp.sum(sel2 * base, axis=1, keepdims=True)
    vals = (i1.astype(F32), i2.astype(F32), 1.0 / den, e2 / den, r1, r2)
    out = jnp.zeros((tm, LANES), F32)
    for k, v in enumerate(vals):
        out = jnp.where(lane == k, v, out)
    route_ref[0] = out
    new_run = run_ref[...] + jnp.sum(cnt, axis=0, keepdims=True)
    run_ref[...] = new_run
    cnt_ref[...] = new_run


def router(x, g, shift, scale, router_w, tm):
    b, l, d = x.shape
    rw = jnp.zeros((d, LANES), F32).at[:, :N_EXPERTS].set(router_w)
    vec = pl.BlockSpec((1, 1, d), lambda i, m: (i, 0, 0))
    return pl.pallas_call(
        functools.partial(_router_kernel, tm=tm),
        grid=(b, l // tm),
        in_specs=[pl.BlockSpec((1, tm, d), lambda i, m: (i, m, 0)),
                  pl.BlockSpec((1, d), lambda i, m: (0, 0)),
                  vec, vec, pl.BlockSpec((d, LANES), lambda i, m: (0, 0))],
        out_specs=[pl.BlockSpec((1, tm, d), lambda i, m: (i, m, 0)),
                   pl.BlockSpec((1, tm, LANES), lambda i, m: (i, m, 0)),
                   pl.BlockSpec((8, LANES), lambda i, m: (0, 0))],
        out_shape=[jax.ShapeDtypeStruct((b, l, d), F32),
                   jax.ShapeDtypeStruct((b, l, LANES), F32),
                   jax.ShapeDtypeStruct((8, LANES), F32)],
        scratch_shapes=[pltpu.VMEM((8, LANES), F32)],
        compiler_params=_cparams(("arbitrary", "arbitrary")),
        name="router",
    )(x, g.reshape(1, d), shift, scale, rw)


def _gffn_kernel(te_ref, nv_ref, x_ref, w1_ref, w3_ref, w2_ref, o_ref):
    i = pl.program_id(0)
    f = pl.program_id(1)

    @pl.when(i < nv_ref[0])
    def _():
        x = x_ref[...].astype(BF16)
        a = _dot(x, w1_ref[0])
        bb = _dot(x, w3_ref[0])
        part = _dot((_silu(a) * bb).astype(BF16), w2_ref[0])

        @pl.when(f == 0)
        def _():
            o_ref[...] = part

        @pl.when(f > 0)
        def _():
            o_ref[...] += part

    @pl.when((i >= nv_ref[0]) & (f == 0))
    def _():
        o_ref[...] = jnp.zeros_like(o_ref)


def grouped_ffn(xs, tile_expert, n_valid, w1, w3, w2, tm, nf):
    rows, d = xs.shape
    nt = rows // tm
    f = w1.shape[2]
    fw = f // nf
    grid_spec = pltpu.PrefetchScalarGridSpec(
        num_scalar_prefetch=2,
        grid=(nt, nf),
        in_specs=[pl.BlockSpec((tm, d), lambda i, k, te, nv: (i, 0)),
                  pl.BlockSpec((1, d, fw), lambda i, k, te, nv: (te[i], 0, k)),
                  pl.BlockSpec((1, d, fw), lambda i, k, te, nv: (te[i], 0, k)),
                  pl.BlockSpec((1, fw, d), lambda i, k, te, nv: (te[i], k, 0))],
        out_specs=pl.BlockSpec((tm, d), lambda i, k, te, nv: (i, 0)),
    )
    return pl.pallas_call(
        _gffn_kernel,
        grid_spec=grid_spec,
        out_shape=jax.ShapeDtypeStruct((rows, d), F32),
        compiler_params=_cparams(("parallel", "arbitrary")),
        name="grouped_ffn",
    )(tile_expert, n_valid, xs, w1, w3, w2)


def _combine_kernel(x_ref, y0_ref, y1_ref, route_ref, gate_ref, fg_ref, o_ref):
    r = route_ref[0]
    moe = r[:, 2:3] * y0_ref[0, 0] + r[:, 3:4] * y1_ref[0, 0]
    xn = x_ref[0] + gate_ref[0] * moe
    ms = jnp.mean(xn * xn, axis=-1, keepdims=True)
    o_ref[0] = xn * lax.rsqrt(ms + EPS) * fg_ref[...]


def combine_final(x, yg, route, gate, final_g, tm):
    b, l, d = x.shape
    return pl.pallas_call(
        _combine_kernel,
        grid=(b, l // tm),
        in_specs=[pl.BlockSpec((1, tm, d), lambda i, m: (i, m, 0)),
                  pl.BlockSpec((1, 1, tm, d), lambda i, m: (0, i, m, 0)),
                  pl.BlockSpec((1, 1, tm, d), lambda i, m: (1, i, m, 0)),
                  pl.BlockSpec((1, tm, LANES), lambda i, m: (i, m, 0)),
                  pl.BlockSpec((1, 1, d), lambda i, m: (i, 0, 0)),
                  pl.BlockSpec((1, d), lambda i, m: (0, 0))],
        out_specs=pl.BlockSpec((1, tm, d), lambda i, m: (i, m, 0)),
        out_shape=jax.ShapeDtypeStruct((b, l, d), F32),
        compiler_params=_cparams(("parallel", "parallel")),
        name="combine_final",
    )(x, yg, yg, route, gate, final_g.reshape(1, d))


SC_CORES = 2
SC_SUBCORES = 16
SC_WORKERS = SC_CORES * SC_SUBCORES
SC_ROWS = 64


def _sc_mesh():
    return plsc.VectorSubcoreMesh(core_axis_name="c", subcore_axis_name="s",
                                  num_cores=SC_CORES, num_subcores=SC_SUBCORES)


def dispatch_rows(h, pos0, pos1, n_rows):
    t, d = h.shape
    per_w = t // SC_WORKERS
    ch = min(SC_ROWS, per_w)

    @functools.partial(
        pl.kernel, mesh=_sc_mesh(),
        out_type=jax.ShapeDtypeStruct((n_rows, d), h.dtype),
        scratch_types=[pltpu.VMEM((ch,), jnp.int32), pltpu.VMEM((ch,), jnp.int32),
                       pltpu.VMEM((ch, d), h.dtype), pltpu.SemaphoreType.DMA],
        name="moe_dispatch")
    def scatter(h_hbm, p0_hbm, p1_hbm, out_hbm, i0_v, i1_v, rows_v, sem):
        base = (lax.axis_index("s") * SC_CORES + lax.axis_index("c")) * per_w

        @pl.loop(0, per_w // ch)
        def _(j):
            off = base + j * ch
            pltpu.sync_copy(h_hbm.at[pl.ds(off, ch)], rows_v)
            pltpu.sync_copy(p0_hbm.at[pl.ds(off, ch)], i0_v)
            pltpu.sync_copy(p1_hbm.at[pl.ds(off, ch)], i1_v)
            pltpu.async_copy(rows_v, out_hbm.at[i0_v], sem).wait()
            pltpu.async_copy(rows_v, out_hbm.at[i1_v], sem).wait()

    return scatter(h, pos0, pos1)


def return_rows(ys, idx):
    n = idx.shape[0]
    d = ys.shape[1]
    per_w = n // SC_WORKERS
    ch = min(SC_ROWS, per_w)

    @functools.partial(
        pl.kernel, mesh=_sc_mesh(),
        out_type=jax.ShapeDtypeStruct((n, d), ys.dtype),
        scratch_types=[pltpu.VMEM((ch,), jnp.int32), pltpu.VMEM((ch, d), ys.dtype), pltpu.SemaphoreType.DMA],
        name="moe_return")
    def gather(ys_hbm, idx_hbm, out_hbm, idx_v, rows_v, sem):
        base = (lax.axis_index("s") * SC_CORES + lax.axis_index("c")) * per_w

        @pl.loop(0, per_w // ch)
        def _(j):
            off = base + j * ch
            pltpu.sync_copy(idx_hbm.at[pl.ds(off, ch)], idx_v)
            pltpu.async_copy(ys_hbm.at[idx_v], rows_v, sem).wait()
            pltpu.sync_copy(rows_v, out_hbm.at[pl.ds(off, ch)])

    return gather(ys, idx)


def moe_block(x, g, shift, scale, gate, router_w, w1, w3, w2, final_g):
    b, l, d = x.shape
    t = b * l
    tm = MOE_TM
    h, route, counts = router(x, g, shift, scale, router_w, min(512, l))
    rt = route.reshape(t, LANES)
    cnt = counts[0, :N_EXPERTS].astype(jnp.int32)
    gs = ((cnt + tm - 1) // tm) * tm
    ends = jnp.cumsum(gs)
    offs = ends - gs
    pos0 = offs[rt[:, 0].astype(jnp.int32)] + rt[:, 4].astype(jnp.int32)
    pos1 = offs[rt[:, 1].astype(jnp.int32)] + rt[:, 5].astype(jnp.int32)
    nt = (2 * t) // tm + N_EXPERTS
    n_valid = (ends[-1] // tm).astype(jnp.int32).reshape(1)
    tile = jnp.minimum(jnp.arange(nt, dtype=jnp.int32), n_valid[0] - 1)
    tile_expert = jnp.sum((tile[:, None] >= (ends // tm)[None, :]).astype(jnp.int32), axis=1)
    xs = dispatch_rows(h.reshape(t, d), pos0, pos1, nt * tm)
    ys = grouped_ffn(xs, tile_expert, n_valid, w1, w3, w2, tm, 2)
    yg = return_rows(ys, jnp.concatenate([pos0, pos1])).reshape(2, b, l, d)
    return combine_final(x, yg, route, gate, final_g, min(512, l))


def _in_weights(w_in):
    o1 = SSD_INNER
    o2 = o1 + XBC_WIDTH
    o3 = o2 + 2 * SSD_HEADS
    w_main = jnp.concatenate([w_in[:, :o2], w_in[:, o3:]], axis=1).astype(BF16)
    w_dt = jnp.pad(w_in[:, o2:o3], ((0, 0), (0, LANES - 2 * SSD_HEADS))).astype(BF16)
    return w_main, w_dt


def kernel(x, c, ctx, c_ctx, w_mod, b_mod, norm1_g, norm2_g, w_in, b_gate, ssd_conv_w, ssd_conv_b, ssd_dt_bias, ssd_a_log, ssd_d, ssd_norm_g, w_ssd_out, sc_conv_w, w_sc_out, w_o, ffn_w1, ffn_w3, ffn_w2, router_w, moe_w1, moe_w3, moe_w2, final_g):
    b, l, d = x.shape
    lc = ctx.shape[1]
    depth = w_mod.shape[0]
    cc = jnp.zeros((16, d), F32).at[:b].set(c).at[b].set(c_ctx)
    mod = modulation(cc, w_mod, b_mod)
    zeros_state = jnp.zeros((b, SSD_GROUPS, 2, SSD_STATE, GROUP_W), F32)

    for i in range(depth):
        last = i == depth - 1
        mx = mod[i, :b].reshape(b, N_MOD, 1, d)
        mc = jnp.broadcast_to(mod[i, b].reshape(1, N_MOD, 1, d), (b, N_MOD, 1, d))
        w_main, w_dt = _in_weights(w_in[i])
        ssd_p = (ssd_conv_w[i], ssd_conv_b[i], ssd_dt_bias[i], ssd_a_log[i], ssd_d[i])
        out_p = (ssd_norm_g[i], b_gate[i], sc_conv_w[i], w_ssd_out[i].astype(BF16),
                 w_sc_out[i].astype(BF16), w_o[i].astype(BF16))

        if last:
            w_xbc = w_main[:, SSD_INNER:SSD_INNER + XBC_WIDTH]
            proj_c, dt_c = in_proj(ctx, norm1_g[i], mc[:, 0], mc[:, 1], w_xbc, w_dt, lc, 1024)
            _, s_ctx = ssd_mixer(proj_c, 0, dt_c, *ssd_p, zeros_state, SSD_Q)
        else:
            proj_c, dt_c = in_proj(ctx, norm1_g[i], mc[:, 0], mc[:, 1], w_main, w_dt, lc, 1024)
            y_c, s_ctx = ssd_mixer(proj_c, SSD_INNER, dt_c, *ssd_p, zeros_state, SSD_Q)
            ctx = mixer_out(y_c, proj_c, ctx, mc[:, 2], *out_p, lc, lc)

        proj_x, dt_x = in_proj(x, norm1_g[i], mx[:, 0], mx[:, 1], w_main, w_dt, min(1024, l), 1024)
        y_x, _ = ssd_mixer(proj_x, SSD_INNER, dt_x, *ssd_p, s_ctx, SSD_Q)
        x = mixer_out(y_x, proj_x, x, mx[:, 2], *out_p, min(512, l), GRID_W)

        j = i // 2
        if i % 2 == 0:
            w1, w3, w2 = ffn_w1[j].astype(BF16), ffn_w3[j].astype(BF16), ffn_w2[j].astype(BF16)
            x = ffn_dense(x, norm2_g[i], mx[:, 3], mx[:, 4], mx[:, 5], w1, w3, w2, min(512, l))
            if not last:
                ctx = ffn_dense(ctx, norm2_g[i], mc[:, 3], mc[:, 4], mc[:, 5], w1, w3, w2, lc)
        else:
            assert last, "the routed channel mixer is fused with the final norm"
            w1, w3, w2 = moe_w1[j].astype(BF16), moe_w3[j].astype(BF16), moe_w2[j].astype(BF16)
            x = moe_block(x, norm2_g[i], mx[:, 3], mx[:, 4], mx[:, 5], router_w[j], w1, w3, w2, final_g)
    return x
```

```python
import functools

import numpy as np
import jax
import jax.numpy as jnp
from jax import lax
from jax.experimental import pallas as pl
from jax.experimental.pallas import tpu as pltpu
from jax.experimental.pallas import tpu_sc as plsc

F32 = jnp.float32
BF16 = jnp.bfloat16

D_MODEL = 1024
GRID_W = 64
SSD_INNER = 2048
SSD_HEADS = 32
SSD_GROUPS = 4
SSD_HPG = 8
SSD_HEAD_DIM = 64
SSD_STATE = 128
GROUP_W = SSD_HPG * SSD_HEAD_DIM
XBC_WIDTH = SSD_INNER + 2 * SSD_GROUPS * SSD_STATE
SC_WIDTH = 1024
N_MOD = 6
N_EXPERTS = 8
EPS = 1e-6

LANES = 128
SSD_Q = 128
MOE_TM = 512
VMEM_LIMIT = 56 * 1024 * 1024


def _dot(a, b):
    return jnp.dot(a, b, preferred_element_type=F32)


def _sigmoid(v):
    return 1.0 / (1.0 + jnp.exp(-v))


def _silu(v):
    return v * _sigmoid(v)


def _split2(a):
    hi = a.astype(BF16)
    lo = (a - hi.astype(F32)).astype(BF16)
    return hi, lo


def _split3(a):
    hi = a.astype(BF16)
    r = a - hi.astype(F32)
    mid = r.astype(BF16)
    lo = (r - mid.astype(F32)).astype(BF16)
    return hi, mid, lo


def _norm_mod(x, g, shift, scale):
    ms = jnp.mean(x * x, axis=-1, keepdims=True)
    return (x * lax.rsqrt(ms + EPS) * g) * (1.0 + scale) + shift


def _cparams(sem, vmem=VMEM_LIMIT):
    return pltpu.CompilerParams(dimension_semantics=sem, vmem_limit_bytes=vmem)


def _mod_kernel(c_ref, w_ref, b_ref, o_ref):
    a_hi, a_lo = _split2(_silu(c_ref[...]))
    w_hi, w_lo = _split2(w_ref[0])
    o_ref[0] = _dot(a_hi, w_hi) + _dot(a_lo, w_hi) + _dot(a_hi, w_lo) + b_ref[0]


def modulation(cc, w_mod, b_mod):
    depth, d, n = w_mod.shape
    tn = 1536
    return pl.pallas_call(
        _mod_kernel,
        grid=(depth, n // tn),
        in_specs=[pl.BlockSpec((16, d), lambda i, j: (0, 0)),
                  pl.BlockSpec((1, d, tn), lambda i, j: (i, 0, j)),
                  pl.BlockSpec((1, 1, tn), lambda i, j: (i, 0, j))],
        out_specs=pl.BlockSpec((1, 16, tn), lambda i, j: (i, 0, j)),
        out_shape=jax.ShapeDtypeStruct((depth, 16, n), F32),
        compiler_params=_cparams(("parallel", "parallel")),
        name="modulation",
    )(cc, w_mod, b_mod.reshape(depth, 1, n))


def _inproj_kernel(x_ref, g_ref, sh_ref, sc_ref, w_ref, wdt_ref, o_ref, dt_ref, h_ref):
    @pl.when(pl.program_id(2) == 0)
    def _():
        hb = _norm_mod(x_ref[0], g_ref[...], sh_ref[0], sc_ref[0]).astype(BF16)
        h_ref[...] = hb
        dt_ref[0] = _dot(hb, wdt_ref[...])

    o_ref[0] = _dot(h_ref[...], w_ref[...]).astype(o_ref.dtype)


def in_proj(x, g, shift, scale, w, wdt, tm, tn):
    b, l, d = x.shape
    n = w.shape[1]
    return pl.pallas_call(
        _inproj_kernel,
        grid=(b, l // tm, n // tn),
        in_specs=[pl.BlockSpec((1, tm, d), lambda i, m, j: (i, m, 0)),
                  pl.BlockSpec((1, d), lambda i, m, j: (0, 0)),
                  pl.BlockSpec((1, 1, d), lambda i, m, j: (i, 0, 0)),
                  pl.BlockSpec((1, 1, d), lambda i, m, j: (i, 0, 0)),
                  pl.BlockSpec((d, tn), lambda i, m, j: (0, j)),
                  pl.BlockSpec((d, LANES), lambda i, m, j: (0, 0))],
        out_specs=[pl.BlockSpec((1, tm, tn), lambda i, m, j: (i, m, j)),
                   pl.BlockSpec((1, tm, LANES), lambda i, m, j: (i, m, 0))],
        out_shape=[jax.ShapeDtypeStruct((b, l, n), BF16),
                   jax.ShapeDtypeStruct((b, l, LANES), F32)],
        scratch_shapes=[pltpu.VMEM((tm, d), BF16)],
        compiler_params=_cparams(("parallel", "parallel", "arbitrary")),
        name="in_proj",
    )(x, g.reshape(1, d), shift, scale, w, wdt)


def _dt_kernel(raw_ref, bias_ref, a_ref, pc_ref, ph_ref, plo_ref, col_ref, row_ref, sp_ref, *, q, ch):
    row = lax.broadcasted_iota(jnp.int32, (q, q), 0)
    col = lax.broadcasted_iota(jnp.int32, (q, q), 1)
    tri_l = jnp.where(col <= row, 1.0, 0.0).astype(BF16)
    tri_u = jnp.where(col >= row, 1.0, 0.0).astype(BF16)
    lane = lax.broadcasted_iota(jnp.int32, (q, LANES), 1)
    fwd = lane < SSD_HEADS
    ph, plo = ph_ref[...], plo_ref[...]

    def place_split(t):
        t_hi, t_lo = _split2(t)
        return (_dot(t_hi, ph) + _dot(t_lo, plo)).astype(BF16)

    for k in range(ch):
        v = raw_ref[0, k * q:(k + 1) * q, :] + bias_ref[...]
        dt = jnp.maximum(v, 0.0) + jnp.log1p(jnp.exp(-jnp.abs(v)))
        d1, d2, d3 = _split3(dt * a_ref[...])
        cs_f = _dot(tri_l, d1) + _dot(tri_l, d2) + _dot(tri_l, d3)
        cs_b = _dot(tri_u, d1) + _dot(tri_u, d2) + _dot(tri_u, d3)
        cs = jnp.where(fwd, cs_f, cs_b)
        tot = jnp.where(fwd[0:1], cs_f[q - 1:q, :], cs_b[0:1, :])
        sl = slice(k * q, (k + 1) * q)
        c1, c2, c3 = _split3(cs)
        r_t = (cs - jnp.log(dt)).T
        dt_t = dt.T
        for g in range(SSD_GROUPS):
            pc = pc_ref[g]
            col_ref[0, g, sl, :] = _dot(c1, pc) + _dot(c2, pc) + _dot(c3, pc)
            lo = g * SSD_HPG
            row_ref[0, g, k, 0:8, :] = r_t[lo:lo + 8, :]
            row_ref[0, g, k, 8:16, :] = r_t[SSD_HEADS + lo:SSD_HEADS + lo + 8, :]
            row_ref[0, g, k, 16:24, :] = dt_t[SSD_HEADS + lo:SSD_HEADS + lo + 8, :]
        sp_ref[0, sl, 0:LANES] = place_split(dt * jnp.exp(tot - cs))
        sp_ref[0, sl, LANES:2 * LANES] = place_split(jnp.exp(cs))


def _placements():
    pc = np.zeros((SSD_GROUPS, LANES, LANES), np.float32)
    ph = np.zeros((LANES, LANES), np.float32)
    plo = np.zeros((LANES, LANES), np.float32)
    for d in range(2):
        for g in range(SSD_GROUPS):
            for j in range(SSD_HPG):
                src = d * SSD_HEADS + g * SSD_HPG + j
                pc[g, src, d * 8 + j] = 1.0
                ph[src, (d * SSD_GROUPS + g) * 16 + j] = 1.0
                plo[src, (d * SSD_GROUPS + g) * 16 + 8 + j] = 1.0
    return jnp.asarray(pc, BF16), jnp.asarray(ph, BF16), jnp.asarray(plo, BF16)


def dt_prep(raw, bias, a, q):
    b, l, _ = raw.shape
    nc = l // q
    ch = min(8, nc)
    gn = SSD_GROUPS
    vspec = pl.BlockSpec((1, LANES), lambda i, c: (0, 0))
    pspec = pl.BlockSpec((LANES, LANES), lambda i, c: (0, 0))
    return pl.pallas_call(
        functools.partial(_dt_kernel, q=q, ch=ch),
        grid=(b, nc // ch),
        in_specs=[pl.BlockSpec((1, ch * q, LANES), lambda i, c: (i, c, 0)), vspec, vspec,
                  pl.BlockSpec((gn, LANES, LANES), lambda i, c: (0, 0, 0)), pspec, pspec],
        out_specs=[pl.BlockSpec((1, gn, ch * q, LANES), lambda i, c: (i, 0, c, 0)),
                   pl.BlockSpec((1, gn, ch, 24, q), lambda i, c: (i, 0, c, 0, 0)),
                   pl.BlockSpec((1, ch * q, 2 * LANES), lambda i, c: (i, c, 0))],
        out_shape=[jax.ShapeDtypeStruct((b, gn, l, LANES), F32),
                   jax.ShapeDtypeStruct((b, gn, nc, 24, q), F32),
                   jax.ShapeDtypeStruct((b, l, 2 * LANES), BF16)],
        compiler_params=_cparams(("parallel", "parallel")),
        name="dt_prep",
    )(raw, bias, a, *_placements())


def _ssd_kernel(xp_ref, bp_ref, cp_ref, cwx_ref, cwb_ref, cwc_ref, cbx_ref, cbb_ref, cbc_ref,
                col_ref, row_ref, sp_ref, e2f_ref, e2b_ref, dexp_ref, s0_ref,
                y_ref, sfin_ref,
                xs_ref, cc_ref, bt_ref, xwf_ref, sbe_ref, edge_ref, sf_ref, sb_ref, yo_ref, *, l, q):
    nc = l // q
    sr = lax.broadcasted_iota(jnp.int32, (q, q + 32), 0)
    sc = lax.broadcasted_iota(jnp.int32, (q, q + 32), 1)
    shift_prev = jnp.where((sc == sr - 1) | ((sr == 0) & (sc == q + 15)), 1.0, 0.0).astype(BF16)
    shift_next = jnp.where(((sc == sr + 1) & (sc < q)) | ((sr == q - 1) & (sc == q + 16)), 1.0, 0.0).astype(BF16)

    def conv_silu(refs, w, b, k):
        r0 = pl.multiple_of(k * q, q)
        p0 = pl.multiple_of(jnp.maximum(r0 - 16, 0), 16)
        n0 = pl.multiple_of(jnp.minimum(r0 + q, l - 16), 16)

        def rows_at(start, n):
            parts = [r[0, pl.ds(start, n), :] for r in refs]
            return parts[0] if len(parts) == 1 else jnp.concatenate(parts, axis=1)

        blk = rows_at(r0, q)
        before = rows_at(p0, 16)
        after = rows_at(n0, 16)
        before = jnp.where(k > 0, before, jnp.zeros_like(before))
        after = jnp.where(k < nc - 1, after, jnp.zeros_like(after))
        stacked = jnp.concatenate([blk, before, after], axis=0)
        v = (w[0:1] * _dot(shift_prev, stacked) + w[1:2] * blk.astype(F32)
             + w[2:3] * _dot(shift_next, stacked) + b)
        hv = 0.5 * v
        return hv + hv * jnp.tanh(hv)

    def prep(k):
        r0 = pl.multiple_of(k * q, q)
        x = conv_silu((xp_ref,), cwx_ref[...], cbx_ref[...], k)
        xs_ref[pl.ds(r0, q), :] = x.astype(BF16)
        bc = conv_silu((bp_ref, cp_ref),
                       jnp.concatenate([cwb_ref[...], cwc_ref[...]], axis=1),
                       jnp.concatenate([cbb_ref[...], cbc_ref[...]], axis=1), k)
        b_t = bc[:, :SSD_STATE].T.astype(BF16)
        bt_ref[k] = b_t
        cc_ref[pl.ds(r0, q), :] = bc[:, SSD_STATE:].astype(BF16)
        sp_w = sp_ref[0, pl.ds(r0, q), 0:LANES]
        xwf_ref[pl.ds(r0, q), :] = (x * _dot(sp_w, e2f_ref[0, 0])).astype(BF16)
        xw_b = (x * _dot(sp_w, e2b_ref[0, 0])).astype(BF16)
        e_last = sp_ref[0, pl.ds(pl.multiple_of(r0 + q - 16, 16), 16), LANES:2 * LANES]
        e_first = sp_ref[0, pl.ds(r0, 16), LANES:2 * LANES]
        edge_ref[k] = _dot(e_last, e2f_ref[0, 0])[15:16, :]
        edge_b = _dot(e_first, e2b_ref[0, 0])[0:1, :]
        return b_t, xw_b, edge_b

    sb_ref[...] = s0_ref[0, 0, 1]

    def bstep(i, carry):
        c = nc - 1 - i
        b_t, xw_b, edge_b = prep(c)
        sbe_ref[c] = sb_ref[...].astype(BF16)
        sb_ref[...] = sb_ref[...] * edge_b + _dot(b_t, xw_b)
        return carry

    lax.fori_loop(0, nc, bstep, 0, unroll=2)
    sfin_ref[0, 0, 1] = sb_ref[...]

    sf_ref[...] = s0_ref[0, 0, 0]
    li = lax.broadcasted_iota(jnp.int32, (q, q), 0)
    si = lax.broadcasted_iota(jnp.int32, (q, q), 1)
    lower = si <= li
    eye = si == li
    left = lax.broadcasted_iota(jnp.int32, (q, LANES), 1) < SSD_HEAD_DIM
    fu = yo_ref.shape[0]

    def chunk(c, slot):
        r0 = pl.multiple_of(c * q, q)
        cc = cc_ref[pl.ds(r0, q), :]
        sp_e = sp_ref[0, pl.ds(r0, q), LANES:2 * LANES]
        yo_ref[slot] = (_dot(sp_e, e2f_ref[0, 0]) * _dot(cc, sf_ref[...].astype(BF16))
                        + _dot(sp_e, e2b_ref[0, 0]) * _dot(cc, sbe_ref[c]))
        sf_ref[...] = sf_ref[...] * edge_ref[c] + _dot(bt_ref[c], xwf_ref[pl.ds(r0, q), :])
        g = _dot(cc, bt_ref[c])
        col = col_ref[0, 0, pl.ds(r0, q), :]
        row = row_ref[0, 0, c]
        for j in range(SSD_HPG // 2):
            lanes = slice(j * LANES, (j + 1) * LANES)
            xf = xs_ref[pl.ds(r0, q), lanes].astype(F32)
            x_diag = jnp.concatenate([jnp.where(left, xf, 0.0), jnp.where(left, 0.0, xf)], axis=0).astype(BF16)
            ms = []
            for h in (2 * j, 2 * j + 1):
                cs_l = jnp.take_along_axis(col, jnp.where(lower, h, 8 + h), axis=1)
                arg = cs_l - jnp.where(lower, row[h:h + 1, :], row[8 + h:9 + h, :])
                p = jnp.exp(arg) + jnp.where(eye, row[16 + h:17 + h, :], 0.0)
                ms.append((g * p).astype(BF16))
            y = _dot(jnp.concatenate(ms, axis=1), x_diag) + yo_ref[slot, :, lanes] + dexp_ref[0, :, lanes] * xf
            y_ref[0, pl.ds(r0, q), lanes] = y.astype(y_ref.dtype)

    def fstep(i, carry):
        for slot in range(fu):
            chunk(fu * i + slot, slot)
        return carry

    lax.fori_loop(0, nc // fu, fstep, 0)
    sfin_ref[0, 0, 0] = sf_ref[...]


def _expanders():
    e = np.zeros((2, SSD_GROUPS, LANES, GROUP_W), np.float32)
    for d in range(2):
        for g in range(SSD_GROUPS):
            for s in range(2):
                for j in range(SSD_HPG):
                    e[d, g, (d * SSD_GROUPS + g) * 16 + s * 8 + j, j * SSD_HEAD_DIM:(j + 1) * SSD_HEAD_DIM] = 1.0
    return jnp.asarray(e, BF16)


def ssd_mixer(proj, xbc_col0, dt_raw, conv_w, conv_b, dt_bias, a_log, d_skip, s0, q):
    b, l, _ = proj.shape
    nc = l // q
    gn = SSD_GROUPS
    bias = jnp.zeros((1, LANES), F32).at[0, :2 * SSD_HEADS].set(dt_bias.reshape(-1))
    a = jnp.zeros((1, LANES), F32).at[0, :2 * SSD_HEADS].set(-jnp.exp(a_log.reshape(-1)))
    colg, rowg, sp = dt_prep(dt_raw, bias, a, q)
    dexp = jnp.repeat(d_skip.astype(F32), SSD_HEAD_DIM).reshape(gn, 1, GROUP_W)
    e2 = _expanders()

    cw = conv_w.astype(F32)
    cb = conv_b.astype(F32).reshape(1, -1)
    nb = SSD_INNER
    cwx, cwb, cwc = cw[:, :nb], cw[:, nb:nb + gn * SSD_STATE], cw[:, nb + gn * SSD_STATE:]
    cbx, cbb, cbc = cb[:, :nb], cb[:, nb:nb + gn * SSD_STATE], cb[:, nb + gn * SSD_STATE:]

    xo = xbc_col0 // GROUP_W
    bo = (xbc_col0 + SSD_INNER) // SSD_STATE
    co = bo + gn
    st_spec = pl.BlockSpec((1, 1, 2, SSD_STATE, GROUP_W), lambda i, g: (i, g, 0, 0, 0))
    y, sfin = pl.pallas_call(
        functools.partial(_ssd_kernel, l=l, q=q),
        grid=(b, gn),
        in_specs=[pl.BlockSpec((1, l, GROUP_W), lambda i, g: (i, 0, xo + g)),
                  pl.BlockSpec((1, l, SSD_STATE), lambda i, g: (i, 0, bo + g)),
                  pl.BlockSpec((1, l, SSD_STATE), lambda i, g: (i, 0, co + g)),
                  pl.BlockSpec((3, GROUP_W), lambda i, g: (0, g)),
                  pl.BlockSpec((3, SSD_STATE), lambda i, g: (0, g)),
                  pl.BlockSpec((3, SSD_STATE), lambda i, g: (0, g)),
                  pl.BlockSpec((1, GROUP_W), lambda i, g: (0, g)),
                  pl.BlockSpec((1, SSD_STATE), lambda i, g: (0, g)),
                  pl.BlockSpec((1, SSD_STATE), lambda i, g: (0, g)),
                  pl.BlockSpec((1, 1, l, LANES), lambda i, g: (i, g, 0, 0)),
                  pl.BlockSpec((1, 1, nc, 24, q), lambda i, g: (i, g, 0, 0, 0)),
                  pl.BlockSpec((1, l, 2 * LANES), lambda i, g: (i, 0, 0)),
                  pl.BlockSpec((1, 1, LANES, GROUP_W), lambda i, g: (0, g, 0, 0)),
                  pl.BlockSpec((1, 1, LANES, GROUP_W), lambda i, g: (1, g, 0, 0)),
                  pl.BlockSpec((1, 1, GROUP_W), lambda i, g: (g, 0, 0)),
                  st_spec],
        out_specs=[pl.BlockSpec((1, l, GROUP_W), lambda i, g: (i, 0, g)), st_spec],
        out_shape=[jax.ShapeDtypeStruct((b, l, SSD_INNER), BF16),
                   jax.ShapeDtypeStruct((b, gn, 2, SSD_STATE, GROUP_W), F32)],
        scratch_shapes=[pltpu.VMEM((l, GROUP_W), BF16),
                        pltpu.VMEM((l, SSD_STATE), BF16),
                        pltpu.VMEM((nc, SSD_STATE, q), BF16),
                        pltpu.VMEM((l, GROUP_W), BF16),
                        pltpu.VMEM((nc, SSD_STATE, GROUP_W), BF16),
                        pltpu.VMEM((nc, 1, GROUP_W), F32),
                        pltpu.VMEM((SSD_STATE, GROUP_W), F32),
                        pltpu.VMEM((SSD_STATE, GROUP_W), F32),
                        pltpu.VMEM((2 if nc % 2 == 0 else 1, q, GROUP_W), F32)],
        compiler_params=_cparams(("parallel", "parallel")),
        name="ssd",
    )(proj, proj, proj, cwx, cwb, cwc, cbx, cbb, cbc, colg, rowg, sp, e2, e2, dexp, s0)
    return y, sfin


def _mixout_kernel(y_ref, z_ref, gb_ref, gc_ref, hv_ref, g0_ref, g1_ref, x_ref, gate_ref,
                   ng_ref, bg_ref, scw_ref, wssd_ref, wsc_ref, wo_ref, o_ref, *, tm, period):
    yz = y_ref[0].astype(F32) * _silu(z_ref[0].astype(F32))
    ms = jnp.mean(yz * yz, axis=-1, keepdims=True)
    yn = (yz * lax.rsqrt(ms + EPS) * ng_ref[...]).astype(BF16)
    y_ssd = _dot(yn, wssd_ref[...])

    u = gc_ref[0].astype(F32) * hv_ref[0].astype(F32)
    pos = lax.broadcasted_iota(jnp.int32, (tm, 1), 0) % period
    u_prev = jnp.where(pos == 0, 0.0, pltpu.roll(u, 1, 0))
    u_next = jnp.where(pos == period - 1, 0.0, pltpu.roll(u, tm - 1, 0))
    w = scw_ref[...]
    v = w[0:1] * u_prev + w[1:2] * u + w[2:3] * u_next
    y_sc = _dot((gb_ref[0].astype(F32) * v).astype(BF16), wsc_ref[...])

    bg = bg_ref[...]
    g0 = _sigmoid(g0_ref[0].astype(F32) + bg[:, :D_MODEL])
    g1 = _sigmoid(g1_ref[0].astype(F32) + bg[:, D_MODEL:])
    out = _dot((g0 * y_ssd + g1 * y_sc).astype(BF16), wo_ref[...])
    o_ref[0] = x_ref[0] + gate_ref[0] * out


def mixer_out(y, proj, x, gate, norm_g, b_gate, sc_conv_w, w_ssd, w_sc, w_o, tm, period):
    b, l, d = x.shape
    pc = lambda k: pl.BlockSpec((1, tm, d), lambda i, m, k=k: (i, m, k))
    full = lambda shp: pl.BlockSpec(shp, lambda i, m: (0,) * len(shp))
    return pl.pallas_call(
        functools.partial(_mixout_kernel, tm=tm, period=period),
        grid=(b, l // tm),
        in_specs=[pl.BlockSpec((1, tm, SSD_INNER), lambda i, m: (i, m, 0)),
                  pl.BlockSpec((1, tm, SSD_INNER), lambda i, m: (i, m, 0)),
                  pc(5), pc(6), pc(7), pc(8), pc(9),
                  pl.BlockSpec((1, tm, d), lambda i, m: (i, m, 0)),
                  pl.BlockSpec((1, 1, d), lambda i, m: (i, 0, 0)),
                  full((1, SSD_INNER)), full((1, 2 * d)), full((3, SC_WIDTH)),
                  full((SSD_INNER, d)), full((SC_WIDTH, d)), full((d, d))],
        out_specs=pl.BlockSpec((1, tm, d), lambda i, m: (i, m, 0)),
        out_shape=jax.ShapeDtypeStruct((b, l, d), F32),
        compiler_params=_cparams(("parallel", "parallel")),
        name="mixer_out",
    )(y, proj, proj, proj, proj, proj, proj, x, gate,
      norm_g.reshape(1, -1), b_gate.reshape(1, -1), sc_conv_w, w_ssd, w_sc, w_o)


def _ffn_kernel(x_ref, g_ref, sh_ref, sc_ref, gate_ref, w1_ref, w3_ref, w2_ref, o_ref, *, nf):
    x = x_ref[0]
    hb = _norm_mod(x, g_ref[...], sh_ref[0], sc_ref[0]).astype(BF16)
    fw = w1_ref.shape[1] // nf
    acc = None
    for k in range(nf):
        a = _dot(hb, w1_ref[:, k * fw:(k + 1) * fw])
        bb = _dot(hb, w3_ref[:, k * fw:(k + 1) * fw])
        part = _dot((_silu(a) * bb).astype(BF16), w2_ref[k * fw:(k + 1) * fw, :])
        acc = part if acc is None else acc + part
    o_ref[0] = x + gate_ref[0] * acc


def ffn_dense(x, g, shift, scale, gate, w1, w3, w2, tm):
    b, l, d = x.shape
    f = w1.shape[1]
    vec = pl.BlockSpec((1, 1, d), lambda i, m: (i, 0, 0))
    const = lambda shp: pl.BlockSpec(shp, lambda i, m: (0, 0), pipeline_mode=pl.Buffered(1))
    return pl.pallas_call(
        functools.partial(_ffn_kernel, nf=2),
        grid=(b, l // tm),
        in_specs=[pl.BlockSpec((1, tm, d), lambda i, m: (i, m, 0)),
                  pl.BlockSpec((1, d), lambda i, m: (0, 0)),
                  vec, vec, vec, const((d, f)), const((d, f)), const((f, d))],
        out_specs=pl.BlockSpec((1, tm, d), lambda i, m: (i, m, 0)),
        out_shape=jax.ShapeDtypeStruct((b, l, d), F32),
        compiler_params=_cparams(("parallel", "parallel")),
        name="ffn_dense",
    )(x, g.reshape(1, d), shift, scale, gate, w1, w3, w2)


def _router_kernel(x_ref, g_ref, sh_ref, sc_ref, rw_ref, h_ref, route_ref, cnt_ref, run_ref, *, tm):
    @pl.when((pl.program_id(0) == 0) & (pl.program_id(1) == 0))
    def _():
        run_ref[...] = jnp.zeros_like(run_ref)

    h = _norm_mod(x_ref[0], g_ref[...], sh_ref[0], sc_ref[0])
    h_hi, h_lo = _split2(h)
    h_ref[0] = h
    w_hi, w_lo = _split2(rw_ref[...])
    logits = _dot(h_hi, w_hi) + _dot(h_lo, w_hi) + _dot(h_hi, w_lo)
    lane = lax.broadcasted_iota(jnp.int32, (tm, LANES), 1)
    ninf = float("-inf")
    lg = jnp.where(lane < N_EXPERTS, logits, ninf)
    m1 = jnp.max(lg, axis=1, keepdims=True)
    i1 = jnp.min(jnp.where(lg == m1, lane, LANES), axis=1, keepdims=True)
    lg2 = jnp.where(lane == i1, ninf, lg)
    m2 = jnp.max(lg2, axis=1, keepdims=True)
    i2 = jnp.min(jnp.where(lg2 == m2, lane, LANES), axis=1, keepdims=True)
    e2 = jnp.exp(m2 - m1)
    den = 1.0 + e2
    sel1 = jnp.where(lane == i1, 1.0, 0.0)
    sel2 = jnp.where(lane == i2, 1.0, 0.0)
    cnt = sel1 + sel2
    r = lax.broadcasted_iota(jnp.int32, (tm, tm), 0)
    c = lax.broadcasted_iota(jnp.int32, (tm, tm), 1)
    tri = jnp.where(c < r, 1.0, 0.0).astype(BF16)
    base = _dot(tri, cnt.astype(BF16)) + run_ref[0:1, :]
    r1 = jnp.sum(sel1 * base, axis=1, keepdims=True)
    r2 = jnp.sum(sel2 * base, axis=1, keepdims=True)
    vals = (i1.astype(F32), i2.astype(F32), 1.0 / den, e2 / den, r1, r2)
    out = jnp.zeros((tm, LANES), F32)
    for k, v in enumerate(vals):
        out = jnp.where(lane == k, v, out)
    route_ref[0] = out
    new_run = run_ref[...] + jnp.sum(cnt, axis=0, keepdims=True)
    run_ref[...] = new_run
    cnt_ref[...] = new_run


def router(x, g, shift, scale, router_w, tm):
    b, l, d = x.shape
    rw = jnp.zeros((d, LANES), F32).at[:, :N_EXPERTS].set(router_w)
    vec = pl.BlockSpec((1, 1, d), lambda i, m: (i, 0, 0))
    return pl.pallas_call(
        functools.partial(_router_kernel, tm=tm),
        grid=(b, l // tm),
        in_specs=[pl.BlockSpec((1, tm, d), lambda i, m: (i, m, 0)),
                  pl.BlockSpec((1, d), lambda i, m: (0, 0)),
                  vec, vec, pl.BlockSpec((d, LANES), lambda i, m: (0, 0))],
        out_specs=[pl.BlockSpec((1, tm, d), lambda i, m: (i, m, 0)),
                   pl.BlockSpec((1, tm, LANES), lambda i, m: (i, m, 0)),
                   pl.BlockSpec((8, LANES), lambda i, m: (0, 0))],
        out_shape=[jax.ShapeDtypeStruct((b, l, d), F32),
                   jax.ShapeDtypeStruct((b, l, LANES), F32),
                   jax.ShapeDtypeStruct((8, LANES), F32)],
        scratch_shapes=[pltpu.VMEM((8, LANES), F32)],
        compiler_params=_cparams(("arbitrary", "arbitrary")),
        name="router",
    )(x, g.reshape(1, d), shift, scale, rw)


def _gffn_kernel(te_ref, nv_ref, x_ref, w1_ref, w3_ref, w2_ref, o_ref):
    i = pl.program_id(0)
    f = pl.program_id(1)

    @pl.when(i < nv_ref[0])
    def _():
        x = x_ref[...].astype(BF16)
        a = _dot(x, w1_ref[0])
        bb = _dot(x, w3_ref[0])
        part = _dot((_silu(a) * bb).astype(BF16), w2_ref[0])

        @pl.when(f == 0)
        def _():
            o_ref[...] = part

        @pl.when(f > 0)
        def _():
            o_ref[...] += part

    @pl.when((i >= nv_ref[0]) & (f == 0))
    def _():
        o_ref[...] = jnp.zeros_like(o_ref)


def grouped_ffn(xs, tile_expert, n_valid, w1, w3, w2, tm, nf):
    rows, d = xs.shape
    nt = rows // tm
    f = w1.shape[2]
    fw = f // nf
    grid_spec = pltpu.PrefetchScalarGridSpec(
        num_scalar_prefetch=2,
        grid=(nt, nf),
        in_specs=[pl.BlockSpec((tm, d), lambda i, k, te, nv: (i, 0)),
                  pl.BlockSpec((1, d, fw), lambda i, k, te, nv: (te[i], 0, k)),
                  pl.BlockSpec((1, d, fw), lambda i, k, te, nv: (te[i], 0, k)),
                  pl.BlockSpec((1, fw, d), lambda i, k, te, nv: (te[i], k, 0))],
        out_specs=pl.BlockSpec((tm, d), lambda i, k, te, nv: (i, 0)),
    )
    return pl.pallas_call(
        _gffn_kernel,
        grid_spec=grid_spec,
        out_shape=jax.ShapeDtypeStruct((rows, d), F32),
        compiler_params=_cparams(("parallel", "arbitrary")),
        name="grouped_ffn",
    )(tile_expert, n_valid, xs, w1, w3, w2)


def _combine_kernel(x_ref, y0_ref, y1_ref, route_ref, gate_ref, fg_ref, o_ref):
    r = route_ref[0]
    moe = r[:, 2:3] * y0_ref[0, 0] + r[:, 3:4] * y1_ref[0, 0]
    xn = x_ref[0] + gate_ref[0] * moe
    ms = jnp.mean(xn * xn, axis=-1, keepdims=True)
    o_ref[0] = xn * lax.rsqrt(ms + EPS) * fg_ref[...]


def combine_final(x, yg, route, gate, final_g, tm):
    b, l, d = x.shape
    return pl.pallas_call(
        _combine_kernel,
        grid=(b, l // tm),
        in_specs=[pl.BlockSpec((1, tm, d), lambda i, m: (i, m, 0)),
                  pl.BlockSpec((1, 1, tm, d), lambda i, m: (0, i, m, 0)),
                  pl.BlockSpec((1, 1, tm, d), lambda i, m: (1, i, m, 0)),
                  pl.BlockSpec((1, tm, LANES), lambda i, m: (i, m, 0)),
                  pl.BlockSpec((1, 1, d), lambda i, m: (i, 0, 0)),
                  pl.BlockSpec((1, d), lambda i, m: (0, 0))],
        out_specs=pl.BlockSpec((1, tm, d), lambda i, m: (i, m, 0)),
        out_shape=jax.ShapeDtypeStruct((b, l, d), F32),
        compiler_params=_cparams(("parallel", "parallel")),
        name="combine_final",
    )(x, yg, yg, route, gate, final_g.reshape(1, d))


SC_CORES = 2
SC_SUBCORES = 16
SC_WORKERS = SC_CORES * SC_SUBCORES
SC_ROWS = 64


def _sc_mesh():
    return plsc.VectorSubcoreMesh(core_axis_name="c", subcore_axis_name="s",
                                  num_cores=SC_CORES, num_subcores=SC_SUBCORES)


def dispatch_rows(h, pos0, pos1, n_rows):
    t, d = h.shape
    per_w = t // SC_WORKERS
    ch = min(SC_ROWS, per_w)

    @functools.partial(
        pl.kernel, mesh=_sc_mesh(),
        out_type=jax.ShapeDtypeStruct((n_rows, d), h.dtype),
        scratch_types=[pltpu.VMEM((ch,), jnp.int32), pltpu.VMEM((ch,), jnp.int32),
                       pltpu.VMEM((ch, d), h.dtype), pltpu.SemaphoreType.DMA],
        name="moe_dispatch")
    def scatter(h_hbm, p0_hbm, p1_hbm, out_hbm, i0_v, i1_v, rows_v, sem):
        base = (lax.axis_index("s") * SC_CORES + lax.axis_index("c")) * per_w

        @pl.loop(0, per_w // ch)
        def _(j):
            off = base + j * ch
            pltpu.sync_copy(h_hbm.at[pl.ds(off, ch)], rows_v)
            pltpu.sync_copy(p0_hbm.at[pl.ds(off, ch)], i0_v)
            pltpu.sync_copy(p1_hbm.at[pl.ds(off, ch)], i1_v)
            pltpu.async_copy(rows_v, out_hbm.at[i0_v], sem).wait()
            pltpu.async_copy(rows_v, out_hbm.at[i1_v], sem).wait()

    return scatter(h, pos0, pos1)


def return_rows(ys, idx):
    n = idx.shape[0]
    d = ys.shape[1]
    per_w = n // SC_WORKERS
    ch = min(SC_ROWS, per_w)

    @functools.partial(
        pl.kernel, mesh=_sc_mesh(),
        out_type=jax.ShapeDtypeStruct((n, d), ys.dtype),
        scratch_types=[pltpu.VMEM((ch,), jnp.int32), pltpu.VMEM((ch, d), ys.dtype), pltpu.SemaphoreType.DMA],
        name="moe_return")
    def gather(ys_hbm, idx_hbm, out_hbm, idx_v, rows_v, sem):
        base = (lax.axis_index("s") * SC_CORES + lax.axis_index("c")) * per_w

        @pl.loop(0, per_w // ch)
        def _(j):
            off = base + j * ch
            pltpu.sync_copy(idx_hbm.at[pl.ds(off, ch)], idx_v)
            pltpu.async_copy(ys_hbm.at[idx_v], rows_v, sem).wait()
            pltpu.sync_copy(rows_v, out_hbm.at[pl.ds(off, ch)])

    return gather(ys, idx)


def moe_block(x, g, shift, scale, gate, router_w, w1, w3, w2, final_g):
    b, l, d = x.shape
    t = b * l
    tm = MOE_TM
    h, route, counts = router(x, g, shift, scale, router_w, min(512, l))
    rt = route.reshape(t, LANES)
    cnt = counts[0, :N_EXPERTS].astype(jnp.int32)
    gs = ((cnt + tm - 1) // tm) * tm
    ends = jnp.cumsum(gs)
    offs = ends - gs
    pos0 = offs[rt[:, 0].astype(jnp.int32)] + rt[:, 4].astype(jnp.int32)
    pos1 = offs[rt[:, 1].astype(jnp.int32)] + rt[:, 5].astype(jnp.int32)
    nt = (2 * t) // tm + N_EXPERTS
    n_valid = (ends[-1] // tm).astype(jnp.int32).reshape(1)
    tile = jnp.minimum(jnp.arange(nt, dtype=jnp.int32), n_valid[0] - 1)
    tile_expert = jnp.sum((tile[:, None] >= (ends // tm)[None, :]).astype(jnp.int32), axis=1)
    xs = dispatch_rows(h.reshape(t, d), pos0, pos1, nt * tm)
    ys = grouped_ffn(xs, tile_expert, n_valid, w1, w3, w2, tm, 2)
    yg = return_rows(ys, jnp.concatenate([pos0, pos1])).reshape(2, b, l, d)
    return combine_final(x, yg, route, gate, final_g, min(512, l))


def _in_weights(w_in):
    o1 = SSD_INNER
    o2 = o1 + XBC_WIDTH
    o3 = o2 + 2 * SSD_HEADS
    w_main = jnp.concatenate([w_in[:, :o2], w_in[:, o3:]], axis=1).astype(BF16)
    w_dt = jnp.pad(w_in[:, o2:o3], ((0, 0), (0, LANES - 2 * SSD_HEADS))).astype(BF16)
    return w_main, w_dt


def kernel(x, c, ctx, c_ctx, w_mod, b_mod, norm1_g, norm2_g, w_in, b_gate, ssd_conv_w, ssd_conv_b, ssd_dt_bias, ssd_a_log, ssd_d, ssd_norm_g, w_ssd_out, sc_conv_w, w_sc_out, w_o, ffn_w1, ffn_w3, ffn_w2, router_w, moe_w1, moe_w3, moe_w2, final_g):
    b, l, d = x.shape
    lc = ctx.shape[1]
    depth = w_mod.shape[0]
    cc = jnp.zeros((16, d), F32).at[:b].set(c).at[b].set(c_ctx)
    mod = modulation(cc, w_mod, b_mod)
    zeros_state = jnp.zeros((b, SSD_GROUPS, 2, SSD_STATE, GROUP_W), F32)

    for i in range(depth):
        last = i == depth - 1
        mx = mod[i, :b].reshape(b, N_MOD, 1, d)
        mc = jnp.broadcast_to(mod[i, b].reshape(1, N_MOD, 1, d), (b, N_MOD, 1, d))
        w_main, w_dt = _in_weights(w_in[i])
        ssd_p = (ssd_conv_w[i], ssd_conv_b[i], ssd_dt_bias[i], ssd_a_log[i], ssd_d[i])
        out_p = (ssd_norm_g[i], b_gate[i], sc_conv_w[i], w_ssd_out[i].astype(BF16),
                 w_sc_out[i].astype(BF16), w_o[i].astype(BF16))

        if last:
            w_xbc = w_main[:, SSD_INNER:SSD_INNER + XBC_WIDTH]
            proj_c, dt_c = in_proj(ctx, norm1_g[i], mc[:, 0], mc[:, 1], w_xbc, w_dt, lc, 1024)
            _, s_ctx = ssd_mixer(proj_c, 0, dt_c, *ssd_p, zeros_state, SSD_Q)
        else:
            proj_c, dt_c = in_proj(ctx, norm1_g[i], mc[:, 0], mc[:, 1], w_main, w_dt, lc, 1024)
            y_c, s_ctx = ssd_mixer(proj_c, SSD_INNER, dt_c, *ssd_p, zeros_state, SSD_Q)
            ctx = mixer_out(y_c, proj_c, ctx, mc[:, 2], *out_p, lc, lc)

        proj_x, dt_x = in_proj(x, norm1_g[i], mx[:, 0], mx[:, 1], w_main, w_dt, min(1024, l), 2048)
        y_x, _ = ssd_mixer(proj_x, SSD_INNER, dt_x, *ssd_p, s_ctx, SSD_Q)
        x = mixer_out(y_x, proj_x, x, mx[:, 2], *out_p, min(512, l), GRID_W)

        j = i // 2
        if i % 2 == 0:
            w1, w3, w2 = ffn_w1[j].astype(BF16), ffn_w3[j].astype(BF16), ffn_w2[j].astype(BF16)
            x = ffn_dense(x, norm2_g[i], mx[:, 3], mx[:, 4], mx[:, 5], w1, w3, w2, min(512, l))
            if not last:
                ctx = ffn_dense(ctx, norm2_g[i], mc[:, 3], mc[:, 4], mc[:, 5], w1, w3, w2, lc)
        else:
            assert last, "the routed channel mixer is fused with the final norm"
            w1, w3, w2 = moe_w1[j].astype(BF16), moe_w3[j].astype(BF16), moe_w2[j].astype(BF16)
            x = moe_block(x, norm2_g[i], mx[:, 3], mx[:, 4], mx[:, 5], router_w[j], w1, w3, w2, final_g)
    return x
```

```python
import functools

import numpy as np
import jax
import jax.numpy as jnp
from jax import lax
from jax.experimental import pallas as pl
from jax.experimental.pallas import tpu as pltpu
from jax.experimental.pallas import tpu_sc as plsc

F32 = jnp.float32
BF16 = jnp.bfloat16

D_MODEL = 1024
GRID_W = 64
SSD_INNER = 2048
SSD_HEADS = 32
SSD_GROUPS = 4
SSD_HPG = 8
SSD_HEAD_DIM = 64
SSD_STATE = 128
GROUP_W = SSD_HPG * SSD_HEAD_DIM
XBC_WIDTH = SSD_INNER + 2 * SSD_GROUPS * SSD_STATE
SC_WIDTH = 1024
N_MOD = 6
N_EXPERTS = 8
EPS = 1e-6

LANES = 128
SSD_Q = 128
MOE_TM = 512
VMEM_LIMIT = 56 * 1024 * 1024


def _dot(a, b):
    return jnp.dot(a, b, preferred_element_type=F32)


def _sigmoid(v):
    return 1.0 / (1.0 + jnp.exp(-v))


def _silu(v):
    return v * _sigmoid(v)


def _sigmoid_t(v):
    return 0.5 + 0.5 * jnp.tanh(0.5 * v)


def _silu_t(v):
    hv = 0.5 * v
    return hv + hv * jnp.tanh(hv)


def _split2(a):
    hi = a.astype(BF16)
    lo = (a - hi.astype(F32)).astype(BF16)
    return hi, lo


def _split3(a):
    hi = a.astype(BF16)
    r = a - hi.astype(F32)
    mid = r.astype(BF16)
    lo = (r - mid.astype(F32)).astype(BF16)
    return hi, mid, lo


def _norm_mod(x, g, shift, scale):
    ms = jnp.mean(x * x, axis=-1, keepdims=True)
    return (x * lax.rsqrt(ms + EPS) * g) * (1.0 + scale) + shift


def _cparams(sem, vmem=VMEM_LIMIT):
    return pltpu.CompilerParams(dimension_semantics=sem, vmem_limit_bytes=vmem)


def _mod_kernel(c_ref, w_ref, b_ref, o_ref):
    a_hi, a_lo = _split2(_silu(c_ref[...]))
    w_hi, w_lo = _split2(w_ref[0])
    o_ref[0] = _dot(a_hi, w_hi) + _dot(a_lo, w_hi) + _dot(a_hi, w_lo) + b_ref[0]


def modulation(cc, w_mod, b_mod):
    depth, d, n = w_mod.shape
    tn = 1536
    return pl.pallas_call(
        _mod_kernel,
        grid=(depth, n // tn),
        in_specs=[pl.BlockSpec((16, d), lambda i, j: (0, 0)),
                  pl.BlockSpec((1, d, tn), lambda i, j: (i, 0, j)),
                  pl.BlockSpec((1, 1, tn), lambda i, j: (i, 0, j))],
        out_specs=pl.BlockSpec((1, 16, tn), lambda i, j: (i, 0, j)),
        out_shape=jax.ShapeDtypeStruct((depth, 16, n), F32),
        compiler_params=_cparams(("parallel", "parallel")),
        name="modulation",
    )(cc, w_mod, b_mod.reshape(depth, 1, n))


def _inproj_kernel(x_ref, g_ref, sh_ref, sc_ref, w_ref, wdt_ref, o_ref, dt_ref, h_ref):
    @pl.when(pl.program_id(2) == 0)
    def _():
        hb = _norm_mod(x_ref[0], g_ref[...], sh_ref[0], sc_ref[0]).astype(BF16)
        h_ref[...] = hb
        dt_ref[0] = _dot(hb, wdt_ref[...])

    o_ref[0] = _dot(h_ref[...], w_ref[...]).astype(o_ref.dtype)


def in_proj(x, g, shift, scale, w, wdt, tm, tn):
    b, l, d = x.shape
    n = w.shape[1]
    return pl.pallas_call(
        _inproj_kernel,
        grid=(b, l // tm, n // tn),
        in_specs=[pl.BlockSpec((1, tm, d), lambda i, m, j: (i, m, 0)),
                  pl.BlockSpec((1, d), lambda i, m, j: (0, 0)),
                  pl.BlockSpec((1, 1, d), lambda i, m, j: (i, 0, 0)),
                  pl.BlockSpec((1, 1, d), lambda i, m, j: (i, 0, 0)),
                  pl.BlockSpec((d, tn), lambda i, m, j: (0, j)),
                  pl.BlockSpec((d, LANES), lambda i, m, j: (0, 0))],
        out_specs=[pl.BlockSpec((1, tm, tn), lambda i, m, j: (i, m, j)),
                   pl.BlockSpec((1, tm, LANES), lambda i, m, j: (i, m, 0))],
        out_shape=[jax.ShapeDtypeStruct((b, l, n), BF16),
                   jax.ShapeDtypeStruct((b, l, LANES), F32)],
        scratch_shapes=[pltpu.VMEM((tm, d), BF16)],
        compiler_params=_cparams(("parallel", "parallel", "arbitrary")),
        name="in_proj",
    )(x, g.reshape(1, d), shift, scale, w, wdt)


def _dt_kernel(raw_ref, bias_ref, a_ref, pc_ref, ph_ref, plo_ref, col_ref, row_ref, sp_ref, *, q, ch):
    row = lax.broadcasted_iota(jnp.int32, (q, q), 0)
    col = lax.broadcasted_iota(jnp.int32, (q, q), 1)
    tri_l = jnp.where(col <= row, 1.0, 0.0).astype(BF16)
    tri_u = jnp.where(col >= row, 1.0, 0.0).astype(BF16)
    lane = lax.broadcasted_iota(jnp.int32, (q, LANES), 1)
    fwd = lane < SSD_HEADS
    ph, plo = ph_ref[...], plo_ref[...]

    def place_split(t):
        t_hi, t_lo = _split2(t)
        return (_dot(t_hi, ph) + _dot(t_lo, plo)).astype(BF16)

    for k in range(ch):
        v = raw_ref[0, k * q:(k + 1) * q, :] + bias_ref[...]
        dt = jnp.maximum(v, 0.0) + jnp.log1p(jnp.exp(-jnp.abs(v)))
        d1, d2, d3 = _split3(dt * a_ref[...])
        cs_f = _dot(tri_l, d1) + _dot(tri_l, d2) + _dot(tri_l, d3)
        cs_b = _dot(tri_u, d1) + _dot(tri_u, d2) + _dot(tri_u, d3)
        cs = jnp.where(fwd, cs_f, cs_b)
        tot = jnp.where(fwd[0:1], cs_f[q - 1:q, :], cs_b[0:1, :])
        sl = slice(k * q, (k + 1) * q)
        c1, c2, c3 = _split3(cs)
        pc = pc_ref[...]
        col_ref[0, sl, :] = _dot(c1, pc) + _dot(c2, pc) + _dot(c3, pc)
        r_t = (cs - jnp.log(dt)).T
        dt_t = dt.T
        for g in range(SSD_GROUPS):
            lo = g * SSD_HPG
            row_ref[0, g, k, 0:8, :] = r_t[lo:lo + 8, :]
            row_ref[0, g, k, 8:16, :] = r_t[SSD_HEADS + lo:SSD_HEADS + lo + 8, :]
            row_ref[0, g, k, 16:24, :] = dt_t[SSD_HEADS + lo:SSD_HEADS + lo + 8, :]
        sp_ref[0, sl, 0:LANES] = place_split(dt * jnp.exp(tot - cs))
        sp_ref[0, sl, LANES:2 * LANES] = place_split(jnp.exp(cs))


def _placements():
    pc = np.zeros((LANES, LANES), np.float32)
    ph = np.zeros((LANES, LANES), np.float32)
    plo = np.zeros((LANES, LANES), np.float32)
    for d in range(2):
        for g in range(SSD_GROUPS):
            for j in range(SSD_HPG):
                src = d * SSD_HEADS + g * SSD_HPG + j
                pc[src, g * 16 + d * 8 + j] = 1.0
                ph[src, (d * SSD_GROUPS + g) * 16 + j] = 1.0
                plo[src, (d * SSD_GROUPS + g) * 16 + 8 + j] = 1.0
    return jnp.asarray(pc, BF16), jnp.asarray(ph, BF16), jnp.asarray(plo, BF16)


def dt_prep(raw, bias, a, q):
    b, l, _ = raw.shape
    nc = l // q
    ch = min(8, nc)
    gn = SSD_GROUPS
    vspec = pl.BlockSpec((1, LANES), lambda i, c: (0, 0))
    pspec = pl.BlockSpec((LANES, LANES), lambda i, c: (0, 0))
    return pl.pallas_call(
        functools.partial(_dt_kernel, q=q, ch=ch),
        grid=(b, nc // ch),
        in_specs=[pl.BlockSpec((1, ch * q, LANES), lambda i, c: (i, c, 0)), vspec, vspec, pspec, pspec, pspec],
        out_specs=[pl.BlockSpec((1, ch * q, LANES), lambda i, c: (i, c, 0)),
                   pl.BlockSpec((1, gn, ch, 24, q), lambda i, c: (i, 0, c, 0, 0)),
                   pl.BlockSpec((1, ch * q, 2 * LANES), lambda i, c: (i, c, 0))],
        out_shape=[jax.ShapeDtypeStruct((b, l, LANES), F32),
                   jax.ShapeDtypeStruct((b, gn, nc, 24, q), F32),
                   jax.ShapeDtypeStruct((b, l, 2 * LANES), BF16)],
        compiler_params=_cparams(("parallel", "parallel")),
        name="dt_prep",
    )(raw, bias, a, *_placements())


def _ssd_kernel(xp_ref, bp_ref, cp_ref, cwx_ref, cwb_ref, cwc_ref, cbx_ref, cbb_ref, cbc_ref,
                col_ref, row_ref, sp_ref, e2f_ref, e2b_ref, dexp_ref, s0_ref,
                y_ref, sfin_ref,
                xs_ref, cc_ref, bt_ref, xwf_ref, sbe_ref, edge_ref, sf_ref, sb_ref, yo_ref, *, l, q):
    nc = l // q
    sr = lax.broadcasted_iota(jnp.int32, (q, q + 32), 0)
    sc = lax.broadcasted_iota(jnp.int32, (q, q + 32), 1)
    shift_prev = jnp.where((sc == sr - 1) | ((sr == 0) & (sc == q + 15)), 1.0, 0.0).astype(BF16)
    shift_next = jnp.where(((sc == sr + 1) & (sc < q)) | ((sr == q - 1) & (sc == q + 16)), 1.0, 0.0).astype(BF16)

    def conv_silu(refs, w, b, k):
        r0 = pl.multiple_of(k * q, q)
        p0 = pl.multiple_of(jnp.maximum(r0 - 16, 0), 16)
        n0 = pl.multiple_of(jnp.minimum(r0 + q, l - 16), 16)

        def rows_at(start, n):
            parts = [r[0, pl.ds(start, n), :] for r in refs]
            return parts[0] if len(parts) == 1 else jnp.concatenate(parts, axis=1)

        blk = rows_at(r0, q)
        before = rows_at(p0, 16)
        after = rows_at(n0, 16)
        before = jnp.where(k > 0, before, jnp.zeros_like(before))
        after = jnp.where(k < nc - 1, after, jnp.zeros_like(after))
        stacked = jnp.concatenate([blk, before, after], axis=0)
        v = (w[0:1] * _dot(shift_prev, stacked) + w[1:2] * blk.astype(F32)
             + w[2:3] * _dot(shift_next, stacked) + b)
        return _silu_t(v)

    def prep(k):
        r0 = pl.multiple_of(k * q, q)
        x = conv_silu((xp_ref,), cwx_ref[...], cbx_ref[...], k)
        xs_ref[pl.ds(r0, q), :] = x.astype(BF16)
        bc = conv_silu((bp_ref, cp_ref),
                       jnp.concatenate([cwb_ref[...], cwc_ref[...]], axis=1),
                       jnp.concatenate([cbb_ref[...], cbc_ref[...]], axis=1), k)
        b_t = bc[:, :SSD_STATE].T.astype(BF16)
        bt_ref[k] = b_t
        cc_ref[pl.ds(r0, q), :] = bc[:, SSD_STATE:].astype(BF16)
        sp_w = sp_ref[0, pl.ds(r0, q), 0:LANES]
        xwf_ref[pl.ds(r0, q), :] = (x * _dot(sp_w, e2f_ref[0, 0])).astype(BF16)
        xw_b = (x * _dot(sp_w, e2b_ref[0, 0])).astype(BF16)
        e_last = sp_ref[0, pl.ds(pl.multiple_of(r0 + q - 16, 16), 16), LANES:2 * LANES]
        e_first = sp_ref[0, pl.ds(r0, 16), LANES:2 * LANES]
        edge_ref[k] = _dot(e_last, e2f_ref[0, 0])[15:16, :]
        edge_b = _dot(e_first, e2b_ref[0, 0])[0:1, :]
        return b_t, xw_b, edge_b

    sb_ref[...] = s0_ref[0, 0, 1]

    def bstep(i, carry):
        c = nc - 1 - i
        b_t, xw_b, edge_b = prep(c)
        sbe_ref[c] = sb_ref[...].astype(BF16)
        sb_ref[...] = sb_ref[...] * edge_b + _dot(b_t, xw_b)
        return carry

    lax.fori_loop(0, nc, bstep, 0, unroll=2)
    sfin_ref[0, 0, 1] = sb_ref[...]

    sf_ref[...] = s0_ref[0, 0, 0]
    li = lax.broadcasted_iota(jnp.int32, (q, q), 0)
    si = lax.broadcasted_iota(jnp.int32, (q, q), 1)
    lower = si <= li
    eye = si == li
    left = lax.broadcasted_iota(jnp.int32, (q, LANES), 1) < SSD_HEAD_DIM
    fu = yo_ref.shape[0]
    lane0 = 16 * pl.program_id(1)

    def chunk(c, slot):
        r0 = pl.multiple_of(c * q, q)
        cc = cc_ref[pl.ds(r0, q), :]
        sp_e = sp_ref[0, pl.ds(r0, q), LANES:2 * LANES]
        yo_ref[slot] = (_dot(sp_e, e2f_ref[0, 0]) * _dot(cc, sf_ref[...].astype(BF16))
                        + _dot(sp_e, e2b_ref[0, 0]) * _dot(cc, sbe_ref[c]))
        sf_ref[...] = sf_ref[...] * edge_ref[c] + _dot(bt_ref[c], xwf_ref[pl.ds(r0, q), :])
        g = _dot(cc, bt_ref[c])
        col = pltpu.roll(col_ref[0, pl.ds(r0, q), :], LANES - lane0, 1)
        row = row_ref[0, 0, c]
        for j in range(SSD_HPG // 2):
            lanes = slice(j * LANES, (j + 1) * LANES)
            xf = xs_ref[pl.ds(r0, q), lanes].astype(F32)
            x_diag = jnp.concatenate([jnp.where(left, xf, 0.0), jnp.where(left, 0.0, xf)], axis=0).astype(BF16)
            ms = []
            for h in (2 * j, 2 * j + 1):
                cs_l = jnp.take_along_axis(col, jnp.where(lower, h, 8 + h), axis=1)
                arg = cs_l - jnp.where(lower, row[h:h + 1, :], row[8 + h:9 + h, :])
                p = jnp.exp(arg) + jnp.where(eye, row[16 + h:17 + h, :], 0.0)
                ms.append((g * p).astype(BF16))
            y = _dot(jnp.concatenate(ms, axis=1), x_diag) + yo_ref[slot, :, lanes] + dexp_ref[0, :, lanes] * xf
            y_ref[0, pl.ds(r0, q), lanes] = y.astype(y_ref.dtype)

    def fstep(i, carry):
        for slot in range(fu):
            chunk(fu * i + slot, slot)
        return carry

    lax.fori_loop(0, nc // fu, fstep, 0)
    sfin_ref[0, 0, 0] = sf_ref[...]


def _expanders():
    e = np.zeros((2, SSD_GROUPS, LANES, GROUP_W), np.float32)
    for d in range(2):
        for g in range(SSD_GROUPS):
            for s in range(2):
                for j in range(SSD_HPG):
                    e[d, g, (d * SSD_GROUPS + g) * 16 + s * 8 + j, j * SSD_HEAD_DIM:(j + 1) * SSD_HEAD_DIM] = 1.0
    return jnp.asarray(e, BF16)


def ssd_mixer(proj, xbc_col0, dt_raw, conv_w, conv_b, dt_bias, a_log, d_skip, s0, q):
    b, l, _ = proj.shape
    nc = l // q
    gn = SSD_GROUPS
    bias = jnp.zeros((1, LANES), F32).at[0, :2 * SSD_HEADS].set(dt_bias.reshape(-1))
    a = jnp.zeros((1, LANES), F32).at[0, :2 * SSD_HEADS].set(-jnp.exp(a_log.reshape(-1)))
    colg, rowg, sp = dt_prep(dt_raw, bias, a, q)
    dexp = jnp.repeat(d_skip.astype(F32), SSD_HEAD_DIM).reshape(gn, 1, GROUP_W)
    e2 = _expanders()

    cw = conv_w.astype(F32)
    cb = conv_b.astype(F32).reshape(1, -1)
    nb = SSD_INNER
    cwx, cwb, cwc = cw[:, :nb], cw[:, nb:nb + gn * SSD_STATE], cw[:, nb + gn * SSD_STATE:]
    cbx, cbb, cbc = cb[:, :nb], cb[:, nb:nb + gn * SSD_STATE], cb[:, nb + gn * SSD_STATE:]

    xo = xbc_col0 // GROUP_W
    bo = (xbc_col0 + SSD_INNER) // SSD_STATE
    co = bo + gn
    st_spec = pl.BlockSpec((1, 1, 2, SSD_STATE, GROUP_W), lambda i, g: (i, g, 0, 0, 0))
    y, sfin = pl.pallas_call(
        functools.partial(_ssd_kernel, l=l, q=q),
        grid=(b, gn),
        in_specs=[pl.BlockSpec((1, l, GROUP_W), lambda i, g: (i, 0, xo + g)),
                  pl.BlockSpec((1, l, SSD_STATE), lambda i, g: (i, 0, bo + g)),
                  pl.BlockSpec((1, l, SSD_STATE), lambda i, g: (i, 0, co + g)),
                  pl.BlockSpec((3, GROUP_W), lambda i, g: (0, g)),
                  pl.BlockSpec((3, SSD_STATE), lambda i, g: (0, g)),
                  pl.BlockSpec((3, SSD_STATE), lambda i, g: (0, g)),
                  pl.BlockSpec((1, GROUP_W), lambda i, g: (0, g)),
                  pl.BlockSpec((1, SSD_STATE), lambda i, g: (0, g)),
                  pl.BlockSpec((1, SSD_STATE), lambda i, g: (0, g)),
                  pl.BlockSpec((1, l, LANES), lambda i, g: (i, 0, 0)),
                  pl.BlockSpec((1, 1, nc, 24, q), lambda i, g: (i, g, 0, 0, 0)),
                  pl.BlockSpec((1, l, 2 * LANES), lambda i, g: (i, 0, 0)),
                  pl.BlockSpec((1, 1, LANES, GROUP_W), lambda i, g: (0, g, 0, 0)),
                  pl.BlockSpec((1, 1, LANES, GROUP_W), lambda i, g: (1, g, 0, 0)),
                  pl.BlockSpec((1, 1, GROUP_W), lambda i, g: (g, 0, 0)),
                  st_spec],
        out_specs=[pl.BlockSpec((1, l, GROUP_W), lambda i, g: (i, 0, g)), st_spec],
        out_shape=[jax.ShapeDtypeStruct((b, l, SSD_INNER), BF16),
                   jax.ShapeDtypeStruct((b, gn, 2, SSD_STATE, GROUP_W), F32)],
        scratch_shapes=[pltpu.VMEM((l, GROUP_W), BF16),
                        pltpu.VMEM((l, SSD_STATE), BF16),
                        pltpu.VMEM((nc, SSD_STATE, q), BF16),
                        pltpu.VMEM((l, GROUP_W), BF16),
                        pltpu.VMEM((nc, SSD_STATE, GROUP_W), BF16),
                        pltpu.VMEM((nc, 1, GROUP_W), F32),
                        pltpu.VMEM((SSD_STATE, GROUP_W), F32),
                        pltpu.VMEM((SSD_STATE, GROUP_W), F32),
                        pltpu.VMEM((2 if nc % 2 == 0 else 1, q, GROUP_W), F32)],
        compiler_params=_cparams(("parallel", "parallel")),
        name="ssd",
    )(proj, proj, proj, cwx, cwb, cwc, cbx, cbb, cbc, colg, rowg, sp, e2, e2, dexp, s0)
    return y, sfin


def _mixout_kernel(y_ref, z_ref, gb_ref, gc_ref, hv_ref, g0_ref, g1_ref, x_ref, gate_ref,
                   ng_ref, bg_ref, scw_ref, wssd_ref, wsc_ref, wo_ref, o_ref, *, tm, period):
    ns = 2 if (tm // 2) % period == 0 else 1
    ts = tm // ns
    pos = lax.broadcasted_iota(jnp.int32, (ts, 1), 0) % period
    w = scw_ref[...]
    bg = bg_ref[...]
    for s in range(ns):
        r = slice(s * ts, (s + 1) * ts)
        yz = y_ref[0, r, :].astype(F32) * _silu_t(z_ref[0, r, :].astype(F32))
        ms = jnp.mean(yz * yz, axis=-1, keepdims=True)
        yn = (yz * lax.rsqrt(ms + EPS) * ng_ref[...]).astype(BF16)
        y_ssd = _dot(yn, wssd_ref[...])

        u = gc_ref[0, r, :].astype(F32) * hv_ref[0, r, :].astype(F32)
        u_prev = jnp.where(pos == 0, 0.0, pltpu.roll(u, 1, 0))
        u_next = jnp.where(pos == period - 1, 0.0, pltpu.roll(u, ts - 1, 0))
        v = w[0:1] * u_prev + w[1:2] * u + w[2:3] * u_next
        y_sc = _dot((gb_ref[0, r, :].astype(F32) * v).astype(BF16), wsc_ref[...])

        g0 = _sigmoid_t(g0_ref[0, r, :].astype(F32) + bg[:, :D_MODEL])
        g1 = _sigmoid_t(g1_ref[0, r, :].astype(F32) + bg[:, D_MODEL:])
        out = _dot((g0 * y_ssd + g1 * y_sc).astype(BF16), wo_ref[...])
        o_ref[0, r, :] = x_ref[0, r, :] + gate_ref[0] * out


def mixer_out(y, proj, x, gate, norm_g, b_gate, sc_conv_w, w_ssd, w_sc, w_o, tm, period):
    b, l, d = x.shape
    pc = lambda k: pl.BlockSpec((1, tm, d), lambda i, m, k=k: (i, m, k))
    full = lambda shp: pl.BlockSpec(shp, lambda i, m: (0,) * len(shp))
    return pl.pallas_call(
        functools.partial(_mixout_kernel, tm=tm, period=period),
        grid=(b, l // tm),
        in_specs=[pl.BlockSpec((1, tm, SSD_INNER), lambda i, m: (i, m, 0)),
                  pl.BlockSpec((1, tm, SSD_INNER), lambda i, m: (i, m, 0)),
                  pc(5), pc(6), pc(7), pc(8), pc(9),
                  pl.BlockSpec((1, tm, d), lambda i, m: (i, m, 0)),
                  pl.BlockSpec((1, 1, d), lambda i, m: (i, 0, 0)),
                  full((1, SSD_INNER)), full((1, 2 * d)), full((3, SC_WIDTH)),
                  full((SSD_INNER, d)), full((SC_WIDTH, d)), full((d, d))],
        out_specs=pl.BlockSpec((1, tm, d), lambda i, m: (i, m, 0)),
        out_shape=jax.ShapeDtypeStruct((b, l, d), F32),
        compiler_params=_cparams(("parallel", "parallel")),
        name="mixer_out",
    )(y, proj, proj, proj, proj, proj, proj, x, gate,
      norm_g.reshape(1, -1), b_gate.reshape(1, -1), sc_conv_w, w_ssd, w_sc, w_o)


def _ffn_kernel(x_ref, g_ref, sh_ref, sc_ref, gate_ref, w1_ref, w3_ref, w2_ref, o_ref, *, nf):
    x = x_ref[0]
    hb = _norm_mod(x, g_ref[...], sh_ref[0], sc_ref[0]).astype(BF16)
    fw = w1_ref.shape[1] // nf
    acc = None
    for k in range(nf):
        a = _dot(hb, w1_ref[:, k * fw:(k + 1) * fw])
        bb = _dot(hb, w3_ref[:, k * fw:(k + 1) * fw])
        part = _dot((_silu(a) * bb).astype(BF16), w2_ref[k * fw:(k + 1) * fw, :])
        acc = part if acc is None else acc + part
    o_ref[0] = x + gate_ref[0] * acc


def ffn_dense(x, g, shift, scale, gate, w1, w3, w2, tm):
    b, l, d = x.shape
    f = w1.shape[1]
    vec = pl.BlockSpec((1, 1, d), lambda i, m: (i, 0, 0))
    const = lambda shp: pl.BlockSpec(shp, lambda i, m: (0, 0), pipeline_mode=pl.Buffered(1))
    return pl.pallas_call(
        functools.partial(_ffn_kernel, nf=2),
        grid=(b, l // tm),
        in_specs=[pl.BlockSpec((1, tm, d), lambda i, m: (i, m, 0)),
                  pl.BlockSpec((1, d), lambda i, m: (0, 0)),
                  vec, vec, vec, const((d, f)), const((d, f)), const((f, d))],
        out_specs=pl.BlockSpec((1, tm, d), lambda i, m: (i, m, 0)),
        out_shape=jax.ShapeDtypeStruct((b, l, d), F32),
        compiler_params=_cparams(("parallel", "parallel")),
        name="ffn_dense",
    )(x, g.reshape(1, d), shift, scale, gate, w1, w3, w2)


def _router_kernel(x_ref, g_ref, sh_ref, sc_ref, rw_ref, h_ref, route_ref, cnt_ref, run_ref, *, tm):
    @pl.when((pl.program_id(0) == 0) & (pl.program_id(1) == 0))
    def _():
        run_ref[...] = jnp.zeros_like(run_ref)

    h = _norm_mod(x_ref[0], g_ref[...], sh_ref[0], sc_ref[0])
    h_hi, h_lo = _split2(h)
    h_ref[0] = h
    w_hi, w_lo = _split2(rw_ref[...])
    logits = _dot(h_hi, w_hi) + _dot(h_lo, w_hi) + _dot(h_hi, w_lo)
    lane = lax.broadcasted_iota(jnp.int32, (tm, LANES), 1)
    ninf = float("-inf")
    lg = jnp.where(lane < N_EXPERTS, logits, ninf)
    m1 = jnp.max(lg, axis=1, keepdims=True)
    i1 = jnp.min(jnp.where(lg == m1, lane, LANES), axis=1, keepdims=True)
    lg2 = jnp.where(lane == i1, ninf, lg)
    m2 = jnp.max(lg2, axis=1, keepdims=True)
    i2 = jnp.min(jnp.where(lg2 == m2, lane, LANES), axis=1, keepdims=True)
    e2 = jnp.exp(m2 - m1)
    den = 1.0 + e2
    sel1 = jnp.where(lane == i1, 1.0, 0.0)
    sel2 = jnp.where(lane == i2, 1.0, 0.0)
    cnt = sel1 + sel2
    r = lax.broadcasted_iota(jnp.int32, (tm, tm), 0)
    c = lax.broadcasted_iota(jnp.int32, (tm, tm), 1)
    tri = jnp.where(c < r, 1.0, 0.0).astype(BF16)
    base = _dot(tri, cnt.astype(BF16)) + run_ref[0:1, :]
    r1 = jnp.sum(sel1 * base, axis=1, keepdims=True)
    r2 = jnp.sum(sel2 * base, axis=1, keepdims=True)
    vals = (i1.astype(F32), i2.astype(F32), 1.0 / den, e2 / den, r1, r2)
    out = jnp.zeros((tm, LANES), F32)
    for k, v in enumerate(vals):
        out = jnp.where(lane == k, v, out)
    route_ref[0] = out
    new_run = run_ref[...] + jnp.sum(cnt, axis=0, keepdims=True)
    run_ref[...] = new_run
    cnt_ref[...] = new_run


def router(x, g, shift, scale, router_w, tm):
    b, l, d = x.shape
    rw = jnp.zeros((d, LANES), F32).at[:, :N_EXPERTS].set(router_w)
    vec = pl.BlockSpec((1, 1, d), lambda i, m: (i, 0, 0))
    return pl.pallas_call(
        functools.partial(_router_kernel, tm=tm),
        grid=(b, l // tm),
        in_specs=[pl.BlockSpec((1, tm, d), lambda i, m: (i, m, 0)),
                  pl.BlockSpec((1, d), lambda i, m: (0, 0)),
                  vec, vec, pl.BlockSpec((d, LANES), lambda i, m: (0, 0))],
        out_specs=[pl.BlockSpec((1, tm, d), lambda i, m: (i, m, 0)),
                   pl.BlockSpec((1, tm, LANES), lambda i, m: (i, m, 0)),
                   pl.BlockSpec((8, LANES), lambda i, m: (0, 0))],
        out_shape=[jax.ShapeDtypeStruct((b, l, d), F32),
                   jax.ShapeDtypeStruct((b, l, LANES), F32),
                   jax.ShapeDtypeStruct((8, LANES), F32)],
        scratch_shapes=[pltpu.VMEM((8, LANES), F32)],
        compiler_params=_cparams(("arbitrary", "arbitrary")),
        name="router",
    )(x, g.reshape(1, d), shift, scale, rw)


def _pack_bf16_pairs(y):
    k = y.shape[1] // 2
    bits = lax.bitcast_convert_type(y.astype(BF16).astype(F32), jnp.uint32)
    return bits[:, :k] | (bits[:, k:] >> 16)


def _unpack_bf16_pairs(p):
    hi = lax.bitcast_convert_type(p & jnp.uint32(0xFFFF0000), F32)
    lo = lax.bitcast_convert_type(p << 16, F32)
    return jnp.concatenate([hi, lo], axis=1)


def _gffn_kernel(te_ref, nv_ref, x_ref, w1_ref, w3_ref, w2_ref, o_ref, acc_ref, *, nf):
    i = pl.program_id(0)
    f = pl.program_id(1)

    @pl.when(i < nv_ref[0])
    def _():
        x = x_ref[...].astype(BF16)
        a = _dot(x, w1_ref[0])
        bb = _dot(x, w3_ref[0])
        part = _dot((_silu(a) * bb).astype(BF16), w2_ref[0])

        @pl.when(f == 0)
        def _():
            acc_ref[...] = part

        @pl.when(f > 0)
        def _():
            acc_ref[...] += part

        @pl.when(f == nf - 1)
        def _():
            o_ref[...] = _pack_bf16_pairs(acc_ref[...])

    @pl.when((i >= nv_ref[0]) & (f == nf - 1))
    def _():
        o_ref[...] = jnp.zeros_like(o_ref)


def grouped_ffn(xs, tile_expert, n_valid, w1, w3, w2, tm, nf):
    rows, d = xs.shape
    nt = rows // tm
    f = w1.shape[2]
    fw = f // nf
    grid_spec = pltpu.PrefetchScalarGridSpec(
        num_scalar_prefetch=2,
        grid=(nt, nf),
        in_specs=[pl.BlockSpec((tm, d), lambda i, k, te, nv: (i, 0)),
                  pl.BlockSpec((1, d, fw), lambda i, k, te, nv: (te[i], 0, k)),
                  pl.BlockSpec((1, d, fw), lambda i, k, te, nv: (te[i], 0, k)),
                  pl.BlockSpec((1, fw, d), lambda i, k, te, nv: (te[i], k, 0))],
        out_specs=pl.BlockSpec((tm, d // 2), lambda i, k, te, nv: (i, 0)),
        scratch_shapes=[pltpu.VMEM((tm, d), F32)],
    )
    return pl.pallas_call(
        functools.partial(_gffn_kernel, nf=nf),
        grid_spec=grid_spec,
        out_shape=jax.ShapeDtypeStruct((rows, d // 2), jnp.uint32),
        compiler_params=_cparams(("parallel", "arbitrary")),
        name="grouped_ffn",
    )(tile_expert, n_valid, xs, w1, w3, w2)


def _combine_kernel(x_ref, y0_ref, y1_ref, route_ref, gate_ref, fg_ref, o_ref):
    r = route_ref[0]
    moe = r[:, 2:3] * _unpack_bf16_pairs(y0_ref[0, 0]) + r[:, 3:4] * _unpack_bf16_pairs(y1_ref[0, 0])
    xn = x_ref[0] + gate_ref[0] * moe
    ms = jnp.mean(xn * xn, axis=-1, keepdims=True)
    o_ref[0] = xn * lax.rsqrt(ms + EPS) * fg_ref[...]


def combine_final(x, yg, route, gate, final_g, tm):
    b, l, d = x.shape
    return pl.pallas_call(
        _combine_kernel,
        grid=(b, l // tm),
        in_specs=[pl.BlockSpec((1, tm, d), lambda i, m: (i, m, 0)),
                  pl.BlockSpec((1, 1, tm, d // 2), lambda i, m: (0, i, m, 0)),
                  pl.BlockSpec((1, 1, tm, d // 2), lambda i, m: (1, i, m, 0)),
                  pl.BlockSpec((1, tm, LANES), lambda i, m: (i, m, 0)),
                  pl.BlockSpec((1, 1, d), lambda i, m: (i, 0, 0)),
                  pl.BlockSpec((1, d), lambda i, m: (0, 0))],
        out_specs=pl.BlockSpec((1, tm, d), lambda i, m: (i, m, 0)),
        out_shape=jax.ShapeDtypeStruct((b, l, d), F32),
        compiler_params=_cparams(("parallel", "parallel")),
        name="combine_final",
    )(x, yg, yg, route, gate, final_g.reshape(1, d))


SC_CORES = 2
SC_SUBCORES = 16
SC_WORKERS = SC_CORES * SC_SUBCORES
SC_STREAM_BYTES = 256 * 1024
SC_STREAM_ROWS = 128


def _sc_rows(per_worker, d, dtype):
    return min(SC_STREAM_ROWS, SC_STREAM_BYTES // (d * jnp.dtype(dtype).itemsize), per_worker)


def _sc_mesh():
    return plsc.VectorSubcoreMesh(core_axis_name="c", subcore_axis_name="s",
                                  num_cores=SC_CORES, num_subcores=SC_SUBCORES)


def dispatch_rows(h, pos0, pos1, n_rows):
    t, d = h.shape
    per_w = t // SC_WORKERS
    ch = _sc_rows(per_w, d, h.dtype)

    @functools.partial(
        pl.kernel, mesh=_sc_mesh(),
        out_type=jax.ShapeDtypeStruct((n_rows, d), h.dtype),
        scratch_types=[pltpu.VMEM((ch,), jnp.int32), pltpu.VMEM((ch,), jnp.int32),
                       pltpu.VMEM((ch, d), h.dtype), pltpu.SemaphoreType.DMA],
        name="moe_dispatch")
    def scatter(h_hbm, p0_hbm, p1_hbm, out_hbm, i0_v, i1_v, rows_v, sem):
        base = (lax.axis_index("s") * SC_CORES + lax.axis_index("c")) * per_w

        @pl.loop(0, per_w // ch)
        def _(j):
            off = base + j * ch
            pltpu.sync_copy(h_hbm.at[pl.ds(off, ch)], rows_v)
            pltpu.sync_copy(p0_hbm.at[pl.ds(off, ch)], i0_v)
            pltpu.sync_copy(p1_hbm.at[pl.ds(off, ch)], i1_v)
            pltpu.async_copy(rows_v, out_hbm.at[i0_v], sem).wait()
            pltpu.async_copy(rows_v, out_hbm.at[i1_v], sem).wait()

    return scatter(h, pos0, pos1)


def return_rows(ys, idx):
    n = idx.shape[0]
    d = ys.shape[1]
    per_w = n // SC_WORKERS
    ch = _sc_rows(per_w, d, ys.dtype)

    @functools.partial(
        pl.kernel, mesh=_sc_mesh(),
        out_type=jax.ShapeDtypeStruct((n, d), ys.dtype),
        scratch_types=[pltpu.VMEM((ch,), jnp.int32), pltpu.VMEM((ch, d), ys.dtype), pltpu.SemaphoreType.DMA],
        name="moe_return")
    def gather(ys_hbm, idx_hbm, out_hbm, idx_v, rows_v, sem):
        base = (lax.axis_index("s") * SC_CORES + lax.axis_index("c")) * per_w

        @pl.loop(0, per_w // ch)
        def _(j):
            off = base + j * ch
            pltpu.sync_copy(idx_hbm.at[pl.ds(off, ch)], idx_v)
            pltpu.async_copy(ys_hbm.at[idx_v], rows_v, sem).wait()
            pltpu.sync_copy(rows_v, out_hbm.at[pl.ds(off, ch)])

    return gather(ys, idx)


def moe_block(x, g, shift, scale, gate, router_w, w1, w3, w2, final_g):
    b, l, d = x.shape
    t = b * l
    tm = MOE_TM
    h, route, counts = router(x, g, shift, scale, router_w, min(512, l))
    rt = route.reshape(t, LANES)
    cnt = counts[0, :N_EXPERTS].astype(jnp.int32)
    gs = ((cnt + tm - 1) // tm) * tm
    ends = jnp.cumsum(gs)
    offs = ends - gs
    pos0 = offs[rt[:, 0].astype(jnp.int32)] + rt[:, 4].astype(jnp.int32)
    pos1 = offs[rt[:, 1].astype(jnp.int32)] + rt[:, 5].astype(jnp.int32)
    nt = (2 * t) // tm + N_EXPERTS
    n_valid = (ends[-1] // tm).astype(jnp.int32).reshape(1)
    tile = jnp.minimum(jnp.arange(nt, dtype=jnp.int32), n_valid[0] - 1)
    tile_expert = jnp.sum((tile[:, None] >= (ends // tm)[None, :]).astype(jnp.int32), axis=1)
    xs = dispatch_rows(h.reshape(t, d), pos0, pos1, nt * tm)
    ys = grouped_ffn(xs, tile_expert, n_valid, w1, w3, w2, tm, 2)
    yg = return_rows(ys, jnp.concatenate([pos0, pos1])).reshape(2, b, l, d // 2)
    return combine_final(x, yg, route, gate, final_g, min(512, l))


def _in_weights(w_in):
    o1 = SSD_INNER
    o2 = o1 + XBC_WIDTH
    o3 = o2 + 2 * SSD_HEADS
    w_main = jnp.concatenate([w_in[:, :o2], w_in[:, o3:]], axis=1).astype(BF16)
    w_dt = jnp.pad(w_in[:, o2:o3], ((0, 0), (0, LANES - 2 * SSD_HEADS))).astype(BF16)
    return w_main, w_dt


def kernel(x, c, ctx, c_ctx, w_mod, b_mod, norm1_g, norm2_g, w_in, b_gate, ssd_conv_w, ssd_conv_b, ssd_dt_bias, ssd_a_log, ssd_d, ssd_norm_g, w_ssd_out, sc_conv_w, w_sc_out, w_o, ffn_w1, ffn_w3, ffn_w2, router_w, moe_w1, moe_w3, moe_w2, final_g):
    b, l, d = x.shape
    lc = ctx.shape[1]
    depth = w_mod.shape[0]
    cc = jnp.zeros((16, d), F32).at[:b].set(c).at[b].set(c_ctx)
    mod = modulation(cc, w_mod, b_mod)
    zeros_state = jnp.zeros((b, SSD_GROUPS, 2, SSD_STATE, GROUP_W), F32)
    nctx = b * lc
    ctx = ctx.reshape(1, nctx, d)
    tmc = lc * max(1, min(512, nctx) // lc)

    def per_seq(t):
        return t.reshape(b, lc, t.shape[-1])

    for i in range(depth):
        last = i == depth - 1
        mx = mod[i, :b].reshape(b, N_MOD, 1, d)
        mc = mod[i, b].reshape(1, N_MOD, 1, d)
        w_main, w_dt = _in_weights(w_in[i])
        ssd_p = (ssd_conv_w[i], ssd_conv_b[i], ssd_dt_bias[i], ssd_a_log[i], ssd_d[i])
        out_p = (ssd_norm_g[i], b_gate[i], sc_conv_w[i], w_ssd_out[i].astype(BF16),
                 w_sc_out[i].astype(BF16), w_o[i].astype(BF16))

        if last:
            w_xbc = w_main[:, SSD_INNER:SSD_INNER + XBC_WIDTH]
            proj_c, dt_c = in_proj(ctx, norm1_g[i], mc[:, 0], mc[:, 1], w_xbc, w_dt, min(1024, nctx), 1024)
            _, s_ctx = ssd_mixer(per_seq(proj_c), 0, per_seq(dt_c), *ssd_p, zeros_state, SSD_Q)
        else:
            proj_c, dt_c = in_proj(ctx, norm1_g[i], mc[:, 0], mc[:, 1], w_main, w_dt, min(1024, nctx), 2048)
            y_c, s_ctx = ssd_mixer(per_seq(proj_c), SSD_INNER, per_seq(dt_c), *ssd_p, zeros_state, SSD_Q)
            ctx = mixer_out(y_c.reshape(1, nctx, SSD_INNER), proj_c, ctx, mc[:, 2], *out_p, tmc, lc)

        proj_x, dt_x = in_proj(x, norm1_g[i], mx[:, 0], mx[:, 1], w_main, w_dt, min(1024, l), 2048)
        y_x, _ = ssd_mixer(proj_x, SSD_INNER, dt_x, *ssd_p, s_ctx, SSD_Q)
        x = mixer_out(y_x, proj_x, x, mx[:, 2], *out_p, min(512, l), GRID_W)

        j = i // 2
        if i % 2 == 0:
            w1, w3, w2 = ffn_w1[j].astype(BF16), ffn_w3[j].astype(BF16), ffn_w2[j].astype(BF16)
            x = ffn_dense(x, norm2_g[i], mx[:, 3], mx[:, 4], mx[:, 5], w1, w3, w2, min(512, l))
            if not last:
                ctx = ffn_dense(ctx, norm2_g[i], mc[:, 3], mc[:, 4], mc[:, 5], w1, w3, w2, tmc)
        else:
            assert last, "the routed channel mixer is fused with the final norm"
            w1, w3, w2 = moe_w1[j].astype(BF16), moe_w3[j].astype(BF16), moe_w2[j].astype(BF16)
            x = moe_block(x, norm2_g[i], mx[:, 3], mx[:, 4], mx[:, 5], router_w[j], w1, w3, w2, final_g)
    return x
```

```python
import functools

import numpy as np
import jax
import jax.numpy as jnp
from jax import lax
from jax.experimental import pallas as pl
from jax.experimental.pallas import tpu as pltpu
from jax.experimental.pallas import tpu_sc as plsc

F32 = jnp.float32
BF16 = jnp.bfloat16

D_MODEL = 1024
GRID_W = 64
SSD_INNER = 2048
SSD_HEADS = 32
SSD_GROUPS = 4
SSD_HPG = 8
SSD_HEAD_DIM = 64
SSD_STATE = 128
GROUP_W = SSD_HPG * SSD_HEAD_DIM
XBC_WIDTH = SSD_INNER + 2 * SSD_GROUPS * SSD_STATE
SC_WIDTH = 1024
N_MOD = 6
N_EXPERTS = 8
EPS = 1e-6

LANES = 128
SSD_Q = 128
MOE_TM = 512
VMEM_LIMIT = 56 * 1024 * 1024


def _dot(a, b):
    return jnp.dot(a, b, preferred_element_type=F32)


def _sigmoid(v):
    return 1.0 / (1.0 + jnp.exp(-v))


def _silu(v):
    return v * _sigmoid(v)


def _sigmoid_t(v):
    return 0.5 + 0.5 * jnp.tanh(0.5 * v)


def _silu_t(v):
    hv = 0.5 * v
    return hv + hv * jnp.tanh(hv)


def _split2(a):
    hi = a.astype(BF16)
    lo = (a - hi.astype(F32)).astype(BF16)
    return hi, lo


def _split3(a):
    hi = a.astype(BF16)
    r = a - hi.astype(F32)
    mid = r.astype(BF16)
    lo = (r - mid.astype(F32)).astype(BF16)
    return hi, mid, lo


def _norm_mod(x, g, shift, scale):
    ms = jnp.mean(x * x, axis=-1, keepdims=True)
    return (x * lax.rsqrt(ms + EPS) * g) * (1.0 + scale) + shift


def _cparams(sem, vmem=VMEM_LIMIT):
    return pltpu.CompilerParams(dimension_semantics=sem, vmem_limit_bytes=vmem)


def _mod_kernel(c_ref, w_ref, b_ref, o_ref):
    a_hi, a_lo = _split2(_silu(c_ref[...]))
    w_hi, w_lo = _split2(w_ref[0])
    o_ref[0] = _dot(a_hi, w_hi) + _dot(a_lo, w_hi) + _dot(a_hi, w_lo) + b_ref[0]


def modulation(cc, w_mod, b_mod):
    depth, d, n = w_mod.shape
    tn = 1536
    return pl.pallas_call(
        _mod_kernel,
        grid=(depth, n // tn),
        in_specs=[pl.BlockSpec((16, d), lambda i, j: (0, 0)),
                  pl.BlockSpec((1, d, tn), lambda i, j: (i, 0, j)),
                  pl.BlockSpec((1, 1, tn), lambda i, j: (i, 0, j))],
        out_specs=pl.BlockSpec((1, 16, tn), lambda i, j: (i, 0, j)),
        out_shape=jax.ShapeDtypeStruct((depth, 16, n), F32),
        compiler_params=_cparams(("parallel", "parallel")),
        name="modulation",
    )(cc, w_mod, b_mod.reshape(depth, 1, n))


def _inproj_kernel(x_ref, g_ref, sh_ref, sc_ref, w_ref, wdt_ref, o_ref, dt_ref, h_ref):
    @pl.when(pl.program_id(2) == 0)
    def _():
        hb = _norm_mod(x_ref[0], g_ref[...], sh_ref[0], sc_ref[0]).astype(BF16)
        h_ref[...] = hb
        dt_ref[0] = _dot(hb, wdt_ref[...])

    o_ref[0] = _dot(h_ref[...], w_ref[...]).astype(o_ref.dtype)


def in_proj(x, g, shift, scale, w, wdt, tm, tn):
    b, l, d = x.shape
    n = w.shape[1]
    return pl.pallas_call(
        _inproj_kernel,
        grid=(b, l // tm, n // tn),
        in_specs=[pl.BlockSpec((1, tm, d), lambda i, m, j: (i, m, 0)),
                  pl.BlockSpec((1, d), lambda i, m, j: (0, 0)),
                  pl.BlockSpec((1, 1, d), lambda i, m, j: (i, 0, 0)),
                  pl.BlockSpec((1, 1, d), lambda i, m, j: (i, 0, 0)),
                  pl.BlockSpec((d, tn), lambda i, m, j: (0, j)),
                  pl.BlockSpec((d, LANES), lambda i, m, j: (0, 0))],
        out_specs=[pl.BlockSpec((1, tm, tn), lambda i, m, j: (i, m, j)),
                   pl.BlockSpec((1, tm, LANES), lambda i, m, j: (i, m, 0))],
        out_shape=[jax.ShapeDtypeStruct((b, l, n), BF16),
                   jax.ShapeDtypeStruct((b, l, LANES), F32)],
        scratch_shapes=[pltpu.VMEM((tm, d), BF16)],
        compiler_params=_cparams(("parallel", "parallel", "arbitrary")),
        name="in_proj",
    )(x, g.reshape(1, d), shift, scale, w, wdt)


def _dt_kernel(raw_ref, bias_ref, a_ref, pc_ref, ph_ref, plo_ref, col_ref, row_ref, sp_ref, *, q, ch):
    row = lax.broadcasted_iota(jnp.int32, (q, q), 0)
    col = lax.broadcasted_iota(jnp.int32, (q, q), 1)
    tri_l = jnp.where(col <= row, 1.0, 0.0).astype(BF16)
    tri_u = jnp.where(col >= row, 1.0, 0.0).astype(BF16)
    lane = lax.broadcasted_iota(jnp.int32, (q, LANES), 1)
    fwd = lane < SSD_HEADS
    ph, plo = ph_ref[...], plo_ref[...]

    def place_split(t):
        t_hi, t_lo = _split2(t)
        return (_dot(t_hi, ph) + _dot(t_lo, plo)).astype(BF16)

    for k in range(ch):
        v = raw_ref[0, k * q:(k + 1) * q, :] + bias_ref[...]
        dt = jnp.maximum(v, 0.0) + jnp.log1p(jnp.exp(-jnp.abs(v)))
        d1, d2, d3 = _split3(dt * a_ref[...])
        cs_f = _dot(tri_l, d1) + _dot(tri_l, d2) + _dot(tri_l, d3)
        cs_b = _dot(tri_u, d1) + _dot(tri_u, d2) + _dot(tri_u, d3)
        cs = jnp.where(fwd, cs_f, cs_b)
        tot = jnp.where(fwd[0:1], cs_f[q - 1:q, :], cs_b[0:1, :])
        sl = slice(k * q, (k + 1) * q)
        c1, c2, c3 = _split3(cs)
        pc = pc_ref[...]
        col_ref[0, sl, :] = _dot(c1, pc) + _dot(c2, pc) + _dot(c3, pc)
        r_t = (cs - jnp.log(dt)).T
        dt_t = dt.T
        for g in range(SSD_GROUPS):
            lo = g * SSD_HPG
            row_ref[0, g, k, 0:8, :] = r_t[lo:lo + 8, :]
            row_ref[0, g, k, 8:16, :] = r_t[SSD_HEADS + lo:SSD_HEADS + lo + 8, :]
            row_ref[0, g, k, 16:24, :] = dt_t[SSD_HEADS + lo:SSD_HEADS + lo + 8, :]
        sp_ref[0, sl, 0:LANES] = place_split(dt * jnp.exp(tot - cs))
        sp_ref[0, sl, LANES:2 * LANES] = place_split(jnp.exp(cs))


def _placements():
    pc = np.zeros((LANES, LANES), np.float32)
    ph = np.zeros((LANES, LANES), np.float32)
    plo = np.zeros((LANES, LANES), np.float32)
    for d in range(2):
        for g in range(SSD_GROUPS):
            for j in range(SSD_HPG):
                src = d * SSD_HEADS + g * SSD_HPG + j
                pc[src, g * 16 + d * 8 + j] = 1.0
                ph[src, (d * SSD_GROUPS + g) * 16 + j] = 1.0
                plo[src, (d * SSD_GROUPS + g) * 16 + 8 + j] = 1.0
    return jnp.asarray(pc, BF16), jnp.asarray(ph, BF16), jnp.asarray(plo, BF16)


def dt_prep(raw, bias, a, q):
    b, l, _ = raw.shape
    nc = l // q
    ch = min(8, nc)
    gn = SSD_GROUPS
    vspec = pl.BlockSpec((1, LANES), lambda i, c: (0, 0))
    pspec = pl.BlockSpec((LANES, LANES), lambda i, c: (0, 0))
    return pl.pallas_call(
        functools.partial(_dt_kernel, q=q, ch=ch),
        grid=(b, nc // ch),
        in_specs=[pl.BlockSpec((1, ch * q, LANES), lambda i, c: (i, c, 0)), vspec, vspec, pspec, pspec, pspec],
        out_specs=[pl.BlockSpec((1, ch * q, LANES), lambda i, c: (i, c, 0)),
                   pl.BlockSpec((1, gn, ch, 24, q), lambda i, c: (i, 0, c, 0, 0)),
                   pl.BlockSpec((1, ch * q, 2 * LANES), lambda i, c: (i, c, 0))],
        out_shape=[jax.ShapeDtypeStruct((b, l, LANES), F32),
                   jax.ShapeDtypeStruct((b, gn, nc, 24, q), F32),
                   jax.ShapeDtypeStruct((b, l, 2 * LANES), BF16)],
        compiler_params=_cparams(("parallel", "parallel")),
        name="dt_prep",
    )(raw, bias, a, *_placements())


def _ssd_kernel(xp_ref, bp_ref, cp_ref, cwx_ref, cwb_ref, cwc_ref, cbx_ref, cbb_ref, cbc_ref,
                col_ref, row_ref, sp_ref, e2f_ref, e2b_ref, dexp_ref, s0_ref,
                y_ref, sfin_ref,
                xs_ref, cc_ref, bt_ref, xwf_ref, sbe_ref, edge_ref, sf_ref, sb_ref, yo_ref, *, l, q):
    nc = l // q
    sr = lax.broadcasted_iota(jnp.int32, (q, q + 32), 0)
    sc = lax.broadcasted_iota(jnp.int32, (q, q + 32), 1)
    shift_prev = jnp.where((sc == sr - 1) | ((sr == 0) & (sc == q + 15)), 1.0, 0.0).astype(BF16)
    shift_next = jnp.where(((sc == sr + 1) & (sc < q)) | ((sr == q - 1) & (sc == q + 16)), 1.0, 0.0).astype(BF16)

    def conv_silu(srcs, w, b, k):
        r0 = pl.multiple_of(k * q, q)
        p0 = pl.multiple_of(jnp.maximum(r0 - 16, 0), 16)
        n0 = pl.multiple_of(jnp.minimum(r0 + q, l - 16), 16)

        def rows_at(start, n):
            parts = [r[0, pl.ds(start, n), ls] for r, ls in srcs]
            return parts[0] if len(parts) == 1 else jnp.concatenate(parts, axis=1)

        blk = rows_at(r0, q)
        before = rows_at(p0, 16)
        after = rows_at(n0, 16)
        before = jnp.where(k > 0, before, jnp.zeros_like(before))
        after = jnp.where(k < nc - 1, after, jnp.zeros_like(after))
        stacked = jnp.concatenate([blk, before, after], axis=0)
        v = (w[0:1] * _dot(shift_prev, stacked) + w[1:2] * blk.astype(F32)
             + w[2:3] * _dot(shift_next, stacked) + b)
        return _silu_t(v)

    all_lanes = slice(None)

    def prep(k):
        r0 = pl.multiple_of(k * q, q)
        x = conv_silu(((xp_ref, all_lanes),), cwx_ref[...], cbx_ref[...], k)
        xs_ref[pl.ds(r0, q), :] = x.astype(BF16)
        bc = conv_silu(((bp_ref, all_lanes), (cp_ref, all_lanes)),
                       jnp.concatenate([cwb_ref[...], cwc_ref[...]], axis=1),
                       jnp.concatenate([cbb_ref[...], cbc_ref[...]], axis=1), k)
        b_t = bc[:, :SSD_STATE].T.astype(BF16)
        bt_ref[k] = b_t
        cc_ref[pl.ds(r0, q), :] = bc[:, SSD_STATE:].astype(BF16)
        sp_w = sp_ref[0, pl.ds(r0, q), 0:LANES]
        xwf_ref[pl.ds(r0, q), :] = (x * _dot(sp_w, e2f_ref[0, 0])).astype(BF16)
        xw_b = (x * _dot(sp_w, e2b_ref[0, 0])).astype(BF16)
        e_last = sp_ref[0, pl.ds(pl.multiple_of(r0 + q - 16, 16), 16), LANES:2 * LANES]
        e_first = sp_ref[0, pl.ds(r0, 16), LANES:2 * LANES]
        edge_ref[k] = _dot(e_last, e2f_ref[0, 0])[15:16, :]
        edge_b = _dot(e_first, e2b_ref[0, 0])[0:1, :]
        return b_t, xw_b, edge_b

    sb_ref[...] = s0_ref[0, 0, 1]

    def bstep(i, carry):
        c = nc - 1 - i
        b_t, xw_b, edge_b = prep(c)
        sbe_ref[c] = sb_ref[...].astype(BF16)
        sb_ref[...] = sb_ref[...] * edge_b + _dot(b_t, xw_b)
        return carry

    lax.fori_loop(0, nc, bstep, 0, unroll=4)
    sfin_ref[0, 0, 1] = sb_ref[...]

    sf_ref[...] = s0_ref[0, 0, 0]
    li = lax.broadcasted_iota(jnp.int32, (q, q), 0)
    si = lax.broadcasted_iota(jnp.int32, (q, q), 1)
    lower = si <= li
    eye = si == li
    left = lax.broadcasted_iota(jnp.int32, (q, LANES), 1) < SSD_HEAD_DIM
    fu = yo_ref.shape[0]
    lane0 = 16 * pl.program_id(1)

    def chunk(c, slot):
        r0 = pl.multiple_of(c * q, q)
        cc = cc_ref[pl.ds(r0, q), :]
        sp_e = sp_ref[0, pl.ds(r0, q), LANES:2 * LANES]
        yo_ref[slot] = (_dot(sp_e, e2f_ref[0, 0]) * _dot(cc, sf_ref[...].astype(BF16))
                        + _dot(sp_e, e2b_ref[0, 0]) * _dot(cc, sbe_ref[c]))
        sf_ref[...] = sf_ref[...] * edge_ref[c] + _dot(bt_ref[c], xwf_ref[pl.ds(r0, q), :])
        g = _dot(cc, bt_ref[c])
        col = pltpu.roll(col_ref[0, pl.ds(r0, q), :], LANES - lane0, 1)
        row = row_ref[0, 0, c]
        for j in range(SSD_HPG // 2):
            lanes = slice(j * LANES, (j + 1) * LANES)
            xf = xs_ref[pl.ds(r0, q), lanes].astype(F32)
            x_diag = jnp.concatenate([jnp.where(left, xf, 0.0), jnp.where(left, 0.0, xf)], axis=0).astype(BF16)
            ms = []
            for h in (2 * j, 2 * j + 1):
                cs_l = jnp.take_along_axis(col, jnp.where(lower, h, 8 + h), axis=1, mode="promise_in_bounds")
                arg = cs_l - jnp.where(lower, row[h:h + 1, :], row[8 + h:9 + h, :])
                p = jnp.exp(arg) + jnp.where(eye, row[16 + h:17 + h, :], 0.0)
                ms.append((g * p).astype(BF16))
            y = _dot(jnp.concatenate(ms, axis=1), x_diag) + yo_ref[slot, :, lanes] + dexp_ref[0, :, lanes] * xf
            y_ref[0, pl.ds(r0, q), lanes] = y.astype(y_ref.dtype)

    def fstep(i, carry):
        for slot in range(fu):
            chunk(fu * i + slot, slot)
        return carry

    lax.fori_loop(0, nc // fu, fstep, 0)
    sfin_ref[0, 0, 0] = sf_ref[...]


def _expanders():
    e = np.zeros((2, SSD_GROUPS, LANES, GROUP_W), np.float32)
    for d in range(2):
        for g in range(SSD_GROUPS):
            for s in range(2):
                for j in range(SSD_HPG):
                    e[d, g, (d * SSD_GROUPS + g) * 16 + s * 8 + j, j * SSD_HEAD_DIM:(j + 1) * SSD_HEAD_DIM] = 1.0
    return jnp.asarray(e, BF16)


def ssd_mixer(proj, xbc_col0, dt_raw, conv_w, conv_b, dt_bias, a_log, d_skip, s0, q):
    b, l, _ = proj.shape
    nc = l // q
    gn = SSD_GROUPS
    bias = jnp.zeros((1, LANES), F32).at[0, :2 * SSD_HEADS].set(dt_bias.reshape(-1))
    a = jnp.zeros((1, LANES), F32).at[0, :2 * SSD_HEADS].set(-jnp.exp(a_log.reshape(-1)))
    colg, rowg, sp = dt_prep(dt_raw, bias, a, q)
    dexp = jnp.repeat(d_skip.astype(F32), SSD_HEAD_DIM).reshape(gn, 1, GROUP_W)
    e2 = _expanders()

    cw = conv_w.astype(F32)
    cb = conv_b.astype(F32).reshape(1, -1)
    nb = SSD_INNER
    cwx, cwb, cwc = cw[:, :nb], cw[:, nb:nb + gn * SSD_STATE], cw[:, nb + gn * SSD_STATE:]
    cbx, cbb, cbc = cb[:, :nb], cb[:, nb:nb + gn * SSD_STATE], cb[:, nb + gn * SSD_STATE:]

    xo = xbc_col0 // GROUP_W
    bo = (xbc_col0 + SSD_INNER) // SSD_STATE
    co = bo + gn
    st_spec = pl.BlockSpec((1, 1, 2, SSD_STATE, GROUP_W), lambda i, g: (i, g, 0, 0, 0))
    y, sfin = pl.pallas_call(
        functools.partial(_ssd_kernel, l=l, q=q),
        grid=(b, gn),
        in_specs=[pl.BlockSpec((1, l, GROUP_W), lambda i, g: (i, 0, xo + g)),
                  pl.BlockSpec((1, l, SSD_STATE), lambda i, g: (i, 0, bo + g)),
                  pl.BlockSpec((1, l, SSD_STATE), lambda i, g: (i, 0, co + g)),
                  pl.BlockSpec((3, GROUP_W), lambda i, g: (0, g)),
                  pl.BlockSpec((3, SSD_STATE), lambda i, g: (0, g)),
                  pl.BlockSpec((3, SSD_STATE), lambda i, g: (0, g)),
                  pl.BlockSpec((1, GROUP_W), lambda i, g: (0, g)),
                  pl.BlockSpec((1, SSD_STATE), lambda i, g: (0, g)),
                  pl.BlockSpec((1, SSD_STATE), lambda i, g: (0, g)),
                  pl.BlockSpec((1, l, LANES), lambda i, g: (i, 0, 0)),
                  pl.BlockSpec((1, 1, nc, 24, q), lambda i, g: (i, g, 0, 0, 0)),
                  pl.BlockSpec((1, l, 2 * LANES), lambda i, g: (i, 0, 0)),
                  pl.BlockSpec((1, 1, LANES, GROUP_W), lambda i, g: (0, g, 0, 0)),
                  pl.BlockSpec((1, 1, LANES, GROUP_W), lambda i, g: (1, g, 0, 0)),
                  pl.BlockSpec((1, 1, GROUP_W), lambda i, g: (g, 0, 0)),
                  st_spec],
        out_specs=[pl.BlockSpec((1, l, GROUP_W), lambda i, g: (i, 0, g)), st_spec],
        out_shape=[jax.ShapeDtypeStruct((b, l, SSD_INNER), BF16),
                   jax.ShapeDtypeStruct((b, gn, 2, SSD_STATE, GROUP_W), F32)],
        scratch_shapes=[pltpu.VMEM((l, GROUP_W), BF16),
                        pltpu.VMEM((l, SSD_STATE), BF16),
                        pltpu.VMEM((nc, SSD_STATE, q), BF16),
                        pltpu.VMEM((l, GROUP_W), BF16),
                        pltpu.VMEM((nc, SSD_STATE, GROUP_W), BF16),
                        pltpu.VMEM((nc, 1, GROUP_W), F32),
                        pltpu.VMEM((SSD_STATE, GROUP_W), F32),
                        pltpu.VMEM((SSD_STATE, GROUP_W), F32),
                        pltpu.VMEM((4 if nc % 4 == 0 else (2 if nc % 2 == 0 else 1), q, GROUP_W), F32)],
        compiler_params=_cparams(("parallel", "parallel")),
        name="ssd",
    )(proj, proj, proj, cwx, cwb, cwc, cbx, cbb, cbc, colg, rowg, sp, e2, e2, dexp, s0)
    return y, sfin


def _mixout_kernel(y_ref, z_ref, gb_ref, gc_ref, hv_ref, g0_ref, g1_ref, x_ref, gate_ref,
                   ng_ref, bg_ref, scw_ref, wssd_ref, wsc_ref, wo_ref, o_ref, *, tm, period):
    ns = 2 if (tm // 2) % period == 0 else 1
    ts = tm // ns
    pos = lax.broadcasted_iota(jnp.int32, (ts, 1), 0) % period
    w = scw_ref[...]
    bg = bg_ref[...]
    for s in range(ns):
        r = slice(s * ts, (s + 1) * ts)
        yz = y_ref[0, r, :].astype(F32) * _silu_t(z_ref[0, r, :].astype(F32))
        ms = jnp.mean(yz * yz, axis=-1, keepdims=True)
        yn = (yz * lax.rsqrt(ms + EPS) * ng_ref[...]).astype(BF16)
        y_ssd = _dot(yn, wssd_ref[...])

        u = gc_ref[0, r, :].astype(F32) * hv_ref[0, r, :].astype(F32)
        u_prev = jnp.where(pos == 0, 0.0, pltpu.roll(u, 1, 0))
        u_next = jnp.where(pos == period - 1, 0.0, pltpu.roll(u, ts - 1, 0))
        v = w[0:1] * u_prev + w[1:2] * u + w[2:3] * u_next
        y_sc = _dot((gb_ref[0, r, :].astype(F32) * v).astype(BF16), wsc_ref[...])

        g0 = _sigmoid_t(g0_ref[0, r, :].astype(F32) + bg[:, :D_MODEL])
        g1 = _sigmoid_t(g1_ref[0, r, :].astype(F32) + bg[:, D_MODEL:])
        out = _dot((g0 * y_ssd + g1 * y_sc).astype(BF16), wo_ref[...])
        o_ref[0, r, :] = x_ref[0, r, :] + gate_ref[0] * out


def mixer_out(y, proj, x, gate, norm_g, b_gate, sc_conv_w, w_ssd, w_sc, w_o, tm, period):
    b, l, d = x.shape
    pc = lambda k: pl.BlockSpec((1, tm, d), lambda i, m, k=k: (i, m, k))
    full = lambda shp: pl.BlockSpec(shp, lambda i, m: (0,) * len(shp))
    return pl.pallas_call(
        functools.partial(_mixout_kernel, tm=tm, period=period),
        grid=(b, l // tm),
        in_specs=[pl.BlockSpec((1, tm, SSD_INNER), lambda i, m: (i, m, 0)),
                  pl.BlockSpec((1, tm, SSD_INNER), lambda i, m: (i, m, 0)),
                  pc(5), pc(6), pc(7), pc(8), pc(9),
                  pl.BlockSpec((1, tm, d), lambda i, m: (i, m, 0)),
                  pl.BlockSpec((1, 1, d), lambda i, m: (i, 0, 0)),
                  full((1, SSD_INNER)), full((1, 2 * d)), full((3, SC_WIDTH)),
                  full((SSD_INNER, d)), full((SC_WIDTH, d)), full((d, d))],
        out_specs=pl.BlockSpec((1, tm, d), lambda i, m: (i, m, 0)),
        out_shape=jax.ShapeDtypeStruct((b, l, d), F32),
        compiler_params=_cparams(("parallel", "parallel")),
        name="mixer_out",
    )(y, proj, proj, proj, proj, proj, proj, x, gate,
      norm_g.reshape(1, -1), b_gate.reshape(1, -1), sc_conv_w, w_ssd, w_sc, w_o)


def _ffn_kernel(x_ref, g_ref, sh_ref, sc_ref, gate_ref, w1_ref, w3_ref, w2_ref, o_ref, *, nf):
    x = x_ref[0]
    hb = _norm_mod(x, g_ref[...], sh_ref[0], sc_ref[0]).astype(BF16)
    fw = w1_ref.shape[1] // nf
    acc = None
    for k in range(nf):
        a = _dot(hb, w1_ref[:, k * fw:(k + 1) * fw])
        bb = _dot(hb, w3_ref[:, k * fw:(k + 1) * fw])
        part = _dot((_silu(a) * bb).astype(BF16), w2_ref[k * fw:(k + 1) * fw, :])
        acc = part if acc is None else acc + part
    o_ref[0] = x + gate_ref[0] * acc


def ffn_dense(x, g, shift, scale, gate, w1, w3, w2, tm):
    b, l, d = x.shape
    f = w1.shape[1]
    vec = pl.BlockSpec((1, 1, d), lambda i, m: (i, 0, 0))
    const = lambda shp: pl.BlockSpec(shp, lambda i, m: (0, 0), pipeline_mode=pl.Buffered(1))
    return pl.pallas_call(
        functools.partial(_ffn_kernel, nf=2),
        grid=(b, l // tm),
        in_specs=[pl.BlockSpec((1, tm, d), lambda i, m: (i, m, 0)),
                  pl.BlockSpec((1, d), lambda i, m: (0, 0)),
                  vec, vec, vec, const((d, f)), const((d, f)), const((f, d))],
        out_specs=pl.BlockSpec((1, tm, d), lambda i, m: (i, m, 0)),
        out_shape=jax.ShapeDtypeStruct((b, l, d), F32),
        compiler_params=_cparams(("parallel", "parallel")),
        name="ffn_dense",
    )(x, g.reshape(1, d), shift, scale, gate, w1, w3, w2)


def _router_kernel(x_ref, g_ref, sh_ref, sc_ref, rw_ref, h_ref, route_ref, cnt_ref, run_ref, *, tm):
    @pl.when((pl.program_id(0) == 0) & (pl.program_id(1) == 0))
    def _():
        run_ref[...] = jnp.zeros_like(run_ref)

    h = _norm_mod(x_ref[0], g_ref[...], sh_ref[0], sc_ref[0])
    h_hi, h_lo = _split2(h)
    h_ref[0] = h
    w_hi, w_lo = _split2(rw_ref[...])
    logits = _dot(h_hi, w_hi) + _dot(h_lo, w_hi) + _dot(h_hi, w_lo)
    lane = lax.broadcasted_iota(jnp.int32, (tm, LANES), 1)
    ninf = float("-inf")
    lg = jnp.where(lane < N_EXPERTS, logits, ninf)
    m1 = jnp.max(lg, axis=1, keepdims=True)
    i1 = jnp.min(jnp.where(lg == m1, lane, LANES), axis=1, keepdims=True)
    lg2 = jnp.where(lane == i1, ninf, lg)
    m2 = jnp.max(lg2, axis=1, keepdims=True)
    i2 = jnp.min(jnp.where(lg2 == m2, lane, LANES), axis=1, keepdims=True)
    e2 = jnp.exp(m2 - m1)
    den = 1.0 + e2
    sel1 = jnp.where(lane == i1, 1.0, 0.0)
    sel2 = jnp.where(lane == i2, 1.0, 0.0)
    cnt = sel1 + sel2
    r = lax.broadcasted_iota(jnp.int32, (tm, tm), 0)
    c = lax.broadcasted_iota(jnp.int32, (tm, tm), 1)
    tri = jnp.where(c < r, 1.0, 0.0).astype(BF16)
    base = _dot(tri, cnt.astype(BF16)) + run_ref[0:1, :]
    r1 = jnp.sum(sel1 * base, axis=1, keepdims=True)
    r2 = jnp.sum(sel2 * base, axis=1, keepdims=True)
    vals = (i1.astype(F32), i2.astype(F32), 1.0 / den, e2 / den, r1, r2)
    out = jnp.zeros((tm, LANES), F32)
    for k, v in enumerate(vals):
        out = jnp.where(lane == k, v, out)
    route_ref[0] = out
    new_run = run_ref[...] + jnp.sum(cnt, axis=0, keepdims=True)
    run_ref[...] = new_run
    cnt_ref[...] = new_run


def router(x, g, shift, scale, router_w, tm):
    b, l, d = x.shape
    rw = jnp.zeros((d, LANES), F32).at[:, :N_EXPERTS].set(router_w)
    vec = pl.BlockSpec((1, 1, d), lambda i, m: (i, 0, 0))
    return pl.pallas_call(
        functools.partial(_router_kernel, tm=tm),
        grid=(b, l // tm),
        in_specs=[pl.BlockSpec((1, tm, d), lambda i, m: (i, m, 0)),
                  pl.BlockSpec((1, d), lambda i, m: (0, 0)),
                  vec, vec, pl.BlockSpec((d, LANES), lambda i, m: (0, 0))],
        out_specs=[pl.BlockSpec((1, tm, d), lambda i, m: (i, m, 0)),
                   pl.BlockSpec((1, tm, LANES), lambda i, m: (i, m, 0)),
                   pl.BlockSpec((8, LANES), lambda i, m: (0, 0))],
        out_shape=[jax.ShapeDtypeStruct((b, l, d), F32),
                   jax.ShapeDtypeStruct((b, l, LANES), F32),
                   jax.ShapeDtypeStruct((8, LANES), F32)],
        scratch_shapes=[pltpu.VMEM((8, LANES), F32)],
        compiler_params=_cparams(("arbitrary", "arbitrary")),
        name="router",
    )(x, g.reshape(1, d), shift, scale, rw)


def _pack_bf16_pairs(y):
    k = y.shape[1] // 2
    bits = lax.bitcast_convert_type(y.astype(BF16).astype(F32), jnp.uint32)
    return bits[:, :k] | (bits[:, k:] >> 16)


def _unpack_bf16_pairs(p):
    hi = lax.bitcast_convert_type(p & jnp.uint32(0xFFFF0000), F32)
    lo = lax.bitcast_convert_type(p << 16, F32)
    return jnp.concatenate([hi, lo], axis=1)


def _gffn_kernel(te_ref, nv_ref, x_ref, w1_ref, w3_ref, w2_ref, o_ref, acc_ref, *, nf):
    i = pl.program_id(0)
    f = pl.program_id(1)

    @pl.when(i < nv_ref[0])
    def _():
        x = x_ref[...].astype(BF16)
        a = _dot(x, w1_ref[0])
        bb = _dot(x, w3_ref[0])
        part = _dot((_silu(a) * bb).astype(BF16), w2_ref[0])

        @pl.when(f == 0)
        def _():
            acc_ref[...] = part

        @pl.when(f > 0)
        def _():
            acc_ref[...] += part

        @pl.when(f == nf - 1)
        def _():
            o_ref[...] = _pack_bf16_pairs(acc_ref[...])

    @pl.when((i >= nv_ref[0]) & (f == nf - 1))
    def _():
        o_ref[...] = jnp.zeros_like(o_ref)


def grouped_ffn(xs, tile_expert, n_valid, w1, w3, w2, tm, nf):
    rows, d = xs.shape
    nt = rows // tm
    f = w1.shape[2]
    fw = f // nf
    grid_spec = pltpu.PrefetchScalarGridSpec(
        num_scalar_prefetch=2,
        grid=(nt, nf),
        in_specs=[pl.BlockSpec((tm, d), lambda i, k, te, nv: (i, 0)),
                  pl.BlockSpec((1, d, fw), lambda i, k, te, nv: (te[i], 0, k)),
                  pl.BlockSpec((1, d, fw), lambda i, k, te, nv: (te[i], 0, k)),
                  pl.BlockSpec((1, fw, d), lambda i, k, te, nv: (te[i], k, 0))],
        out_specs=pl.BlockSpec((tm, d // 2), lambda i, k, te, nv: (i, 0)),
        scratch_shapes=[pltpu.VMEM((tm, d), F32)],
    )
    return pl.pallas_call(
        functools.partial(_gffn_kernel, nf=nf),
        grid_spec=grid_spec,
        out_shape=jax.ShapeDtypeStruct((rows, d // 2), jnp.uint32),
        compiler_params=_cparams(("parallel", "arbitrary")),
        name="grouped_ffn",
    )(tile_expert, n_valid, xs, w1, w3, w2)


def _combine_kernel(x_ref, y0_ref, y1_ref, route_ref, gate_ref, fg_ref, o_ref):
    r = route_ref[0]
    moe = r[:, 2:3] * _unpack_bf16_pairs(y0_ref[0, 0]) + r[:, 3:4] * _unpack_bf16_pairs(y1_ref[0, 0])
    xn = x_ref[0] + gate_ref[0] * moe
    ms = jnp.mean(xn * xn, axis=-1, keepdims=True)
    o_ref[0] = xn * lax.rsqrt(ms + EPS) * fg_ref[...]


def combine_final(x, yg, route, gate, final_g, tm):
    b, l, d = x.shape
    return pl.pallas_call(
        _combine_kernel,
        grid=(b, l // tm),
        in_specs=[pl.BlockSpec((1, tm, d), lambda i, m: (i, m, 0)),
                  pl.BlockSpec((1, 1, tm, d // 2), lambda i, m: (0, i, m, 0)),
                  pl.BlockSpec((1, 1, tm, d // 2), lambda i, m: (1, i, m, 0)),
                  pl.BlockSpec((1, tm, LANES), lambda i, m: (i, m, 0)),
                  pl.BlockSpec((1, 1, d), lambda i, m: (i, 0, 0)),
                  pl.BlockSpec((1, d), lambda i, m: (0, 0))],
        out_specs=pl.BlockSpec((1, tm, d), lambda i, m: (i, m, 0)),
        out_shape=jax.ShapeDtypeStruct((b, l, d), F32),
        compiler_params=_cparams(("parallel", "parallel")),
        name="combine_final",
    )(x, yg, yg, route, gate, final_g.reshape(1, d))


SC_CORES = 2
SC_SUBCORES = 16
SC_WORKERS = SC_CORES * SC_SUBCORES
SC_STREAM_BYTES = 256 * 1024
SC_STREAM_ROWS = 128


def _sc_rows(per_worker, d, dtype):
    return min(SC_STREAM_ROWS, SC_STREAM_BYTES // (d * jnp.dtype(dtype).itemsize), per_worker)


def _sc_mesh():
    return plsc.VectorSubcoreMesh(core_axis_name="c", subcore_axis_name="s",
                                  num_cores=SC_CORES, num_subcores=SC_SUBCORES)


def dispatch_rows(h, pos0, pos1, n_rows):
    t, d = h.shape
    per_w = t // SC_WORKERS
    ch = _sc_rows(per_w, d, h.dtype)

    @functools.partial(
        pl.kernel, mesh=_sc_mesh(),
        out_type=jax.ShapeDtypeStruct((n_rows, d), h.dtype),
        scratch_types=[pltpu.VMEM((ch,), jnp.int32), pltpu.VMEM((ch,), jnp.int32),
                       pltpu.VMEM((ch, d), h.dtype), pltpu.SemaphoreType.DMA],
        name="moe_dispatch")
    def scatter(h_hbm, p0_hbm, p1_hbm, out_hbm, i0_v, i1_v, rows_v, sem):
        base = (lax.axis_index("s") * SC_CORES + lax.axis_index("c")) * per_w

        @pl.loop(0, per_w // ch)
        def _(j):
            off = base + j * ch
            pltpu.sync_copy(h_hbm.at[pl.ds(off, ch)], rows_v)
            pltpu.sync_copy(p0_hbm.at[pl.ds(off, ch)], i0_v)
            pltpu.sync_copy(p1_hbm.at[pl.ds(off, ch)], i1_v)
            pltpu.async_copy(rows_v, out_hbm.at[i0_v], sem).wait()
            pltpu.async_copy(rows_v, out_hbm.at[i1_v], sem).wait()

    return scatter(h, pos0, pos1)


def return_rows(ys, idx):
    n = idx.shape[0]
    d = ys.shape[1]
    per_w = n // SC_WORKERS
    ch = _sc_rows(per_w, d, ys.dtype)

    @functools.partial(
        pl.kernel, mesh=_sc_mesh(),
        out_type=jax.ShapeDtypeStruct((n, d), ys.dtype),
        scratch_types=[pltpu.VMEM((ch,), jnp.int32), pltpu.VMEM((ch, d), ys.dtype), pltpu.SemaphoreType.DMA],
        name="moe_return")
    def gather(ys_hbm, idx_hbm, out_hbm, idx_v, rows_v, sem):
        base = (lax.axis_index("s") * SC_CORES + lax.axis_index("c")) * per_w

        @pl.loop(0, per_w // ch)
        def _(j):
            off = base + j * ch
            pltpu.sync_copy(idx_hbm.at[pl.ds(off, ch)], idx_v)
            pltpu.async_copy(ys_hbm.at[idx_v], rows_v, sem).wait()
            pltpu.sync_copy(rows_v, out_hbm.at[pl.ds(off, ch)])

    return gather(ys, idx)


def moe_block(x, g, shift, scale, gate, router_w, w1, w3, w2, final_g):
    b, l, d = x.shape
    t = b * l
    tm = MOE_TM
    h, route, counts = router(x, g, shift, scale, router_w, min(512, l))
    rt = route.reshape(t, LANES)
    cnt = counts[0, :N_EXPERTS].astype(jnp.int32)
    gs = ((cnt + tm - 1) // tm) * tm
    ends = jnp.cumsum(gs)
    offs = ends - gs
    pos0 = offs[rt[:, 0].astype(jnp.int32)] + rt[:, 4].astype(jnp.int32)
    pos1 = offs[rt[:, 1].astype(jnp.int32)] + rt[:, 5].astype(jnp.int32)
    nt = (2 * t) // tm + N_EXPERTS
    n_valid = (ends[-1] // tm).astype(jnp.int32).reshape(1)
    tile = jnp.minimum(jnp.arange(nt, dtype=jnp.int32), n_valid[0] - 1)
    tile_expert = jnp.sum((tile[:, None] >= (ends // tm)[None, :]).astype(jnp.int32), axis=1)
    xs = dispatch_rows(h.reshape(t, d), pos0, pos1, nt * tm)
    ys = grouped_ffn(xs, tile_expert, n_valid, w1, w3, w2, tm, 2)
    yg = return_rows(ys, jnp.concatenate([pos0, pos1])).reshape(2, b, l, d // 2)
    return combine_final(x, yg, route, gate, final_g, min(512, l))


def _in_weights(w_in):
    o1 = SSD_INNER
    o2 = o1 + XBC_WIDTH
    o3 = o2 + 2 * SSD_HEADS
    w_main = jnp.concatenate([w_in[:, :o2], w_in[:, o3:]], axis=1).astype(BF16)
    w_dt = jnp.pad(w_in[:, o2:o3], ((0, 0), (0, LANES - 2 * SSD_HEADS))).astype(BF16)
    return w_main, w_dt


def kernel(x, c, ctx, c_ctx, w_mod, b_mod, norm1_g, norm2_g, w_in, b_gate, ssd_conv_w, ssd_conv_b, ssd_dt_bias, ssd_a_log, ssd_d, ssd_norm_g, w_ssd_out, sc_conv_w, w_sc_out, w_o, ffn_w1, ffn_w3, ffn_w2, router_w, moe_w1, moe_w3, moe_w2, final_g):
    b, l, d = x.shape
    lc = ctx.shape[1]
    depth = w_mod.shape[0]
    cc = jnp.zeros((16, d), F32).at[:b].set(c).at[b].set(c_ctx)
    mod = modulation(cc, w_mod, b_mod)
    zeros_state = jnp.zeros((b, SSD_GROUPS, 2, SSD_STATE, GROUP_W), F32)
    nctx = b * lc
    ctx = ctx.reshape(1, nctx, d)
    tmc = lc * max(1, min(512, nctx) // lc)

    def per_seq(t):
        return t.reshape(b, lc, t.shape[-1])

    for i in range(depth):
        last = i == depth - 1
        mx = mod[i, :b].reshape(b, N_MOD, 1, d)
        mc = mod[i, b].reshape(1, N_MOD, 1, d)
        w_main, w_dt = _in_weights(w_in[i])
        ssd_p = (ssd_conv_w[i], ssd_conv_b[i], ssd_dt_bias[i], ssd_a_log[i], ssd_d[i])
        out_p = (ssd_norm_g[i], b_gate[i], sc_conv_w[i], w_ssd_out[i].astype(BF16),
                 w_sc_out[i].astype(BF16), w_o[i].astype(BF16))

        if last:
            w_xbc = w_main[:, SSD_INNER:SSD_INNER + XBC_WIDTH]
            proj_c, dt_c = in_proj(ctx, norm1_g[i], mc[:, 0], mc[:, 1], w_xbc, w_dt, min(1024, nctx), 1024)
            _, s_ctx = ssd_mixer(per_seq(proj_c), 0, per_seq(dt_c), *ssd_p, zeros_state, SSD_Q)
        else:
            proj_c, dt_c = in_proj(ctx, norm1_g[i], mc[:, 0], mc[:, 1], w_main, w_dt, min(1024, nctx), 2048)
            y_c, s_ctx = ssd_mixer(per_seq(proj_c), SSD_INNER, per_seq(dt_c), *ssd_p, zeros_state, SSD_Q)
            ctx = mixer_out(y_c.reshape(1, nctx, SSD_INNER), proj_c, ctx, mc[:, 2], *out_p, tmc, lc)

        proj_x, dt_x = in_proj(x, norm1_g[i], mx[:, 0], mx[:, 1], w_main, w_dt, min(1024, l), 2048)
        y_x, _ = ssd_mixer(proj_x, SSD_INNER, dt_x, *ssd_p, s_ctx, SSD_Q)
        x = mixer_out(y_x, proj_x, x, mx[:, 2], *out_p, min(512, l), GRID_W)

        j = i // 2
        if i % 2 == 0:
            w1, w3, w2 = ffn_w1[j].astype(BF16), ffn_w3[j].astype(BF16), ffn_w2[j].astype(BF16)
            x = ffn_dense(x, norm2_g[i], mx[:, 3], mx[:, 4], mx[:, 5], w1, w3, w2, min(512, l))
            if not last:
                ctx = ffn_dense(ctx, norm2_g[i], mc[:, 3], mc[:, 4], mc[:, 5], w1, w3, w2, tmc)
        else:
            assert last, "the routed channel mixer is fused with the final norm"
            w1, w3, w2 = moe_w1[j].astype(BF16), moe_w3[j].astype(BF16), moe_w2[j].astype(BF16)
            x = moe_block(x, norm2_g[i], mx[:, 3], mx[:, 4], mx[:, 5], router_w[j], w1, w3, w2, final_g)
    return x
```

```python
import functools

import numpy as np
import jax
import jax.numpy as jnp
from jax import lax
from jax.experimental import pallas as pl
from jax.experimental.pallas import tpu as pltpu
from jax.experimental.pallas import tpu_sc as plsc

F32 = jnp.float32
BF16 = jnp.bfloat16

D_MODEL = 1024
GRID_W = 64
SSD_INNER = 2048
SSD_HEADS = 32
SSD_GROUPS = 4
SSD_HPG = 8
SSD_HEAD_DIM = 64
SSD_STATE = 128
GROUP_W = SSD_HPG * SSD_HEAD_DIM
XBC_WIDTH = SSD_INNER + 2 * SSD_GROUPS * SSD_STATE
SC_WIDTH = 1024
N_MOD = 6
N_EXPERTS = 8
EPS = 1e-6
LOG2E = 1.4426950408889634

LANES = 128
SSD_Q = 128
MOE_TM = 512
ROW_SUBTILES = 2
VMEM_LIMIT = 56 * 1024 * 1024


def _dot(a, b):
    return jnp.dot(a, b, preferred_element_type=F32)


def _sigmoid(v):
    return 1.0 / (1.0 + jnp.exp(-v))


def _silu(v):
    return v * _sigmoid(v)


def _sigmoid_t(v):
    return 0.5 + 0.5 * jnp.tanh(0.5 * v)


def _silu_t(v):
    hv = 0.5 * v
    return hv + hv * jnp.tanh(hv)


def _split2(a):
    hi = a.astype(BF16)
    lo = (a - hi.astype(F32)).astype(BF16)
    return hi, lo


def _split3(a):
    hi = a.astype(BF16)
    r = a - hi.astype(F32)
    mid = r.astype(BF16)
    lo = (r - mid.astype(F32)).astype(BF16)
    return hi, mid, lo


def _norm_mod(x, g, shift, scale):
    ms = jnp.mean(x * x, axis=-1, keepdims=True)
    return (x * lax.rsqrt(ms + EPS) * g) * (1.0 + scale) + shift


def _cparams(sem, vmem=VMEM_LIMIT):
    return pltpu.CompilerParams(dimension_semantics=sem, vmem_limit_bytes=vmem)


def _mod_kernel(c_ref, w_ref, b_ref, o_ref):
    a_hi, a_lo = _split2(_silu(c_ref[...]))
    w_hi, w_lo = _split2(w_ref[0])
    o_ref[0] = _dot(a_hi, w_hi) + _dot(a_lo, w_hi) + _dot(a_hi, w_lo) + b_ref[0]


def modulation(cc, w_mod, b_mod):
    depth, d, n = w_mod.shape
    tn = 1536
    return pl.pallas_call(
        _mod_kernel,
        grid=(depth, n // tn),
        in_specs=[pl.BlockSpec((16, d), lambda i, j: (0, 0)),
                  pl.BlockSpec((1, d, tn), lambda i, j: (i, 0, j)),
                  pl.BlockSpec((1, 1, tn), lambda i, j: (i, 0, j))],
        out_specs=pl.BlockSpec((1, 16, tn), lambda i, j: (i, 0, j)),
        out_shape=jax.ShapeDtypeStruct((depth, 16, n), F32),
        compiler_params=_cparams(("parallel", "parallel")),
        name="modulation",
    )(cc, w_mod, b_mod.reshape(depth, 1, n))


def _inproj_kernel(x_ref, g_ref, sh_ref, sc_ref, w_ref, wdt_ref, o_ref, dt_ref, h_ref):
    @pl.when(pl.program_id(2) == 0)
    def _():
        hb = _norm_mod(x_ref[0], g_ref[...], sh_ref[0], sc_ref[0]).astype(BF16)
        h_ref[...] = hb
        dt_ref[0] = _dot(hb, wdt_ref[...])

    o_ref[0] = _dot(h_ref[...], w_ref[...]).astype(o_ref.dtype)


def in_proj(x, g, shift, scale, w, wdt, tm, tn):
    b, l, d = x.shape
    n = w.shape[1]
    return pl.pallas_call(
        _inproj_kernel,
        grid=(b, l // tm, n // tn),
        in_specs=[pl.BlockSpec((1, tm, d), lambda i, m, j: (i, m, 0)),
                  pl.BlockSpec((1, d), lambda i, m, j: (0, 0)),
                  pl.BlockSpec((1, 1, d), lambda i, m, j: (i, 0, 0)),
                  pl.BlockSpec((1, 1, d), lambda i, m, j: (i, 0, 0)),
                  pl.BlockSpec((d, tn), lambda i, m, j: (0, j)),
                  pl.BlockSpec((d, LANES), lambda i, m, j: (0, 0))],
        out_specs=[pl.BlockSpec((1, tm, tn), lambda i, m, j: (i, m, j)),
                   pl.BlockSpec((1, tm, LANES), lambda i, m, j: (i, m, 0))],
        out_shape=[jax.ShapeDtypeStruct((b, l, n), BF16),
                   jax.ShapeDtypeStruct((b, l, LANES), F32)],
        scratch_shapes=[pltpu.VMEM((tm, d), BF16)],
        compiler_params=_cparams(("parallel", "parallel", "arbitrary")),
        name="in_proj",
    )(x, g.reshape(1, d), shift, scale, w, wdt)


def _dt_kernel(raw_ref, bias_ref, a_ref, pc_ref, ph_ref, plo_ref, col_ref, row_ref, sp_ref, *, q, ch):
    row = lax.broadcasted_iota(jnp.int32, (q, q), 0)
    col = lax.broadcasted_iota(jnp.int32, (q, q), 1)
    tri_l = jnp.where(col <= row, 1.0, 0.0).astype(BF16)
    tri_u = jnp.where(col >= row, 1.0, 0.0).astype(BF16)
    lane = lax.broadcasted_iota(jnp.int32, (q, LANES), 1)
    fwd = lane < SSD_HEADS
    ph, plo = ph_ref[...], plo_ref[...]

    def place_split(t):
        t_hi, t_lo = _split2(t)
        return (_dot(t_hi, ph) + _dot(t_lo, plo)).astype(BF16)

    for k in range(ch):
        v = raw_ref[0, k * q:(k + 1) * q, :] + bias_ref[...]
        dt = jnp.maximum(v, 0.0) + jnp.log1p(jnp.exp(-jnp.abs(v)))
        d1, d2, d3 = _split3(dt * a_ref[...])
        cs_f = _dot(tri_l, d1) + _dot(tri_l, d2) + _dot(tri_l, d3)
        cs_b = _dot(tri_u, d1) + _dot(tri_u, d2) + _dot(tri_u, d3)
        cs = jnp.where(fwd, cs_f, cs_b)
        tot = jnp.where(fwd[0:1], cs_f[q - 1:q, :], cs_b[0:1, :])
        sl = slice(k * q, (k + 1) * q)
        cs2 = cs * LOG2E
        c1, c2, c3 = _split3(cs2)
        pc = pc_ref[...]
        col_ref[0, sl, :] = _dot(c1, pc) + _dot(c2, pc) + _dot(c3, pc)
        r_t = (cs2 - jnp.log(dt) * LOG2E).T
        dt_t = dt.T
        for g in range(SSD_GROUPS):
            lo = g * SSD_HPG
            dt_f = dt_t[lo:lo + 8, :]
            dt_b = dt_t[SSD_HEADS + lo:SSD_HEADS + lo + 8, :]
            row_ref[0, g, k, 0:8, :] = r_t[lo:lo + 8, :]
            row_ref[0, g, k, 8:16, :] = r_t[SSD_HEADS + lo:SSD_HEADS + lo + 8, :]
            row_ref[0, g, k, 16:24, :] = jnp.log(dt_f + dt_b) * LOG2E
        sp_ref[0, sl, 0:LANES] = place_split(dt * jnp.exp(tot - cs))
        sp_ref[0, sl, LANES:2 * LANES] = place_split(jnp.exp(cs))


def _placements():
    pc = np.zeros((LANES, LANES), np.float32)
    ph = np.zeros((LANES, LANES), np.float32)
    plo = np.zeros((LANES, LANES), np.float32)
    for d in range(2):
        for g in range(SSD_GROUPS):
            for j in range(SSD_HPG):
                src = d * SSD_HEADS + g * SSD_HPG + j
                pc[src, g * 16 + d * 8 + j] = 1.0
                ph[src, (d * SSD_GROUPS + g) * 16 + j] = 1.0
                plo[src, (d * SSD_GROUPS + g) * 16 + 8 + j] = 1.0
    return jnp.asarray(pc, BF16), jnp.asarray(ph, BF16), jnp.asarray(plo, BF16)


def dt_prep(raw, bias, a, q):
    b, l, _ = raw.shape
    nc = l // q
    ch = min(8, nc)
    gn = SSD_GROUPS
    vspec = pl.BlockSpec((1, LANES), lambda i, c: (0, 0))
    pspec = pl.BlockSpec((LANES, LANES), lambda i, c: (0, 0))
    return pl.pallas_call(
        functools.partial(_dt_kernel, q=q, ch=ch),
        grid=(b, nc // ch),
        in_specs=[pl.BlockSpec((1, ch * q, LANES), lambda i, c: (i, c, 0)), vspec, vspec, pspec, pspec, pspec],
        out_specs=[pl.BlockSpec((1, ch * q, LANES), lambda i, c: (i, c, 0)),
                   pl.BlockSpec((1, gn, ch, 24, q), lambda i, c: (i, 0, c, 0, 0)),
                   pl.BlockSpec((1, ch * q, 2 * LANES), lambda i, c: (i, c, 0))],
        out_shape=[jax.ShapeDtypeStruct((b, l, LANES), F32),
                   jax.ShapeDtypeStruct((b, gn, nc, 24, q), F32),
                   jax.ShapeDtypeStruct((b, l, 2 * LANES), BF16)],
        compiler_params=_cparams(("parallel", "parallel")),
        name="dt_prep",
    )(raw, bias, a, *_placements())


def _ssd_kernel(xp_ref, bp_ref, cp_ref, cwx_ref, cwb_ref, cwc_ref, cbx_ref, cbb_ref, cbc_ref,
                col_ref, row_ref, sp_ref, e2f_ref, e2b_ref, dexp_ref, s0_ref,
                y_ref, sfin_ref,
                xs_ref, cc_ref, bt_ref, xwf_ref, sbe_ref, edge_ref, sf_ref, sb_ref, yo_ref, *, l, q):
    nc = l // q
    sr = lax.broadcasted_iota(jnp.int32, (q, q + 32), 0)
    sc = lax.broadcasted_iota(jnp.int32, (q, q + 32), 1)
    shift_prev = jnp.where((sc == sr - 1) | ((sr == 0) & (sc == q + 15)), 1.0, 0.0).astype(BF16)
    shift_next = jnp.where(((sc == sr + 1) & (sc < q)) | ((sr == q - 1) & (sc == q + 16)), 1.0, 0.0).astype(BF16)

    def conv_silu(srcs, w, b, k):
        r0 = pl.multiple_of(k * q, q)
        p0 = pl.multiple_of(jnp.maximum(r0 - 16, 0), 16)
        n0 = pl.multiple_of(jnp.minimum(r0 + q, l - 16), 16)

        def rows_at(start, n):
            parts = [r[0, pl.ds(start, n), ls] for r, ls in srcs]
            return parts[0] if len(parts) == 1 else jnp.concatenate(parts, axis=1)

        blk = rows_at(r0, q)
        before = rows_at(p0, 16)
        after = rows_at(n0, 16)
        before = jnp.where(k > 0, before, jnp.zeros_like(before))
        after = jnp.where(k < nc - 1, after, jnp.zeros_like(after))
        stacked = jnp.concatenate([blk, before, after], axis=0)
        v = (w[0:1] * _dot(shift_prev, stacked) + w[1:2] * blk.astype(F32)
             + w[2:3] * _dot(shift_next, stacked) + b)
        return _silu_t(v)

    all_lanes = slice(None)

    def prep(k):
        r0 = pl.multiple_of(k * q, q)
        x = conv_silu(((xp_ref, all_lanes),), cwx_ref[...], cbx_ref[...], k)
        xs_ref[pl.ds(r0, q), :] = x.astype(BF16)
        bc = conv_silu(((bp_ref, all_lanes), (cp_ref, all_lanes)),
                       jnp.concatenate([cwb_ref[...], cwc_ref[...]], axis=1),
                       jnp.concatenate([cbb_ref[...], cbc_ref[...]], axis=1), k)
        b_t = bc[:, :SSD_STATE].T.astype(BF16)
        bt_ref[k] = b_t
        cc_ref[pl.ds(r0, q), :] = bc[:, SSD_STATE:].astype(BF16)
        sp_w = sp_ref[0, pl.ds(r0, q), 0:LANES]
        xwf_ref[pl.ds(r0, q), :] = (x * _dot(sp_w, e2f_ref[0, 0])).astype(BF16)
        xw_b = (x * _dot(sp_w, e2b_ref[0, 0])).astype(BF16)
        e_last = sp_ref[0, pl.ds(pl.multiple_of(r0 + q - 16, 16), 16), LANES:2 * LANES]
        e_first = sp_ref[0, pl.ds(r0, 16), LANES:2 * LANES]
        edge_ref[k] = _dot(e_last, e2f_ref[0, 0])[15:16, :]
        edge_b = _dot(e_first, e2b_ref[0, 0])[0:1, :]
        return b_t, xw_b, edge_b

    sb_ref[...] = s0_ref[0, 0, 1]

    def bstep(i, carry):
        c = nc - 1 - i
        b_t, xw_b, edge_b = prep(c)
        sbe_ref[c] = sb_ref[...].astype(BF16)
        sb_ref[...] = sb_ref[...] * edge_b + _dot(b_t, xw_b)
        return carry

    lax.fori_loop(0, nc, bstep, 0, unroll=8)
    sfin_ref[0, 0, 1] = sb_ref[...]

    sf_ref[...] = s0_ref[0, 0, 0]
    li = lax.broadcasted_iota(jnp.int32, (q, q), 0)
    si = lax.broadcasted_iota(jnp.int32, (q, q), 1)
    lower = si <= li
    eye = si == li
    left = lax.broadcasted_iota(jnp.int32, (q, LANES), 1) < SSD_HEAD_DIM
    fu = yo_ref.shape[0]
    lane0 = 16 * pl.program_id(1)

    def chunk(c, slot):
        r0 = pl.multiple_of(c * q, q)
        cc = cc_ref[pl.ds(r0, q), :]
        sp_e = sp_ref[0, pl.ds(r0, q), LANES:2 * LANES]
        yo_ref[slot] = (_dot(sp_e, e2f_ref[0, 0]) * _dot(cc, sf_ref[...].astype(BF16))
                        + _dot(sp_e, e2b_ref[0, 0]) * _dot(cc, sbe_ref[c]))
        sf_ref[...] = sf_ref[...] * edge_ref[c] + _dot(bt_ref[c], xwf_ref[pl.ds(r0, q), :])
        g = _dot(cc, bt_ref[c])
        col = pltpu.roll(col_ref[0, pl.ds(r0, q), :], LANES - lane0, 1)
        row = row_ref[0, 0, c]
        for j in range(SSD_HPG // 2):
            lanes = slice(j * LANES, (j + 1) * LANES)
            xf = xs_ref[pl.ds(r0, q), lanes].astype(F32)
            x_diag = jnp.concatenate([jnp.where(left, xf, 0.0), jnp.where(left, 0.0, xf)], axis=0).astype(BF16)
            ms = []
            for h in (2 * j, 2 * j + 1):
                cs_l = jnp.take_along_axis(col, jnp.where(lower, h, 8 + h), axis=1, mode="promise_in_bounds")
                arg = cs_l - jnp.where(lower, row[h:h + 1, :], row[8 + h:9 + h, :])
                p = jnp.exp2(jnp.where(eye, row[16 + h:17 + h, :], arg))
                ms.append((g * p).astype(BF16))
            y = _dot(jnp.concatenate(ms, axis=1), x_diag) + yo_ref[slot, :, lanes] + dexp_ref[0, :, lanes] * xf
            y_ref[0, pl.ds(r0, q), lanes] = y.astype(y_ref.dtype)

    def fstep(i, carry):
        for slot in range(fu):
            chunk(fu * i + slot, slot)
        return carry

    lax.fori_loop(0, nc // fu, fstep, 0)
    sfin_ref[0, 0, 0] = sf_ref[...]


def _expanders():
    e = np.zeros((2, SSD_GROUPS, LANES, GROUP_W), np.float32)
    for d in range(2):
        for g in range(SSD_GROUPS):
            for s in range(2):
                for j in range(SSD_HPG):
                    e[d, g, (d * SSD_GROUPS + g) * 16 + s * 8 + j, j * SSD_HEAD_DIM:(j + 1) * SSD_HEAD_DIM] = 1.0
    return jnp.asarray(e, BF16)


def ssd_mixer(proj, xbc_col0, dt_raw, conv_w, conv_b, dt_bias, a_log, d_skip, s0, q):
    b, l, _ = proj.shape
    nc = l // q
    gn = SSD_GROUPS
    bias = jnp.zeros((1, LANES), F32).at[0, :2 * SSD_HEADS].set(dt_bias.reshape(-1))
    a = jnp.zeros((1, LANES), F32).at[0, :2 * SSD_HEADS].set(-jnp.exp(a_log.reshape(-1)))
    colg, rowg, sp = dt_prep(dt_raw, bias, a, q)
    dexp = jnp.repeat(d_skip.astype(F32), SSD_HEAD_DIM).reshape(gn, 1, GROUP_W)
    e2 = _expanders()

    cw = conv_w.astype(F32)
    cb = conv_b.astype(F32).reshape(1, -1)
    nb = SSD_INNER
    cwx, cwb, cwc = cw[:, :nb], cw[:, nb:nb + gn * SSD_STATE], cw[:, nb + gn * SSD_STATE:]
    cbx, cbb, cbc = cb[:, :nb], cb[:, nb:nb + gn * SSD_STATE], cb[:, nb + gn * SSD_STATE:]

    fu = max(u for u in (8, 4, 2, 1) if nc % u == 0)
    xo = xbc_col0 // GROUP_W
    bo = (xbc_col0 + SSD_INNER) // SSD_STATE
    co = bo + gn
    st_spec = pl.BlockSpec((1, 1, 2, SSD_STATE, GROUP_W), lambda i, g: (i, g, 0, 0, 0))
    y, sfin = pl.pallas_call(
        functools.partial(_ssd_kernel, l=l, q=q),
        grid=(b, gn),
        in_specs=[pl.BlockSpec((1, l, GROUP_W), lambda i, g: (i, 0, xo + g)),
                  pl.BlockSpec((1, l, SSD_STATE), lambda i, g: (i, 0, bo + g)),
                  pl.BlockSpec((1, l, SSD_STATE), lambda i, g: (i, 0, co + g)),
                  pl.BlockSpec((3, GROUP_W), lambda i, g: (0, g)),
                  pl.BlockSpec((3, SSD_STATE), lambda i, g: (0, g)),
                  pl.BlockSpec((3, SSD_STATE), lambda i, g: (0, g)),
                  pl.BlockSpec((1, GROUP_W), lambda i, g: (0, g)),
                  pl.BlockSpec((1, SSD_STATE), lambda i, g: (0, g)),
                  pl.BlockSpec((1, SSD_STATE), lambda i, g: (0, g)),
                  pl.BlockSpec((1, l, LANES), lambda i, g: (i, 0, 0)),
                  pl.BlockSpec((1, 1, nc, 24, q), lambda i, g: (i, g, 0, 0, 0)),
                  pl.BlockSpec((1, l, 2 * LANES), lambda i, g: (i, 0, 0)),
                  pl.BlockSpec((1, 1, LANES, GROUP_W), lambda i, g: (0, g, 0, 0)),
                  pl.BlockSpec((1, 1, LANES, GROUP_W), lambda i, g: (1, g, 0, 0)),
                  pl.BlockSpec((1, 1, GROUP_W), lambda i, g: (g, 0, 0)),
                  st_spec],
        out_specs=[pl.BlockSpec((1, l, GROUP_W), lambda i, g: (i, 0, g)), st_spec],
        out_shape=[jax.ShapeDtypeStruct((b, l, SSD_INNER), BF16),
                   jax.ShapeDtypeStruct((b, gn, 2, SSD_STATE, GROUP_W), F32)],
        scratch_shapes=[pltpu.VMEM((l, GROUP_W), BF16),
                        pltpu.VMEM((l, SSD_STATE), BF16),
                        pltpu.VMEM((nc, SSD_STATE, q), BF16),
                        pltpu.VMEM((l, GROUP_W), BF16),
                        pltpu.VMEM((nc, SSD_STATE, GROUP_W), BF16),
                        pltpu.VMEM((nc, 1, GROUP_W), F32),
                        pltpu.VMEM((SSD_STATE, GROUP_W), F32),
                        pltpu.VMEM((SSD_STATE, GROUP_W), F32),
                        pltpu.VMEM((fu, q, GROUP_W), F32)],
        compiler_params=_cparams(("parallel", "parallel")),
        name="ssd",
    )(proj, proj, proj, cwx, cwb, cwc, cbx, cbb, cbc, colg, rowg, sp, e2, e2, dexp, s0)
    return y, sfin


def _mixout_kernel(y_ref, z_ref, gb_ref, gc_ref, hv_ref, g0_ref, g1_ref, x_ref, gate_ref,
                   ng_ref, bg_ref, scw_ref, wssd_ref, wsc_ref, wo_ref, o_ref, *, tm, period):
    ns = 2 if (tm // 2) % period == 0 else 1
    ts = tm // ns
    pos = lax.broadcasted_iota(jnp.int32, (ts, 1), 0) % period
    w = scw_ref[...]
    bg = bg_ref[...]
    for s in range(ns):
        r = slice(s * ts, (s + 1) * ts)
        yz = y_ref[0, r, :].astype(F32) * _silu_t(z_ref[0, r, :].astype(F32))
        ms = jnp.mean(yz * yz, axis=-1, keepdims=True)
        yn = (yz * lax.rsqrt(ms + EPS) * ng_ref[...]).astype(BF16)
        y_ssd = _dot(yn, wssd_ref[...])

        u = gc_ref[0, r, :].astype(F32) * hv_ref[0, r, :].astype(F32)
        u_prev = jnp.where(pos == 0, 0.0, pltpu.roll(u, 1, 0))
        u_next = jnp.where(pos == period - 1, 0.0, pltpu.roll(u, ts - 1, 0))
        v = w[0:1] * u_prev + w[1:2] * u + w[2:3] * u_next
        y_sc = _dot((gb_ref[0, r, :].astype(F32) * v).astype(BF16), wsc_ref[...])

        g0 = _sigmoid_t(g0_ref[0, r, :].astype(F32) + bg[:, :D_MODEL])
        g1 = _sigmoid_t(g1_ref[0, r, :].astype(F32) + bg[:, D_MODEL:])
        out = _dot((g0 * y_ssd + g1 * y_sc).astype(BF16), wo_ref[...])
        o_ref[0, r, :] = x_ref[0, r, :] + gate_ref[0] * out


def mixer_out(y, proj, x, gate, norm_g, b_gate, sc_conv_w, w_ssd, w_sc, w_o, tm, period):
    b, l, d = x.shape
    pc = lambda k: pl.BlockSpec((1, tm, d), lambda i, m, k=k: (i, m, k))
    full = lambda shp: pl.BlockSpec(shp, lambda i, m: (0,) * len(shp))
    return pl.pallas_call(
        functools.partial(_mixout_kernel, tm=tm, period=period),
        grid=(b, l // tm),
        in_specs=[pl.BlockSpec((1, tm, SSD_INNER), lambda i, m: (i, m, 0)),
                  pl.BlockSpec((1, tm, SSD_INNER), lambda i, m: (i, m, 0)),
                  pc(5), pc(6), pc(7), pc(8), pc(9),
                  pl.BlockSpec((1, tm, d), lambda i, m: (i, m, 0)),
                  pl.BlockSpec((1, 1, d), lambda i, m: (i, 0, 0)),
                  full((1, SSD_INNER)), full((1, 2 * d)), full((3, SC_WIDTH)),
                  full((SSD_INNER, d)), full((SC_WIDTH, d)), full((d, d))],
        out_specs=pl.BlockSpec((1, tm, d), lambda i, m: (i, m, 0)),
        out_shape=jax.ShapeDtypeStruct((b, l, d), F32),
        compiler_params=_cparams(("parallel", "parallel")),
        name="mixer_out",
    )(y, proj, proj, proj, proj, proj, proj, x, gate,
      norm_g.reshape(1, -1), b_gate.reshape(1, -1), sc_conv_w, w_ssd, w_sc, w_o)


def _ffn_kernel(x_ref, g_ref, sh_ref, sc_ref, gate_ref, w1_ref, w3_ref, w2_ref, o_ref, *, nf):
    fw = w1_ref.shape[1] // nf
    ts = x_ref.shape[1] // ROW_SUBTILES
    for s in range(ROW_SUBTILES):
        r = slice(s * ts, (s + 1) * ts)
        x = x_ref[0, r, :]
        hb = _norm_mod(x, g_ref[...], sh_ref[0], sc_ref[0]).astype(BF16)
        acc = None
        for k in range(nf):
            a = _dot(hb, w1_ref[:, k * fw:(k + 1) * fw])
            bb = _dot(hb, w3_ref[:, k * fw:(k + 1) * fw])
            part = _dot((_silu_t(a) * bb).astype(BF16), w2_ref[k * fw:(k + 1) * fw, :])
            acc = part if acc is None else acc + part
        o_ref[0, r, :] = x + gate_ref[0] * acc


def ffn_dense(x, g, shift, scale, gate, w1, w3, w2, tm):
    b, l, d = x.shape
    f = w1.shape[1]
    vec = pl.BlockSpec((1, 1, d), lambda i, m: (i, 0, 0))
    const = lambda shp: pl.BlockSpec(shp, lambda i, m: (0, 0), pipeline_mode=pl.Buffered(1))
    return pl.pallas_call(
        functools.partial(_ffn_kernel, nf=2),
        grid=(b, l // tm),
        in_specs=[pl.BlockSpec((1, tm, d), lambda i, m: (i, m, 0)),
                  pl.BlockSpec((1, d), lambda i, m: (0, 0)),
                  vec, vec, vec, const((d, f)), const((d, f)), const((f, d))],
        out_specs=pl.BlockSpec((1, tm, d), lambda i, m: (i, m, 0)),
        out_shape=jax.ShapeDtypeStruct((b, l, d), F32),
        compiler_params=_cparams(("parallel", "parallel")),
        name="ffn_dense",
    )(x, g.reshape(1, d), shift, scale, gate, w1, w3, w2)


def _router_kernel(x_ref, g_ref, sh_ref, sc_ref, rw_ref, h_ref, route_ref, cnt_ref, run_ref, *, tm):
    @pl.when((pl.program_id(0) == 0) & (pl.program_id(1) == 0))
    def _():
        run_ref[...] = jnp.zeros_like(run_ref)

    h = _norm_mod(x_ref[0], g_ref[...], sh_ref[0], sc_ref[0])
    h_hi, h_lo = _split2(h)
    h_ref[0] = h
    w_hi, w_lo = _split2(rw_ref[...])
    logits = _dot(h_hi, w_hi) + _dot(h_lo, w_hi) + _dot(h_hi, w_lo)
    lane = lax.broadcasted_iota(jnp.int32, (tm, LANES), 1)
    ninf = float("-inf")
    lg = jnp.where(lane < N_EXPERTS, logits, ninf)
    m1 = jnp.max(lg, axis=1, keepdims=True)
    i1 = jnp.min(jnp.where(lg == m1, lane, LANES), axis=1, keepdims=True)
    lg2 = jnp.where(lane == i1, ninf, lg)
    m2 = jnp.max(lg2, axis=1, keepdims=True)
    i2 = jnp.min(jnp.where(lg2 == m2, lane, LANES), axis=1, keepdims=True)
    e2 = jnp.exp(m2 - m1)
    den = 1.0 + e2
    sel1 = jnp.where(lane == i1, 1.0, 0.0)
    sel2 = jnp.where(lane == i2, 1.0, 0.0)
    cnt = sel1 + sel2
    r = lax.broadcasted_iota(jnp.int32, (tm, tm), 0)
    c = lax.broadcasted_iota(jnp.int32, (tm, tm), 1)
    tri = jnp.where(c < r, 1.0, 0.0).astype(BF16)
    base = _dot(tri, cnt.astype(BF16)) + run_ref[0:1, :]
    r1 = jnp.sum(sel1 * base, axis=1, keepdims=True)
    r2 = jnp.sum(sel2 * base, axis=1, keepdims=True)
    vals = (i1.astype(F32), i2.astype(F32), 1.0 / den, e2 / den, r1, r2)
    out = jnp.zeros((tm, LANES), F32)
    for k, v in enumerate(vals):
        out = jnp.where(lane == k, v, out)
    route_ref[0] = out
    new_run = run_ref[...] + jnp.sum(cnt, axis=0, keepdims=True)
    run_ref[...] = new_run
    cnt_ref[...] = new_run


def router(x, g, shift, scale, router_w, tm):
    b, l, d = x.shape
    rw = jnp.zeros((d, LANES), F32).at[:, :N_EXPERTS].set(router_w)
    vec = pl.BlockSpec((1, 1, d), lambda i, m: (i, 0, 0))
    return pl.pallas_call(
        functools.partial(_router_kernel, tm=tm),
        grid=(b, l // tm),
        in_specs=[pl.BlockSpec((1, tm, d), lambda i, m: (i, m, 0)),
                  pl.BlockSpec((1, d), lambda i, m: (0, 0)),
                  vec, vec, pl.BlockSpec((d, LANES), lambda i, m: (0, 0))],
        out_specs=[pl.BlockSpec((1, tm, d), lambda i, m: (i, m, 0)),
                   pl.BlockSpec((1, tm, LANES), lambda i, m: (i, m, 0)),
                   pl.BlockSpec((8, LANES), lambda i, m: (0, 0))],
        out_shape=[jax.ShapeDtypeStruct((b, l, d), F32),
                   jax.ShapeDtypeStruct((b, l, LANES), F32),
                   jax.ShapeDtypeStruct((8, LANES), F32)],
        scratch_shapes=[pltpu.VMEM((8, LANES), F32)],
        compiler_params=_cparams(("arbitrary", "arbitrary")),
        name="router",
    )(x, g.reshape(1, d), shift, scale, rw)


def _pack_bf16_pairs(y):
    k = y.shape[1] // 2
    bits = lax.bitcast_convert_type(y.astype(BF16).astype(F32), jnp.uint32)
    return bits[:, :k] | (bits[:, k:] >> 16)


def _unpack_bf16_pairs(p):
    hi = lax.bitcast_convert_type(p & jnp.uint32(0xFFFF0000), F32)
    lo = lax.bitcast_convert_type(p << 16, F32)
    return jnp.concatenate([hi, lo], axis=1)


def _gffn_kernel(te_ref, nv_ref, x_ref, w1_ref, w3_ref, w2_ref, o_ref, acc_ref, *, nf):
    i = pl.program_id(0)
    f = pl.program_id(1)

    @pl.when(i < nv_ref[0])
    def _():
        ts = x_ref.shape[0] // ROW_SUBTILES
        parts = []
        for s in range(ROW_SUBTILES):
            x = x_ref[s * ts:(s + 1) * ts, :].astype(BF16)
            a = _dot(x, w1_ref[0])
            bb = _dot(x, w3_ref[0])
            parts.append(_dot((_silu_t(a) * bb).astype(BF16), w2_ref[0]))
        part = jnp.concatenate(parts, axis=0)

        @pl.when(f == 0)
        def _():
            acc_ref[...] = part

        @pl.when(f > 0)
        def _():
            acc_ref[...] += part

        @pl.when(f == nf - 1)
        def _():
            o_ref[...] = _pack_bf16_pairs(acc_ref[...])

    @pl.when((i >= nv_ref[0]) & (f == nf - 1))
    def _():
        o_ref[...] = jnp.zeros_like(o_ref)


def grouped_ffn(xs, tile_expert, n_valid, w1, w3, w2, tm, nf):
    rows, d = xs.shape
    nt = rows // tm
    f = w1.shape[2]
    fw = f // nf
    grid_spec = pltpu.PrefetchScalarGridSpec(
        num_scalar_prefetch=2,
        grid=(nt, nf),
        in_specs=[pl.BlockSpec((tm, d), lambda i, k, te, nv: (i, 0)),
                  pl.BlockSpec((1, d, fw), lambda i, k, te, nv: (te[i], 0, k)),
                  pl.BlockSpec((1, d, fw), lambda i, k, te, nv: (te[i], 0, k)),
                  pl.BlockSpec((1, fw, d), lambda i, k, te, nv: (te[i], k, 0))],
        out_specs=pl.BlockSpec((tm, d // 2), lambda i, k, te, nv: (i, 0)),
        scratch_shapes=[pltpu.VMEM((tm, d), F32)],
    )
    return pl.pallas_call(
        functools.partial(_gffn_kernel, nf=nf),
        grid_spec=grid_spec,
        out_shape=jax.ShapeDtypeStruct((rows, d // 2), jnp.uint32),
        compiler_params=_cparams(("parallel", "arbitrary")),
        name="grouped_ffn",
    )(tile_expert, n_valid, xs, w1, w3, w2)


def _combine_kernel(x_ref, y0_ref, y1_ref, route_ref, gate_ref, fg_ref, o_ref):
    r = route_ref[0]
    moe = r[:, 2:3] * _unpack_bf16_pairs(y0_ref[0, 0]) + r[:, 3:4] * _unpack_bf16_pairs(y1_ref[0, 0])
    xn = x_ref[0] + gate_ref[0] * moe
    ms = jnp.mean(xn * xn, axis=-1, keepdims=True)
    o_ref[0] = xn * lax.rsqrt(ms + EPS) * fg_ref[...]


def combine_final(x, yg, route, gate, final_g, tm):
    b, l, d = x.shape
    return pl.pallas_call(
        _combine_kernel,
        grid=(b, l // tm),
        in_specs=[pl.BlockSpec((1, tm, d), lambda i, m: (i, m, 0)),
                  pl.BlockSpec((1, 1, tm, d // 2), lambda i, m: (0, i, m, 0)),
                  pl.BlockSpec((1, 1, tm, d // 2), lambda i, m: (1, i, m, 0)),
                  pl.BlockSpec((1, tm, LANES), lambda i, m: (i, m, 0)),
                  pl.BlockSpec((1, 1, d), lambda i, m: (i, 0, 0)),
                  pl.BlockSpec((1, d), lambda i, m: (0, 0))],
        out_specs=pl.BlockSpec((1, tm, d), lambda i, m: (i, m, 0)),
        out_shape=jax.ShapeDtypeStruct((b, l, d), F32),
        compiler_params=_cparams(("parallel", "parallel")),
        name="combine_final",
    )(x, yg, yg, route, gate, final_g.reshape(1, d))


SC_CORES = 2
SC_SUBCORES = 16
SC_WORKERS = SC_CORES * SC_SUBCORES
SC_STREAM_BYTES = 256 * 1024
SC_STREAM_ROWS = 128


def _sc_rows(per_worker, d, dtype):
    return min(SC_STREAM_ROWS, SC_STREAM_BYTES // (d * jnp.dtype(dtype).itemsize), per_worker)


def _sc_mesh():
    return plsc.VectorSubcoreMesh(core_axis_name="c", subcore_axis_name="s",
                                  num_cores=SC_CORES, num_subcores=SC_SUBCORES)


def dispatch_rows(h, pos0, pos1, n_rows):
    t, d = h.shape
    per_w = t // SC_WORKERS
    ch = _sc_rows(per_w, d, h.dtype)

    @functools.partial(
        pl.kernel, mesh=_sc_mesh(),
        out_type=jax.ShapeDtypeStruct((n_rows, d), h.dtype),
        scratch_types=[pltpu.VMEM((ch,), jnp.int32), pltpu.VMEM((ch,), jnp.int32),
                       pltpu.VMEM((ch, d), h.dtype), pltpu.SemaphoreType.DMA],
        name="moe_dispatch")
    def scatter(h_hbm, p0_hbm, p1_hbm, out_hbm, i0_v, i1_v, rows_v, sem):
        base = (lax.axis_index("s") * SC_CORES + lax.axis_index("c")) * per_w

        @pl.loop(0, per_w // ch)
        def _(j):
            off = base + j * ch
            pltpu.sync_copy(h_hbm.at[pl.ds(off, ch)], rows_v)
            pltpu.sync_copy(p0_hbm.at[pl.ds(off, ch)], i0_v)
            pltpu.sync_copy(p1_hbm.at[pl.ds(off, ch)], i1_v)
            pltpu.async_copy(rows_v, out_hbm.at[i0_v], sem).wait()
            pltpu.async_copy(rows_v, out_hbm.at[i1_v], sem).wait()

    return scatter(h, pos0, pos1)


def return_rows(ys, idx):
    n = idx.shape[0]
    d = ys.shape[1]
    per_w = n // SC_WORKERS
    ch = _sc_rows(per_w, d, ys.dtype)

    @functools.partial(
        pl.kernel, mesh=_sc_mesh(),
        out_type=jax.ShapeDtypeStruct((n, d), ys.dtype),
        scratch_types=[pltpu.VMEM((ch,), jnp.int32), pltpu.VMEM((ch, d), ys.dtype), pltpu.SemaphoreType.DMA],
        name="moe_return")
    def gather(ys_hbm, idx_hbm, out_hbm, idx_v, rows_v, sem):
        base = (lax.axis_index("s") * SC_CORES + lax.axis_index("c")) * per_w

        @pl.loop(0, per_w // ch)
        def _(j):
            off = base + j * ch
            pltpu.sync_copy(idx_hbm.at[pl.ds(off, ch)], idx_v)
            pltpu.async_copy(ys_hbm.at[idx_v], rows_v, sem).wait()
            pltpu.sync_copy(rows_v, out_hbm.at[pl.ds(off, ch)])

    return gather(ys, idx)


def moe_block(x, g, shift, scale, gate, router_w, w1, w3, w2, final_g):
    b, l, d = x.shape
    t = b * l
    tm = MOE_TM
    h, route, counts = router(x, g, shift, scale, router_w, min(512, l))
    rt = route.reshape(t, LANES)
    cnt = counts[0, :N_EXPERTS].astype(jnp.int32)
    gs = ((cnt + tm - 1) // tm) * tm
    ends = jnp.cumsum(gs)
    offs = ends - gs
    pos0 = offs[rt[:, 0].astype(jnp.int32)] + rt[:, 4].astype(jnp.int32)
    pos1 = offs[rt[:, 1].astype(jnp.int32)] + rt[:, 5].astype(jnp.int32)
    nt = (2 * t) // tm + N_EXPERTS
    n_valid = (ends[-1] // tm).astype(jnp.int32).reshape(1)
    tile = jnp.minimum(jnp.arange(nt, dtype=jnp.int32), n_valid[0] - 1)
    tile_expert = jnp.sum((tile[:, None] >= (ends // tm)[None, :]).astype(jnp.int32), axis=1)
    xs = dispatch_rows(h.reshape(t, d), pos0, pos1, nt * tm)
    ys = grouped_ffn(xs, tile_expert, n_valid, w1, w3, w2, tm, 2)
    yg = return_rows(ys, jnp.concatenate([pos0, pos1])).reshape(2, b, l, d // 2)
    return combine_final(x, yg, route, gate, final_g, min(512, l))


def _in_weights(w_in):
    o1 = SSD_INNER
    o2 = o1 + XBC_WIDTH
    o3 = o2 + 2 * SSD_HEADS
    w_main = jnp.concatenate([w_in[:, :o2], w_in[:, o3:]], axis=1).astype(BF16)
    w_dt = jnp.pad(w_in[:, o2:o3], ((0, 0), (0, LANES - 2 * SSD_HEADS))).astype(BF16)
    return w_main, w_dt


def kernel(x, c, ctx, c_ctx, w_mod, b_mod, norm1_g, norm2_g, w_in, b_gate, ssd_conv_w, ssd_conv_b, ssd_dt_bias, ssd_a_log, ssd_d, ssd_norm_g, w_ssd_out, sc_conv_w, w_sc_out, w_o, ffn_w1, ffn_w3, ffn_w2, router_w, moe_w1, moe_w3, moe_w2, final_g):
    b, l, d = x.shape
    lc = ctx.shape[1]
    depth = w_mod.shape[0]
    cc = jnp.zeros((16, d), F32).at[:b].set(c).at[b].set(c_ctx)
    mod = modulation(cc, w_mod, b_mod)
    zeros_state = jnp.zeros((b, SSD_GROUPS, 2, SSD_STATE, GROUP_W), F32)
    nctx = b * lc
    ctx = ctx.reshape(1, nctx, d)
    tmc = lc * max(1, min(512, nctx) // lc)

    def per_seq(t):
        return t.reshape(b, lc, t.shape[-1])

    for i in range(depth):
        last = i == depth - 1
        mx = mod[i, :b].reshape(b, N_MOD, 1, d)
        mc = mod[i, b].reshape(1, N_MOD, 1, d)
        w_main, w_dt = _in_weights(w_in[i])
        ssd_p = (ssd_conv_w[i], ssd_conv_b[i], ssd_dt_bias[i], ssd_a_log[i], ssd_d[i])
        out_p = (ssd_norm_g[i], b_gate[i], sc_conv_w[i], w_ssd_out[i].astype(BF16),
                 w_sc_out[i].astype(BF16), w_o[i].astype(BF16))

        if last:
            w_xbc = w_main[:, SSD_INNER:SSD_INNER + XBC_WIDTH]
            proj_c, dt_c = in_proj(ctx, norm1_g[i], mc[:, 0], mc[:, 1], w_xbc, w_dt, min(1024, nctx), 1024)
            _, s_ctx = ssd_mixer(per_seq(proj_c), 0, per_seq(dt_c), *ssd_p, zeros_state, SSD_Q)
        else:
            proj_c, dt_c = in_proj(ctx, norm1_g[i], mc[:, 0], mc[:, 1], w_main, w_dt, min(1024, nctx), 2048)
            y_c, s_ctx = ssd_mixer(per_seq(proj_c), SSD_INNER, per_seq(dt_c), *ssd_p, zeros_state, SSD_Q)
            ctx = mixer_out(y_c.reshape(1, nctx, SSD_INNER), proj_c, ctx, mc[:, 2], *out_p, tmc, lc)

        proj_x, dt_x = in_proj(x, norm1_g[i], mx[:, 0], mx[:, 1], w_main, w_dt, min(1024, l), 2048)
        y_x, _ = ssd_mixer(proj_x, SSD_INNER, dt_x, *ssd_p, s_ctx, SSD_Q)
        x = mixer_out(y_x, proj_x, x, mx[:, 2], *out_p, min(512, l), GRID_W)

        j = i // 2
        if i % 2 == 0:
            w1, w3, w2 = ffn_w1[j].astype(BF16), ffn_w3[j].astype(BF16), ffn_w2[j].astype(BF16)
            x = ffn_dense(x, norm2_g[i], mx[:, 3], mx[:, 4], mx[:, 5], w1, w3, w2, min(512, l))
            if not last:
                ctx = ffn_dense(ctx, norm2_g[i], mc[:, 3], mc[:, 4], mc[:, 5], w1, w3, w2, tmc)
        else:
            assert last, "the routed channel mixer is fused with the final norm"
            w1, w3, w2 = moe_w1[j].astype(BF16), moe_w3[j].astype(BF16), moe_w2[j].astype(BF16)
            x = moe_block(x, norm2_g[i], mx[:, 3], mx[:, 4], mx[:, 5], router_w[j], w1, w3, w2, final_g)
    return x
```

```python
import functools

import numpy as np
import jax
import jax.numpy as jnp
from jax import lax
from jax.experimental import pallas as pl
from jax.experimental.pallas import tpu as pltpu
from jax.experimental.pallas import tpu_sc as plsc

F32 = jnp.float32
BF16 = jnp.bfloat16

D_MODEL = 1024
GRID_W = 64
SSD_INNER = 2048
SSD_HEADS = 32
SSD_GROUPS = 4
SSD_HPG = 8
SSD_HEAD_DIM = 64
SSD_STATE = 128
GROUP_W = SSD_HPG * SSD_HEAD_DIM
XBC_WIDTH = SSD_INNER + 2 * SSD_GROUPS * SSD_STATE
SC_WIDTH = 1024
N_MOD = 6
N_EXPERTS = 8
EPS = 1e-6
LOG2E = 1.4426950408889634

LANES = 128
SSD_Q = 128
MOE_TM = 512
ROW_SUBTILES = 2
PROJ_TM = 1024
PROJ_TN = 2048
PROJ_TN_XBC = 1024
ROW_TM = 512


def _row_tile(rows, want, multiple=1):
    t = multiple * max(1, min(want, rows) // multiple)
    assert rows % t == 0, (rows, t)
    return t
VMEM_LIMIT = 56 * 1024 * 1024


def _dot(a, b):
    return jnp.dot(a, b, preferred_element_type=F32)


def _sigmoid(v):
    return 1.0 / (1.0 + jnp.exp(-v))


def _silu(v):
    return v * _sigmoid(v)


def _sigmoid_t(v):
    return 0.5 + 0.5 * jnp.tanh(0.5 * v)


def _silu_t(v):
    hv = 0.5 * v
    return hv + hv * jnp.tanh(hv)


def _split2(a):
    hi = a.astype(BF16)
    lo = (a - hi.astype(F32)).astype(BF16)
    return hi, lo


def _split3(a):
    hi = a.astype(BF16)
    r = a - hi.astype(F32)
    mid = r.astype(BF16)
    lo = (r - mid.astype(F32)).astype(BF16)
    return hi, mid, lo


def _norm_mod(x, g, shift, scale):
    ms = jnp.mean(x * x, axis=-1, keepdims=True)
    return (x * lax.rsqrt(ms + EPS) * g) * (1.0 + scale) + shift


def _cparams(sem, vmem=VMEM_LIMIT):
    return pltpu.CompilerParams(dimension_semantics=sem, vmem_limit_bytes=vmem)


def _mod_kernel(c_ref, w_ref, b_ref, o_ref):
    a_hi, a_lo = _split2(_silu(c_ref[...]))
    w_hi, w_lo = _split2(w_ref[0])
    o_ref[0] = _dot(a_hi, w_hi) + _dot(a_lo, w_hi) + _dot(a_hi, w_lo) + b_ref[0]


def modulation(cc, w_mod, b_mod):
    depth, d, n = w_mod.shape
    tn = 1536
    return pl.pallas_call(
        _mod_kernel,
        grid=(depth, n // tn),
        in_specs=[pl.BlockSpec((16, d), lambda i, j: (0, 0)),
                  pl.BlockSpec((1, d, tn), lambda i, j: (i, 0, j)),
                  pl.BlockSpec((1, 1, tn), lambda i, j: (i, 0, j))],
        out_specs=pl.BlockSpec((1, 16, tn), lambda i, j: (i, 0, j)),
        out_shape=jax.ShapeDtypeStruct((depth, 16, n), F32),
        compiler_params=_cparams(("parallel", "parallel")),
        name="modulation",
    )(cc, w_mod, b_mod.reshape(depth, 1, n))


def _inproj_kernel(x_ref, g_ref, sh_ref, sc_ref, w_ref, wdt_ref, o_ref, dt_ref, h_ref):
    @pl.when(pl.program_id(2) == 0)
    def _():
        hb = _norm_mod(x_ref[0], g_ref[...], sh_ref[0], sc_ref[0]).astype(BF16)
        h_ref[...] = hb
        dt_ref[0] = _dot(hb, wdt_ref[...])

    o_ref[0] = _dot(h_ref[...], w_ref[...]).astype(o_ref.dtype)


def in_proj(x, g, shift, scale, w, wdt, tm, tn):
    b, l, d = x.shape
    n = w.shape[1]
    return pl.pallas_call(
        _inproj_kernel,
        grid=(b, l // tm, n // tn),
        in_specs=[pl.BlockSpec((1, tm, d), lambda i, m, j: (i, m, 0)),
                  pl.BlockSpec((1, d), lambda i, m, j: (0, 0)),
                  pl.BlockSpec((1, 1, d), lambda i, m, j: (i, 0, 0)),
                  pl.BlockSpec((1, 1, d), lambda i, m, j: (i, 0, 0)),
                  pl.BlockSpec((d, tn), lambda i, m, j: (0, j)),
                  pl.BlockSpec((d, LANES), lambda i, m, j: (0, 0))],
        out_specs=[pl.BlockSpec((1, tm, tn), lambda i, m, j: (i, m, j)),
                   pl.BlockSpec((1, tm, LANES), lambda i, m, j: (i, m, 0))],
        out_shape=[jax.ShapeDtypeStruct((b, l, n), BF16),
                   jax.ShapeDtypeStruct((b, l, LANES), F32)],
        scratch_shapes=[pltpu.VMEM((tm, d), BF16)],
        compiler_params=_cparams(("parallel", "parallel", "arbitrary")),
        name="in_proj",
    )(x, g.reshape(1, d), shift, scale, w, wdt)


def _dt_kernel(raw_ref, bias_ref, a_ref, pc_ref, ph_ref, plo_ref, col_ref, row_ref, sp_ref, *, q, ch):
    row = lax.broadcasted_iota(jnp.int32, (q, q), 0)
    col = lax.broadcasted_iota(jnp.int32, (q, q), 1)
    tri_l = jnp.where(col <= row, 1.0, 0.0).astype(BF16)
    tri_u = jnp.where(col >= row, 1.0, 0.0).astype(BF16)
    lane = lax.broadcasted_iota(jnp.int32, (q, LANES), 1)
    fwd = lane < SSD_HEADS
    ph, plo = ph_ref[...], plo_ref[...]

    def place_split(t):
        t_hi, t_lo = _split2(t)
        return (_dot(t_hi, ph) + _dot(t_lo, plo)).astype(BF16)

    for k in range(ch):
        v = raw_ref[0, k * q:(k + 1) * q, :] + bias_ref[...]
        dt = jnp.maximum(v, 0.0) + jnp.log1p(jnp.exp(-jnp.abs(v)))
        d1, d2, d3 = _split3(dt * a_ref[...])
        cs_f = _dot(tri_l, d1) + _dot(tri_l, d2) + _dot(tri_l, d3)
        cs_b = _dot(tri_u, d1) + _dot(tri_u, d2) + _dot(tri_u, d3)
        cs = jnp.where(fwd, cs_f, cs_b)
        tot = jnp.where(fwd[0:1], cs_f[q - 1:q, :], cs_b[0:1, :])
        sl = slice(k * q, (k + 1) * q)
        cs2 = cs * LOG2E
        c1, c2, c3 = _split3(cs2)
        pc = pc_ref[...]
        col_ref[0, sl, :] = _dot(c1, pc) + _dot(c2, pc) + _dot(c3, pc)
        r_t = (cs2 - jnp.log(dt) * LOG2E).T
        dt_t = dt.T
        for g in range(SSD_GROUPS):
            lo = g * SSD_HPG
            dt_f = dt_t[lo:lo + 8, :]
            dt_b = dt_t[SSD_HEADS + lo:SSD_HEADS + lo + 8, :]
            row_ref[0, g, k, 0:8, :] = r_t[lo:lo + 8, :]
            row_ref[0, g, k, 8:16, :] = r_t[SSD_HEADS + lo:SSD_HEADS + lo + 8, :]
            row_ref[0, g, k, 16:24, :] = jnp.log(dt_f + dt_b) * LOG2E
        sp_ref[0, sl, 0:LANES] = place_split(dt * jnp.exp(tot - cs))
        sp_ref[0, sl, LANES:2 * LANES] = place_split(jnp.exp(cs))


def _placements():
    pc = np.zeros((LANES, LANES), np.float32)
    ph = np.zeros((LANES, LANES), np.float32)
    plo = np.zeros((LANES, LANES), np.float32)
    for d in range(2):
        for g in range(SSD_GROUPS):
            for j in range(SSD_HPG):
                src = d * SSD_HEADS + g * SSD_HPG + j
                pc[src, g * 16 + d * 8 + j] = 1.0
                ph[src, (d * SSD_GROUPS + g) * 16 + j] = 1.0
                plo[src, (d * SSD_GROUPS + g) * 16 + 8 + j] = 1.0
    return jnp.asarray(pc, BF16), jnp.asarray(ph, BF16), jnp.asarray(plo, BF16)


def dt_prep(raw, bias, a, q):
    b, l, _ = raw.shape
    nc = l // q
    ch = min(8, nc)
    gn = SSD_GROUPS
    vspec = pl.BlockSpec((1, LANES), lambda i, c: (0, 0))
    pspec = pl.BlockSpec((LANES, LANES), lambda i, c: (0, 0))
    return pl.pallas_call(
        functools.partial(_dt_kernel, q=q, ch=ch),
        grid=(b, nc // ch),
        in_specs=[pl.BlockSpec((1, ch * q, LANES), lambda i, c: (i, c, 0)), vspec, vspec, pspec, pspec, pspec],
        out_specs=[pl.BlockSpec((1, ch * q, LANES), lambda i, c: (i, c, 0)),
                   pl.BlockSpec((1, gn, ch, 24, q), lambda i, c: (i, 0, c, 0, 0)),
                   pl.BlockSpec((1, ch * q, 2 * LANES), lambda i, c: (i, c, 0))],
        out_shape=[jax.ShapeDtypeStruct((b, l, LANES), F32),
                   jax.ShapeDtypeStruct((b, gn, nc, 24, q), F32),
                   jax.ShapeDtypeStruct((b, l, 2 * LANES), BF16)],
        compiler_params=_cparams(("parallel", "parallel")),
        name="dt_prep",
    )(raw, bias, a, *_placements())


def _ssd_kernel(xp_ref, bp_ref, cp_ref, cwx_ref, cwb_ref, cwc_ref, cbx_ref, cbb_ref, cbc_ref,
                col_ref, row_ref, sp_ref, e2f_ref, e2b_ref, dexp_ref, s0_ref,
                y_ref, sfin_ref,
                xs_ref, cc_ref, bt_ref, xwf_ref, sbe_ref, edge_ref, sf_ref, sb_ref, yo_ref, *, l, q):
    nc = l // q
    sr = lax.broadcasted_iota(jnp.int32, (q, q + 32), 0)
    sc = lax.broadcasted_iota(jnp.int32, (q, q + 32), 1)
    shift_prev = jnp.where((sc == sr - 1) | ((sr == 0) & (sc == q + 15)), 1.0, 0.0).astype(BF16)
    shift_next = jnp.where(((sc == sr + 1) & (sc < q)) | ((sr == q - 1) & (sc == q + 16)), 1.0, 0.0).astype(BF16)

    def conv_silu(srcs, w, b, k):
        r0 = pl.multiple_of(k * q, q)
        p0 = pl.multiple_of(jnp.maximum(r0 - 16, 0), 16)
        n0 = pl.multiple_of(jnp.minimum(r0 + q, l - 16), 16)

        def rows_at(start, n):
            parts = [r[0, pl.ds(start, n), ls] for r, ls in srcs]
            return parts[0] if len(parts) == 1 else jnp.concatenate(parts, axis=1)

        blk = rows_at(r0, q)
        before = rows_at(p0, 16)
        after = rows_at(n0, 16)
        before = jnp.where(k > 0, before, jnp.zeros_like(before))
        after = jnp.where(k < nc - 1, after, jnp.zeros_like(after))
        stacked = jnp.concatenate([blk, before, after], axis=0)
        v = (w[0:1] * _dot(shift_prev, stacked) + w[1:2] * blk.astype(F32)
             + w[2:3] * _dot(shift_next, stacked) + b)
        return _silu_t(v)

    all_lanes = slice(None)

    def prep(k):
        r0 = pl.multiple_of(k * q, q)
        x = conv_silu(((xp_ref, all_lanes),), cwx_ref[...], cbx_ref[...], k)
        xs_ref[pl.ds(r0, q), :] = x.astype(BF16)
        bc = conv_silu(((bp_ref, all_lanes), (cp_ref, all_lanes)),
                       jnp.concatenate([cwb_ref[...], cwc_ref[...]], axis=1),
                       jnp.concatenate([cbb_ref[...], cbc_ref[...]], axis=1), k)
        b_t = bc[:, :SSD_STATE].T.astype(BF16)
        bt_ref[k] = b_t
        cc_ref[pl.ds(r0, q), :] = bc[:, SSD_STATE:].astype(BF16)
        sp_w = sp_ref[0, pl.ds(r0, q), 0:LANES]
        xwf_ref[pl.ds(r0, q), :] = (x * _dot(sp_w, e2f_ref[0, 0])).astype(BF16)
        xw_b = (x * _dot(sp_w, e2b_ref[0, 0])).astype(BF16)
        e_last = sp_ref[0, pl.ds(pl.multiple_of(r0 + q - 16, 16), 16), LANES:2 * LANES]
        e_first = sp_ref[0, pl.ds(r0, 16), LANES:2 * LANES]
        edge_ref[k] = _dot(e_last, e2f_ref[0, 0])[15:16, :]
        edge_b = _dot(e_first, e2b_ref[0, 0])[0:1, :]
        return b_t, xw_b, edge_b

    sb_ref[...] = s0_ref[0, 0, 1]

    def bstep(i, carry):
        c = nc - 1 - i
        b_t, xw_b, edge_b = prep(c)
        sbe_ref[c] = sb_ref[...].astype(BF16)
        sb_ref[...] = sb_ref[...] * edge_b + _dot(b_t, xw_b)
        return carry

    lax.fori_loop(0, nc, bstep, 0, unroll=8)
    sfin_ref[0, 0, 1] = sb_ref[...]

    sf_ref[...] = s0_ref[0, 0, 0]
    li = lax.broadcasted_iota(jnp.int32, (q, q), 0)
    si = lax.broadcasted_iota(jnp.int32, (q, q), 1)
    lower = si <= li
    eye = si == li
    left = lax.broadcasted_iota(jnp.int32, (q, LANES), 1) < SSD_HEAD_DIM
    fu = yo_ref.shape[0]
    lane0 = 16 * pl.program_id(1)

    def chunk(c, slot):
        r0 = pl.multiple_of(c * q, q)
        cc = cc_ref[pl.ds(r0, q), :]
        sp_e = sp_ref[0, pl.ds(r0, q), LANES:2 * LANES]
        yo_ref[slot] = (_dot(sp_e, e2f_ref[0, 0]) * _dot(cc, sf_ref[...].astype(BF16))
                        + _dot(sp_e, e2b_ref[0, 0]) * _dot(cc, sbe_ref[c]))
        sf_ref[...] = sf_ref[...] * edge_ref[c] + _dot(bt_ref[c], xwf_ref[pl.ds(r0, q), :])
        g = _dot(cc, bt_ref[c])
        col = pltpu.roll(col_ref[0, pl.ds(r0, q), :], LANES - lane0, 1)
        row = row_ref[0, 0, c]
        for j in range(SSD_HPG // 2):
            lanes = slice(j * LANES, (j + 1) * LANES)
            xf = xs_ref[pl.ds(r0, q), lanes].astype(F32)
            x_diag = jnp.concatenate([jnp.where(left, xf, 0.0), jnp.where(left, 0.0, xf)], axis=0).astype(BF16)
            ms = []
            for h in (2 * j, 2 * j + 1):
                cs_l = jnp.take_along_axis(col, jnp.where(lower, h, 8 + h), axis=1, mode="promise_in_bounds")
                arg = cs_l - jnp.where(lower, row[h:h + 1, :], row[8 + h:9 + h, :])
                p = jnp.exp2(jnp.where(eye, row[16 + h:17 + h, :], arg))
                ms.append((g * p).astype(BF16))
            y = _dot(jnp.concatenate(ms, axis=1), x_diag) + yo_ref[slot, :, lanes] + dexp_ref[0, :, lanes] * xf
            y_ref[0, pl.ds(r0, q), lanes] = y.astype(y_ref.dtype)

    def fstep(i, carry):
        for slot in range(fu):
            chunk(fu * i + slot, slot)
        return carry

    lax.fori_loop(0, nc // fu, fstep, 0)
    sfin_ref[0, 0, 0] = sf_ref[...]


def _expanders():
    e = np.zeros((2, SSD_GROUPS, LANES, GROUP_W), np.float32)
    for d in range(2):
        for g in range(SSD_GROUPS):
            for s in range(2):
                for j in range(SSD_HPG):
                    e[d, g, (d * SSD_GROUPS + g) * 16 + s * 8 + j, j * SSD_HEAD_DIM:(j + 1) * SSD_HEAD_DIM] = 1.0
    return jnp.asarray(e, BF16)


def ssd_mixer(proj, xbc_col0, dt_raw, conv_w, conv_b, dt_bias, a_log, d_skip, s0, q):
    b, l, _ = proj.shape
    assert l % q == 0 and q % 16 == 0, (l, q)
    nc = l // q
    gn = SSD_GROUPS
    bias = jnp.zeros((1, LANES), F32).at[0, :2 * SSD_HEADS].set(dt_bias.reshape(-1))
    a = jnp.zeros((1, LANES), F32).at[0, :2 * SSD_HEADS].set(-jnp.exp(a_log.reshape(-1)))
    colg, rowg, sp = dt_prep(dt_raw, bias, a, q)
    dexp = jnp.repeat(d_skip.astype(F32), SSD_HEAD_DIM).reshape(gn, 1, GROUP_W)
    e2 = _expanders()

    cw = conv_w.astype(F32)
    cb = conv_b.astype(F32).reshape(1, -1)
    nb = SSD_INNER
    cwx, cwb, cwc = cw[:, :nb], cw[:, nb:nb + gn * SSD_STATE], cw[:, nb + gn * SSD_STATE:]
    cbx, cbb, cbc = cb[:, :nb], cb[:, nb:nb + gn * SSD_STATE], cb[:, nb + gn * SSD_STATE:]

    fu = max(u for u in (8, 4, 2, 1) if nc % u == 0)
    xo = xbc_col0 // GROUP_W
    bo = (xbc_col0 + SSD_INNER) // SSD_STATE
    co = bo + gn
    st_spec = pl.BlockSpec((1, 1, 2, SSD_STATE, GROUP_W), lambda i, g: (i, g, 0, 0, 0))
    y, sfin = pl.pallas_call(
        functools.partial(_ssd_kernel, l=l, q=q),
        grid=(b, gn),
        in_specs=[pl.BlockSpec((1, l, GROUP_W), lambda i, g: (i, 0, xo + g)),
                  pl.BlockSpec((1, l, SSD_STATE), lambda i, g: (i, 0, bo + g)),
                  pl.BlockSpec((1, l, SSD_STATE), lambda i, g: (i, 0, co + g)),
                  pl.BlockSpec((3, GROUP_W), lambda i, g: (0, g)),
                  pl.BlockSpec((3, SSD_STATE), lambda i, g: (0, g)),
                  pl.BlockSpec((3, SSD_STATE), lambda i, g: (0, g)),
                  pl.BlockSpec((1, GROUP_W), lambda i, g: (0, g)),
                  pl.BlockSpec((1, SSD_STATE), lambda i, g: (0, g)),
                  pl.BlockSpec((1, SSD_STATE), lambda i, g: (0, g)),
                  pl.BlockSpec((1, l, LANES), lambda i, g: (i, 0, 0)),
                  pl.BlockSpec((1, 1, nc, 24, q), lambda i, g: (i, g, 0, 0, 0)),
                  pl.BlockSpec((1, l, 2 * LANES), lambda i, g: (i, 0, 0)),
                  pl.BlockSpec((1, 1, LANES, GROUP_W), lambda i, g: (0, g, 0, 0)),
                  pl.BlockSpec((1, 1, LANES, GROUP_W), lambda i, g: (1, g, 0, 0)),
                  pl.BlockSpec((1, 1, GROUP_W), lambda i, g: (g, 0, 0)),
                  st_spec],
        out_specs=[pl.BlockSpec((1, l, GROUP_W), lambda i, g: (i, 0, g)), st_spec],
        out_shape=[jax.ShapeDtypeStruct((b, l, SSD_INNER), BF16),
                   jax.ShapeDtypeStruct((b, gn, 2, SSD_STATE, GROUP_W), F32)],
        scratch_shapes=[pltpu.VMEM((l, GROUP_W), BF16),
                        pltpu.VMEM((l, SSD_STATE), BF16),
                        pltpu.VMEM((nc, SSD_STATE, q), BF16),
                        pltpu.VMEM((l, GROUP_W), BF16),
                        pltpu.VMEM((nc, SSD_STATE, GROUP_W), BF16),
                        pltpu.VMEM((nc, 1, GROUP_W), F32),
                        pltpu.VMEM((SSD_STATE, GROUP_W), F32),
                        pltpu.VMEM((SSD_STATE, GROUP_W), F32),
                        pltpu.VMEM((fu, q, GROUP_W), F32)],
        compiler_params=_cparams(("parallel", "parallel")),
        name="ssd",
    )(proj, proj, proj, cwx, cwb, cwc, cbx, cbb, cbc, colg, rowg, sp, e2, e2, dexp, s0)
    return y, sfin


def _mixout_kernel(y_ref, z_ref, gb_ref, gc_ref, hv_ref, g0_ref, g1_ref, x_ref, gate_ref,
                   ng_ref, bg_ref, scw_ref, wssd_ref, wsc_ref, wo_ref, o_ref, *, tm, period):
    ns = 2 if (tm // 2) % period == 0 else 1
    ts = tm // ns
    pos = lax.broadcasted_iota(jnp.int32, (ts, 1), 0) % period
    w = scw_ref[...]
    bg = bg_ref[...]
    for s in range(ns):
        r = slice(s * ts, (s + 1) * ts)
        yz = y_ref[0, r, :].astype(F32) * _silu_t(z_ref[0, r, :].astype(F32))
        ms = jnp.mean(yz * yz, axis=-1, keepdims=True)
        yn = (yz * lax.rsqrt(ms + EPS) * ng_ref[...]).astype(BF16)
        y_ssd = _dot(yn, wssd_ref[...])

        u = gc_ref[0, r, :].astype(F32) * hv_ref[0, r, :].astype(F32)
        u_prev = jnp.where(pos == 0, 0.0, pltpu.roll(u, 1, 0))
        u_next = jnp.where(pos == period - 1, 0.0, pltpu.roll(u, ts - 1, 0))
        v = w[0:1] * u_prev + w[1:2] * u + w[2:3] * u_next
        y_sc = _dot((gb_ref[0, r, :].astype(F32) * v).astype(BF16), wsc_ref[...])

        g0 = _sigmoid_t(g0_ref[0, r, :].astype(F32) + bg[:, :D_MODEL])
        g1 = _sigmoid_t(g1_ref[0, r, :].astype(F32) + bg[:, D_MODEL:])
        out = _dot((g0 * y_ssd + g1 * y_sc).astype(BF16), wo_ref[...])
        o_ref[0, r, :] = x_ref[0, r, :] + gate_ref[0] * out


def mixer_out(y, proj, x, gate, norm_g, b_gate, sc_conv_w, w_ssd, w_sc, w_o, tm, period):
    b, l, d = x.shape
    pc = lambda k: pl.BlockSpec((1, tm, d), lambda i, m, k=k: (i, m, k))
    full = lambda shp: pl.BlockSpec(shp, lambda i, m: (0,) * len(shp))
    return pl.pallas_call(
        functools.partial(_mixout_kernel, tm=tm, period=period),
        grid=(b, l // tm),
        in_specs=[pl.BlockSpec((1, tm, SSD_INNER), lambda i, m: (i, m, 0)),
                  pl.BlockSpec((1, tm, SSD_INNER), lambda i, m: (i, m, 0)),
                  pc(5), pc(6), pc(7), pc(8), pc(9),
                  pl.BlockSpec((1, tm, d), lambda i, m: (i, m, 0)),
                  pl.BlockSpec((1, 1, d), lambda i, m: (i, 0, 0)),
                  full((1, SSD_INNER)), full((1, 2 * d)), full((3, SC_WIDTH)),
                  full((SSD_INNER, d)), full((SC_WIDTH, d)), full((d, d))],
        out_specs=pl.BlockSpec((1, tm, d), lambda i, m: (i, m, 0)),
        out_shape=jax.ShapeDtypeStruct((b, l, d), F32),
        compiler_params=_cparams(("parallel", "parallel")),
        name="mixer_out",
    )(y, proj, proj, proj, proj, proj, proj, x, gate,
      norm_g.reshape(1, -1), b_gate.reshape(1, -1), sc_conv_w, w_ssd, w_sc, w_o)


def _ffn_kernel(x_ref, g_ref, sh_ref, sc_ref, gate_ref, w1_ref, w3_ref, w2_ref, o_ref, *, nf):
    fw = w1_ref.shape[1] // nf
    ts = x_ref.shape[1] // ROW_SUBTILES
    for s in range(ROW_SUBTILES):
        r = slice(s * ts, (s + 1) * ts)
        x = x_ref[0, r, :]
        hb = _norm_mod(x, g_ref[...], sh_ref[0], sc_ref[0]).astype(BF16)
        acc = None
        for k in range(nf):
            a = _dot(hb, w1_ref[:, k * fw:(k + 1) * fw])
            bb = _dot(hb, w3_ref[:, k * fw:(k + 1) * fw])
            part = _dot((_silu_t(a) * bb).astype(BF16), w2_ref[k * fw:(k + 1) * fw, :])
            acc = part if acc is None else acc + part
        o_ref[0, r, :] = x + gate_ref[0] * acc


def ffn_dense(x, g, shift, scale, gate, w1, w3, w2, tm):
    b, l, d = x.shape
    f = w1.shape[1]
    vec = pl.BlockSpec((1, 1, d), lambda i, m: (i, 0, 0))
    const = lambda shp: pl.BlockSpec(shp, lambda i, m: (0, 0), pipeline_mode=pl.Buffered(1))
    return pl.pallas_call(
        functools.partial(_ffn_kernel, nf=2),
        grid=(b, l // tm),
        in_specs=[pl.BlockSpec((1, tm, d), lambda i, m: (i, m, 0)),
                  pl.BlockSpec((1, d), lambda i, m: (0, 0)),
                  vec, vec, vec, const((d, f)), const((d, f)), const((f, d))],
        out_specs=pl.BlockSpec((1, tm, d), lambda i, m: (i, m, 0)),
        out_shape=jax.ShapeDtypeStruct((b, l, d), F32),
        compiler_params=_cparams(("parallel", "parallel")),
        name="ffn_dense",
    )(x, g.reshape(1, d), shift, scale, gate, w1, w3, w2)


def _router_kernel(x_ref, g_ref, sh_ref, sc_ref, rw_ref, h_ref, route_ref, cnt_ref, run_ref, *, tm):
    @pl.when((pl.program_id(0) == 0) & (pl.program_id(1) == 0))
    def _():
        run_ref[...] = jnp.zeros_like(run_ref)

    h = _norm_mod(x_ref[0], g_ref[...], sh_ref[0], sc_ref[0])
    h_hi, h_lo = _split2(h)
    h_ref[0] = h
    w_hi, w_lo = _split2(rw_ref[...])
    logits = _dot(h_hi, w_hi) + _dot(h_lo, w_hi) + _dot(h_hi, w_lo)
    lane = lax.broadcasted_iota(jnp.int32, (tm, LANES), 1)
    ninf = float("-inf")
    lg = jnp.where(lane < N_EXPERTS, logits, ninf)
    m1 = jnp.max(lg, axis=1, keepdims=True)
    i1 = jnp.min(jnp.where(lg == m1, lane, LANES), axis=1, keepdims=True)
    lg2 = jnp.where(lane == i1, ninf, lg)
    m2 = jnp.max(lg2, axis=1, keepdims=True)
    i2 = jnp.min(jnp.where(lg2 == m2, lane, LANES), axis=1, keepdims=True)
    e2 = jnp.exp(m2 - m1)
    den = 1.0 + e2
    sel1 = jnp.where(lane == i1, 1.0, 0.0)
    sel2 = jnp.where(lane == i2, 1.0, 0.0)
    cnt = sel1 + sel2
    r = lax.broadcasted_iota(jnp.int32, (tm, tm), 0)
    c = lax.broadcasted_iota(jnp.int32, (tm, tm), 1)
    tri = jnp.where(c < r, 1.0, 0.0).astype(BF16)
    base = _dot(tri, cnt.astype(BF16)) + run_ref[0:1, :]
    r1 = jnp.sum(sel1 * base, axis=1, keepdims=True)
    r2 = jnp.sum(sel2 * base, axis=1, keepdims=True)
    vals = (i1.astype(F32), i2.astype(F32), 1.0 / den, e2 / den, r1, r2)
    out = jnp.zeros((tm, LANES), F32)
    for k, v in enumerate(vals):
        out = jnp.where(lane == k, v, out)
    route_ref[0] = out
    new_run = run_ref[...] + jnp.sum(cnt, axis=0, keepdims=True)
    run_ref[...] = new_run
    cnt_ref[...] = new_run


def router(x, g, shift, scale, router_w, tm):
    b, l, d = x.shape
    rw = jnp.zeros((d, LANES), F32).at[:, :N_EXPERTS].set(router_w)
    vec = pl.BlockSpec((1, 1, d), lambda i, m: (i, 0, 0))
    return pl.pallas_call(
        functools.partial(_router_kernel, tm=tm),
        grid=(b, l // tm),
        in_specs=[pl.BlockSpec((1, tm, d), lambda i, m: (i, m, 0)),
                  pl.BlockSpec((1, d), lambda i, m: (0, 0)),
                  vec, vec, pl.BlockSpec((d, LANES), lambda i, m: (0, 0))],
        out_specs=[pl.BlockSpec((1, tm, d), lambda i, m: (i, m, 0)),
                   pl.BlockSpec((1, tm, LANES), lambda i, m: (i, m, 0)),
                   pl.BlockSpec((8, LANES), lambda i, m: (0, 0))],
        out_shape=[jax.ShapeDtypeStruct((b, l, d), F32),
                   jax.ShapeDtypeStruct((b, l, LANES), F32),
                   jax.ShapeDtypeStruct((8, LANES), F32)],
        scratch_shapes=[pltpu.VMEM((8, LANES), F32)],
        compiler_params=_cparams(("arbitrary", "arbitrary")),
        name="router",
    )(x, g.reshape(1, d), shift, scale, rw)


def _pack_bf16_pairs(y):
    k = y.shape[1] // 2
    bits = lax.bitcast_convert_type(y.astype(BF16).astype(F32), jnp.uint32)
    return bits[:, :k] | (bits[:, k:] >> 16)


def _unpack_bf16_pairs(p):
    hi = lax.bitcast_convert_type(p & jnp.uint32(0xFFFF0000), F32)
    lo = lax.bitcast_convert_type(p << 16, F32)
    return jnp.concatenate([hi, lo], axis=1)


def _gffn_kernel(te_ref, nv_ref, x_ref, w1_ref, w3_ref, w2_ref, o_ref, acc_ref, *, nf):
    i = pl.program_id(0)
    f = pl.program_id(1)

    @pl.when(i < nv_ref[0])
    def _():
        ts = x_ref.shape[0] // ROW_SUBTILES
        parts = []
        for s in range(ROW_SUBTILES):
            x = x_ref[s * ts:(s + 1) * ts, :].astype(BF16)
            a = _dot(x, w1_ref[0])
            bb = _dot(x, w3_ref[0])
            parts.append(_dot((_silu_t(a) * bb).astype(BF16), w2_ref[0]))
        part = jnp.concatenate(parts, axis=0)

        @pl.when(f == 0)
        def _():
            acc_ref[...] = part

        @pl.when(f > 0)
        def _():
            acc_ref[...] += part

        @pl.when(f == nf - 1)
        def _():
            o_ref[...] = _pack_bf16_pairs(acc_ref[...])

    @pl.when((i >= nv_ref[0]) & (f == nf - 1))
    def _():
        o_ref[...] = jnp.zeros_like(o_ref)


def grouped_ffn(xs, tile_expert, n_valid, w1, w3, w2, tm, nf):
    rows, d = xs.shape
    nt = rows // tm
    f = w1.shape[2]
    fw = f // nf
    grid_spec = pltpu.PrefetchScalarGridSpec(
        num_scalar_prefetch=2,
        grid=(nt, nf),
        in_specs=[pl.BlockSpec((tm, d), lambda i, k, te, nv: (i, 0)),
                  pl.BlockSpec((1, d, fw), lambda i, k, te, nv: (te[i], 0, k)),
                  pl.BlockSpec((1, d, fw), lambda i, k, te, nv: (te[i], 0, k)),
                  pl.BlockSpec((1, fw, d), lambda i, k, te, nv: (te[i], k, 0))],
        out_specs=pl.BlockSpec((tm, d // 2), lambda i, k, te, nv: (i, 0)),
        scratch_shapes=[pltpu.VMEM((tm, d), F32)],
    )
    return pl.pallas_call(
        functools.partial(_gffn_kernel, nf=nf),
        grid_spec=grid_spec,
        out_shape=jax.ShapeDtypeStruct((rows, d // 2), jnp.uint32),
        compiler_params=_cparams(("parallel", "arbitrary")),
        name="grouped_ffn",
    )(tile_expert, n_valid, xs, w1, w3, w2)


def _combine_kernel(x_ref, y0_ref, y1_ref, route_ref, gate_ref, fg_ref, o_ref):
    r = route_ref[0]
    moe = r[:, 2:3] * _unpack_bf16_pairs(y0_ref[0, 0]) + r[:, 3:4] * _unpack_bf16_pairs(y1_ref[0, 0])
    xn = x_ref[0] + gate_ref[0] * moe
    ms = jnp.mean(xn * xn, axis=-1, keepdims=True)
    o_ref[0] = xn * lax.rsqrt(ms + EPS) * fg_ref[...]


def combine_final(x, yg, route, gate, final_g, tm):
    b, l, d = x.shape
    return pl.pallas_call(
        _combine_kernel,
        grid=(b, l // tm),
        in_specs=[pl.BlockSpec((1, tm, d), lambda i, m: (i, m, 0)),
                  pl.BlockSpec((1, 1, tm, d // 2), lambda i, m: (0, i, m, 0)),
                  pl.BlockSpec((1, 1, tm, d // 2), lambda i, m: (1, i, m, 0)),
                  pl.BlockSpec((1, tm, LANES), lambda i, m: (i, m, 0)),
                  pl.BlockSpec((1, 1, d), lambda i, m: (i, 0, 0)),
                  pl.BlockSpec((1, d), lambda i, m: (0, 0))],
        out_specs=pl.BlockSpec((1, tm, d), lambda i, m: (i, m, 0)),
        out_shape=jax.ShapeDtypeStruct((b, l, d), F32),
        compiler_params=_cparams(("parallel", "parallel")),
        name="combine_final",
    )(x, yg, yg, route, gate, final_g.reshape(1, d))


SC_CORES = 2
SC_SUBCORES = 16
SC_WORKERS = SC_CORES * SC_SUBCORES
SC_STREAM_BYTES = 256 * 1024
SC_STREAM_ROWS = 128


def _sc_rows(per_worker, d, dtype):
    return min(SC_STREAM_ROWS, SC_STREAM_BYTES // (d * jnp.dtype(dtype).itemsize), per_worker)


def _sc_mesh():
    return plsc.VectorSubcoreMesh(core_axis_name="c", subcore_axis_name="s",
                                  num_cores=SC_CORES, num_subcores=SC_SUBCORES)


def dispatch_rows(h, pos0, pos1, n_rows):
    t, d = h.shape
    per_w = t // SC_WORKERS
    ch = _sc_rows(per_w, d, h.dtype)
    assert t % SC_WORKERS == 0 and per_w % ch == 0 and ch % 8 == 0, (t, ch)

    @functools.partial(
        pl.kernel, mesh=_sc_mesh(),
        out_type=jax.ShapeDtypeStruct((n_rows, d), h.dtype),
        scratch_types=[pltpu.VMEM((ch,), jnp.int32), pltpu.VMEM((ch,), jnp.int32),
                       pltpu.VMEM((ch, d), h.dtype), pltpu.SemaphoreType.DMA],
        name="moe_dispatch")
    def scatter(h_hbm, p0_hbm, p1_hbm, out_hbm, i0_v, i1_v, rows_v, sem):
        base = (lax.axis_index("s") * SC_CORES + lax.axis_index("c")) * per_w

        @pl.loop(0, per_w // ch)
        def _(j):
            off = base + j * ch
            pltpu.sync_copy(h_hbm.at[pl.ds(off, ch)], rows_v)
            pltpu.sync_copy(p0_hbm.at[pl.ds(off, ch)], i0_v)
            pltpu.sync_copy(p1_hbm.at[pl.ds(off, ch)], i1_v)
            pltpu.async_copy(rows_v, out_hbm.at[i0_v], sem).wait()
            pltpu.async_copy(rows_v, out_hbm.at[i1_v], sem).wait()

    return scatter(h, pos0, pos1)


def return_rows(ys, idx):
    n = idx.shape[0]
    d = ys.shape[1]
    per_w = n // SC_WORKERS
    ch = _sc_rows(per_w, d, ys.dtype)
    assert n % SC_WORKERS == 0 and per_w % ch == 0 and ch % 8 == 0, (n, ch)

    @functools.partial(
        pl.kernel, mesh=_sc_mesh(),
        out_type=jax.ShapeDtypeStruct((n, d), ys.dtype),
        scratch_types=[pltpu.VMEM((ch,), jnp.int32), pltpu.VMEM((ch, d), ys.dtype), pltpu.SemaphoreType.DMA],
        name="moe_return")
    def gather(ys_hbm, idx_hbm, out_hbm, idx_v, rows_v, sem):
        base = (lax.axis_index("s") * SC_CORES + lax.axis_index("c")) * per_w

        @pl.loop(0, per_w // ch)
        def _(j):
            off = base + j * ch
            pltpu.sync_copy(idx_hbm.at[pl.ds(off, ch)], idx_v)
            pltpu.async_copy(ys_hbm.at[idx_v], rows_v, sem).wait()
            pltpu.sync_copy(rows_v, out_hbm.at[pl.ds(off, ch)])

    return gather(ys, idx)


def moe_block(x, g, shift, scale, gate, router_w, w1, w3, w2, final_g):
    b, l, d = x.shape
    t = b * l
    tm = MOE_TM
    h, route, counts = router(x, g, shift, scale, router_w, _row_tile(l, ROW_TM))
    rt = route.reshape(t, LANES)
    cnt = counts[0, :N_EXPERTS].astype(jnp.int32)
    gs = ((cnt + tm - 1) // tm) * tm
    ends = jnp.cumsum(gs)
    offs = ends - gs
    pos0 = offs[rt[:, 0].astype(jnp.int32)] + rt[:, 4].astype(jnp.int32)
    pos1 = offs[rt[:, 1].astype(jnp.int32)] + rt[:, 5].astype(jnp.int32)
    nt = (2 * t) // tm + N_EXPERTS
    n_valid = (ends[-1] // tm).astype(jnp.int32).reshape(1)
    tile = jnp.minimum(jnp.arange(nt, dtype=jnp.int32), n_valid[0] - 1)
    tile_expert = jnp.sum((tile[:, None] >= (ends // tm)[None, :]).astype(jnp.int32), axis=1)
    xs = dispatch_rows(h.reshape(t, d), pos0, pos1, nt * tm)
    ys = grouped_ffn(xs, tile_expert, n_valid, w1, w3, w2, tm, 2)
    yg = return_rows(ys, jnp.concatenate([pos0, pos1])).reshape(2, b, l, d // 2)
    return combine_final(x, yg, route, gate, final_g, _row_tile(l, ROW_TM))


def _in_weights(w_in):
    o1 = SSD_INNER
    o2 = o1 + XBC_WIDTH
    o3 = o2 + 2 * SSD_HEADS
    w_main = jnp.concatenate([w_in[:, :o2], w_in[:, o3:]], axis=1).astype(BF16)
    w_dt = jnp.pad(w_in[:, o2:o3], ((0, 0), (0, LANES - 2 * SSD_HEADS))).astype(BF16)
    return w_main, w_dt


def kernel(x, c, ctx, c_ctx, w_mod, b_mod, norm1_g, norm2_g, w_in, b_gate, ssd_conv_w, ssd_conv_b, ssd_dt_bias, ssd_a_log, ssd_d, ssd_norm_g, w_ssd_out, sc_conv_w, w_sc_out, w_o, ffn_w1, ffn_w3, ffn_w2, router_w, moe_w1, moe_w3, moe_w2, final_g):
    b, l, d = x.shape
    lc = ctx.shape[1]
    depth = w_mod.shape[0]
    assert depth % 2 == 0, "the final norm is fused into the routed channel mixer of the (odd) last layer"
    cc = jnp.zeros((16, d), F32).at[:b].set(c).at[b].set(c_ctx)
    mod = modulation(cc, w_mod, b_mod)
    zeros_state = jnp.zeros((b, SSD_GROUPS, 2, SSD_STATE, GROUP_W), F32)
    nctx = b * lc
    ctx = ctx.reshape(1, nctx, d)
    tmc = _row_tile(nctx, ROW_TM, lc)
    tmx = _row_tile(l, ROW_TM, GRID_W)

    def per_seq(t):
        return t.reshape(b, lc, t.shape[-1])

    for i in range(depth):
        last = i == depth - 1
        mx = mod[i, :b].reshape(b, N_MOD, 1, d)
        mc = mod[i, b].reshape(1, N_MOD, 1, d)
        w_main, w_dt = _in_weights(w_in[i])
        ssd_p = (ssd_conv_w[i], ssd_conv_b[i], ssd_dt_bias[i], ssd_a_log[i], ssd_d[i])
        out_p = (ssd_norm_g[i], b_gate[i], sc_conv_w[i], w_ssd_out[i].astype(BF16),
                 w_sc_out[i].astype(BF16), w_o[i].astype(BF16))

        if last:
            w_xbc = w_main[:, SSD_INNER:SSD_INNER + XBC_WIDTH]
            proj_c, dt_c = in_proj(ctx, norm1_g[i], mc[:, 0], mc[:, 1], w_xbc, w_dt,
                                   _row_tile(nctx, PROJ_TM), PROJ_TN_XBC)
            _, s_ctx = ssd_mixer(per_seq(proj_c), 0, per_seq(dt_c), *ssd_p, zeros_state, SSD_Q)
        else:
            proj_c, dt_c = in_proj(ctx, norm1_g[i], mc[:, 0], mc[:, 1], w_main, w_dt,
                                   _row_tile(nctx, PROJ_TM), PROJ_TN)
            y_c, s_ctx = ssd_mixer(per_seq(proj_c), SSD_INNER, per_seq(dt_c), *ssd_p, zeros_state, SSD_Q)
            ctx = mixer_out(y_c.reshape(1, nctx, SSD_INNER), proj_c, ctx, mc[:, 2], *out_p, tmc, lc)

        proj_x, dt_x = in_proj(x, norm1_g[i], mx[:, 0], mx[:, 1], w_main, w_dt, _row_tile(l, PROJ_TM), PROJ_TN)
        y_x, _ = ssd_mixer(proj_x, SSD_INNER, dt_x, *ssd_p, s_ctx, SSD_Q)
        x = mixer_out(y_x, proj_x, x, mx[:, 2], *out_p, tmx, GRID_W)

        j = i // 2
        if i % 2 == 0:
            w1, w3, w2 = ffn_w1[j].astype(BF16), ffn_w3[j].astype(BF16), ffn_w2[j].astype(BF16)
            x = ffn_dense(x, norm2_g[i], mx[:, 3], mx[:, 4], mx[:, 5], w1, w3, w2, tmx)
            if not last:
                ctx = ffn_dense(ctx, norm2_g[i], mc[:, 3], mc[:, 4], mc[:, 5], w1, w3, w2, tmc)
        else:
            assert last, "the routed channel mixer is fused with the final norm"
            w1, w3, w2 = moe_w1[j].astype(BF16), moe_w3[j].astype(BF16), moe_w2[j].astype(BF16)
            x = moe_block(x, norm2_g[i], mx[:, 3], mx[:, 4], mx[:, 5], router_w[j], w1, w3, w2, final_g)
    return x
```

```python
import functools

import numpy as np
import jax
import jax.numpy as jnp
from jax import lax
from jax.experimental import pallas as pl
from jax.experimental.pallas import tpu as pltpu
from jax.experimental.pallas import tpu_sc as plsc

F32 = jnp.float32
BF16 = jnp.bfloat16

D_MODEL = 1024
GRID_W = 64
SSD_INNER = 2048
SSD_HEADS = 32
SSD_GROUPS = 4
SSD_HPG = 8
SSD_HEAD_DIM = 64
SSD_STATE = 128
GROUP_W = SSD_HPG * SSD_HEAD_DIM
XBC_WIDTH = SSD_INNER + 2 * SSD_GROUPS * SSD_STATE
SC_WIDTH = 1024
N_MOD = 6
N_EXPERTS = 8
EPS = 1e-6
LOG2E = 1.4426950408889634

LANES = 128
SSD_Q = 128
MOE_TM = 512
ROW_SUBTILES = 2
PROJ_TM = 1024
PROJ_TN = 2048
PROJ_TN_XBC = 1024
ROW_TM = 512


def _row_tile(rows, want, multiple=1):
    t = multiple * max(1, min(want, rows) // multiple)
    assert rows % t == 0, (rows, t)
    return t
VMEM_LIMIT = 56 * 1024 * 1024


def _dot(a, b):
    return jnp.dot(a, b, preferred_element_type=F32)


def _sigmoid(v):
    return 1.0 / (1.0 + jnp.exp(-v))


def _silu(v):
    return v * _sigmoid(v)


def _sigmoid_t(v):
    return 0.5 + 0.5 * jnp.tanh(0.5 * v)


def _silu_t(v):
    hv = 0.5 * v
    return hv + hv * jnp.tanh(hv)


def _split2(a):
    hi = a.astype(BF16)
    lo = (a - hi.astype(F32)).astype(BF16)
    return hi, lo


def _split3(a):
    hi = a.astype(BF16)
    r = a - hi.astype(F32)
    mid = r.astype(BF16)
    lo = (r - mid.astype(F32)).astype(BF16)
    return hi, mid, lo


def _norm_mod(x, g, shift, scale):
    ms = jnp.mean(x * x, axis=-1, keepdims=True)
    return (x * lax.rsqrt(ms + EPS) * g) * (1.0 + scale) + shift


def _cparams(sem, vmem=VMEM_LIMIT):
    return pltpu.CompilerParams(dimension_semantics=sem, vmem_limit_bytes=vmem)


def _mod_kernel(c_ref, w_ref, b_ref, o_ref):
    a_hi, a_lo = _split2(_silu(c_ref[...]))
    w_hi, w_lo = _split2(w_ref[0])
    o_ref[0] = _dot(a_hi, w_hi) + _dot(a_lo, w_hi) + _dot(a_hi, w_lo) + b_ref[0]


def modulation(cc, w_mod, b_mod):
    depth, d, n = w_mod.shape
    tn = 1536
    return pl.pallas_call(
        _mod_kernel,
        grid=(depth, n // tn),
        in_specs=[pl.BlockSpec((16, d), lambda i, j: (0, 0)),
                  pl.BlockSpec((1, d, tn), lambda i, j: (i, 0, j)),
                  pl.BlockSpec((1, 1, tn), lambda i, j: (i, 0, j))],
        out_specs=pl.BlockSpec((1, 16, tn), lambda i, j: (i, 0, j)),
        out_shape=jax.ShapeDtypeStruct((depth, 16, n), F32),
        compiler_params=_cparams(("parallel", "parallel")),
        name="modulation",
    )(cc, w_mod, b_mod.reshape(depth, 1, n))


def _inproj_kernel(x_ref, g_ref, sh_ref, sc_ref, w_ref, wdt_ref, o_ref, dt_ref, h_ref):
    @pl.when(pl.program_id(2) == 0)
    def _():
        hb = _norm_mod(x_ref[0], g_ref[...], sh_ref[0], sc_ref[0]).astype(BF16)
        h_ref[...] = hb
        dt_ref[0] = _dot(hb, wdt_ref[...])

    o_ref[0] = _dot(h_ref[...], w_ref[...]).astype(o_ref.dtype)


def in_proj(x, g, shift, scale, w, wdt, tm, tn):
    b, l, d = x.shape
    n = w.shape[1]
    return pl.pallas_call(
        _inproj_kernel,
        grid=(b, l // tm, n // tn),
        in_specs=[pl.BlockSpec((1, tm, d), lambda i, m, j: (i, m, 0)),
                  pl.BlockSpec((1, d), lambda i, m, j: (0, 0)),
                  pl.BlockSpec((1, 1, d), lambda i, m, j: (i, 0, 0)),
                  pl.BlockSpec((1, 1, d), lambda i, m, j: (i, 0, 0)),
                  pl.BlockSpec((d, tn), lambda i, m, j: (0, j)),
                  pl.BlockSpec((d, LANES), lambda i, m, j: (0, 0))],
        out_specs=[pl.BlockSpec((1, tm, tn), lambda i, m, j: (i, m, j)),
                   pl.BlockSpec((1, tm, LANES), lambda i, m, j: (i, m, 0))],
        out_shape=[jax.ShapeDtypeStruct((b, l, n), BF16),
                   jax.ShapeDtypeStruct((b, l, LANES), F32)],
        scratch_shapes=[pltpu.VMEM((tm, d), BF16)],
        compiler_params=_cparams(("parallel", "parallel", "arbitrary")),
        name="in_proj",
    )(x, g.reshape(1, d), shift, scale, w, wdt)


def _dt_kernel(raw_ref, bias_ref, a_ref, pc_ref, ph_ref, plo_ref, col_ref, row_ref, sp_ref, *, q, ch):
    row = lax.broadcasted_iota(jnp.int32, (q, q), 0)
    col = lax.broadcasted_iota(jnp.int32, (q, q), 1)
    tri_l = jnp.where(col <= row, 1.0, 0.0).astype(BF16)
    tri_u = jnp.where(col >= row, 1.0, 0.0).astype(BF16)
    lane = lax.broadcasted_iota(jnp.int32, (q, LANES), 1)
    fwd = lane < SSD_HEADS
    ph, plo = ph_ref[...], plo_ref[...]

    def place_split(t):
        t_hi, t_lo = _split2(t)
        return (_dot(t_hi, ph) + _dot(t_lo, plo)).astype(BF16)

    for k in range(ch):
        v = raw_ref[0, k * q:(k + 1) * q, :] + bias_ref[...]
        dt = jnp.maximum(v, 0.0) + jnp.log1p(jnp.exp(-jnp.abs(v)))
        d1, d2, d3 = _split3(dt * a_ref[...])
        cs_f = _dot(tri_l, d1) + _dot(tri_l, d2) + _dot(tri_l, d3)
        cs_b = _dot(tri_u, d1) + _dot(tri_u, d2) + _dot(tri_u, d3)
        cs = jnp.where(fwd, cs_f, cs_b)
        tot = jnp.where(fwd[0:1], cs_f[q - 1:q, :], cs_b[0:1, :])
        sl = slice(k * q, (k + 1) * q)
        cs2 = cs * LOG2E
        c1, c2, c3 = _split3(cs2)
        pc = pc_ref[...]
        col_ref[0, sl, :] = _dot(c1, pc) + _dot(c2, pc) + _dot(c3, pc)
        r_t = (cs2 - jnp.log(dt) * LOG2E).T
        dt_t = dt.T
        for g in range(SSD_GROUPS):
            lo = g * SSD_HPG
            dt_f = dt_t[lo:lo + 8, :]
            dt_b = dt_t[SSD_HEADS + lo:SSD_HEADS + lo + 8, :]
            row_ref[0, g, k, 0:8, :] = r_t[lo:lo + 8, :]
            row_ref[0, g, k, 8:16, :] = r_t[SSD_HEADS + lo:SSD_HEADS + lo + 8, :]
            row_ref[0, g, k, 16:24, :] = jnp.log(dt_f + dt_b) * LOG2E
        sp_ref[0, sl, 0:LANES] = place_split(dt * jnp.exp(tot - cs))
        sp_ref[0, sl, LANES:2 * LANES] = place_split(jnp.exp(cs))


def _placements():
    pc = np.zeros((LANES, LANES), np.float32)
    ph = np.zeros((LANES, LANES), np.float32)
    plo = np.zeros((LANES, LANES), np.float32)
    for d in range(2):
        for g in range(SSD_GROUPS):
            for j in range(SSD_HPG):
                src = d * SSD_HEADS + g * SSD_HPG + j
                pc[src, g * 16 + d * 8 + j] = 1.0
                ph[src, (d * SSD_GROUPS + g) * 16 + j] = 1.0
                plo[src, (d * SSD_GROUPS + g) * 16 + 8 + j] = 1.0
    return jnp.asarray(pc, BF16), jnp.asarray(ph, BF16), jnp.asarray(plo, BF16)


def dt_prep(raw, bias, a, q):
    b, l, _ = raw.shape
    nc = l // q
    ch = min(8, nc)
    gn = SSD_GROUPS
    vspec = pl.BlockSpec((1, LANES), lambda i, c: (0, 0))
    pspec = pl.BlockSpec((LANES, LANES), lambda i, c: (0, 0))
    return pl.pallas_call(
        functools.partial(_dt_kernel, q=q, ch=ch),
        grid=(b, nc // ch),
        in_specs=[pl.BlockSpec((1, ch * q, LANES), lambda i, c: (i, c, 0)), vspec, vspec, pspec, pspec, pspec],
        out_specs=[pl.BlockSpec((1, ch * q, LANES), lambda i, c: (i, c, 0)),
                   pl.BlockSpec((1, gn, ch, 24, q), lambda i, c: (i, 0, c, 0, 0)),
                   pl.BlockSpec((1, ch * q, 2 * LANES), lambda i, c: (i, c, 0))],
        out_shape=[jax.ShapeDtypeStruct((b, l, LANES), F32),
                   jax.ShapeDtypeStruct((b, gn, nc, 24, q), F32),
                   jax.ShapeDtypeStruct((b, l, 2 * LANES), BF16)],
        compiler_params=_cparams(("parallel", "parallel")),
        name="dt_prep",
    )(raw, bias, a, *_placements())


def _ssd_kernel(xp_ref, bp_ref, cp_ref, cwx_ref, cwb_ref, cwc_ref, cbx_ref, cbb_ref, cbc_ref,
                col_ref, row_ref, sp_ref, e2f_ref, e2b_ref, dexp_ref, s0_ref,
                y_ref, sfin_ref,
                xs_ref, cc_ref, bt_ref, xwf_ref, sbe_ref, edge_ref, sf_ref, sb_ref, yo_ref, *, l, q):
    nc = l // q
    sr = lax.broadcasted_iota(jnp.int32, (q, q + 32), 0)
    sc = lax.broadcasted_iota(jnp.int32, (q, q + 32), 1)
    shift_prev = jnp.where((sc == sr - 1) | ((sr == 0) & (sc == q + 15)), 1.0, 0.0).astype(BF16)
    shift_next = jnp.where(((sc == sr + 1) & (sc < q)) | ((sr == q - 1) & (sc == q + 16)), 1.0, 0.0).astype(BF16)

    def conv_silu(srcs, w, b, k):
        r0 = pl.multiple_of(k * q, q)
        p0 = pl.multiple_of(jnp.maximum(r0 - 16, 0), 16)
        n0 = pl.multiple_of(jnp.minimum(r0 + q, l - 16), 16)

        def rows_at(start, n):
            parts = [r[0, pl.ds(start, n), ls] for r, ls in srcs]
            return parts[0] if len(parts) == 1 else jnp.concatenate(parts, axis=1)

        blk = rows_at(r0, q)
        before = rows_at(p0, 16)
        after = rows_at(n0, 16)
        before = jnp.where(k > 0, before, jnp.zeros_like(before))
        after = jnp.where(k < nc - 1, after, jnp.zeros_like(after))
        stacked = jnp.concatenate([blk, before, after], axis=0)
        v = (w[0:1] * _dot(shift_prev, stacked) + w[1:2] * blk.astype(F32)
             + w[2:3] * _dot(shift_next, stacked) + b)
        return _silu_t(v)

    all_lanes = slice(None)

    def prep(k):
        r0 = pl.multiple_of(k * q, q)
        x = conv_silu(((xp_ref, all_lanes),), cwx_ref[...], cbx_ref[...], k)
        xs_ref[pl.ds(r0, q), :] = x.astype(BF16)
        bc = conv_silu(((bp_ref, all_lanes), (cp_ref, all_lanes)),
                       jnp.concatenate([cwb_ref[...], cwc_ref[...]], axis=1),
                       jnp.concatenate([cbb_ref[...], cbc_ref[...]], axis=1), k)
        b_t = bc[:, :SSD_STATE].T.astype(BF16)
        bt_ref[k] = b_t
        cc_ref[pl.ds(r0, q), :] = bc[:, SSD_STATE:].astype(BF16)
        sp_w = sp_ref[0, pl.ds(r0, q), 0:LANES]
        xwf_ref[pl.ds(r0, q), :] = (x * _dot(sp_w, e2f_ref[0, 0])).astype(BF16)
        xw_b = (x * _dot(sp_w, e2b_ref[0, 0])).astype(BF16)
        e_last = sp_ref[0, pl.ds(pl.multiple_of(r0 + q - 16, 16), 16), LANES:2 * LANES]
        e_first = sp_ref[0, pl.ds(r0, 16), LANES:2 * LANES]
        edge_ref[k] = _dot(e_last, e2f_ref[0, 0])[15:16, :]
        edge_b = _dot(e_first, e2b_ref[0, 0])[0:1, :]
        return b_t, xw_b, edge_b

    sb_ref[...] = s0_ref[0, 0, 1]

    def bstep(i, carry):
        c = nc - 1 - i
        b_t, xw_b, edge_b = prep(c)
        sbe_ref[c] = sb_ref[...].astype(BF16)
        sb_ref[...] = sb_ref[...] * edge_b + _dot(b_t, xw_b)
        return carry

    lax.fori_loop(0, nc, bstep, 0, unroll=8)
    sfin_ref[0, 0, 1] = sb_ref[...]

    sf_ref[...] = s0_ref[0, 0, 0]
    li = lax.broadcasted_iota(jnp.int32, (q, q), 0)
    si = lax.broadcasted_iota(jnp.int32, (q, q), 1)
    lower = si <= li
    eye = si == li
    left = lax.broadcasted_iota(jnp.int32, (q, LANES), 1) < SSD_HEAD_DIM
    fu = yo_ref.shape[0]
    lane0 = 16 * pl.program_id(1)

    def chunk(c, slot):
        r0 = pl.multiple_of(c * q, q)
        cc = cc_ref[pl.ds(r0, q), :]
        sp_e = sp_ref[0, pl.ds(r0, q), LANES:2 * LANES]
        yo_ref[slot] = (_dot(sp_e, e2f_ref[0, 0]) * _dot(cc, sf_ref[...].astype(BF16))
                        + _dot(sp_e, e2b_ref[0, 0]) * _dot(cc, sbe_ref[c]))
        sf_ref[...] = sf_ref[...] * edge_ref[c] + _dot(bt_ref[c], xwf_ref[pl.ds(r0, q), :])
        g = _dot(cc, bt_ref[c])
        col = pltpu.roll(col_ref[0, pl.ds(r0, q), :], LANES - lane0, 1)
        row = row_ref[0, 0, c]
        for j in range(SSD_HPG // 2):
            lanes = slice(j * LANES, (j + 1) * LANES)
            xf = xs_ref[pl.ds(r0, q), lanes].astype(F32)
            x_diag = jnp.concatenate([jnp.where(left, xf, 0.0), jnp.where(left, 0.0, xf)], axis=0).astype(BF16)
            ms = []
            for h in (2 * j, 2 * j + 1):
                cs_l = jnp.take_along_axis(col, jnp.where(lower, h, 8 + h), axis=1, mode="promise_in_bounds")
                arg = cs_l - jnp.where(lower, row[h:h + 1, :], row[8 + h:9 + h, :])
                p = jnp.exp2(jnp.where(eye, row[16 + h:17 + h, :], arg))
                ms.append((g * p).astype(BF16))
            y = _dot(jnp.concatenate(ms, axis=1), x_diag) + yo_ref[slot, :, lanes] + dexp_ref[0, :, lanes] * xf
            y_ref[0, pl.ds(r0, q), lanes] = y.astype(y_ref.dtype)

    def fstep(i, carry):
        for slot in range(fu):
            chunk(fu * i + slot, slot)
        return carry

    lax.fori_loop(0, nc // fu, fstep, 0)
    sfin_ref[0, 0, 0] = sf_ref[...]


def _expanders():
    e = np.zeros((2, SSD_GROUPS, LANES, GROUP_W), np.float32)
    for d in range(2):
        for g in range(SSD_GROUPS):
            for s in range(2):
                for j in range(SSD_HPG):
                    e[d, g, (d * SSD_GROUPS + g) * 16 + s * 8 + j, j * SSD_HEAD_DIM:(j + 1) * SSD_HEAD_DIM] = 1.0
    return jnp.asarray(e, BF16)


def ssd_mixer(proj, xbc_col0, dt_raw, conv_w, conv_b, dt_bias, a_log, d_skip, s0, q):
    b, l, _ = proj.shape
    assert l % q == 0 and q % 16 == 0, (l, q)
    nc = l // q
    gn = SSD_GROUPS
    bias = jnp.zeros((1, LANES), F32).at[0, :2 * SSD_HEADS].set(dt_bias.reshape(-1))
    a = jnp.zeros((1, LANES), F32).at[0, :2 * SSD_HEADS].set(-jnp.exp(a_log.reshape(-1)))
    colg, rowg, sp = dt_prep(dt_raw, bias, a, q)
    dexp = jnp.repeat(d_skip.astype(F32), SSD_HEAD_DIM).reshape(gn, 1, GROUP_W)
    e2 = _expanders()

    cw = conv_w.astype(F32)
    cb = conv_b.astype(F32).reshape(1, -1)
    nb = SSD_INNER
    cwx, cwb, cwc = cw[:, :nb], cw[:, nb:nb + gn * SSD_STATE], cw[:, nb + gn * SSD_STATE:]
    cbx, cbb, cbc = cb[:, :nb], cb[:, nb:nb + gn * SSD_STATE], cb[:, nb + gn * SSD_STATE:]

    fu = max(u for u in (8, 4, 2, 1) if nc % u == 0)
    xo = xbc_col0 // GROUP_W
    bo = (xbc_col0 + SSD_INNER) // SSD_STATE
    co = bo + gn
    st_spec = pl.BlockSpec((1, 1, 2, SSD_STATE, GROUP_W), lambda i, g: (i, g, 0, 0, 0))
    y, sfin = pl.pallas_call(
        functools.partial(_ssd_kernel, l=l, q=q),
        grid=(b, gn),
        in_specs=[pl.BlockSpec((1, l, GROUP_W), lambda i, g: (i, 0, xo + g)),
                  pl.BlockSpec((1, l, SSD_STATE), lambda i, g: (i, 0, bo + g)),
                  pl.BlockSpec((1, l, SSD_STATE), lambda i, g: (i, 0, co + g)),
                  pl.BlockSpec((3, GROUP_W), lambda i, g: (0, g)),
                  pl.BlockSpec((3, SSD_STATE), lambda i, g: (0, g)),
                  pl.BlockSpec((3, SSD_STATE), lambda i, g: (0, g)),
                  pl.BlockSpec((1, GROUP_W), lambda i, g: (0, g)),
                  pl.BlockSpec((1, SSD_STATE), lambda i, g: (0, g)),
                  pl.BlockSpec((1, SSD_STATE), lambda i, g: (0, g)),
                  pl.BlockSpec((1, l, LANES), lambda i, g: (i, 0, 0)),
                  pl.BlockSpec((1, 1, nc, 24, q), lambda i, g: (i, g, 0, 0, 0)),
                  pl.BlockSpec((1, l, 2 * LANES), lambda i, g: (i, 0, 0)),
                  pl.BlockSpec((1, 1, LANES, GROUP_W), lambda i, g: (0, g, 0, 0)),
                  pl.BlockSpec((1, 1, LANES, GROUP_W), lambda i, g: (1, g, 0, 0)),
                  pl.BlockSpec((1, 1, GROUP_W), lambda i, g: (g, 0, 0)),
                  st_spec],
        out_specs=[pl.BlockSpec((1, l, GROUP_W), lambda i, g: (i, 0, g)), st_spec],
        out_shape=[jax.ShapeDtypeStruct((b, l, SSD_INNER), BF16),
                   jax.ShapeDtypeStruct((b, gn, 2, SSD_STATE, GROUP_W), F32)],
        scratch_shapes=[pltpu.VMEM((l, GROUP_W), BF16),
                        pltpu.VMEM((l, SSD_STATE), BF16),
                        pltpu.VMEM((nc, SSD_STATE, q), BF16),
                        pltpu.VMEM((l, GROUP_W), BF16),
                        pltpu.VMEM((nc, SSD_STATE, GROUP_W), BF16),
                        pltpu.VMEM((nc, 1, GROUP_W), F32),
                        pltpu.VMEM((SSD_STATE, GROUP_W), F32),
                        pltpu.VMEM((SSD_STATE, GROUP_W), F32),
                        pltpu.VMEM((fu, q, GROUP_W), F32)],
        compiler_params=_cparams(("parallel", "parallel")),
        name="ssd",
    )(proj, proj, proj, cwx, cwb, cwc, cbx, cbb, cbc, colg, rowg, sp, e2, e2, dexp, s0)
    return y, sfin


def _mixout_kernel(y_ref, z_ref, gb_ref, gc_ref, hv_ref, g0_ref, g1_ref, x_ref, gate_ref,
                   ng_ref, bg_ref, scw_ref, wssd_ref, wsc_ref, wo_ref, o_ref, *, tm, period):
    ns = 2 if (tm // 2) % period == 0 else 1
    ts = tm // ns
    pos = lax.broadcasted_iota(jnp.int32, (ts, 1), 0) % period
    w = scw_ref[...]
    bg = bg_ref[...]
    for s in range(ns):
        r = slice(s * ts, (s + 1) * ts)
        yz = y_ref[0, r, :].astype(F32) * _silu_t(z_ref[0, r, :].astype(F32))
        ms = jnp.mean(yz * yz, axis=-1, keepdims=True)
        yn = (yz * lax.rsqrt(ms + EPS) * ng_ref[...]).astype(BF16)
        y_ssd = _dot(yn, wssd_ref[...])

        u = gc_ref[0, r, :].astype(F32) * hv_ref[0, r, :].astype(F32)
        u_prev = jnp.where(pos == 0, 0.0, pltpu.roll(u, 1, 0))
        u_next = jnp.where(pos == period - 1, 0.0, pltpu.roll(u, ts - 1, 0))
        v = w[0:1] * u_prev + w[1:2] * u + w[2:3] * u_next
        y_sc = _dot((gb_ref[0, r, :].astype(F32) * v).astype(BF16), wsc_ref[...])

        g0 = _sigmoid_t(g0_ref[0, r, :].astype(F32) + bg[:, :D_MODEL])
        g1 = _sigmoid_t(g1_ref[0, r, :].astype(F32) + bg[:, D_MODEL:])
        out = _dot((g0 * y_ssd + g1 * y_sc).astype(BF16), wo_ref[...])
        o_ref[0, r, :] = x_ref[0, r, :] + gate_ref[0] * out


def mixer_out(y, proj, x, gate, norm_g, b_gate, sc_conv_w, w_ssd, w_sc, w_o, tm, period):
    b, l, d = x.shape
    pc = lambda k: pl.BlockSpec((1, tm, d), lambda i, m, k=k: (i, m, k))
    full = lambda shp: pl.BlockSpec(shp, lambda i, m: (0,) * len(shp))
    return pl.pallas_call(
        functools.partial(_mixout_kernel, tm=tm, period=period),
        grid=(b, l // tm),
        in_specs=[pl.BlockSpec((1, tm, SSD_INNER), lambda i, m: (i, m, 0)),
                  pl.BlockSpec((1, tm, SSD_INNER), lambda i, m: (i, m, 0)),
                  pc(5), pc(6), pc(7), pc(8), pc(9),
                  pl.BlockSpec((1, tm, d), lambda i, m: (i, m, 0)),
                  pl.BlockSpec((1, 1, d), lambda i, m: (i, 0, 0)),
                  full((1, SSD_INNER)), full((1, 2 * d)), full((3, SC_WIDTH)),
                  full((SSD_INNER, d)), full((SC_WIDTH, d)), full((d, d))],
        out_specs=pl.BlockSpec((1, tm, d), lambda i, m: (i, m, 0)),
        out_shape=jax.ShapeDtypeStruct((b, l, d), F32),
        compiler_params=_cparams(("parallel", "parallel")),
        name="mixer_out",
    )(y, proj, proj, proj, proj, proj, proj, x, gate,
      norm_g.reshape(1, -1), b_gate.reshape(1, -1), sc_conv_w, w_ssd, w_sc, w_o)


def _ffn_kernel(x_ref, g_ref, sh_ref, sc_ref, gate_ref, w1_ref, w3_ref, w2_ref, o_ref, *, nf):
    fw = w1_ref.shape[1] // nf
    ts = x_ref.shape[1] // ROW_SUBTILES
    for s in range(ROW_SUBTILES):
        r = slice(s * ts, (s + 1) * ts)
        x = x_ref[0, r, :]
        hb = _norm_mod(x, g_ref[...], sh_ref[0], sc_ref[0]).astype(BF16)
        acc = None
        for k in range(nf):
            a = _dot(hb, w1_ref[:, k * fw:(k + 1) * fw])
            bb = _dot(hb, w3_ref[:, k * fw:(k + 1) * fw])
            part = _dot((_silu_t(a) * bb).astype(BF16), w2_ref[k * fw:(k + 1) * fw, :])
            acc = part if acc is None else acc + part
        o_ref[0, r, :] = x + gate_ref[0] * acc


def ffn_dense(x, g, shift, scale, gate, w1, w3, w2, tm):
    b, l, d = x.shape
    f = w1.shape[1]
    vec = pl.BlockSpec((1, 1, d), lambda i, m: (i, 0, 0))
    const = lambda shp: pl.BlockSpec(shp, lambda i, m: (0, 0), pipeline_mode=pl.Buffered(1))
    return pl.pallas_call(
        functools.partial(_ffn_kernel, nf=2),
        grid=(b, l // tm),
        in_specs=[pl.BlockSpec((1, tm, d), lambda i, m: (i, m, 0)),
                  pl.BlockSpec((1, d), lambda i, m: (0, 0)),
                  vec, vec, vec, const((d, f)), const((d, f)), const((f, d))],
        out_specs=pl.BlockSpec((1, tm, d), lambda i, m: (i, m, 0)),
        out_shape=jax.ShapeDtypeStruct((b, l, d), F32),
        compiler_params=_cparams(("parallel", "parallel")),
        name="ffn_dense",
    )(x, g.reshape(1, d), shift, scale, gate, w1, w3, w2)


def _router_kernel(x_ref, g_ref, sh_ref, sc_ref, rw_ref, h_ref, route_ref, route_t_ref, cnt_ref, run_ref, *, tm):
    @pl.when((pl.program_id(0) == 0) & (pl.program_id(1) == 0))
    def _():
        run_ref[...] = jnp.zeros_like(run_ref)

    h = _norm_mod(x_ref[0], g_ref[...], sh_ref[0], sc_ref[0])
    h_hi, h_lo = _split2(h)
    h_ref[0] = h
    w_hi, w_lo = _split2(rw_ref[...])
    logits = _dot(h_hi, w_hi) + _dot(h_lo, w_hi) + _dot(h_hi, w_lo)
    lane = lax.broadcasted_iota(jnp.int32, (tm, LANES), 1)
    ninf = float("-inf")
    lg = jnp.where(lane < N_EXPERTS, logits, ninf)
    m1 = jnp.max(lg, axis=1, keepdims=True)
    i1 = jnp.min(jnp.where(lg == m1, lane, LANES), axis=1, keepdims=True)
    lg2 = jnp.where(lane == i1, ninf, lg)
    m2 = jnp.max(lg2, axis=1, keepdims=True)
    i2 = jnp.min(jnp.where(lg2 == m2, lane, LANES), axis=1, keepdims=True)
    e2 = jnp.exp(m2 - m1)
    den = 1.0 + e2
    sel1 = jnp.where(lane == i1, 1.0, 0.0)
    sel2 = jnp.where(lane == i2, 1.0, 0.0)
    cnt = sel1 + sel2
    r = lax.broadcasted_iota(jnp.int32, (tm, tm), 0)
    c = lax.broadcasted_iota(jnp.int32, (tm, tm), 1)
    tri = jnp.where(c < r, 1.0, 0.0).astype(BF16)
    base = _dot(tri, cnt.astype(BF16)) + run_ref[0:1, :]
    r1 = jnp.sum(sel1 * base, axis=1, keepdims=True)
    r2 = jnp.sum(sel2 * base, axis=1, keepdims=True)
    vals = (i1.astype(F32), i2.astype(F32), 1.0 / den, e2 / den, r1, r2)
    out = jnp.zeros((tm, LANES), F32)
    for k, v in enumerate(vals):
        out = jnp.where(lane == k, v, out)
    route_ref[0] = out
    route_t_ref[...] = out.T
    new_run = run_ref[...] + jnp.sum(cnt, axis=0, keepdims=True)
    run_ref[...] = new_run
    cnt_ref[...] = new_run


def router(x, g, shift, scale, router_w, tm):
    b, l, d = x.shape
    rw = jnp.zeros((d, LANES), F32).at[:, :N_EXPERTS].set(router_w)
    vec = pl.BlockSpec((1, 1, d), lambda i, m: (i, 0, 0))
    mt = l // tm
    return pl.pallas_call(
        functools.partial(_router_kernel, tm=tm),
        grid=(b, l // tm),
        in_specs=[pl.BlockSpec((1, tm, d), lambda i, m: (i, m, 0)),
                  pl.BlockSpec((1, d), lambda i, m: (0, 0)),
                  vec, vec, pl.BlockSpec((d, LANES), lambda i, m: (0, 0))],
        out_specs=[pl.BlockSpec((1, tm, d), lambda i, m: (i, m, 0)),
                   pl.BlockSpec((1, tm, LANES), lambda i, m: (i, m, 0)),
                   pl.BlockSpec((LANES, tm), lambda i, m: (0, i * mt + m)),
                   pl.BlockSpec((8, LANES), lambda i, m: (0, 0))],
        out_shape=[jax.ShapeDtypeStruct((b, l, d), F32),
                   jax.ShapeDtypeStruct((b, l, LANES), F32),
                   jax.ShapeDtypeStruct((LANES, b * l), F32),
                   jax.ShapeDtypeStruct((8, LANES), F32)],
        scratch_shapes=[pltpu.VMEM((8, LANES), F32)],
        compiler_params=_cparams(("arbitrary", "arbitrary")),
        name="router",
    )(x, g.reshape(1, d), shift, scale, rw)


def _pack_bf16_pairs(y):
    k = y.shape[1] // 2
    bits = lax.bitcast_convert_type(y.astype(BF16).astype(F32), jnp.uint32)
    return bits[:, :k] | (bits[:, k:] >> 16)


def _unpack_bf16_pairs(p):
    hi = lax.bitcast_convert_type(p & jnp.uint32(0xFFFF0000), F32)
    lo = lax.bitcast_convert_type(p << 16, F32)
    return jnp.concatenate([hi, lo], axis=1)


def _gffn_kernel(te_ref, nv_ref, x_ref, w1_ref, w3_ref, w2_ref, o_ref, acc_ref, *, nf):
    i = pl.program_id(0)
    f = pl.program_id(1)

    @pl.when(i < nv_ref[0])
    def _():
        ts = x_ref.shape[0] // ROW_SUBTILES
        parts = []
        for s in range(ROW_SUBTILES):
            x = x_ref[s * ts:(s + 1) * ts, :].astype(BF16)
            a = _dot(x, w1_ref[0])
            bb = _dot(x, w3_ref[0])
            parts.append(_dot((_silu_t(a) * bb).astype(BF16), w2_ref[0]))
        part = jnp.concatenate(parts, axis=0)

        @pl.when(f == 0)
        def _():
            acc_ref[...] = part

        @pl.when(f > 0)
        def _():
            acc_ref[...] += part

        @pl.when(f == nf - 1)
        def _():
            o_ref[...] = _pack_bf16_pairs(acc_ref[...])


def grouped_ffn(xs, tile_expert, n_valid, w1, w3, w2, tm, nf):
    rows, d = xs.shape
    nt = rows // tm
    f = w1.shape[2]
    fw = f // nf

    def tile(i, nv):
        return jnp.minimum(i, nv[0] - 1)

    def fchunk(i, k, nv):
        return jnp.where(i < nv[0], k, nf - 1)

    grid_spec = pltpu.PrefetchScalarGridSpec(
        num_scalar_prefetch=2,
        grid=(nt, nf),
        in_specs=[pl.BlockSpec((tm, d), lambda i, k, te, nv: (tile(i, nv), 0)),
                  pl.BlockSpec((1, d, fw), lambda i, k, te, nv: (te[i], 0, fchunk(i, k, nv))),
                  pl.BlockSpec((1, d, fw), lambda i, k, te, nv: (te[i], 0, fchunk(i, k, nv))),
                  pl.BlockSpec((1, fw, d), lambda i, k, te, nv: (te[i], fchunk(i, k, nv), 0))],
        out_specs=pl.BlockSpec((tm, d // 2), lambda i, k, te, nv: (tile(i, nv), 0)),
        scratch_shapes=[pltpu.VMEM((tm, d), F32)],
    )
    return pl.pallas_call(
        functools.partial(_gffn_kernel, nf=nf),
        grid_spec=grid_spec,
        out_shape=jax.ShapeDtypeStruct((rows, d // 2), jnp.uint32),
        compiler_params=_cparams(("arbitrary", "arbitrary")),
        name="grouped_ffn",
    )(tile_expert, n_valid, xs, w1, w3, w2)


def _combine_kernel(x_ref, y0_ref, y1_ref, route_ref, gate_ref, fg_ref, o_ref):
    r = route_ref[0]
    moe = r[:, 2:3] * _unpack_bf16_pairs(y0_ref[0, 0]) + r[:, 3:4] * _unpack_bf16_pairs(y1_ref[0, 0])
    xn = x_ref[0] + gate_ref[0] * moe
    ms = jnp.mean(xn * xn, axis=-1, keepdims=True)
    o_ref[0] = xn * lax.rsqrt(ms + EPS) * fg_ref[...]


def combine_final(x, yg, route, gate, final_g, tm):
    b, l, d = x.shape
    return pl.pallas_call(
        _combine_kernel,
        grid=(b, l // tm),
        in_specs=[pl.BlockSpec((1, tm, d), lambda i, m: (i, m, 0)),
                  pl.BlockSpec((1, 1, tm, d // 2), lambda i, m: (0, i, m, 0)),
                  pl.BlockSpec((1, 1, tm, d // 2), lambda i, m: (1, i, m, 0)),
                  pl.BlockSpec((1, tm, LANES), lambda i, m: (i, m, 0)),
                  pl.BlockSpec((1, 1, d), lambda i, m: (i, 0, 0)),
                  pl.BlockSpec((1, d), lambda i, m: (0, 0))],
        out_specs=pl.BlockSpec((1, tm, d), lambda i, m: (i, m, 0)),
        out_shape=jax.ShapeDtypeStruct((b, l, d), F32),
        compiler_params=_cparams(("parallel", "parallel")),
        name="combine_final",
    )(x, yg, yg, route, gate, final_g.reshape(1, d))


SC_CORES = 2
SC_SUBCORES = 16
SC_WORKERS = SC_CORES * SC_SUBCORES
SC_STREAM_BYTES = 256 * 1024
SC_STREAM_ROWS = 128


def _sc_rows(per_worker, d, dtype):
    return min(SC_STREAM_ROWS, SC_STREAM_BYTES // (d * jnp.dtype(dtype).itemsize), per_worker)


def _sc_mesh():
    return plsc.VectorSubcoreMesh(core_axis_name="c", subcore_axis_name="s",
                                  num_cores=SC_CORES, num_subcores=SC_SUBCORES)


def dispatch_rows(h, pos0, pos1, n_rows):
    t, d = h.shape
    per_w = t // SC_WORKERS
    ch = _sc_rows(per_w, d, h.dtype)
    assert t % SC_WORKERS == 0 and per_w % ch == 0 and ch % 8 == 0, (t, ch)

    @functools.partial(
        pl.kernel, mesh=_sc_mesh(),
        out_type=jax.ShapeDtypeStruct((n_rows, d), h.dtype),
        scratch_types=[pltpu.VMEM((ch,), jnp.int32), pltpu.VMEM((ch,), jnp.int32),
                       pltpu.VMEM((ch, d), h.dtype), pltpu.SemaphoreType.DMA],
        name="moe_dispatch")
    def scatter(h_hbm, p0_hbm, p1_hbm, out_hbm, i0_v, i1_v, rows_v, sem):
        base = (lax.axis_index("s") * SC_CORES + lax.axis_index("c")) * per_w

        @pl.loop(0, per_w // ch)
        def _(j):
            off = base + j * ch
            pltpu.sync_copy(h_hbm.at[pl.ds(off, ch)], rows_v)
            pltpu.sync_copy(p0_hbm.at[pl.ds(off, ch)], i0_v)
            pltpu.sync_copy(p1_hbm.at[pl.ds(off, ch)], i1_v)
            pltpu.async_copy(rows_v, out_hbm.at[i0_v], sem).wait()
            pltpu.async_copy(rows_v, out_hbm.at[i1_v], sem).wait()

    return scatter(h, pos0, pos1)


def return_rows(ys, idx):
    n = idx.shape[0]
    d = ys.shape[1]
    per_w = n // SC_WORKERS
    ch = _sc_rows(per_w, d, ys.dtype)
    assert n % SC_WORKERS == 0 and per_w % ch == 0 and ch % 8 == 0, (n, ch)

    @functools.partial(
        pl.kernel, mesh=_sc_mesh(),
        out_type=jax.ShapeDtypeStruct((n, d), ys.dtype),
        scratch_types=[pltpu.VMEM((ch,), jnp.int32), pltpu.VMEM((ch, d), ys.dtype), pltpu.SemaphoreType.DMA],
        name="moe_return")
    def gather(ys_hbm, idx_hbm, out_hbm, idx_v, rows_v, sem):
        base = (lax.axis_index("s") * SC_CORES + lax.axis_index("c")) * per_w

        @pl.loop(0, per_w // ch)
        def _(j):
            off = base + j * ch
            pltpu.sync_copy(idx_hbm.at[pl.ds(off, ch)], idx_v)
            pltpu.async_copy(ys_hbm.at[idx_v], rows_v, sem).wait()
            pltpu.sync_copy(rows_v, out_hbm.at[pl.ds(off, ch)])

    return gather(ys, idx)


def moe_block(x, g, shift, scale, gate, router_w, w1, w3, w2, final_g):
    b, l, d = x.shape
    t = b * l
    tm = MOE_TM
    h, route, route_t, counts = router(x, g, shift, scale, router_w, _row_tile(l, ROW_TM))
    cnt = counts[0, :N_EXPERTS].astype(jnp.int32)
    gs = ((cnt + tm - 1) // tm) * tm
    ends = jnp.cumsum(gs)
    offs = ends - gs

    def sorted_row(choice):
        e = route_t[choice].astype(jnp.int32)
        start = sum(jnp.where(e == k, offs[k], 0) for k in range(N_EXPERTS))
        return start + route_t[4 + choice].astype(jnp.int32)

    pos0, pos1 = sorted_row(0), sorted_row(1)
    nt = (2 * t) // tm + N_EXPERTS
    n_valid = (ends[-1] // tm).astype(jnp.int32).reshape(1)
    tile = jnp.minimum(jnp.arange(nt, dtype=jnp.int32), n_valid[0] - 1)
    tile_expert = jnp.sum((tile[:, None] >= (ends // tm)[None, :]).astype(jnp.int32), axis=1)
    xs = dispatch_rows(h.reshape(t, d), pos0, pos1, nt * tm)
    ys = grouped_ffn(xs, tile_expert, n_valid, w1, w3, w2, tm, 2)
    yg = return_rows(ys, jnp.concatenate([pos0, pos1])).reshape(2, b, l, d // 2)
    return combine_final(x, yg, route, gate, final_g, _row_tile(l, ROW_TM))


def _in_weights(w_in):
    o1 = SSD_INNER
    o2 = o1 + XBC_WIDTH
    o3 = o2 + 2 * SSD_HEADS
    w_main = jnp.concatenate([w_in[:, :o2], w_in[:, o3:]], axis=1).astype(BF16)
    w_dt = jnp.pad(w_in[:, o2:o3], ((0, 0), (0, LANES - 2 * SSD_HEADS))).astype(BF16)
    return w_main, w_dt


def kernel(x, c, ctx, c_ctx, w_mod, b_mod, norm1_g, norm2_g, w_in, b_gate, ssd_conv_w, ssd_conv_b, ssd_dt_bias, ssd_a_log, ssd_d, ssd_norm_g, w_ssd_out, sc_conv_w, w_sc_out, w_o, ffn_w1, ffn_w3, ffn_w2, router_w, moe_w1, moe_w3, moe_w2, final_g):
    b, l, d = x.shape
    lc = ctx.shape[1]
    depth = w_mod.shape[0]
    assert depth % 2 == 0, "the final norm is fused into the routed channel mixer of the (odd) last layer"
    cc = jnp.zeros((16, d), F32).at[:b].set(c).at[b].set(c_ctx)
    mod = modulation(cc, w_mod, b_mod)
    zeros_state = jnp.zeros((b, SSD_GROUPS, 2, SSD_STATE, GROUP_W), F32)
    nctx = b * lc
    ctx = ctx.reshape(1, nctx, d)
    tmc = _row_tile(nctx, ROW_TM, lc)
    tmx = _row_tile(l, ROW_TM, GRID_W)

    def per_seq(t):
        return t.reshape(b, lc, t.shape[-1])

    for i in range(depth):
        last = i == depth - 1
        mx = mod[i, :b].reshape(b, N_MOD, 1, d)
        mc = mod[i, b].reshape(1, N_MOD, 1, d)
        w_main, w_dt = _in_weights(w_in[i])
        ssd_p = (ssd_conv_w[i], ssd_conv_b[i], ssd_dt_bias[i], ssd_a_log[i], ssd_d[i])
        out_p = (ssd_norm_g[i], b_gate[i], sc_conv_w[i], w_ssd_out[i].astype(BF16),
                 w_sc_out[i].astype(BF16), w_o[i].astype(BF16))

        if last:
            w_xbc = w_main[:, SSD_INNER:SSD_INNER + XBC_WIDTH]
            proj_c, dt_c = in_proj(ctx, norm1_g[i], mc[:, 0], mc[:, 1], w_xbc, w_dt,
                                   _row_tile(nctx, PROJ_TM), PROJ_TN_XBC)
            _, s_ctx = ssd_mixer(per_seq(proj_c), 0, per_seq(dt_c), *ssd_p, zeros_state, SSD_Q)
        else:
            proj_c, dt_c = in_proj(ctx, norm1_g[i], mc[:, 0], mc[:, 1], w_main, w_dt,
                                   _row_tile(nctx, PROJ_TM), PROJ_TN)
            y_c, s_ctx = ssd_mixer(per_seq(proj_c), SSD_INNER, per_seq(dt_c), *ssd_p, zeros_state, SSD_Q)
            ctx = mixer_out(y_c.reshape(1, nctx, SSD_INNER), proj_c, ctx, mc[:, 2], *out_p, tmc, lc)

        proj_x, dt_x = in_proj(x, norm1_g[i], mx[:, 0], mx[:, 1], w_main, w_dt, _row_tile(l, PROJ_TM), PROJ_TN)
        y_x, _ = ssd_mixer(proj_x, SSD_INNER, dt_x, *ssd_p, s_ctx, SSD_Q)
        x = mixer_out(y_x, proj_x, x, mx[:, 2], *out_p, tmx, GRID_W)

        j = i // 2
        if i % 2 == 0:
            w1, w3, w2 = ffn_w1[j].astype(BF16), ffn_w3[j].astype(BF16), ffn_w2[j].astype(BF16)
            x = ffn_dense(x, norm2_g[i], mx[:, 3], mx[:, 4], mx[:, 5], w1, w3, w2, tmx)
            if not last:
                ctx = ffn_dense(ctx, norm2_g[i], mc[:, 3], mc[:, 4], mc[:, 5], w1, w3, w2, tmc)
        else:
            assert last, "the routed channel mixer is fused with the final norm"
            w1, w3, w2 = moe_w1[j].astype(BF16), moe_w3[j].astype(BF16), moe_w2[j].astype(BF16)
            x = moe_block(x, norm2_g[i], mx[:, 3], mx[:, 4], mx[:, 5], router_w[j], w1, w3, w2, final_g)
    return x
```

```python
import functools

import numpy as np
import jax
import jax.numpy as jnp
from jax import lax
from jax.experimental import pallas as pl
from jax.experimental.pallas import tpu as pltpu
from jax.experimental.pallas import tpu_sc as plsc

F32 = jnp.float32
BF16 = jnp.bfloat16

D_MODEL = 1024
GRID_W = 64
SSD_INNER = 2048
SSD_HEADS = 32
SSD_GROUPS = 4
SSD_HPG = 8
SSD_HEAD_DIM = 64
SSD_STATE = 128
GROUP_W = SSD_HPG * SSD_HEAD_DIM
XBC_WIDTH = SSD_INNER + 2 * SSD_GROUPS * SSD_STATE
SC_WIDTH = 1024
N_MOD = 6
N_EXPERTS = 8
EPS = 1e-6
LOG2E = 1.4426950408889634

LANES = 128
SSD_Q = 128
MOE_TM = 512
ROW_SUBTILES = 2
PROJ_TM = 1024
PROJ_TN = 2048
PROJ_TN_XBC = 1024
ROW_TM = 512


def _row_tile(rows, want, multiple=1):
    t = multiple * max(1, min(want, rows) // multiple)
    assert rows % t == 0, (rows, t)
    return t
VMEM_LIMIT = 56 * 1024 * 1024


def _dot(a, b):
    return jnp.dot(a, b, preferred_element_type=F32)


def _sigmoid(v):
    return 1.0 / (1.0 + jnp.exp(-v))


def _silu(v):
    return v * _sigmoid(v)


def _sigmoid_t(v):
    return 0.5 + 0.5 * jnp.tanh(0.5 * v)


def _silu_t(v):
    hv = 0.5 * v
    return hv + hv * jnp.tanh(hv)


def _split2(a):
    hi = a.astype(BF16)
    lo = (a - hi.astype(F32)).astype(BF16)
    return hi, lo


def _split3(a):
    hi = a.astype(BF16)
    r = a - hi.astype(F32)
    mid = r.astype(BF16)
    lo = (r - mid.astype(F32)).astype(BF16)
    return hi, mid, lo


def _norm_mod(x, g, shift, scale):
    ms = jnp.mean(x * x, axis=-1, keepdims=True)
    return (x * lax.rsqrt(ms + EPS) * g) * (1.0 + scale) + shift


def _cparams(sem, vmem=VMEM_LIMIT):
    return pltpu.CompilerParams(dimension_semantics=sem, vmem_limit_bytes=vmem)


def _mod_kernel(c_ref, w_ref, b_ref, o_ref):
    a_hi, a_lo = _split2(_silu(c_ref[...]))
    w_hi, w_lo = _split2(w_ref[0])
    o_ref[0] = _dot(a_hi, w_hi) + _dot(a_lo, w_hi) + _dot(a_hi, w_lo) + b_ref[0]


def modulation(cc, w_mod, b_mod):
    depth, d, n = w_mod.shape
    tn = 1536
    return pl.pallas_call(
        _mod_kernel,
        grid=(depth, n // tn),
        in_specs=[pl.BlockSpec((16, d), lambda i, j: (0, 0)),
                  pl.BlockSpec((1, d, tn), lambda i, j: (i, 0, j)),
                  pl.BlockSpec((1, 1, tn), lambda i, j: (i, 0, j))],
        out_specs=pl.BlockSpec((1, 16, tn), lambda i, j: (i, 0, j)),
        out_shape=jax.ShapeDtypeStruct((depth, 16, n), F32),
        compiler_params=_cparams(("parallel", "parallel")),
        name="modulation",
    )(cc, w_mod, b_mod.reshape(depth, 1, n))


def _inproj_kernel(x_ref, g_ref, sh_ref, sc_ref, w_ref, wdt_ref, o_ref, dt_ref, h_ref):
    @pl.when(pl.program_id(2) == 0)
    def _():
        hb = _norm_mod(x_ref[0], g_ref[...], sh_ref[0], sc_ref[0]).astype(BF16)
        h_ref[...] = hb
        dt_ref[0] = _dot(hb, wdt_ref[...])

    o_ref[0] = _dot(h_ref[...], w_ref[...]).astype(o_ref.dtype)


def in_proj(x, g, shift, scale, w, wdt, tm, tn):
    b, l, d = x.shape
    n = w.shape[1]
    return pl.pallas_call(
        _inproj_kernel,
        grid=(b, l // tm, n // tn),
        in_specs=[pl.BlockSpec((1, tm, d), lambda i, m, j: (i, m, 0)),
                  pl.BlockSpec((1, d), lambda i, m, j: (0, 0)),
                  pl.BlockSpec((1, 1, d), lambda i, m, j: (i, 0, 0)),
                  pl.BlockSpec((1, 1, d), lambda i, m, j: (i, 0, 0)),
                  pl.BlockSpec((d, tn), lambda i, m, j: (0, j)),
                  pl.BlockSpec((d, LANES), lambda i, m, j: (0, 0))],
        out_specs=[pl.BlockSpec((1, tm, tn), lambda i, m, j: (i, m, j)),
                   pl.BlockSpec((1, tm, LANES), lambda i, m, j: (i, m, 0))],
        out_shape=[jax.ShapeDtypeStruct((b, l, n), BF16),
                   jax.ShapeDtypeStruct((b, l, LANES), F32)],
        scratch_shapes=[pltpu.VMEM((tm, d), BF16)],
        compiler_params=_cparams(("parallel", "parallel", "arbitrary")),
        name="in_proj",
    )(x, g.reshape(1, d), shift, scale, w, wdt)


def _dt_kernel(raw_ref, bias_ref, a_ref, pc_ref, phl_ref, col_ref, row_ref, sp_ref, *, q, ch):
    row = lax.broadcasted_iota(jnp.int32, (q, q), 0)
    col = lax.broadcasted_iota(jnp.int32, (q, q), 1)
    tri_l = jnp.where(col <= row, 1.0, 0.0).astype(BF16)
    tri_u = jnp.where(col >= row, 1.0, 0.0).astype(BF16)
    lane = lax.broadcasted_iota(jnp.int32, (q, LANES), 1)
    fwd = lane < SSD_HEADS

    def sum3(t):
        return t[:, :LANES] + t[:, LANES:2 * LANES] + t[:, 2 * LANES:]

    def place_split(t):
        return _dot(jnp.concatenate(_split2(t), axis=1), phl_ref[...]).astype(BF16)

    for k in range(ch):
        v = raw_ref[0, k * q:(k + 1) * q, :] + bias_ref[...]
        dt = jnp.maximum(v, 0.0) + jnp.log1p(jnp.exp(-jnp.abs(v)))
        d3 = jnp.concatenate(_split3(dt * a_ref[...]), axis=1)
        cs_f = sum3(_dot(tri_l, d3))
        cs_b = sum3(_dot(tri_u, d3))
        cs = jnp.where(fwd, cs_f, cs_b)
        tot = jnp.where(fwd[0:1], cs_f[q - 1:q, :], cs_b[0:1, :])
        sl = slice(k * q, (k + 1) * q)
        cs2 = cs * LOG2E
        col_ref[0, sl, :] = _dot(jnp.concatenate(_split3(cs2), axis=1), pc_ref[...])
        r_t = (cs2 - jnp.log(dt) * LOG2E).T
        dt_t = dt.T
        for g in range(SSD_GROUPS):
            lo = g * SSD_HPG
            dt_f = dt_t[lo:lo + 8, :]
            dt_b = dt_t[SSD_HEADS + lo:SSD_HEADS + lo + 8, :]
            row_ref[0, g, k, 0:8, :] = r_t[lo:lo + 8, :]
            row_ref[0, g, k, 8:16, :] = r_t[SSD_HEADS + lo:SSD_HEADS + lo + 8, :]
            row_ref[0, g, k, 16:24, :] = jnp.log(dt_f + dt_b) * LOG2E
        sp_ref[0, sl, 0:LANES] = place_split(dt * jnp.exp(tot - cs))
        sp_ref[0, sl, LANES:2 * LANES] = place_split(jnp.exp(cs))


def _placements():
    pc = np.zeros((LANES, LANES), np.float32)
    ph = np.zeros((LANES, LANES), np.float32)
    plo = np.zeros((LANES, LANES), np.float32)
    for d in range(2):
        for g in range(SSD_GROUPS):
            for j in range(SSD_HPG):
                src = d * SSD_HEADS + g * SSD_HPG + j
                pc[src, g * 16 + d * 8 + j] = 1.0
                ph[src, (d * SSD_GROUPS + g) * 16 + j] = 1.0
                plo[src, (d * SSD_GROUPS + g) * 16 + 8 + j] = 1.0
    return jnp.asarray(np.concatenate([pc, pc, pc]), BF16), jnp.asarray(np.concatenate([ph, plo]), BF16)


def dt_prep(raw, bias, a, q):
    b, l, _ = raw.shape
    nc = l // q
    ch = min(8, nc)
    gn = SSD_GROUPS
    vspec = pl.BlockSpec((1, LANES), lambda i, c: (0, 0))
    return pl.pallas_call(
        functools.partial(_dt_kernel, q=q, ch=ch),
        grid=(b, nc // ch),
        in_specs=[pl.BlockSpec((1, ch * q, LANES), lambda i, c: (i, c, 0)), vspec, vspec,
                  pl.BlockSpec((3 * LANES, LANES), lambda i, c: (0, 0)),
                  pl.BlockSpec((2 * LANES, LANES), lambda i, c: (0, 0))],
        out_specs=[pl.BlockSpec((1, ch * q, LANES), lambda i, c: (i, c, 0)),
                   pl.BlockSpec((1, gn, ch, 24, q), lambda i, c: (i, 0, c, 0, 0)),
                   pl.BlockSpec((1, ch * q, 2 * LANES), lambda i, c: (i, c, 0))],
        out_shape=[jax.ShapeDtypeStruct((b, l, LANES), F32),
                   jax.ShapeDtypeStruct((b, gn, nc, 24, q), F32),
                   jax.ShapeDtypeStruct((b, l, 2 * LANES), BF16)],
        compiler_params=_cparams(("parallel", "parallel")),
        name="dt_prep",
    )(raw, bias, a, *_placements())


def _ssd_kernel(xp_ref, bp_ref, cp_ref, cwx_ref, cwb_ref, cwc_ref, cbx_ref, cbb_ref, cbc_ref,
                col_ref, row_ref, sp_ref, e2f_ref, e2b_ref, dexp_ref, s0_ref,
                y_ref, sfin_ref,
                xs_ref, cc_ref, bt_ref, xwf_ref, sbe_ref, edge_ref, sf_ref, sb_ref, yo_ref, *, l, q):
    nc = l // q
    sr = lax.broadcasted_iota(jnp.int32, (q, q + 32), 0)
    sc = lax.broadcasted_iota(jnp.int32, (q, q + 32), 1)
    shift_prev = jnp.where((sc == sr - 1) | ((sr == 0) & (sc == q + 15)), 1.0, 0.0).astype(BF16)
    shift_next = jnp.where(((sc == sr + 1) & (sc < q)) | ((sr == q - 1) & (sc == q + 16)), 1.0, 0.0).astype(BF16)

    def conv_silu(srcs, w, b, k):
        r0 = pl.multiple_of(k * q, q)
        p0 = pl.multiple_of(jnp.maximum(r0 - 16, 0), 16)
        n0 = pl.multiple_of(jnp.minimum(r0 + q, l - 16), 16)

        def rows_at(start, n):
            parts = [r[0, pl.ds(start, n), ls] for r, ls in srcs]
            return parts[0] if len(parts) == 1 else jnp.concatenate(parts, axis=1)

        blk = rows_at(r0, q)
        before = rows_at(p0, 16)
        after = rows_at(n0, 16)
        before = jnp.where(k > 0, before, jnp.zeros_like(before))
        after = jnp.where(k < nc - 1, after, jnp.zeros_like(after))
        stacked = jnp.concatenate([blk, before, after], axis=0)
        v = (w[0:1] * _dot(shift_prev, stacked) + w[1:2] * blk.astype(F32)
             + w[2:3] * _dot(shift_next, stacked) + b)
        return _silu_t(v)

    all_lanes = slice(None)

    def prep(k):
        r0 = pl.multiple_of(k * q, q)
        x = conv_silu(((xp_ref, all_lanes),), cwx_ref[...], cbx_ref[...], k)
        xs_ref[pl.ds(r0, q), :] = x.astype(BF16)
        bc = conv_silu(((bp_ref, all_lanes), (cp_ref, all_lanes)),
                       jnp.concatenate([cwb_ref[...], cwc_ref[...]], axis=1),
                       jnp.concatenate([cbb_ref[...], cbc_ref[...]], axis=1), k)
        b_t = bc[:, :SSD_STATE].T.astype(BF16)
        bt_ref[k] = b_t
        cc_ref[pl.ds(r0, q), :] = bc[:, SSD_STATE:].astype(BF16)
        sp_w = sp_ref[0, pl.ds(r0, q), 0:LANES]
        xwf_ref[pl.ds(r0, q), :] = (x * _dot(sp_w, e2f_ref[0, 0])).astype(BF16)
        xw_b = (x * _dot(sp_w, e2b_ref[0, 0])).astype(BF16)
        e_last = sp_ref[0, pl.ds(pl.multiple_of(r0 + q - 16, 16), 16), LANES:2 * LANES]
        e_first = sp_ref[0, pl.ds(r0, 16), LANES:2 * LANES]
        edge_ref[k] = _dot(e_last, e2f_ref[0, 0])[15:16, :]
        edge_b = _dot(e_first, e2b_ref[0, 0])[0:1, :]
        return b_t, xw_b, edge_b

    sb_ref[...] = s0_ref[0, 0, 1]

    def bstep(i, carry):
        c = nc - 1 - i
        b_t, xw_b, edge_b = prep(c)
        sbe_ref[c] = sb_ref[...].astype(BF16)
        sb_ref[...] = sb_ref[...] * edge_b + _dot(b_t, xw_b)
        return carry

    lax.fori_loop(0, nc, bstep, 0, unroll=8)
    sfin_ref[0, 0, 1] = sb_ref[...]

    sf_ref[...] = s0_ref[0, 0, 0]
    li = lax.broadcasted_iota(jnp.int32, (q, q), 0)
    si = lax.broadcasted_iota(jnp.int32, (q, q), 1)
    lower = si <= li
    eye = si == li
    left = lax.broadcasted_iota(jnp.int32, (q, LANES), 1) < SSD_HEAD_DIM
    fu = yo_ref.shape[0]
    lane0 = 16 * pl.program_id(1)

    def chunk(c, slot):
        r0 = pl.multiple_of(c * q, q)
        cc = cc_ref[pl.ds(r0, q), :]
        sp_e = sp_ref[0, pl.ds(r0, q), LANES:2 * LANES]
        yo_ref[slot] = (_dot(sp_e, e2f_ref[0, 0]) * _dot(cc, sf_ref[...].astype(BF16))
                        + _dot(sp_e, e2b_ref[0, 0]) * _dot(cc, sbe_ref[c]))
        sf_ref[...] = sf_ref[...] * edge_ref[c] + _dot(bt_ref[c], xwf_ref[pl.ds(r0, q), :])
        g = _dot(cc, bt_ref[c])
        col = pltpu.roll(col_ref[0, pl.ds(r0, q), :], LANES - lane0, 1)
        row = row_ref[0, 0, c]
        for j in range(SSD_HPG // 2):
            lanes = slice(j * LANES, (j + 1) * LANES)
            xf = xs_ref[pl.ds(r0, q), lanes].astype(F32)
            x_diag = jnp.concatenate([jnp.where(left, xf, 0.0), jnp.where(left, 0.0, xf)], axis=0).astype(BF16)
            ms = []
            for h in (2 * j, 2 * j + 1):
                cs_l = jnp.take_along_axis(col, jnp.where(lower, h, 8 + h), axis=1, mode="promise_in_bounds")
                arg = cs_l - jnp.where(lower, row[h:h + 1, :], row[8 + h:9 + h, :])
                p = jnp.exp2(jnp.where(eye, row[16 + h:17 + h, :], arg))
                ms.append((g * p).astype(BF16))
            y = _dot(jnp.concatenate(ms, axis=1), x_diag) + yo_ref[slot, :, lanes] + dexp_ref[0, :, lanes] * xf
            y_ref[0, pl.ds(r0, q), lanes] = y.astype(y_ref.dtype)

    def fstep(i, carry):
        for slot in range(fu):
            chunk(fu * i + slot, slot)
        return carry

    lax.fori_loop(0, nc // fu, fstep, 0)
    sfin_ref[0, 0, 0] = sf_ref[...]


def _expanders():
    e = np.zeros((2, SSD_GROUPS, LANES, GROUP_W), np.float32)
    for d in range(2):
        for g in range(SSD_GROUPS):
            for s in range(2):
                for j in range(SSD_HPG):
                    e[d, g, (d * SSD_GROUPS + g) * 16 + s * 8 + j, j * SSD_HEAD_DIM:(j + 1) * SSD_HEAD_DIM] = 1.0
    return jnp.asarray(e, BF16)


def ssd_mixer(proj, xbc_col0, dt_raw, conv_w, conv_b, dt_bias, a_log, d_skip, s0, q):
    b, l, _ = proj.shape
    assert l % q == 0 and q % 16 == 0, (l, q)
    nc = l // q
    gn = SSD_GROUPS
    bias = jnp.zeros((1, LANES), F32).at[0, :2 * SSD_HEADS].set(dt_bias.reshape(-1))
    a = jnp.zeros((1, LANES), F32).at[0, :2 * SSD_HEADS].set(-jnp.exp(a_log.reshape(-1)))
    colg, rowg, sp = dt_prep(dt_raw, bias, a, q)
    dexp = jnp.repeat(d_skip.astype(F32), SSD_HEAD_DIM).reshape(gn, 1, GROUP_W)
    e2 = _expanders()

    cw = conv_w.astype(F32)
    cb = conv_b.astype(F32).reshape(1, -1)
    nb = SSD_INNER
    cwx, cwb, cwc = cw[:, :nb], cw[:, nb:nb + gn * SSD_STATE], cw[:, nb + gn * SSD_STATE:]
    cbx, cbb, cbc = cb[:, :nb], cb[:, nb:nb + gn * SSD_STATE], cb[:, nb + gn * SSD_STATE:]

    fu = max(u for u in (8, 4, 2, 1) if nc % u == 0)
    xo = xbc_col0 // GROUP_W
    bo = (xbc_col0 + SSD_INNER) // SSD_STATE
    co = bo + gn
    st_spec = pl.BlockSpec((1, 1, 2, SSD_STATE, GROUP_W), lambda i, g: (i, g, 0, 0, 0))
    y, sfin = pl.pallas_call(
        functools.partial(_ssd_kernel, l=l, q=q),
        grid=(b, gn),
        in_specs=[pl.BlockSpec((1, l, GROUP_W), lambda i, g: (i, 0, xo + g)),
                  pl.BlockSpec((1, l, SSD_STATE), lambda i, g: (i, 0, bo + g)),
                  pl.BlockSpec((1, l, SSD_STATE), lambda i, g: (i, 0, co + g)),
                  pl.BlockSpec((3, GROUP_W), lambda i, g: (0, g)),
                  pl.BlockSpec((3, SSD_STATE), lambda i, g: (0, g)),
                  pl.BlockSpec((3, SSD_STATE), lambda i, g: (0, g)),
                  pl.BlockSpec((1, GROUP_W), lambda i, g: (0, g)),
                  pl.BlockSpec((1, SSD_STATE), lambda i, g: (0, g)),
                  pl.BlockSpec((1, SSD_STATE), lambda i, g: (0, g)),
                  pl.BlockSpec((1, l, LANES), lambda i, g: (i, 0, 0)),
                  pl.BlockSpec((1, 1, nc, 24, q), lambda i, g: (i, g, 0, 0, 0)),
                  pl.BlockSpec((1, l, 2 * LANES), lambda i, g: (i, 0, 0)),
                  pl.BlockSpec((1, 1, LANES, GROUP_W), lambda i, g: (0, g, 0, 0)),
                  pl.BlockSpec((1, 1, LANES, GROUP_W), lambda i, g: (1, g, 0, 0)),
                  pl.BlockSpec((1, 1, GROUP_W), lambda i, g: (g, 0, 0)),
                  st_spec],
        out_specs=[pl.BlockSpec((1, l, GROUP_W), lambda i, g: (i, 0, g)), st_spec],
        out_shape=[jax.ShapeDtypeStruct((b, l, SSD_INNER), BF16),
                   jax.ShapeDtypeStruct((b, gn, 2, SSD_STATE, GROUP_W), F32)],
        scratch_shapes=[pltpu.VMEM((l, GROUP_W), BF16),
                        pltpu.VMEM((l, SSD_STATE), BF16),
                        pltpu.VMEM((nc, SSD_STATE, q), BF16),
                        pltpu.VMEM((l, GROUP_W), BF16),
                        pltpu.VMEM((nc, SSD_STATE, GROUP_W), BF16),
                        pltpu.VMEM((nc, 1, GROUP_W), F32),
                        pltpu.VMEM((SSD_STATE, GROUP_W), F32),
                        pltpu.VMEM((SSD_STATE, GROUP_W), F32),
                        pltpu.VMEM((fu, q, GROUP_W), F32)],
        compiler_params=_cparams(("parallel", "parallel")),
        name="ssd",
    )(proj, proj, proj, cwx, cwb, cwc, cbx, cbb, cbc, colg, rowg, sp, e2, e2, dexp, s0)
    return y, sfin


def _mixout_kernel(y_ref, z_ref, gb_ref, gc_ref, hv_ref, g0_ref, g1_ref, x_ref, gate_ref,
                   ng_ref, bg_ref, scw_ref, wssd_ref, wsc_ref, wo_ref, o_ref, *, tm, period):
    ns = 2 if (tm // 2) % period == 0 else 1
    ts = tm // ns
    pos = lax.broadcasted_iota(jnp.int32, (ts, 1), 0) % period
    w = scw_ref[...]
    bg = bg_ref[...]
    for s in range(ns):
        r = slice(s * ts, (s + 1) * ts)
        yz = y_ref[0, r, :].astype(F32) * _silu_t(z_ref[0, r, :].astype(F32))
        ms = jnp.mean(yz * yz, axis=-1, keepdims=True)
        yn = (yz * lax.rsqrt(ms + EPS) * ng_ref[...]).astype(BF16)
        y_ssd = _dot(yn, wssd_ref[...])

        u = gc_ref[0, r, :].astype(F32) * hv_ref[0, r, :].astype(F32)
        u_prev = jnp.where(pos == 0, 0.0, pltpu.roll(u, 1, 0))
        u_next = jnp.where(pos == period - 1, 0.0, pltpu.roll(u, ts - 1, 0))
        v = w[0:1] * u_prev + w[1:2] * u + w[2:3] * u_next
        y_sc = _dot((gb_ref[0, r, :].astype(F32) * v).astype(BF16), wsc_ref[...])

        g0 = _sigmoid_t(g0_ref[0, r, :].astype(F32) + bg[:, :D_MODEL])
        g1 = _sigmoid_t(g1_ref[0, r, :].astype(F32) + bg[:, D_MODEL:])
        out = _dot((g0 * y_ssd + g1 * y_sc).astype(BF16), wo_ref[...])
        o_ref[0, r, :] = x_ref[0, r, :] + gate_ref[0] * out


def mixer_out(y, proj, x, gate, norm_g, b_gate, sc_conv_w, w_ssd, w_sc, w_o, tm, period):
    b, l, d = x.shape
    pc = lambda k: pl.BlockSpec((1, tm, d), lambda i, m, k=k: (i, m, k))
    full = lambda shp: pl.BlockSpec(shp, lambda i, m: (0,) * len(shp))
    return pl.pallas_call(
        functools.partial(_mixout_kernel, tm=tm, period=period),
        grid=(b, l // tm),
        in_specs=[pl.BlockSpec((1, tm, SSD_INNER), lambda i, m: (i, m, 0)),
                  pl.BlockSpec((1, tm, SSD_INNER), lambda i, m: (i, m, 0)),
                  pc(5), pc(6), pc(7), pc(8), pc(9),
                  pl.BlockSpec((1, tm, d), lambda i, m: (i, m, 0)),
                  pl.BlockSpec((1, 1, d), lambda i, m: (i, 0, 0)),
                  full((1, SSD_INNER)), full((1, 2 * d)), full((3, SC_WIDTH)),
                  full((SSD_INNER, d)), full((SC_WIDTH, d)), full((d, d))],
        out_specs=pl.BlockSpec((1, tm, d), lambda i, m: (i, m, 0)),
        out_shape=jax.ShapeDtypeStruct((b, l, d), F32),
        compiler_params=_cparams(("parallel", "parallel")),
        name="mixer_out",
    )(y, proj, proj, proj, proj, proj, proj, x, gate,
      norm_g.reshape(1, -1), b_gate.reshape(1, -1), sc_conv_w, w_ssd, w_sc, w_o)


def _ffn_kernel(x_ref, g_ref, sh_ref, sc_ref, gate_ref, w1_ref, w3_ref, w2_ref, o_ref, *, nf):
    fw = w1_ref.shape[1] // nf
    ts = x_ref.shape[1] // ROW_SUBTILES
    for s in range(ROW_SUBTILES):
        r = slice(s * ts, (s + 1) * ts)
        x = x_ref[0, r, :]
        hb = _norm_mod(x, g_ref[...], sh_ref[0], sc_ref[0]).astype(BF16)
        acc = None
        for k in range(nf):
            a = _dot(hb, w1_ref[:, k * fw:(k + 1) * fw])
            bb = _dot(hb, w3_ref[:, k * fw:(k + 1) * fw])
            part = _dot((_silu_t(a) * bb).astype(BF16), w2_ref[k * fw:(k + 1) * fw, :])
            acc = part if acc is None else acc + part
        o_ref[0, r, :] = x + gate_ref[0] * acc


def ffn_dense(x, g, shift, scale, gate, w1, w3, w2, tm):
    b, l, d = x.shape
    f = w1.shape[1]
    vec = pl.BlockSpec((1, 1, d), lambda i, m: (i, 0, 0))
    const = lambda shp: pl.BlockSpec(shp, lambda i, m: (0, 0), pipeline_mode=pl.Buffered(1))
    return pl.pallas_call(
        functools.partial(_ffn_kernel, nf=2),
        grid=(b, l // tm),
        in_specs=[pl.BlockSpec((1, tm, d), lambda i, m: (i, m, 0)),
                  pl.BlockSpec((1, d), lambda i, m: (0, 0)),
                  vec, vec, vec, const((d, f)), const((d, f)), const((f, d))],
        out_specs=pl.BlockSpec((1, tm, d), lambda i, m: (i, m, 0)),
        out_shape=jax.ShapeDtypeStruct((b, l, d), F32),
        compiler_params=_cparams(("parallel", "parallel")),
        name="ffn_dense",
    )(x, g.reshape(1, d), shift, scale, gate, w1, w3, w2)


def _router_kernel(x_ref, g_ref, sh_ref, sc_ref, rw_ref, h_ref, route_ref, route_t_ref, cnt_ref, run_ref, *, tm):
    @pl.when((pl.program_id(0) == 0) & (pl.program_id(1) == 0))
    def _():
        run_ref[...] = jnp.zeros_like(run_ref)

    h = _norm_mod(x_ref[0], g_ref[...], sh_ref[0], sc_ref[0])
    h_hi, h_lo = _split2(h)
    h_ref[0] = h
    w_hi, w_lo = _split2(rw_ref[...])
    logits = _dot(h_hi, w_hi) + _dot(h_lo, w_hi) + _dot(h_hi, w_lo)
    lane = lax.broadcasted_iota(jnp.int32, (tm, LANES), 1)
    ninf = float("-inf")
    lg = jnp.where(lane < N_EXPERTS, logits, ninf)
    m1 = jnp.max(lg, axis=1, keepdims=True)
    i1 = jnp.min(jnp.where(lg == m1, lane, LANES), axis=1, keepdims=True)
    lg2 = jnp.where(lane == i1, ninf, lg)
    m2 = jnp.max(lg2, axis=1, keepdims=True)
    i2 = jnp.min(jnp.where(lg2 == m2, lane, LANES), axis=1, keepdims=True)
    e2 = jnp.exp(m2 - m1)
    den = 1.0 + e2
    sel1 = jnp.where(lane == i1, 1.0, 0.0)
    sel2 = jnp.where(lane == i2, 1.0, 0.0)
    cnt = sel1 + sel2
    r = lax.broadcasted_iota(jnp.int32, (tm, tm), 0)
    c = lax.broadcasted_iota(jnp.int32, (tm, tm), 1)
    tri = jnp.where(c < r, 1.0, 0.0).astype(BF16)
    base = _dot(tri, cnt.astype(BF16)) + run_ref[0:1, :]
    r1 = jnp.sum(sel1 * base, axis=1, keepdims=True)
    r2 = jnp.sum(sel2 * base, axis=1, keepdims=True)
    vals = (i1.astype(F32), i2.astype(F32), 1.0 / den, e2 / den, r1, r2)
    out = jnp.zeros((tm, LANES), F32)
    for k, v in enumerate(vals):
        out = jnp.where(lane == k, v, out)
    route_ref[0] = out
    route_t_ref[...] = out.T
    new_run = run_ref[...] + jnp.sum(cnt, axis=0, keepdims=True)
    run_ref[...] = new_run
    cnt_ref[...] = new_run


def router(x, g, shift, scale, router_w, tm):
    b, l, d = x.shape
    rw = jnp.zeros((d, LANES), F32).at[:, :N_EXPERTS].set(router_w)
    vec = pl.BlockSpec((1, 1, d), lambda i, m: (i, 0, 0))
    mt = l // tm
    return pl.pallas_call(
        functools.partial(_router_kernel, tm=tm),
        grid=(b, l // tm),
        in_specs=[pl.BlockSpec((1, tm, d), lambda i, m: (i, m, 0)),
                  pl.BlockSpec((1, d), lambda i, m: (0, 0)),
                  vec, vec, pl.BlockSpec((d, LANES), lambda i, m: (0, 0))],
        out_specs=[pl.BlockSpec((1, tm, d), lambda i, m: (i, m, 0)),
                   pl.BlockSpec((1, tm, LANES), lambda i, m: (i, m, 0)),
                   pl.BlockSpec((LANES, tm), lambda i, m: (0, i * mt + m)),
                   pl.BlockSpec((8, LANES), lambda i, m: (0, 0))],
        out_shape=[jax.ShapeDtypeStruct((b, l, d), F32),
                   jax.ShapeDtypeStruct((b, l, LANES), F32),
                   jax.ShapeDtypeStruct((LANES, b * l), F32),
                   jax.ShapeDtypeStruct((8, LANES), F32)],
        scratch_shapes=[pltpu.VMEM((8, LANES), F32)],
        compiler_params=_cparams(("arbitrary", "arbitrary")),
        name="router",
    )(x, g.reshape(1, d), shift, scale, rw)


def _pack_bf16_pairs(y):
    k = y.shape[1] // 2
    bits = lax.bitcast_convert_type(y.astype(BF16).astype(F32), jnp.uint32)
    return bits[:, :k] | (bits[:, k:] >> 16)


def _unpack_bf16_pairs(p):
    hi = lax.bitcast_convert_type(p & jnp.uint32(0xFFFF0000), F32)
    lo = lax.bitcast_convert_type(p << 16, F32)
    return jnp.concatenate([hi, lo], axis=1)


def _gffn_kernel(te_ref, nv_ref, x_ref, w1_ref, w3_ref, w2_ref, o_ref, acc_ref, *, nf):
    i = pl.program_id(0)
    f = pl.program_id(1)

    @pl.when(i < nv_ref[0])
    def _():
        ts = x_ref.shape[0] // ROW_SUBTILES
        parts = []
        for s in range(ROW_SUBTILES):
            x = x_ref[s * ts:(s + 1) * ts, :].astype(BF16)
            a = _dot(x, w1_ref[0])
            bb = _dot(x, w3_ref[0])
            parts.append(_dot((_silu_t(a) * bb).astype(BF16), w2_ref[0]))
        part = jnp.concatenate(parts, axis=0)

        @pl.when(f == 0)
        def _():
            acc_ref[...] = part

        @pl.when(f > 0)
        def _():
            acc_ref[...] += part

        @pl.when(f == nf - 1)
        def _():
            o_ref[...] = _pack_bf16_pairs(acc_ref[...])


def grouped_ffn(xs, tile_expert, n_valid, w1, w3, w2, tm, nf):
    rows, d = xs.shape
    nt = rows // tm
    f = w1.shape[2]
    fw = f // nf

    def tile(i, nv):
        return jnp.minimum(i, nv[0] - 1)

    def fchunk(i, k, nv):
        return jnp.where(i < nv[0], k, nf - 1)

    grid_spec = pltpu.PrefetchScalarGridSpec(
        num_scalar_prefetch=2,
        grid=(nt, nf),
        in_specs=[pl.BlockSpec((tm, d), lambda i, k, te, nv: (tile(i, nv), 0)),
                  pl.BlockSpec((1, d, fw), lambda i, k, te, nv: (te[i], 0, fchunk(i, k, nv))),
                  pl.BlockSpec((1, d, fw), lambda i, k, te, nv: (te[i], 0, fchunk(i, k, nv))),
                  pl.BlockSpec((1, fw, d), lambda i, k, te, nv: (te[i], fchunk(i, k, nv), 0))],
        out_specs=pl.BlockSpec((tm, d // 2), lambda i, k, te, nv: (tile(i, nv), 0)),
        scratch_shapes=[pltpu.VMEM((tm, d), F32)],
    )
    return pl.pallas_call(
        functools.partial(_gffn_kernel, nf=nf),
        grid_spec=grid_spec,
        out_shape=jax.ShapeDtypeStruct((rows, d // 2), jnp.uint32),
        compiler_params=_cparams(("arbitrary", "arbitrary")),
        name="grouped_ffn",
    )(tile_expert, n_valid, xs, w1, w3, w2)


def _combine_kernel(x_ref, y0_ref, y1_ref, route_ref, gate_ref, fg_ref, o_ref):
    r = route_ref[0]
    moe = r[:, 2:3] * _unpack_bf16_pairs(y0_ref[0, 0]) + r[:, 3:4] * _unpack_bf16_pairs(y1_ref[0, 0])
    xn = x_ref[0] + gate_ref[0] * moe
    ms = jnp.mean(xn * xn, axis=-1, keepdims=True)
    o_ref[0] = xn * lax.rsqrt(ms + EPS) * fg_ref[...]


def combine_final(x, yg, route, gate, final_g, tm):
    b, l, d = x.shape
    return pl.pallas_call(
        _combine_kernel,
        grid=(b, l // tm),
        in_specs=[pl.BlockSpec((1, tm, d), lambda i, m: (i, m, 0)),
                  pl.BlockSpec((1, 1, tm, d // 2), lambda i, m: (0, i, m, 0)),
                  pl.BlockSpec((1, 1, tm, d // 2), lambda i, m: (1, i, m, 0)),
                  pl.BlockSpec((1, tm, LANES), lambda i, m: (i, m, 0)),
                  pl.BlockSpec((1, 1, d), lambda i, m: (i, 0, 0)),
                  pl.BlockSpec((1, d), lambda i, m: (0, 0))],
        out_specs=pl.BlockSpec((1, tm, d), lambda i, m: (i, m, 0)),
        out_shape=jax.ShapeDtypeStruct((b, l, d), F32),
        compiler_params=_cparams(("parallel", "parallel")),
        name="combine_final",
    )(x, yg, yg, route, gate, final_g.reshape(1, d))


SC_CORES = 2
SC_SUBCORES = 16
SC_WORKERS = SC_CORES * SC_SUBCORES
SC_STREAM_BYTES = 256 * 1024
SC_STREAM_ROWS = 128


def _sc_rows(per_worker, d, dtype):
    return min(SC_STREAM_ROWS, SC_STREAM_BYTES // (d * jnp.dtype(dtype).itemsize), per_worker)


def _sc_mesh():
    return plsc.VectorSubcoreMesh(core_axis_name="c", subcore_axis_name="s",
                                  num_cores=SC_CORES, num_subcores=SC_SUBCORES)


def dispatch_rows(h, pos0, pos1, n_rows):
    t, d = h.shape
    per_w = t // SC_WORKERS
    ch = _sc_rows(per_w, d, h.dtype)
    assert t % SC_WORKERS == 0 and per_w % ch == 0 and ch % 8 == 0, (t, ch)

    @functools.partial(
        pl.kernel, mesh=_sc_mesh(),
        out_type=jax.ShapeDtypeStruct((n_rows, d), h.dtype),
        scratch_types=[pltpu.VMEM((ch,), jnp.int32), pltpu.VMEM((ch,), jnp.int32),
                       pltpu.VMEM((ch, d), h.dtype), pltpu.SemaphoreType.DMA],
        name="moe_dispatch")
    def scatter(h_hbm, p0_hbm, p1_hbm, out_hbm, i0_v, i1_v, rows_v, sem):
        base = (lax.axis_index("s") * SC_CORES + lax.axis_index("c")) * per_w

        @pl.loop(0, per_w // ch)
        def _(j):
            off = base + j * ch
            pltpu.sync_copy(h_hbm.at[pl.ds(off, ch)], rows_v)
            pltpu.sync_copy(p0_hbm.at[pl.ds(off, ch)], i0_v)
            pltpu.sync_copy(p1_hbm.at[pl.ds(off, ch)], i1_v)
            pltpu.async_copy(rows_v, out_hbm.at[i0_v], sem).wait()
            pltpu.async_copy(rows_v, out_hbm.at[i1_v], sem).wait()

    return scatter(h, pos0, pos1)


def return_rows(ys, idx):
    n = idx.shape[0]
    d = ys.shape[1]
    per_w = n // SC_WORKERS
    ch = _sc_rows(per_w, d, ys.dtype)
    assert n % SC_WORKERS == 0 and per_w % ch == 0 and ch % 8 == 0, (n, ch)

    @functools.partial(
        pl.kernel, mesh=_sc_mesh(),
        out_type=jax.ShapeDtypeStruct((n, d), ys.dtype),
        scratch_types=[pltpu.VMEM((ch,), jnp.int32), pltpu.VMEM((ch, d), ys.dtype), pltpu.SemaphoreType.DMA],
        name="moe_return")
    def gather(ys_hbm, idx_hbm, out_hbm, idx_v, rows_v, sem):
        base = (lax.axis_index("s") * SC_CORES + lax.axis_index("c")) * per_w

        @pl.loop(0, per_w // ch)
        def _(j):
            off = base + j * ch
            pltpu.sync_copy(idx_hbm.at[pl.ds(off, ch)], idx_v)
            pltpu.async_copy(ys_hbm.at[idx_v], rows_v, sem).wait()
            pltpu.sync_copy(rows_v, out_hbm.at[pl.ds(off, ch)])

    return gather(ys, idx)


def moe_block(x, g, shift, scale, gate, router_w, w1, w3, w2, final_g):
    b, l, d = x.shape
    t = b * l
    tm = MOE_TM
    h, route, route_t, counts = router(x, g, shift, scale, router_w, _row_tile(l, ROW_TM))
    cnt = counts[0, :N_EXPERTS].astype(jnp.int32)
    gs = ((cnt + tm - 1) // tm) * tm
    ends = jnp.cumsum(gs)
    offs = ends - gs

    def sorted_row(choice):
        e = route_t[choice].astype(jnp.int32)
        start = sum(jnp.where(e == k, offs[k], 0) for k in range(N_EXPERTS))
        return start + route_t[4 + choice].astype(jnp.int32)

    pos0, pos1 = sorted_row(0), sorted_row(1)
    nt = (2 * t) // tm + N_EXPERTS
    n_valid = (ends[-1] // tm).astype(jnp.int32).reshape(1)
    tile = jnp.minimum(jnp.arange(nt, dtype=jnp.int32), n_valid[0] - 1)
    tile_expert = jnp.sum((tile[:, None] >= (ends // tm)[None, :]).astype(jnp.int32), axis=1)
    xs = dispatch_rows(h.reshape(t, d), pos0, pos1, nt * tm)
    ys = grouped_ffn(xs, tile_expert, n_valid, w1, w3, w2, tm, 2)
    yg = return_rows(ys, jnp.concatenate([pos0, pos1])).reshape(2, b, l, d // 2)
    return combine_final(x, yg, route, gate, final_g, _row_tile(l, ROW_TM))


def _in_weights(w_in):
    o1 = SSD_INNER
    o2 = o1 + XBC_WIDTH
    o3 = o2 + 2 * SSD_HEADS
    w_main = jnp.concatenate([w_in[..., :o2].astype(BF16), w_in[..., o3:].astype(BF16)], axis=-1)
    w_dt = jnp.pad(w_in[..., o2:o3], ((0, 0), (0, 0), (0, LANES - 2 * SSD_HEADS))).astype(BF16)
    return w_main, w_dt


def kernel(x, c, ctx, c_ctx, w_mod, b_mod, norm1_g, norm2_g, w_in, b_gate, ssd_conv_w, ssd_conv_b, ssd_dt_bias, ssd_a_log, ssd_d, ssd_norm_g, w_ssd_out, sc_conv_w, w_sc_out, w_o, ffn_w1, ffn_w3, ffn_w2, router_w, moe_w1, moe_w3, moe_w2, final_g):
    b, l, d = x.shape
    lc = ctx.shape[1]
    depth = w_mod.shape[0]
    assert depth % 2 == 0, "the final norm is fused into the routed channel mixer of the (odd) last layer"
    cc = jnp.zeros((16, d), F32).at[:b].set(c).at[b].set(c_ctx)
    mod = modulation(cc, w_mod, b_mod)
    zeros_state = jnp.zeros((b, SSD_GROUPS, 2, SSD_STATE, GROUP_W), F32)
    nctx = b * lc
    ctx = ctx.reshape(1, nctx, d)
    tmc = _row_tile(nctx, ROW_TM, lc)
    tmx = _row_tile(l, ROW_TM, GRID_W)

    def per_seq(t):
        return t.reshape(b, lc, t.shape[-1])

    w_main_all, w_dt_all = _in_weights(w_in)

    for i in range(depth):
        last = i == depth - 1
        mx = mod[i, :b].reshape(b, N_MOD, 1, d)
        mc = mod[i, b].reshape(1, N_MOD, 1, d)
        w_main, w_dt = w_main_all[i], w_dt_all[i]
        ssd_p = (ssd_conv_w[i], ssd_conv_b[i], ssd_dt_bias[i], ssd_a_log[i], ssd_d[i])
        out_p = (ssd_norm_g[i], b_gate[i], sc_conv_w[i], w_ssd_out[i].astype(BF16),
                 w_sc_out[i].astype(BF16), w_o[i].astype(BF16))

        if last:
            w_xbc = w_main[:, SSD_INNER:SSD_INNER + XBC_WIDTH]
            proj_c, dt_c = in_proj(ctx, norm1_g[i], mc[:, 0], mc[:, 1], w_xbc, w_dt,
                                   _row_tile(nctx, PROJ_TM), PROJ_TN_XBC)
            _, s_ctx = ssd_mixer(per_seq(proj_c), 0, per_seq(dt_c), *ssd_p, zeros_state, SSD_Q)
        else:
            proj_c, dt_c = in_proj(ctx, norm1_g[i], mc[:, 0], mc[:, 1], w_main, w_dt,
                                   _row_tile(nctx, PROJ_TM), PROJ_TN)
            y_c, s_ctx = ssd_mixer(per_seq(proj_c), SSD_INNER, per_seq(dt_c), *ssd_p, zeros_state, SSD_Q)
            ctx = mixer_out(y_c.reshape(1, nctx, SSD_INNER), proj_c, ctx, mc[:, 2], *out_p, tmc, lc)

        proj_x, dt_x = in_proj(x, norm1_g[i], mx[:, 0], mx[:, 1], w_main, w_dt, _row_tile(l, PROJ_TM), PROJ_TN)
        y_x, _ = ssd_mixer(proj_x, SSD_INNER, dt_x, *ssd_p, s_ctx, SSD_Q)
        x = mixer_out(y_x, proj_x, x, mx[:, 2], *out_p, tmx, GRID_W)

        j = i // 2
        if i % 2 == 0:
            w1, w3, w2 = ffn_w1[j].astype(BF16), ffn_w3[j].astype(BF16), ffn_w2[j].astype(BF16)
            x = ffn_dense(x, norm2_g[i], mx[:, 3], mx[:, 4], mx[:, 5], w1, w3, w2, tmx)
            if not last:
                ctx = ffn_dense(ctx, norm2_g[i], mc[:, 3], mc[:, 4], mc[:, 5], w1, w3, w2, tmc)
        else:
            assert last, "the routed channel mixer is fused with the final norm"
            w1, w3, w2 = moe_w1[j].astype(BF16), moe_w3[j].astype(BF16), moe_w2[j].astype(BF16)
            x = moe_block(x, norm2_g[i], mx[:, 3], mx[:, 4], mx[:, 5], router_w[j], w1, w3, w2, final_g)
    return x
```

```python
import functools

import numpy as np
import jax
import jax.numpy as jnp
from jax import lax
from jax.experimental import pallas as pl
from jax.experimental.pallas import tpu as pltpu
from jax.experimental.pallas import tpu_sc as plsc

F32 = jnp.float32
BF16 = jnp.bfloat16

D_MODEL = 1024
GRID_W = 64
SSD_INNER = 2048
SSD_HEADS = 32
SSD_GROUPS = 4
SSD_HPG = 8
SSD_HEAD_DIM = 64
SSD_STATE = 128
GROUP_W = SSD_HPG * SSD_HEAD_DIM
XBC_WIDTH = SSD_INNER + 2 * SSD_GROUPS * SSD_STATE
SC_WIDTH = 1024
N_MOD = 6
N_EXPERTS = 8
EPS = 1e-6
LOG2E = 1.4426950408889634

LANES = 128
SSD_Q = 128
MOE_TM = 512
ROW_SUBTILES = 2
PROJ_TM = 1024
PROJ_TN = 2048
PROJ_TN_XBC = 1024
ROW_TM = 512
CAST_ROWS = 256


def _row_tile(rows, want, multiple=1):
    t = multiple * max(1, min(want, rows) // multiple)
    assert rows % t == 0, (rows, t)
    return t
VMEM_LIMIT = 56 * 1024 * 1024


def _dot(a, b):
    return jnp.dot(a, b, preferred_element_type=F32)


def _sigmoid(v):
    return 1.0 / (1.0 + jnp.exp(-v))


def _silu(v):
    return v * _sigmoid(v)


def _sigmoid_t(v):
    return 0.5 + 0.5 * jnp.tanh(0.5 * v)


def _silu_t(v):
    hv = 0.5 * v
    return hv + hv * jnp.tanh(hv)


def _split2(a):
    hi = a.astype(BF16)
    lo = (a - hi.astype(F32)).astype(BF16)
    return hi, lo


def _split3(a):
    hi = a.astype(BF16)
    r = a - hi.astype(F32)
    mid = r.astype(BF16)
    lo = (r - mid.astype(F32)).astype(BF16)
    return hi, mid, lo


def _norm_mod(x, g, shift, scale):
    ms = jnp.mean(x * x, axis=-1, keepdims=True)
    return (x * lax.rsqrt(ms + EPS) * g) * (1.0 + scale) + shift


def _cparams(sem, vmem=VMEM_LIMIT):
    return pltpu.CompilerParams(dimension_semantics=sem, vmem_limit_bytes=vmem)


def _cast_kernel(x_ref, o_ref):
    o_ref[...] = x_ref[...].astype(o_ref.dtype)


def cast_bf16(w, rows):
    e, r, c = w.shape
    assert r % rows == 0, (r, rows)
    spec = pl.BlockSpec((1, rows, c), lambda i, j: (i, j, 0))
    return pl.pallas_call(
        _cast_kernel,
        grid=(e, r // rows),
        in_specs=[spec],
        out_specs=spec,
        out_shape=jax.ShapeDtypeStruct(w.shape, BF16),
        compiler_params=_cparams(("parallel", "parallel")),
        name="cast_bf16",
    )(w)


def _mod_kernel(c_ref, w_ref, b_ref, o_ref):
    a_hi, a_lo = _split2(_silu(c_ref[...]))
    w_hi, w_lo = _split2(w_ref[0])
    o_ref[0] = _dot(a_hi, w_hi) + _dot(a_lo, w_hi) + _dot(a_hi, w_lo) + b_ref[0]


def modulation(cc, w_mod, b_mod):
    depth, d, n = w_mod.shape
    tn = 1536
    return pl.pallas_call(
        _mod_kernel,
        grid=(depth, n // tn),
        in_specs=[pl.BlockSpec((16, d), lambda i, j: (0, 0)),
                  pl.BlockSpec((1, d, tn), lambda i, j: (i, 0, j)),
                  pl.BlockSpec((1, 1, tn), lambda i, j: (i, 0, j))],
        out_specs=pl.BlockSpec((1, 16, tn), lambda i, j: (i, 0, j)),
        out_shape=jax.ShapeDtypeStruct((depth, 16, n), F32),
        compiler_params=_cparams(("parallel", "parallel")),
        name="modulation",
    )(cc, w_mod, b_mod.reshape(depth, 1, n))


def _inproj_kernel(x_ref, g_ref, sh_ref, sc_ref, w_ref, wdt_ref, o_ref, dt_ref, h_ref):
    @pl.when(pl.program_id(2) == 0)
    def _():
        hb = _norm_mod(x_ref[0], g_ref[...], sh_ref[0], sc_ref[0]).astype(BF16)
        h_ref[...] = hb
        dt_ref[0] = _dot(hb, wdt_ref[...])

    o_ref[0] = _dot(h_ref[...], w_ref[...]).astype(o_ref.dtype)


def in_proj(x, g, shift, scale, w, wdt, tm, tn):
    b, l, d = x.shape
    n = w.shape[1]
    return pl.pallas_call(
        _inproj_kernel,
        grid=(b, l // tm, n // tn),
        in_specs=[pl.BlockSpec((1, tm, d), lambda i, m, j: (i, m, 0)),
                  pl.BlockSpec((1, d), lambda i, m, j: (0, 0)),
                  pl.BlockSpec((1, 1, d), lambda i, m, j: (i, 0, 0)),
                  pl.BlockSpec((1, 1, d), lambda i, m, j: (i, 0, 0)),
                  pl.BlockSpec((d, tn), lambda i, m, j: (0, j)),
                  pl.BlockSpec((d, LANES), lambda i, m, j: (0, 0))],
        out_specs=[pl.BlockSpec((1, tm, tn), lambda i, m, j: (i, m, j)),
                   pl.BlockSpec((1, tm, LANES), lambda i, m, j: (i, m, 0))],
        out_shape=[jax.ShapeDtypeStruct((b, l, n), BF16),
                   jax.ShapeDtypeStruct((b, l, LANES), F32)],
        scratch_shapes=[pltpu.VMEM((tm, d), BF16)],
        compiler_params=_cparams(("parallel", "parallel", "arbitrary")),
        name="in_proj",
    )(x, g.reshape(1, d), shift, scale, w, wdt)


def _dt_kernel(raw_ref, bias_ref, a_ref, pc_ref, ph_ref, plo_ref, col_ref, row_ref, sp_ref, *, q, ch):
    row = lax.broadcasted_iota(jnp.int32, (q, q), 0)
    col = lax.broadcasted_iota(jnp.int32, (q, q), 1)
    tri_l = jnp.where(col <= row, 1.0, 0.0).astype(BF16)
    tri_u = jnp.where(col >= row, 1.0, 0.0).astype(BF16)
    lane = lax.broadcasted_iota(jnp.int32, (q, LANES), 1)
    fwd = lane < SSD_HEADS
    ph, plo = ph_ref[...], plo_ref[...]

    def place_split(t):
        t_hi, t_lo = _split2(t)
        return (_dot(t_hi, ph) + _dot(t_lo, plo)).astype(BF16)

    for k in range(ch):
        v = raw_ref[0, k * q:(k + 1) * q, :] + bias_ref[...]
        dt = jnp.maximum(v, 0.0) + jnp.log1p(jnp.exp(-jnp.abs(v)))
        d1, d2, d3 = _split3(dt * a_ref[...])
        cs_f = _dot(tri_l, d1) + _dot(tri_l, d2) + _dot(tri_l, d3)
        cs_b = _dot(tri_u, d1) + _dot(tri_u, d2) + _dot(tri_u, d3)
        cs = jnp.where(fwd, cs_f, cs_b)
        tot = jnp.where(fwd[0:1], cs_f[q - 1:q, :], cs_b[0:1, :])
        sl = slice(k * q, (k + 1) * q)
        cs2 = cs * LOG2E
        c1, c2, c3 = _split3(cs2)
        pc = pc_ref[...]
        col_ref[0, sl, :] = _dot(c1, pc) + _dot(c2, pc) + _dot(c3, pc)
        r_t = (cs2 - jnp.log(dt) * LOG2E).T
        dt_t = dt.T
        for g in range(SSD_GROUPS):
            lo = g * SSD_HPG
            dt_f = dt_t[lo:lo + 8, :]
            dt_b = dt_t[SSD_HEADS + lo:SSD_HEADS + lo + 8, :]
            row_ref[0, g, k, 0:8, :] = r_t[lo:lo + 8, :]
            row_ref[0, g, k, 8:16, :] = r_t[SSD_HEADS + lo:SSD_HEADS + lo + 8, :]
            row_ref[0, g, k, 16:24, :] = jnp.log(dt_f + dt_b) * LOG2E
        sp_ref[0, sl, 0:LANES] = place_split(dt * jnp.exp(tot - cs))
        sp_ref[0, sl, LANES:2 * LANES] = place_split(jnp.exp(cs))


def _placements():
    pc = np.zeros((LANES, LANES), np.float32)
    ph = np.zeros((LANES, LANES), np.float32)
    plo = np.zeros((LANES, LANES), np.float32)
    for d in range(2):
        for g in range(SSD_GROUPS):
            for j in range(SSD_HPG):
                src = d * SSD_HEADS + g * SSD_HPG + j
                pc[src, g * 16 + d * 8 + j] = 1.0
                ph[src, (d * SSD_GROUPS + g) * 16 + j] = 1.0
                plo[src, (d * SSD_GROUPS + g) * 16 + 8 + j] = 1.0
    return jnp.asarray(pc, BF16), jnp.asarray(ph, BF16), jnp.asarray(plo, BF16)


def dt_prep(raw, bias, a, q):
    b, l, _ = raw.shape
    nc = l // q
    ch = min(8, nc)
    gn = SSD_GROUPS
    vspec = pl.BlockSpec((1, LANES), lambda i, c: (0, 0))
    pspec = pl.BlockSpec((LANES, LANES), lambda i, c: (0, 0))
    return pl.pallas_call(
        functools.partial(_dt_kernel, q=q, ch=ch),
        grid=(b, nc // ch),
        in_specs=[pl.BlockSpec((1, ch * q, LANES), lambda i, c: (i, c, 0)), vspec, vspec, pspec, pspec, pspec],
        out_specs=[pl.BlockSpec((1, ch * q, LANES), lambda i, c: (i, c, 0)),
                   pl.BlockSpec((1, gn, ch, 24, q), lambda i, c: (i, 0, c, 0, 0)),
                   pl.BlockSpec((1, ch * q, 2 * LANES), lambda i, c: (i, c, 0))],
        out_shape=[jax.ShapeDtypeStruct((b, l, LANES), F32),
                   jax.ShapeDtypeStruct((b, gn, nc, 24, q), F32),
                   jax.ShapeDtypeStruct((b, l, 2 * LANES), BF16)],
        compiler_params=_cparams(("parallel", "parallel")),
        name="dt_prep",
    )(raw, bias, a, *_placements())


def _ssd_kernel(xp_ref, bp_ref, cp_ref, cwx_ref, cwb_ref, cwc_ref, cbx_ref, cbb_ref, cbc_ref,
                col_ref, row_ref, sp_ref, e2f_ref, e2b_ref, dexp_ref, s0_ref,
                y_ref, sfin_ref,
                xs_ref, cc_ref, bt_ref, xwf_ref, sbe_ref, edge_ref, sf_ref, sb_ref, yo_ref, *, l, q):
    nc = l // q
    sr = lax.broadcasted_iota(jnp.int32, (q, q + 32), 0)
    sc = lax.broadcasted_iota(jnp.int32, (q, q + 32), 1)
    shift_prev = jnp.where((sc == sr - 1) | ((sr == 0) & (sc == q + 15)), 1.0, 0.0).astype(BF16)
    shift_next = jnp.where(((sc == sr + 1) & (sc < q)) | ((sr == q - 1) & (sc == q + 16)), 1.0, 0.0).astype(BF16)

    def conv_silu(srcs, w, b, k):
        r0 = pl.multiple_of(k * q, q)
        p0 = pl.multiple_of(jnp.maximum(r0 - 16, 0), 16)
        n0 = pl.multiple_of(jnp.minimum(r0 + q, l - 16), 16)

        def rows_at(start, n):
            parts = [r[0, pl.ds(start, n), ls] for r, ls in srcs]
            return parts[0] if len(parts) == 1 else jnp.concatenate(parts, axis=1)

        blk = rows_at(r0, q)
        before = rows_at(p0, 16)
        after = rows_at(n0, 16)
        before = jnp.where(k > 0, before, jnp.zeros_like(before))
        after = jnp.where(k < nc - 1, after, jnp.zeros_like(after))
        stacked = jnp.concatenate([blk, before, after], axis=0)
        v = (w[0:1] * _dot(shift_prev, stacked) + w[1:2] * blk.astype(F32)
             + w[2:3] * _dot(shift_next, stacked) + b)
        return _silu_t(v)

    all_lanes = slice(None)

    def prep(k):
        r0 = pl.multiple_of(k * q, q)
        x = conv_silu(((xp_ref, all_lanes),), cwx_ref[...], cbx_ref[...], k)
        xs_ref[pl.ds(r0, q), :] = x.astype(BF16)
        bc = conv_silu(((bp_ref, all_lanes), (cp_ref, all_lanes)),
                       jnp.concatenate([cwb_ref[...], cwc_ref[...]], axis=1),
                       jnp.concatenate([cbb_ref[...], cbc_ref[...]], axis=1), k)
        b_t = bc[:, :SSD_STATE].T.astype(BF16)
        bt_ref[k] = b_t
        cc_ref[pl.ds(r0, q), :] = bc[:, SSD_STATE:].astype(BF16)
        sp_w = sp_ref[0, pl.ds(r0, q), 0:LANES]
        xwf_ref[pl.ds(r0, q), :] = (x * _dot(sp_w, e2f_ref[0, 0])).astype(BF16)
        xw_b = (x * _dot(sp_w, e2b_ref[0, 0])).astype(BF16)
        e_last = sp_ref[0, pl.ds(pl.multiple_of(r0 + q - 16, 16), 16), LANES:2 * LANES]
        e_first = sp_ref[0, pl.ds(r0, 16), LANES:2 * LANES]
        edge_ref[k] = _dot(e_last, e2f_ref[0, 0])[15:16, :]
        edge_b = _dot(e_first, e2b_ref[0, 0])[0:1, :]
        return b_t, xw_b, edge_b

    sb_ref[...] = s0_ref[0, 0, 1]

    def bstep(i, carry):
        c = nc - 1 - i
        b_t, xw_b, edge_b = prep(c)
        sbe_ref[c] = sb_ref[...].astype(BF16)
        sb_ref[...] = sb_ref[...] * edge_b + _dot(b_t, xw_b)
        return carry

    lax.fori_loop(0, nc, bstep, 0, unroll=8)
    sfin_ref[0, 0, 1] = sb_ref[...]

    sf_ref[...] = s0_ref[0, 0, 0]
    li = lax.broadcasted_iota(jnp.int32, (q, q), 0)
    si = lax.broadcasted_iota(jnp.int32, (q, q), 1)
    lower = si <= li
    eye = si == li
    left = lax.broadcasted_iota(jnp.int32, (q, LANES), 1) < SSD_HEAD_DIM
    fu = yo_ref.shape[0]
    lane0 = 16 * pl.program_id(1)

    def chunk(c, slot):
        r0 = pl.multiple_of(c * q, q)
        cc = cc_ref[pl.ds(r0, q), :]
        sp_e = sp_ref[0, pl.ds(r0, q), LANES:2 * LANES]
        yo_ref[slot] = (_dot(sp_e, e2f_ref[0, 0]) * _dot(cc, sf_ref[...].astype(BF16))
                        + _dot(sp_e, e2b_ref[0, 0]) * _dot(cc, sbe_ref[c]))
        sf_ref[...] = sf_ref[...] * edge_ref[c] + _dot(bt_ref[c], xwf_ref[pl.ds(r0, q), :])
        g = _dot(cc, bt_ref[c])
        col = pltpu.roll(col_ref[0, pl.ds(r0, q), :], LANES - lane0, 1)
        row = row_ref[0, 0, c]
        for j in range(SSD_HPG // 2):
            lanes = slice(j * LANES, (j + 1) * LANES)
            xf = xs_ref[pl.ds(r0, q), lanes].astype(F32)
            x_diag = jnp.concatenate([jnp.where(left, xf, 0.0), jnp.where(left, 0.0, xf)], axis=0).astype(BF16)
            ms = []
            for h in (2 * j, 2 * j + 1):
                cs_l = jnp.take_along_axis(col, jnp.where(lower, h, 8 + h), axis=1, mode="promise_in_bounds")
                arg = cs_l - jnp.where(lower, row[h:h + 1, :], row[8 + h:9 + h, :])
                p = jnp.exp2(jnp.where(eye, row[16 + h:17 + h, :], arg))
                ms.append((g * p).astype(BF16))
            y = _dot(jnp.concatenate(ms, axis=1), x_diag) + yo_ref[slot, :, lanes] + dexp_ref[0, :, lanes] * xf
            y_ref[0, pl.ds(r0, q), lanes] = y.astype(y_ref.dtype)

    def fstep(i, carry):
        for slot in range(fu):
            chunk(fu * i + slot, slot)
        return carry

    lax.fori_loop(0, nc // fu, fstep, 0)
    sfin_ref[0, 0, 0] = sf_ref[...]


def _expanders():
    e = np.zeros((2, SSD_GROUPS, LANES, GROUP_W), np.float32)
    for d in range(2):
        for g in range(SSD_GROUPS):
            for s in range(2):
                for j in range(SSD_HPG):
                    e[d, g, (d * SSD_GROUPS + g) * 16 + s * 8 + j, j * SSD_HEAD_DIM:(j + 1) * SSD_HEAD_DIM] = 1.0
    return jnp.asarray(e, BF16)


def ssd_mixer(proj, xbc_col0, dt_raw, conv_w, conv_b, dt_bias, a_log, d_skip, s0, q):
    b, l, _ = proj.shape
    assert l % q == 0 and q % 16 == 0, (l, q)
    nc = l // q
    gn = SSD_GROUPS
    bias = jnp.zeros((1, LANES), F32).at[0, :2 * SSD_HEADS].set(dt_bias.reshape(-1))
    a = jnp.zeros((1, LANES), F32).at[0, :2 * SSD_HEADS].set(-jnp.exp(a_log.reshape(-1)))
    colg, rowg, sp = dt_prep(dt_raw, bias, a, q)
    dexp = jnp.repeat(d_skip.astype(F32), SSD_HEAD_DIM).reshape(gn, 1, GROUP_W)
    e2 = _expanders()

    cw = conv_w.astype(F32)
    cb = conv_b.astype(F32).reshape(1, -1)
    nb = SSD_INNER
    cwx, cwb, cwc = cw[:, :nb], cw[:, nb:nb + gn * SSD_STATE], cw[:, nb + gn * SSD_STATE:]
    cbx, cbb, cbc = cb[:, :nb], cb[:, nb:nb + gn * SSD_STATE], cb[:, nb + gn * SSD_STATE:]

    fu = max(u for u in (8, 4, 2, 1) if nc % u == 0)
    xo = xbc_col0 // GROUP_W
    bo = (xbc_col0 + SSD_INNER) // SSD_STATE
    co = bo + gn
    st_spec = pl.BlockSpec((1, 1, 2, SSD_STATE, GROUP_W), lambda i, g: (i, g, 0, 0, 0))
    y, sfin = pl.pallas_call(
        functools.partial(_ssd_kernel, l=l, q=q),
        grid=(b, gn),
        in_specs=[pl.BlockSpec((1, l, GROUP_W), lambda i, g: (i, 0, xo + g)),
                  pl.BlockSpec((1, l, SSD_STATE), lambda i, g: (i, 0, bo + g)),
                  pl.BlockSpec((1, l, SSD_STATE), lambda i, g: (i, 0, co + g)),
                  pl.BlockSpec((3, GROUP_W), lambda i, g: (0, g)),
                  pl.BlockSpec((3, SSD_STATE), lambda i, g: (0, g)),
                  pl.BlockSpec((3, SSD_STATE), lambda i, g: (0, g)),
                  pl.BlockSpec((1, GROUP_W), lambda i, g: (0, g)),
                  pl.BlockSpec((1, SSD_STATE), lambda i, g: (0, g)),
                  pl.BlockSpec((1, SSD_STATE), lambda i, g: (0, g)),
                  pl.BlockSpec((1, l, LANES), lambda i, g: (i, 0, 0)),
                  pl.BlockSpec((1, 1, nc, 24, q), lambda i, g: (i, g, 0, 0, 0)),
                  pl.BlockSpec((1, l, 2 * LANES), lambda i, g: (i, 0, 0)),
                  pl.BlockSpec((1, 1, LANES, GROUP_W), lambda i, g: (0, g, 0, 0)),
                  pl.BlockSpec((1, 1, LANES, GROUP_W), lambda i, g: (1, g, 0, 0)),
                  pl.BlockSpec((1, 1, GROUP_W), lambda i, g: (g, 0, 0)),
                  st_spec],
        out_specs=[pl.BlockSpec((1, l, GROUP_W), lambda i, g: (i, 0, g)), st_spec],
        out_shape=[jax.ShapeDtypeStruct((b, l, SSD_INNER), BF16),
                   jax.ShapeDtypeStruct((b, gn, 2, SSD_STATE, GROUP_W), F32)],
        scratch_shapes=[pltpu.VMEM((l, GROUP_W), BF16),
                        pltpu.VMEM((l, SSD_STATE), BF16),
                        pltpu.VMEM((nc, SSD_STATE, q), BF16),
                        pltpu.VMEM((l, GROUP_W), BF16),
                        pltpu.VMEM((nc, SSD_STATE, GROUP_W), BF16),
                        pltpu.VMEM((nc, 1, GROUP_W), F32),
                        pltpu.VMEM((SSD_STATE, GROUP_W), F32),
                        pltpu.VMEM((SSD_STATE, GROUP_W), F32),
                        pltpu.VMEM((fu, q, GROUP_W), F32)],
        compiler_params=_cparams(("parallel", "parallel")),
        name="ssd",
    )(proj, proj, proj, cwx, cwb, cwc, cbx, cbb, cbc, colg, rowg, sp, e2, e2, dexp, s0)
    return y, sfin


def _mixout_kernel(y_ref, z_ref, gb_ref, gc_ref, hv_ref, g0_ref, g1_ref, x_ref, gate_ref,
                   ng_ref, bg_ref, scw_ref, wssd_ref, wsc_ref, wo_ref, o_ref, *, tm, period):
    ns = 2 if (tm // 2) % period == 0 else 1
    ts = tm // ns
    pos = lax.broadcasted_iota(jnp.int32, (ts, 1), 0) % period
    w = scw_ref[...]
    bg = bg_ref[...]
    for s in range(ns):
        r = slice(s * ts, (s + 1) * ts)
        yz = y_ref[0, r, :].astype(F32) * _silu_t(z_ref[0, r, :].astype(F32))
        ms = jnp.mean(yz * yz, axis=-1, keepdims=True)
        yn = (yz * lax.rsqrt(ms + EPS) * ng_ref[...]).astype(BF16)
        y_ssd = _dot(yn, wssd_ref[...])

        u = gc_ref[0, r, :].astype(F32) * hv_ref[0, r, :].astype(F32)
        u_prev = jnp.where(pos == 0, 0.0, pltpu.roll(u, 1, 0))
        u_next = jnp.where(pos == period - 1, 0.0, pltpu.roll(u, ts - 1, 0))
        v = w[0:1] * u_prev + w[1:2] * u + w[2:3] * u_next
        y_sc = _dot((gb_ref[0, r, :].astype(F32) * v).astype(BF16), wsc_ref[...])

        g0 = _sigmoid_t(g0_ref[0, r, :].astype(F32) + bg[:, :D_MODEL])
        g1 = _sigmoid_t(g1_ref[0, r, :].astype(F32) + bg[:, D_MODEL:])
        out = _dot((g0 * y_ssd + g1 * y_sc).astype(BF16), wo_ref[...])
        o_ref[0, r, :] = x_ref[0, r, :] + gate_ref[0] * out


def mixer_out(y, proj, x, gate, norm_g, b_gate, sc_conv_w, w_ssd, w_sc, w_o, tm, period):
    b, l, d = x.shape
    pc = lambda k: pl.BlockSpec((1, tm, d), lambda i, m, k=k: (i, m, k))
    full = lambda shp: pl.BlockSpec(shp, lambda i, m: (0,) * len(shp))
    return pl.pallas_call(
        functools.partial(_mixout_kernel, tm=tm, period=period),
        grid=(b, l // tm),
        in_specs=[pl.BlockSpec((1, tm, SSD_INNER), lambda i, m: (i, m, 0)),
                  pl.BlockSpec((1, tm, SSD_INNER), lambda i, m: (i, m, 0)),
                  pc(5), pc(6), pc(7), pc(8), pc(9),
                  pl.BlockSpec((1, tm, d), lambda i, m: (i, m, 0)),
                  pl.BlockSpec((1, 1, d), lambda i, m: (i, 0, 0)),
                  full((1, SSD_INNER)), full((1, 2 * d)), full((3, SC_WIDTH)),
                  full((SSD_INNER, d)), full((SC_WIDTH, d)), full((d, d))],
        out_specs=pl.BlockSpec((1, tm, d), lambda i, m: (i, m, 0)),
        out_shape=jax.ShapeDtypeStruct((b, l, d), F32),
        compiler_params=_cparams(("parallel", "parallel")),
        name="mixer_out",
    )(y, proj, proj, proj, proj, proj, proj, x, gate,
      norm_g.reshape(1, -1), b_gate.reshape(1, -1), sc_conv_w, w_ssd, w_sc, w_o)


def _ffn_kernel(x_ref, g_ref, sh_ref, sc_ref, gate_ref, w1_ref, w3_ref, w2_ref, o_ref, *, nf):
    fw = w1_ref.shape[1] // nf
    ts = x_ref.shape[1] // ROW_SUBTILES
    for s in range(ROW_SUBTILES):
        r = slice(s * ts, (s + 1) * ts)
        x = x_ref[0, r, :]
        hb = _norm_mod(x, g_ref[...], sh_ref[0], sc_ref[0]).astype(BF16)
        acc = None
        for k in range(nf):
            a = _dot(hb, w1_ref[:, k * fw:(k + 1) * fw])
            bb = _dot(hb, w3_ref[:, k * fw:(k + 1) * fw])
            part = _dot((_silu_t(a) * bb).astype(BF16), w2_ref[k * fw:(k + 1) * fw, :])
            acc = part if acc is None else acc + part
        o_ref[0, r, :] = x + gate_ref[0] * acc


def ffn_dense(x, g, shift, scale, gate, w1, w3, w2, tm):
    b, l, d = x.shape
    f = w1.shape[1]
    vec = pl.BlockSpec((1, 1, d), lambda i, m: (i, 0, 0))
    const = lambda shp: pl.BlockSpec(shp, lambda i, m: (0, 0), pipeline_mode=pl.Buffered(1))
    return pl.pallas_call(
        functools.partial(_ffn_kernel, nf=2),
        grid=(b, l // tm),
        in_specs=[pl.BlockSpec((1, tm, d), lambda i, m: (i, m, 0)),
                  pl.BlockSpec((1, d), lambda i, m: (0, 0)),
                  vec, vec, vec, const((d, f)), const((d, f)), const((f, d))],
        out_specs=pl.BlockSpec((1, tm, d), lambda i, m: (i, m, 0)),
        out_shape=jax.ShapeDtypeStruct((b, l, d), F32),
        compiler_params=_cparams(("parallel", "parallel")),
        name="ffn_dense",
    )(x, g.reshape(1, d), shift, scale, gate, w1, w3, w2)


def _router_kernel(x_ref, g_ref, sh_ref, sc_ref, rw_ref, h_ref, route_ref, route_t_ref, cnt_ref, run_ref, *, tm):
    @pl.when((pl.program_id(0) == 0) & (pl.program_id(1) == 0))
    def _():
        run_ref[...] = jnp.zeros_like(run_ref)

    h = _norm_mod(x_ref[0], g_ref[...], sh_ref[0], sc_ref[0])
    h_hi, h_lo = _split2(h)
    h_ref[0] = h
    w_hi, w_lo = _split2(rw_ref[...])
    logits = _dot(h_hi, w_hi) + _dot(h_lo, w_hi) + _dot(h_hi, w_lo)
    lane = lax.broadcasted_iota(jnp.int32, (tm, LANES), 1)
    ninf = float("-inf")
    lg = jnp.where(lane < N_EXPERTS, logits, ninf)
    m1 = jnp.max(lg, axis=1, keepdims=True)
    i1 = jnp.min(jnp.where(lg == m1, lane, LANES), axis=1, keepdims=True)
    lg2 = jnp.where(lane == i1, ninf, lg)
    m2 = jnp.max(lg2, axis=1, keepdims=True)
    i2 = jnp.min(jnp.where(lg2 == m2, lane, LANES), axis=1, keepdims=True)
    e2 = jnp.exp(m2 - m1)
    den = 1.0 + e2
    sel1 = jnp.where(lane == i1, 1.0, 0.0)
    sel2 = jnp.where(lane == i2, 1.0, 0.0)
    cnt = sel1 + sel2
    r = lax.broadcasted_iota(jnp.int32, (tm, tm), 0)
    c = lax.broadcasted_iota(jnp.int32, (tm, tm), 1)
    tri = jnp.where(c < r, 1.0, 0.0).astype(BF16)
    base = _dot(tri, cnt.astype(BF16)) + run_ref[0:1, :]
    r1 = jnp.sum(sel1 * base, axis=1, keepdims=True)
    r2 = jnp.sum(sel2 * base, axis=1, keepdims=True)
    vals = (i1.astype(F32), i2.astype(F32), 1.0 / den, e2 / den, r1, r2)
    out = jnp.zeros((tm, LANES), F32)
    for k, v in enumerate(vals):
        out = jnp.where(lane == k, v, out)
    route_ref[0] = out
    route_t_ref[...] = out.T
    new_run = run_ref[...] + jnp.sum(cnt, axis=0, keepdims=True)
    run_ref[...] = new_run
    cnt_ref[...] = new_run


def router(x, g, shift, scale, router_w, tm):
    b, l, d = x.shape
    rw = jnp.zeros((d, LANES), F32).at[:, :N_EXPERTS].set(router_w)
    vec = pl.BlockSpec((1, 1, d), lambda i, m: (i, 0, 0))
    mt = l // tm
    return pl.pallas_call(
        functools.partial(_router_kernel, tm=tm),
        grid=(b, l // tm),
        in_specs=[pl.BlockSpec((1, tm, d), lambda i, m: (i, m, 0)),
                  pl.BlockSpec((1, d), lambda i, m: (0, 0)),
                  vec, vec, pl.BlockSpec((d, LANES), lambda i, m: (0, 0))],
        out_specs=[pl.BlockSpec((1, tm, d), lambda i, m: (i, m, 0)),
                   pl.BlockSpec((1, tm, LANES), lambda i, m: (i, m, 0)),
                   pl.BlockSpec((LANES, tm), lambda i, m: (0, i * mt + m)),
                   pl.BlockSpec((8, LANES), lambda i, m: (0, 0))],
        out_shape=[jax.ShapeDtypeStruct((b, l, d), F32),
                   jax.ShapeDtypeStruct((b, l, LANES), F32),
                   jax.ShapeDtypeStruct((LANES, b * l), F32),
                   jax.ShapeDtypeStruct((8, LANES), F32)],
        scratch_shapes=[pltpu.VMEM((8, LANES), F32)],
        compiler_params=_cparams(("arbitrary", "arbitrary")),
        name="router",
    )(x, g.reshape(1, d), shift, scale, rw)


def _pack_bf16_pairs(y):
    k = y.shape[1] // 2
    bits = lax.bitcast_convert_type(y.astype(BF16).astype(F32), jnp.uint32)
    return bits[:, :k] | (bits[:, k:] >> 16)


def _unpack_bf16_pairs(p):
    hi = lax.bitcast_convert_type(p & jnp.uint32(0xFFFF0000), F32)
    lo = lax.bitcast_convert_type(p << 16, F32)
    return jnp.concatenate([hi, lo], axis=1)


def _gffn_kernel(te_ref, nv_ref, x_ref, w1_ref, w3_ref, w2_ref, o_ref, acc_ref, *, nf):
    i = pl.program_id(0)
    f = pl.program_id(1)

    @pl.when(i < nv_ref[0])
    def _():
        ts = x_ref.shape[0] // ROW_SUBTILES
        parts = []
        for s in range(ROW_SUBTILES):
            x = x_ref[s * ts:(s + 1) * ts, :].astype(BF16)
            a = _dot(x, w1_ref[0])
            bb = _dot(x, w3_ref[0])
            parts.append(_dot((_silu_t(a) * bb).astype(BF16), w2_ref[0]))
        part = jnp.concatenate(parts, axis=0)

        @pl.when(f == 0)
        def _():
            acc_ref[...] = part

        @pl.when(f > 0)
        def _():
            acc_ref[...] += part

        @pl.when(f == nf - 1)
        def _():
            o_ref[...] = _pack_bf16_pairs(acc_ref[...])


def grouped_ffn(xs, tile_expert, n_valid, w1, w3, w2, tm, nf):
    rows, d = xs.shape
    nt = rows // tm
    f = w1.shape[2]
    fw = f // nf

    def tile(i, nv):
        return jnp.minimum(i, nv[0] - 1)

    def fchunk(i, k, nv):
        return jnp.where(i < nv[0], k, nf - 1)

    grid_spec = pltpu.PrefetchScalarGridSpec(
        num_scalar_prefetch=2,
        grid=(nt, nf),
        in_specs=[pl.BlockSpec((tm, d), lambda i, k, te, nv: (tile(i, nv), 0)),
                  pl.BlockSpec((1, d, fw), lambda i, k, te, nv: (te[i], 0, fchunk(i, k, nv))),
                  pl.BlockSpec((1, d, fw), lambda i, k, te, nv: (te[i], 0, fchunk(i, k, nv))),
                  pl.BlockSpec((1, fw, d), lambda i, k, te, nv: (te[i], fchunk(i, k, nv), 0))],
        out_specs=pl.BlockSpec((tm, d // 2), lambda i, k, te, nv: (tile(i, nv), 0)),
        scratch_shapes=[pltpu.VMEM((tm, d), F32)],
    )
    return pl.pallas_call(
        functools.partial(_gffn_kernel, nf=nf),
        grid_spec=grid_spec,
        out_shape=jax.ShapeDtypeStruct((rows, d // 2), jnp.uint32),
        compiler_params=_cparams(("arbitrary", "arbitrary")),
        name="grouped_ffn",
    )(tile_expert, n_valid, xs, w1, w3, w2)


def _combine_kernel(x_ref, y0_ref, y1_ref, route_ref, gate_ref, fg_ref, o_ref):
    r = route_ref[0]
    moe = r[:, 2:3] * _unpack_bf16_pairs(y0_ref[0, 0]) + r[:, 3:4] * _unpack_bf16_pairs(y1_ref[0, 0])
    xn = x_ref[0] + gate_ref[0] * moe
    ms = jnp.mean(xn * xn, axis=-1, keepdims=True)
    o_ref[0] = xn * lax.rsqrt(ms + EPS) * fg_ref[...]


def combine_final(x, yg, route, gate, final_g, tm):
    b, l, d = x.shape
    return pl.pallas_call(
        _combine_kernel,
        grid=(b, l // tm),
        in_specs=[pl.BlockSpec((1, tm, d), lambda i, m: (i, m, 0)),
                  pl.BlockSpec((1, 1, tm, d // 2), lambda i, m: (0, i, m, 0)),
                  pl.BlockSpec((1, 1, tm, d // 2), lambda i, m: (1, i, m, 0)),
                  pl.BlockSpec((1, tm, LANES), lambda i, m: (i, m, 0)),
                  pl.BlockSpec((1, 1, d), lambda i, m: (i, 0, 0)),
                  pl.BlockSpec((1, d), lambda i, m: (0, 0))],
        out_specs=pl.BlockSpec((1, tm, d), lambda i, m: (i, m, 0)),
        out_shape=jax.ShapeDtypeStruct((b, l, d), F32),
        compiler_params=_cparams(("parallel", "parallel")),
        name="combine_final",
    )(x, yg, yg, route, gate, final_g.reshape(1, d))


SC_CORES = 2
SC_SUBCORES = 16
SC_WORKERS = SC_CORES * SC_SUBCORES
SC_STREAM_BYTES = 256 * 1024
SC_STREAM_ROWS = 128


def _sc_rows(per_worker, d, dtype):
    return min(SC_STREAM_ROWS, SC_STREAM_BYTES // (d * jnp.dtype(dtype).itemsize), per_worker)


def _sc_mesh():
    return plsc.VectorSubcoreMesh(core_axis_name="c", subcore_axis_name="s",
                                  num_cores=SC_CORES, num_subcores=SC_SUBCORES)


def dispatch_rows(h, pos0, pos1, n_rows):
    t, d = h.shape
    per_w = t // SC_WORKERS
    ch = _sc_rows(per_w, d, h.dtype)
    assert t % SC_WORKERS == 0 and per_w % ch == 0 and ch % 8 == 0, (t, ch)

    @functools.partial(
        pl.kernel, mesh=_sc_mesh(),
        out_type=jax.ShapeDtypeStruct((n_rows, d), h.dtype),
        scratch_types=[pltpu.VMEM((ch,), jnp.int32), pltpu.VMEM((ch,), jnp.int32),
                       pltpu.VMEM((ch, d), h.dtype), pltpu.SemaphoreType.DMA],
        name="moe_dispatch")
    def scatter(h_hbm, p0_hbm, p1_hbm, out_hbm, i0_v, i1_v, rows_v, sem):
        base = (lax.axis_index("s") * SC_CORES + lax.axis_index("c")) * per_w

        @pl.loop(0, per_w // ch)
        def _(j):
            off = base + j * ch
            pltpu.sync_copy(h_hbm.at[pl.ds(off, ch)], rows_v)
            pltpu.sync_copy(p0_hbm.at[pl.ds(off, ch)], i0_v)
            pltpu.sync_copy(p1_hbm.at[pl.ds(off, ch)], i1_v)
            pltpu.async_copy(rows_v, out_hbm.at[i0_v], sem).wait()
            pltpu.async_copy(rows_v, out_hbm.at[i1_v], sem).wait()

    return scatter(h, pos0, pos1)


def return_rows(ys, idx):
    n = idx.shape[0]
    d = ys.shape[1]
    per_w = n // SC_WORKERS
    ch = _sc_rows(per_w, d, ys.dtype)
    assert n % SC_WORKERS == 0 and per_w % ch == 0 and ch % 8 == 0, (n, ch)

    @functools.partial(
        pl.kernel, mesh=_sc_mesh(),
        out_type=jax.ShapeDtypeStruct((n, d), ys.dtype),
        scratch_types=[pltpu.VMEM((ch,), jnp.int32), pltpu.VMEM((ch, d), ys.dtype), pltpu.SemaphoreType.DMA],
        name="moe_return")
    def gather(ys_hbm, idx_hbm, out_hbm, idx_v, rows_v, sem):
        base = (lax.axis_index("s") * SC_CORES + lax.axis_index("c")) * per_w

        @pl.loop(0, per_w // ch)
        def _(j):
            off = base + j * ch
            pltpu.sync_copy(idx_hbm.at[pl.ds(off, ch)], idx_v)
            pltpu.async_copy(ys_hbm.at[idx_v], rows_v, sem).wait()
            pltpu.sync_copy(rows_v, out_hbm.at[pl.ds(off, ch)])

    return gather(ys, idx)


def moe_block(x, g, shift, scale, gate, router_w, w1, w3, w2, final_g):
    b, l, d = x.shape
    t = b * l
    tm = MOE_TM
    h, route, route_t, counts = router(x, g, shift, scale, router_w, _row_tile(l, ROW_TM))
    cnt = counts[0, :N_EXPERTS].astype(jnp.int32)
    gs = ((cnt + tm - 1) // tm) * tm
    ends = jnp.cumsum(gs)
    offs = ends - gs

    def sorted_row(choice):
        e = route_t[choice].astype(jnp.int32)
        start = sum(jnp.where(e == k, offs[k], 0) for k in range(N_EXPERTS))
        return start + route_t[4 + choice].astype(jnp.int32)

    pos0, pos1 = sorted_row(0), sorted_row(1)
    nt = (2 * t) // tm + N_EXPERTS
    n_valid = (ends[-1] // tm).astype(jnp.int32).reshape(1)
    tile = jnp.minimum(jnp.arange(nt, dtype=jnp.int32), n_valid[0] - 1)
    tile_expert = jnp.sum((tile[:, None] >= (ends // tm)[None, :]).astype(jnp.int32), axis=1)
    xs = dispatch_rows(h.reshape(t, d), pos0, pos1, nt * tm)
    ys = grouped_ffn(xs, tile_expert, n_valid, w1, w3, w2, tm, 2)
    yg = return_rows(ys, jnp.concatenate([pos0, pos1])).reshape(2, b, l, d // 2)
    return combine_final(x, yg, route, gate, final_g, _row_tile(l, ROW_TM))


def _in_weights(w_in):
    o1 = SSD_INNER
    o2 = o1 + XBC_WIDTH
    o3 = o2 + 2 * SSD_HEADS
    w_main = jnp.concatenate([w_in[:, :o2], w_in[:, o3:]], axis=1).astype(BF16)
    w_dt = jnp.pad(w_in[:, o2:o3], ((0, 0), (0, LANES - 2 * SSD_HEADS))).astype(BF16)
    return w_main, w_dt


def kernel(x, c, ctx, c_ctx, w_mod, b_mod, norm1_g, norm2_g, w_in, b_gate, ssd_conv_w, ssd_conv_b, ssd_dt_bias, ssd_a_log, ssd_d, ssd_norm_g, w_ssd_out, sc_conv_w, w_sc_out, w_o, ffn_w1, ffn_w3, ffn_w2, router_w, moe_w1, moe_w3, moe_w2, final_g):
    b, l, d = x.shape
    lc = ctx.shape[1]
    depth = w_mod.shape[0]
    assert depth % 2 == 0, "the final norm is fused into the routed channel mixer of the (odd) last layer"
    cc = jnp.zeros((16, d), F32).at[:b].set(c).at[b].set(c_ctx)
    mod = modulation(cc, w_mod, b_mod)
    zeros_state = jnp.zeros((b, SSD_GROUPS, 2, SSD_STATE, GROUP_W), F32)
    nctx = b * lc
    ctx = ctx.reshape(1, nctx, d)
    tmc = _row_tile(nctx, ROW_TM, lc)
    tmx = _row_tile(l, ROW_TM, GRID_W)

    def per_seq(t):
        return t.reshape(b, lc, t.shape[-1])

    for i in range(depth):
        last = i == depth - 1
        mx = mod[i, :b].reshape(b, N_MOD, 1, d)
        mc = mod[i, b].reshape(1, N_MOD, 1, d)
        w_main, w_dt = _in_weights(w_in[i])
        ssd_p = (ssd_conv_w[i], ssd_conv_b[i], ssd_dt_bias[i], ssd_a_log[i], ssd_d[i])
        out_p = (ssd_norm_g[i], b_gate[i], sc_conv_w[i], w_ssd_out[i].astype(BF16),
                 w_sc_out[i].astype(BF16), w_o[i].astype(BF16))

        if last:
            w_xbc = w_main[:, SSD_INNER:SSD_INNER + XBC_WIDTH]
            proj_c, dt_c = in_proj(ctx, norm1_g[i], mc[:, 0], mc[:, 1], w_xbc, w_dt,
                                   _row_tile(nctx, PROJ_TM), PROJ_TN_XBC)
            _, s_ctx = ssd_mixer(per_seq(proj_c), 0, per_seq(dt_c), *ssd_p, zeros_state, SSD_Q)
        else:
            proj_c, dt_c = in_proj(ctx, norm1_g[i], mc[:, 0], mc[:, 1], w_main, w_dt,
                                   _row_tile(nctx, PROJ_TM), PROJ_TN)
            y_c, s_ctx = ssd_mixer(per_seq(proj_c), SSD_INNER, per_seq(dt_c), *ssd_p, zeros_state, SSD_Q)
            ctx = mixer_out(y_c.reshape(1, nctx, SSD_INNER), proj_c, ctx, mc[:, 2], *out_p, tmc, lc)

        proj_x, dt_x = in_proj(x, norm1_g[i], mx[:, 0], mx[:, 1], w_main, w_dt, _row_tile(l, PROJ_TM), PROJ_TN)
        y_x, _ = ssd_mixer(proj_x, SSD_INNER, dt_x, *ssd_p, s_ctx, SSD_Q)
        x = mixer_out(y_x, proj_x, x, mx[:, 2], *out_p, tmx, GRID_W)

        j = i // 2
        if i % 2 == 0:
            w1, w3, w2 = ffn_w1[j].astype(BF16), ffn_w3[j].astype(BF16), ffn_w2[j].astype(BF16)
            x = ffn_dense(x, norm2_g[i], mx[:, 3], mx[:, 4], mx[:, 5], w1, w3, w2, tmx)
            if not last:
                ctx = ffn_dense(ctx, norm2_g[i], mc[:, 3], mc[:, 4], mc[:, 5], w1, w3, w2, tmc)
        else:
            assert last, "the routed channel mixer is fused with the final norm"
            w1, w3 = cast_bf16(moe_w1[j], CAST_ROWS), cast_bf16(moe_w3[j], CAST_ROWS)
            w2 = cast_bf16(moe_w2[j], moe_w2.shape[2] // 4)
            x = moe_block(x, norm2_g[i], mx[:, 3], mx[:, 4], mx[:, 5], router_w[j], w1, w3, w2, final_g)
    return x
```

```python
import functools

import numpy as np
import jax
import jax.numpy as jnp
from jax import lax
from jax.experimental import pallas as pl
from jax.experimental.pallas import tpu as pltpu
from jax.experimental.pallas import tpu_sc as plsc

F32 = jnp.float32
BF16 = jnp.bfloat16

D_MODEL = 1024
GRID_W = 64
SSD_INNER = 2048
SSD_HEADS = 32
SSD_GROUPS = 4
SSD_HPG = 8
SSD_HEAD_DIM = 64
SSD_STATE = 128
GROUP_W = SSD_HPG * SSD_HEAD_DIM
XBC_WIDTH = SSD_INNER + 2 * SSD_GROUPS * SSD_STATE
SC_WIDTH = 1024
N_MOD = 6
N_EXPERTS = 8
EPS = 1e-6
LOG2E = 1.4426950408889634

LANES = 128
SSD_Q = 128
MOE_TM = 512
ROW_SUBTILES = 2
PROJ_TM = 1024
PROJ_TN = 2560
PROJ_TN_XBC = 1024
ROW_TM = 512
FFN_TM = 1024


def _row_tile(rows, want, multiple=1):
    t = multiple * max(1, min(want, rows) // multiple)
    assert rows % t == 0, (rows, t)
    return t
VMEM_LIMIT = 56 * 1024 * 1024


def _dot(a, b):
    return jnp.dot(a, b, preferred_element_type=F32)


def _sigmoid(v):
    return 1.0 / (1.0 + jnp.exp(-v))


def _silu(v):
    return v * _sigmoid(v)


def _sigmoid_t(v):
    return 0.5 + 0.5 * jnp.tanh(0.5 * v)


def _silu_t(v):
    hv = 0.5 * v
    return hv + hv * jnp.tanh(hv)


def _split2(a):
    hi = a.astype(BF16)
    lo = (a - hi.astype(F32)).astype(BF16)
    return hi, lo


def _split3(a):
    hi = a.astype(BF16)
    r = a - hi.astype(F32)
    mid = r.astype(BF16)
    lo = (r - mid.astype(F32)).astype(BF16)
    return hi, mid, lo


def _norm_mod(x, g, shift, scale):
    ms = jnp.mean(x * x, axis=-1, keepdims=True)
    return (x * lax.rsqrt(ms + EPS) * g) * (1.0 + scale) + shift


def _cparams(sem, vmem=VMEM_LIMIT):
    return pltpu.CompilerParams(dimension_semantics=sem, vmem_limit_bytes=vmem)


def _mod_kernel(c_ref, w_ref, b_ref, o_ref):
    a_hi, a_lo = _split2(_silu(c_ref[...]))
    w_hi, w_lo = _split2(w_ref[0])
    o_ref[0] = _dot(a_hi, w_hi) + _dot(a_lo, w_hi) + _dot(a_hi, w_lo) + b_ref[0]


def modulation(cc, w_mod, b_mod):
    depth, d, n = w_mod.shape
    tn = 1536
    return pl.pallas_call(
        _mod_kernel,
        grid=(depth, n // tn),
        in_specs=[pl.BlockSpec((16, d), lambda i, j: (0, 0)),
                  pl.BlockSpec((1, d, tn), lambda i, j: (i, 0, j)),
                  pl.BlockSpec((1, 1, tn), lambda i, j: (i, 0, j))],
        out_specs=pl.BlockSpec((1, 16, tn), lambda i, j: (i, 0, j)),
        out_shape=jax.ShapeDtypeStruct((depth, 16, n), F32),
        compiler_params=_cparams(("parallel", "parallel")),
        name="modulation",
    )(cc, w_mod, b_mod.reshape(depth, 1, n))


def _inproj_kernel(x_ref, g_ref, sh_ref, sc_ref, w_ref, wdt_ref, o_ref, dt_ref, h_ref):
    @pl.when(pl.program_id(2) == 0)
    def _():
        hb = _norm_mod(x_ref[0], g_ref[...], sh_ref[0], sc_ref[0]).astype(BF16)
        h_ref[...] = hb
        dt_ref[0] = _dot(hb, wdt_ref[...])

    o_ref[0] = _dot(h_ref[...], w_ref[...]).astype(o_ref.dtype)


def in_proj(x, g, shift, scale, w, wdt, tm, tn):
    b, l, d = x.shape
    n = w.shape[1]
    return pl.pallas_call(
        _inproj_kernel,
        grid=(b, l // tm, n // tn),
        in_specs=[pl.BlockSpec((1, tm, d), lambda i, m, j: (i, m, 0)),
                  pl.BlockSpec((1, d), lambda i, m, j: (0, 0)),
                  pl.BlockSpec((1, 1, d), lambda i, m, j: (i, 0, 0)),
                  pl.BlockSpec((1, 1, d), lambda i, m, j: (i, 0, 0)),
                  pl.BlockSpec((d, tn), lambda i, m, j: (0, j)),
                  pl.BlockSpec((d, LANES), lambda i, m, j: (0, 0))],
        out_specs=[pl.BlockSpec((1, tm, tn), lambda i, m, j: (i, m, j)),
                   pl.BlockSpec((1, tm, LANES), lambda i, m, j: (i, m, 0))],
        out_shape=[jax.ShapeDtypeStruct((b, l, n), BF16),
                   jax.ShapeDtypeStruct((b, l, LANES), F32)],
        scratch_shapes=[pltpu.VMEM((tm, d), BF16)],
        compiler_params=_cparams(("parallel", "parallel", "arbitrary")),
        name="in_proj",
    )(x, g.reshape(1, d), shift, scale, w, wdt)


def _dt_kernel(raw_ref, bias_ref, a_ref, pc_ref, ph_ref, plo_ref, col_ref, row_ref, sp_ref, *, q, ch):
    row = lax.broadcasted_iota(jnp.int32, (q, q), 0)
    col = lax.broadcasted_iota(jnp.int32, (q, q), 1)
    tri_l = jnp.where(col <= row, 1.0, 0.0).astype(BF16)
    tri_u = jnp.where(col >= row, 1.0, 0.0).astype(BF16)
    lane = lax.broadcasted_iota(jnp.int32, (q, LANES), 1)
    fwd = lane < SSD_HEADS
    ph, plo = ph_ref[...], plo_ref[...]

    def place_split(t):
        t_hi, t_lo = _split2(t)
        return (_dot(t_hi, ph) + _dot(t_lo, plo)).astype(BF16)

    for k in range(ch):
        v = raw_ref[0, k * q:(k + 1) * q, :] + bias_ref[...]
        dt = jnp.maximum(v, 0.0) + jnp.log1p(jnp.exp(-jnp.abs(v)))
        d1, d2, d3 = _split3(dt * a_ref[...])
        cs_f = _dot(tri_l, d1) + _dot(tri_l, d2) + _dot(tri_l, d3)
        cs_b = _dot(tri_u, d1) + _dot(tri_u, d2) + _dot(tri_u, d3)
        cs = jnp.where(fwd, cs_f, cs_b)
        tot = jnp.where(fwd[0:1], cs_f[q - 1:q, :], cs_b[0:1, :])
        sl = slice(k * q, (k + 1) * q)
        cs2 = cs * LOG2E
        c1, c2, c3 = _split3(cs2)
        pc = pc_ref[...]
        col_ref[0, sl, :] = _dot(c1, pc) + _dot(c2, pc) + _dot(c3, pc)
        r_t = (cs2 - jnp.log(dt) * LOG2E).T
        dt_t = dt.T
        for g in range(SSD_GROUPS):
            lo = g * SSD_HPG
            dt_f = dt_t[lo:lo + 8, :]
            dt_b = dt_t[SSD_HEADS + lo:SSD_HEADS + lo + 8, :]
            row_ref[0, g, k, 0:8, :] = r_t[lo:lo + 8, :]
            row_ref[0, g, k, 8:16, :] = r_t[SSD_HEADS + lo:SSD_HEADS + lo + 8, :]
            row_ref[0, g, k, 16:24, :] = jnp.log(dt_f + dt_b) * LOG2E
        sp_ref[0, sl, 0:LANES] = place_split(dt * jnp.exp(tot - cs))
        sp_ref[0, sl, LANES:2 * LANES] = place_split(jnp.exp(cs))


def _placements():
    pc = np.zeros((LANES, LANES), np.float32)
    ph = np.zeros((LANES, LANES), np.float32)
    plo = np.zeros((LANES, LANES), np.float32)
    for d in range(2):
        for g in range(SSD_GROUPS):
            for j in range(SSD_HPG):
                src = d * SSD_HEADS + g * SSD_HPG + j
                pc[src, g * 16 + d * 8 + j] = 1.0
                ph[src, (d * SSD_GROUPS + g) * 16 + j] = 1.0
                plo[src, (d * SSD_GROUPS + g) * 16 + 8 + j] = 1.0
    return jnp.asarray(pc, BF16), jnp.asarray(ph, BF16), jnp.asarray(plo, BF16)


def dt_prep(raw, bias, a, q):
    b, l, _ = raw.shape
    nc = l // q
    ch = min(8, nc)
    gn = SSD_GROUPS
    vspec = pl.BlockSpec((1, LANES), lambda i, c: (0, 0))
    pspec = pl.BlockSpec((LANES, LANES), lambda i, c: (0, 0))
    return pl.pallas_call(
        functools.partial(_dt_kernel, q=q, ch=ch),
        grid=(b, nc // ch),
        in_specs=[pl.BlockSpec((1, ch * q, LANES), lambda i, c: (i, c, 0)), vspec, vspec, pspec, pspec, pspec],
        out_specs=[pl.BlockSpec((1, ch * q, LANES), lambda i, c: (i, c, 0)),
                   pl.BlockSpec((1, gn, ch, 24, q), lambda i, c: (i, 0, c, 0, 0)),
                   pl.BlockSpec((1, ch * q, 2 * LANES), lambda i, c: (i, c, 0))],
        out_shape=[jax.ShapeDtypeStruct((b, l, LANES), F32),
                   jax.ShapeDtypeStruct((b, gn, nc, 24, q), F32),
                   jax.ShapeDtypeStruct((b, l, 2 * LANES), BF16)],
        compiler_params=_cparams(("parallel", "parallel")),
        name="dt_prep",
    )(raw, bias, a, *_placements())


def _ssd_kernel(xp_ref, bp_ref, cp_ref, cwx_ref, cwb_ref, cwc_ref, cbx_ref, cbb_ref, cbc_ref,
                col_ref, row_ref, sp_ref, e2f_ref, e2b_ref, dexp_ref, s0_ref,
                y_ref, sfin_ref,
                xs_ref, cc_ref, bt_ref, xwf_ref, sbe_ref, edge_ref, sf_ref, sb_ref, yo_ref, *, l, q):
    nc = l // q
    sr = lax.broadcasted_iota(jnp.int32, (q, q + 32), 0)
    sc = lax.broadcasted_iota(jnp.int32, (q, q + 32), 1)
    shift_prev = jnp.where((sc == sr - 1) | ((sr == 0) & (sc == q + 15)), 1.0, 0.0).astype(BF16)
    shift_next = jnp.where(((sc == sr + 1) & (sc < q)) | ((sr == q - 1) & (sc == q + 16)), 1.0, 0.0).astype(BF16)

    def conv_silu(srcs, w, b, k):
        r0 = pl.multiple_of(k * q, q)
        p0 = pl.multiple_of(jnp.maximum(r0 - 16, 0), 16)
        n0 = pl.multiple_of(jnp.minimum(r0 + q, l - 16), 16)

        def rows_at(start, n):
            parts = [r[0, pl.ds(start, n), ls] for r, ls in srcs]
            return parts[0] if len(parts) == 1 else jnp.concatenate(parts, axis=1)

        blk = rows_at(r0, q)
        before = rows_at(p0, 16)
        after = rows_at(n0, 16)
        before = jnp.where(k > 0, before, jnp.zeros_like(before))
        after = jnp.where(k < nc - 1, after, jnp.zeros_like(after))
        stacked = jnp.concatenate([blk, before, after], axis=0)
        v = (w[0:1] * _dot(shift_prev, stacked) + w[1:2] * blk.astype(F32)
             + w[2:3] * _dot(shift_next, stacked) + b)
        return _silu_t(v)

    all_lanes = slice(None)

    def prep(k):
        r0 = pl.multiple_of(k * q, q)
        x = conv_silu(((xp_ref, all_lanes),), cwx_ref[...], cbx_ref[...], k)
        xs_ref[pl.ds(r0, q), :] = x.astype(BF16)
        bc = conv_silu(((bp_ref, all_lanes), (cp_ref, all_lanes)),
                       jnp.concatenate([cwb_ref[...], cwc_ref[...]], axis=1),
                       jnp.concatenate([cbb_ref[...], cbc_ref[...]], axis=1), k)
        b_t = bc[:, :SSD_STATE].T.astype(BF16)
        bt_ref[k] = b_t
        cc_ref[pl.ds(r0, q), :] = bc[:, SSD_STATE:].astype(BF16)
        sp_w = sp_ref[0, pl.ds(r0, q), 0:LANES]
        xwf_ref[pl.ds(r0, q), :] = (x * _dot(sp_w, e2f_ref[0, 0])).astype(BF16)
        xw_b = (x * _dot(sp_w, e2b_ref[0, 0])).astype(BF16)
        e_last = sp_ref[0, pl.ds(pl.multiple_of(r0 + q - 16, 16), 16), LANES:2 * LANES]
        e_first = sp_ref[0, pl.ds(r0, 16), LANES:2 * LANES]
        edge_ref[k] = _dot(e_last, e2f_ref[0, 0])[15:16, :]
        edge_b = _dot(e_first, e2b_ref[0, 0])[0:1, :]
        return b_t, xw_b, edge_b

    sb_ref[...] = s0_ref[0, 0, 1]

    def bstep(i, carry):
        c = nc - 1 - i
        b_t, xw_b, edge_b = prep(c)
        sbe_ref[c] = sb_ref[...].astype(BF16)
        sb_ref[...] = sb_ref[...] * edge_b + _dot(b_t, xw_b)
        return carry

    lax.fori_loop(0, nc, bstep, 0, unroll=8)
    sfin_ref[0, 0, 1] = sb_ref[...]

    sf_ref[...] = s0_ref[0, 0, 0]
    li = lax.broadcasted_iota(jnp.int32, (q, q), 0)
    si = lax.broadcasted_iota(jnp.int32, (q, q), 1)
    lower = si <= li
    eye = si == li
    left = lax.broadcasted_iota(jnp.int32, (q, LANES), 1) < SSD_HEAD_DIM
    fu = yo_ref.shape[0]
    lane0 = 16 * pl.program_id(1)

    def chunk(c, slot):
        r0 = pl.multiple_of(c * q, q)
        cc = cc_ref[pl.ds(r0, q), :]
        sp_e = sp_ref[0, pl.ds(r0, q), LANES:2 * LANES]
        yo_ref[slot] = (_dot(sp_e, e2f_ref[0, 0]) * _dot(cc, sf_ref[...].astype(BF16))
                        + _dot(sp_e, e2b_ref[0, 0]) * _dot(cc, sbe_ref[c]))
        sf_ref[...] = sf_ref[...] * edge_ref[c] + _dot(bt_ref[c], xwf_ref[pl.ds(r0, q), :])
        g = _dot(cc, bt_ref[c])
        col = pltpu.roll(col_ref[0, pl.ds(r0, q), :], LANES - lane0, 1)
        row = row_ref[0, 0, c]
        for j in range(SSD_HPG // 2):
            lanes = slice(j * LANES, (j + 1) * LANES)
            xf = xs_ref[pl.ds(r0, q), lanes].astype(F32)
            x_diag = jnp.concatenate([jnp.where(left, xf, 0.0), jnp.where(left, 0.0, xf)], axis=0).astype(BF16)
            ms = []
            for h in (2 * j, 2 * j + 1):
                cs_l = jnp.take_along_axis(col, jnp.where(lower, h, 8 + h), axis=1, mode="promise_in_bounds")
                arg = cs_l - jnp.where(lower, row[h:h + 1, :], row[8 + h:9 + h, :])
                p = jnp.exp2(jnp.where(eye, row[16 + h:17 + h, :], arg))
                ms.append((g * p).astype(BF16))
            y = _dot(jnp.concatenate(ms, axis=1), x_diag) + yo_ref[slot, :, lanes] + dexp_ref[0, :, lanes] * xf
            y_ref[0, pl.ds(r0, q), lanes] = y.astype(y_ref.dtype)

    def fstep(i, carry):
        for slot in range(fu):
            chunk(fu * i + slot, slot)
        return carry

    lax.fori_loop(0, nc // fu, fstep, 0)
    sfin_ref[0, 0, 0] = sf_ref[...]


def _expanders():
    e = np.zeros((2, SSD_GROUPS, LANES, GROUP_W), np.float32)
    for d in range(2):
        for g in range(SSD_GROUPS):
            for s in range(2):
                for j in range(SSD_HPG):
                    e[d, g, (d * SSD_GROUPS + g) * 16 + s * 8 + j, j * SSD_HEAD_DIM:(j + 1) * SSD_HEAD_DIM] = 1.0
    return jnp.asarray(e, BF16)


def ssd_mixer(proj, xbc_col0, dt_raw, conv_w, conv_b, dt_bias, a_log, d_skip, s0, q):
    b, l, _ = proj.shape
    assert l % q == 0 and q % 16 == 0, (l, q)
    nc = l // q
    gn = SSD_GROUPS
    bias = jnp.zeros((1, LANES), F32).at[0, :2 * SSD_HEADS].set(dt_bias.reshape(-1))
    a = jnp.zeros((1, LANES), F32).at[0, :2 * SSD_HEADS].set(-jnp.exp(a_log.reshape(-1)))
    colg, rowg, sp = dt_prep(dt_raw, bias, a, q)
    dexp = jnp.repeat(d_skip.astype(F32), SSD_HEAD_DIM).reshape(gn, 1, GROUP_W)
    e2 = _expanders()

    cw = conv_w.astype(F32)
    cb = conv_b.astype(F32).reshape(1, -1)
    nb = SSD_INNER
    cwx, cwb, cwc = cw[:, :nb], cw[:, nb:nb + gn * SSD_STATE], cw[:, nb + gn * SSD_STATE:]
    cbx, cbb, cbc = cb[:, :nb], cb[:, nb:nb + gn * SSD_STATE], cb[:, nb + gn * SSD_STATE:]

    fu = max(u for u in (8, 4, 2, 1) if nc % u == 0)
    xo = xbc_col0 // GROUP_W
    bo = (xbc_col0 + SSD_INNER) // SSD_STATE
    co = bo + gn
    st_spec = pl.BlockSpec((1, 1, 2, SSD_STATE, GROUP_W), lambda i, g: (i, g, 0, 0, 0))
    y, sfin = pl.pallas_call(
        functools.partial(_ssd_kernel, l=l, q=q),
        grid=(b, gn),
        in_specs=[pl.BlockSpec((1, l, GROUP_W), lambda i, g: (i, 0, xo + g)),
                  pl.BlockSpec((1, l, SSD_STATE), lambda i, g: (i, 0, bo + g)),
                  pl.BlockSpec((1, l, SSD_STATE), lambda i, g: (i, 0, co + g)),
                  pl.BlockSpec((3, GROUP_W), lambda i, g: (0, g)),
                  pl.BlockSpec((3, SSD_STATE), lambda i, g: (0, g)),
                  pl.BlockSpec((3, SSD_STATE), lambda i, g: (0, g)),
                  pl.BlockSpec((1, GROUP_W), lambda i, g: (0, g)),
                  pl.BlockSpec((1, SSD_STATE), lambda i, g: (0, g)),
                  pl.BlockSpec((1, SSD_STATE), lambda i, g: (0, g)),
                  pl.BlockSpec((1, l, LANES), lambda i, g: (i, 0, 0)),
                  pl.BlockSpec((1, 1, nc, 24, q), lambda i, g: (i, g, 0, 0, 0)),
                  pl.BlockSpec((1, l, 2 * LANES), lambda i, g: (i, 0, 0)),
                  pl.BlockSpec((1, 1, LANES, GROUP_W), lambda i, g: (0, g, 0, 0)),
                  pl.BlockSpec((1, 1, LANES, GROUP_W), lambda i, g: (1, g, 0, 0)),
                  pl.BlockSpec((1, 1, GROUP_W), lambda i, g: (g, 0, 0)),
                  st_spec],
        out_specs=[pl.BlockSpec((1, l, GROUP_W), lambda i, g: (i, 0, g)), st_spec],
        out_shape=[jax.ShapeDtypeStruct((b, l, SSD_INNER), BF16),
                   jax.ShapeDtypeStruct((b, gn, 2, SSD_STATE, GROUP_W), F32)],
        scratch_shapes=[pltpu.VMEM((l, GROUP_W), BF16),
                        pltpu.VMEM((l, SSD_STATE), BF16),
                        pltpu.VMEM((nc, SSD_STATE, q), BF16),
                        pltpu.VMEM((l, GROUP_W), BF16),
                        pltpu.VMEM((nc, SSD_STATE, GROUP_W), BF16),
                        pltpu.VMEM((nc, 1, GROUP_W), F32),
                        pltpu.VMEM((SSD_STATE, GROUP_W), F32),
                        pltpu.VMEM((SSD_STATE, GROUP_W), F32),
                        pltpu.VMEM((fu, q, GROUP_W), F32)],
        compiler_params=_cparams(("parallel", "parallel")),
        name="ssd",
    )(proj, proj, proj, cwx, cwb, cwc, cbx, cbb, cbc, colg, rowg, sp, e2, e2, dexp, s0)
    return y, sfin


def _mixout_kernel(y_ref, z_ref, gb_ref, gc_ref, hv_ref, g0_ref, g1_ref, x_ref, gate_ref,
                   ng_ref, bg_ref, scw_ref, wssd_ref, wsc_ref, wo_ref, o_ref, *, tm, period):
    ns = 2 if (tm // 2) % period == 0 else 1
    ts = tm // ns
    pos = lax.broadcasted_iota(jnp.int32, (ts, 1), 0) % period
    w = scw_ref[...]
    bg = bg_ref[...]
    for s in range(ns):
        r = slice(s * ts, (s + 1) * ts)
        yz = y_ref[0, r, :].astype(F32) * _silu_t(z_ref[0, r, :].astype(F32))
        ms = jnp.mean(yz * yz, axis=-1, keepdims=True)
        yn = (yz * lax.rsqrt(ms + EPS) * ng_ref[...]).astype(BF16)
        y_ssd = _dot(yn, wssd_ref[...])

        u = gc_ref[0, r, :].astype(F32) * hv_ref[0, r, :].astype(F32)
        u_prev = jnp.where(pos == 0, 0.0, pltpu.roll(u, 1, 0))
        u_next = jnp.where(pos == period - 1, 0.0, pltpu.roll(u, ts - 1, 0))
        v = w[0:1] * u_prev + w[1:2] * u + w[2:3] * u_next
        y_sc = _dot((gb_ref[0, r, :].astype(F32) * v).astype(BF16), wsc_ref[...])

        g0 = _sigmoid_t(g0_ref[0, r, :].astype(F32) + bg[:, :D_MODEL])
        g1 = _sigmoid_t(g1_ref[0, r, :].astype(F32) + bg[:, D_MODEL:])
        out = _dot((g0 * y_ssd + g1 * y_sc).astype(BF16), wo_ref[...])
        o_ref[0, r, :] = x_ref[0, r, :] + gate_ref[0] * out


def mixer_out(y, proj, x, gate, norm_g, b_gate, sc_conv_w, w_ssd, w_sc, w_o, tm, period):
    b, l, d = x.shape
    pc = lambda k: pl.BlockSpec((1, tm, d), lambda i, m, k=k: (i, m, k))
    full = lambda shp: pl.BlockSpec(shp, lambda i, m: (0,) * len(shp))
    return pl.pallas_call(
        functools.partial(_mixout_kernel, tm=tm, period=period),
        grid=(b, l // tm),
        in_specs=[pl.BlockSpec((1, tm, SSD_INNER), lambda i, m: (i, m, 0)),
                  pl.BlockSpec((1, tm, SSD_INNER), lambda i, m: (i, m, 0)),
                  pc(5), pc(6), pc(7), pc(8), pc(9),
                  pl.BlockSpec((1, tm, d), lambda i, m: (i, m, 0)),
                  pl.BlockSpec((1, 1, d), lambda i, m: (i, 0, 0)),
                  full((1, SSD_INNER)), full((1, 2 * d)), full((3, SC_WIDTH)),
                  full((SSD_INNER, d)), full((SC_WIDTH, d)), full((d, d))],
        out_specs=pl.BlockSpec((1, tm, d), lambda i, m: (i, m, 0)),
        out_shape=jax.ShapeDtypeStruct((b, l, d), F32),
        compiler_params=_cparams(("parallel", "parallel")),
        name="mixer_out",
    )(y, proj, proj, proj, proj, proj, proj, x, gate,
      norm_g.reshape(1, -1), b_gate.reshape(1, -1), sc_conv_w, w_ssd, w_sc, w_o)


def _ffn_kernel(x_ref, g_ref, sh_ref, sc_ref, gate_ref, w1_ref, w3_ref, w2_ref, o_ref, *, nf):
    fw = w1_ref.shape[1] // nf
    ts = x_ref.shape[1] // ROW_SUBTILES
    for s in range(ROW_SUBTILES):
        r = slice(s * ts, (s + 1) * ts)
        x = x_ref[0, r, :]
        hb = _norm_mod(x, g_ref[...], sh_ref[0], sc_ref[0]).astype(BF16)
        acc = None
        for k in range(nf):
            a = _dot(hb, w1_ref[:, k * fw:(k + 1) * fw])
            bb = _dot(hb, w3_ref[:, k * fw:(k + 1) * fw])
            part = _dot((_silu_t(a) * bb).astype(BF16), w2_ref[k * fw:(k + 1) * fw, :])
            acc = part if acc is None else acc + part
        o_ref[0, r, :] = x + gate_ref[0] * acc


def ffn_dense(x, g, shift, scale, gate, w1, w3, w2, tm):
    b, l, d = x.shape
    f = w1.shape[1]
    vec = pl.BlockSpec((1, 1, d), lambda i, m: (i, 0, 0))
    const = lambda shp: pl.BlockSpec(shp, lambda i, m: (0, 0), pipeline_mode=pl.Buffered(1))
    return pl.pallas_call(
        functools.partial(_ffn_kernel, nf=2),
        grid=(b, l // tm),
        in_specs=[pl.BlockSpec((1, tm, d), lambda i, m: (i, m, 0)),
                  pl.BlockSpec((1, d), lambda i, m: (0, 0)),
                  vec, vec, vec, const((d, f)), const((d, f)), const((f, d))],
        out_specs=pl.BlockSpec((1, tm, d), lambda i, m: (i, m, 0)),
        out_shape=jax.ShapeDtypeStruct((b, l, d), F32),
        compiler_params=_cparams(("parallel", "parallel")),
        name="ffn_dense",
    )(x, g.reshape(1, d), shift, scale, gate, w1, w3, w2)


def _router_kernel(x_ref, g_ref, sh_ref, sc_ref, rw_ref, h_ref, route_ref, route_t_ref, cnt_ref, run_ref, *, tm):
    @pl.when((pl.program_id(0) == 0) & (pl.program_id(1) == 0))
    def _():
        run_ref[...] = jnp.zeros_like(run_ref)

    h = _norm_mod(x_ref[0], g_ref[...], sh_ref[0], sc_ref[0])
    h_hi, h_lo = _split2(h)
    h_ref[0] = h
    w_hi, w_lo = _split2(rw_ref[...])
    logits = _dot(h_hi, w_hi) + _dot(h_lo, w_hi) + _dot(h_hi, w_lo)
    lane = lax.broadcasted_iota(jnp.int32, (tm, LANES), 1)
    ninf = float("-inf")
    lg = jnp.where(lane < N_EXPERTS, logits, ninf)
    m1 = jnp.max(lg, axis=1, keepdims=True)
    i1 = jnp.min(jnp.where(lg == m1, lane, LANES), axis=1, keepdims=True)
    lg2 = jnp.where(lane == i1, ninf, lg)
    m2 = jnp.max(lg2, axis=1, keepdims=True)
    i2 = jnp.min(jnp.where(lg2 == m2, lane, LANES), axis=1, keepdims=True)
    e2 = jnp.exp(m2 - m1)
    den = 1.0 + e2
    sel1 = jnp.where(lane == i1, 1.0, 0.0)
    sel2 = jnp.where(lane == i2, 1.0, 0.0)
    cnt = sel1 + sel2
    r = lax.broadcasted_iota(jnp.int32, (tm, tm), 0)
    c = lax.broadcasted_iota(jnp.int32, (tm, tm), 1)
    tri = jnp.where(c < r, 1.0, 0.0).astype(BF16)
    base = _dot(tri, cnt.astype(BF16)) + run_ref[0:1, :]
    r1 = jnp.sum(sel1 * base, axis=1, keepdims=True)
    r2 = jnp.sum(sel2 * base, axis=1, keepdims=True)
    vals = (i1.astype(F32), i2.astype(F32), 1.0 / den, e2 / den, r1, r2)
    out = jnp.zeros((tm, LANES), F32)
    for k, v in enumerate(vals):
        out = jnp.where(lane == k, v, out)
    route_ref[0] = out
    route_t_ref[...] = out.T
    new_run = run_ref[...] + jnp.sum(cnt, axis=0, keepdims=True)
    run_ref[...] = new_run
    cnt_ref[...] = new_run


def router(x, g, shift, scale, router_w, tm):
    b, l, d = x.shape
    rw = jnp.zeros((d, LANES), F32).at[:, :N_EXPERTS].set(router_w)
    vec = pl.BlockSpec((1, 1, d), lambda i, m: (i, 0, 0))
    mt = l // tm
    return pl.pallas_call(
        functools.partial(_router_kernel, tm=tm),
        grid=(b, l // tm),
        in_specs=[pl.BlockSpec((1, tm, d), lambda i, m: (i, m, 0)),
                  pl.BlockSpec((1, d), lambda i, m: (0, 0)),
                  vec, vec, pl.BlockSpec((d, LANES), lambda i, m: (0, 0))],
        out_specs=[pl.BlockSpec((1, tm, d), lambda i, m: (i, m, 0)),
                   pl.BlockSpec((1, tm, LANES), lambda i, m: (i, m, 0)),
                   pl.BlockSpec((LANES, tm), lambda i, m: (0, i * mt + m)),
                   pl.BlockSpec((8, LANES), lambda i, m: (0, 0))],
        out_shape=[jax.ShapeDtypeStruct((b, l, d), F32),
                   jax.ShapeDtypeStruct((b, l, LANES), F32),
                   jax.ShapeDtypeStruct((LANES, b * l), F32),
                   jax.ShapeDtypeStruct((8, LANES), F32)],
        scratch_shapes=[pltpu.VMEM((8, LANES), F32)],
        compiler_params=_cparams(("arbitrary", "arbitrary")),
        name="router",
    )(x, g.reshape(1, d), shift, scale, rw)


def _pack_bf16_pairs(y):
    k = y.shape[1] // 2
    bits = lax.bitcast_convert_type(y.astype(BF16).astype(F32), jnp.uint32)
    return bits[:, :k] | (bits[:, k:] >> 16)


def _unpack_bf16_pairs(p):
    hi = lax.bitcast_convert_type(p & jnp.uint32(0xFFFF0000), F32)
    lo = lax.bitcast_convert_type(p << 16, F32)
    return jnp.concatenate([hi, lo], axis=1)


def _gffn_kernel(te_ref, nv_ref, x_ref, w1_ref, w3_ref, w2_ref, o_ref, acc_ref, *, nf):
    i = pl.program_id(0)
    f = pl.program_id(1)

    @pl.when(i < nv_ref[0])
    def _():
        ts = x_ref.shape[0] // ROW_SUBTILES
        parts = []
        for s in range(ROW_SUBTILES):
            x = x_ref[s * ts:(s + 1) * ts, :].astype(BF16)
            a = _dot(x, w1_ref[0])
            bb = _dot(x, w3_ref[0])
            parts.append(_dot((_silu_t(a) * bb).astype(BF16), w2_ref[0]))
        part = jnp.concatenate(parts, axis=0)

        @pl.when(f == 0)
        def _():
            acc_ref[...] = part

        @pl.when(f > 0)
        def _():
            acc_ref[...] += part

        @pl.when(f == nf - 1)
        def _():
            o_ref[...] = _pack_bf16_pairs(acc_ref[...])


def grouped_ffn(xs, tile_expert, n_valid, w1, w3, w2, tm, nf):
    rows, d = xs.shape
    nt = rows // tm
    f = w1.shape[2]
    fw = f // nf

    def tile(i, nv):
        return jnp.minimum(i, nv[0] - 1)

    def fchunk(i, k, nv):
        return jnp.where(i < nv[0], k, nf - 1)

    grid_spec = pltpu.PrefetchScalarGridSpec(
        num_scalar_prefetch=2,
        grid=(nt, nf),
        in_specs=[pl.BlockSpec((tm, d), lambda i, k, te, nv: (tile(i, nv), 0)),
                  pl.BlockSpec((1, d, fw), lambda i, k, te, nv: (te[i], 0, fchunk(i, k, nv))),
                  pl.BlockSpec((1, d, fw), lambda i, k, te, nv: (te[i], 0, fchunk(i, k, nv))),
                  pl.BlockSpec((1, fw, d), lambda i, k, te, nv: (te[i], fchunk(i, k, nv), 0))],
        out_specs=pl.BlockSpec((tm, d // 2), lambda i, k, te, nv: (tile(i, nv), 0)),
        scratch_shapes=[pltpu.VMEM((tm, d), F32)],
    )
    return pl.pallas_call(
        functools.partial(_gffn_kernel, nf=nf),
        grid_spec=grid_spec,
        out_shape=jax.ShapeDtypeStruct((rows, d // 2), jnp.uint32),
        compiler_params=_cparams(("arbitrary", "arbitrary")),
        name="grouped_ffn",
    )(tile_expert, n_valid, xs, w1, w3, w2)


def _combine_kernel(x_ref, y0_ref, y1_ref, route_ref, gate_ref, fg_ref, o_ref):
    r = route_ref[0]
    moe = r[:, 2:3] * _unpack_bf16_pairs(y0_ref[0, 0]) + r[:, 3:4] * _unpack_bf16_pairs(y1_ref[0, 0])
    xn = x_ref[0] + gate_ref[0] * moe
    ms = jnp.mean(xn * xn, axis=-1, keepdims=True)
    o_ref[0] = xn * lax.rsqrt(ms + EPS) * fg_ref[...]


def combine_final(x, yg, route, gate, final_g, tm):
    b, l, d = x.shape
    return pl.pallas_call(
        _combine_kernel,
        grid=(b, l // tm),
        in_specs=[pl.BlockSpec((1, tm, d), lambda i, m: (i, m, 0)),
                  pl.BlockSpec((1, 1, tm, d // 2), lambda i, m: (0, i, m, 0)),
                  pl.BlockSpec((1, 1, tm, d // 2), lambda i, m: (1, i, m, 0)),
                  pl.BlockSpec((1, tm, LANES), lambda i, m: (i, m, 0)),
                  pl.BlockSpec((1, 1, d), lambda i, m: (i, 0, 0)),
                  pl.BlockSpec((1, d), lambda i, m: (0, 0))],
        out_specs=pl.BlockSpec((1, tm, d), lambda i, m: (i, m, 0)),
        out_shape=jax.ShapeDtypeStruct((b, l, d), F32),
        compiler_params=_cparams(("parallel", "parallel")),
        name="combine_final",
    )(x, yg, yg, route, gate, final_g.reshape(1, d))


SC_CORES = 2
SC_SUBCORES = 16
SC_WORKERS = SC_CORES * SC_SUBCORES
SC_STREAM_BYTES = 256 * 1024
SC_STREAM_ROWS = 128


def _sc_rows(per_worker, d, dtype):
    return min(SC_STREAM_ROWS, SC_STREAM_BYTES // (d * jnp.dtype(dtype).itemsize), per_worker)


def _sc_mesh():
    return plsc.VectorSubcoreMesh(core_axis_name="c", subcore_axis_name="s",
                                  num_cores=SC_CORES, num_subcores=SC_SUBCORES)


def dispatch_rows(h, pos0, pos1, n_rows):
    t, d = h.shape
    per_w = t // SC_WORKERS
    ch = _sc_rows(per_w, d, h.dtype)
    assert t % SC_WORKERS == 0 and per_w % ch == 0 and ch % 8 == 0, (t, ch)

    @functools.partial(
        pl.kernel, mesh=_sc_mesh(),
        out_type=jax.ShapeDtypeStruct((n_rows, d), h.dtype),
        scratch_types=[pltpu.VMEM((ch,), jnp.int32), pltpu.VMEM((ch,), jnp.int32),
                       pltpu.VMEM((ch, d), h.dtype), pltpu.SemaphoreType.DMA],
        name="moe_dispatch")
    def scatter(h_hbm, p0_hbm, p1_hbm, out_hbm, i0_v, i1_v, rows_v, sem):
        base = (lax.axis_index("s") * SC_CORES + lax.axis_index("c")) * per_w

        @pl.loop(0, per_w // ch)
        def _(j):
            off = base + j * ch
            pltpu.sync_copy(h_hbm.at[pl.ds(off, ch)], rows_v)
            pltpu.sync_copy(p0_hbm.at[pl.ds(off, ch)], i0_v)
            pltpu.sync_copy(p1_hbm.at[pl.ds(off, ch)], i1_v)
            pltpu.async_copy(rows_v, out_hbm.at[i0_v], sem).wait()
            pltpu.async_copy(rows_v, out_hbm.at[i1_v], sem).wait()

    return scatter(h, pos0, pos1)


def return_rows(ys, idx):
    n = idx.shape[0]
    d = ys.shape[1]
    per_w = n // SC_WORKERS
    ch = _sc_rows(per_w, d, ys.dtype)
    assert n % SC_WORKERS == 0 and per_w % ch == 0 and ch % 8 == 0, (n, ch)

    @functools.partial(
        pl.kernel, mesh=_sc_mesh(),
        out_type=jax.ShapeDtypeStruct((n, d), ys.dtype),
        scratch_types=[pltpu.VMEM((ch,), jnp.int32), pltpu.VMEM((ch, d), ys.dtype), pltpu.SemaphoreType.DMA],
        name="moe_return")
    def gather(ys_hbm, idx_hbm, out_hbm, idx_v, rows_v, sem):
        base = (lax.axis_index("s") * SC_CORES + lax.axis_index("c")) * per_w

        @pl.loop(0, per_w // ch)
        def _(j):
            off = base + j * ch
            pltpu.sync_copy(idx_hbm.at[pl.ds(off, ch)], idx_v)
            pltpu.async_copy(ys_hbm.at[idx_v], rows_v, sem).wait()
            pltpu.sync_copy(rows_v, out_hbm.at[pl.ds(off, ch)])

    return gather(ys, idx)


def moe_block(x, g, shift, scale, gate, router_w, w1, w3, w2, final_g):
    b, l, d = x.shape
    t = b * l
    tm = MOE_TM
    h, route, route_t, counts = router(x, g, shift, scale, router_w, _row_tile(l, ROW_TM))
    cnt = counts[0, :N_EXPERTS].astype(jnp.int32)
    gs = ((cnt + tm - 1) // tm) * tm
    ends = jnp.cumsum(gs)
    offs = ends - gs

    def sorted_row(choice):
        e = route_t[choice].astype(jnp.int32)
        start = sum(jnp.where(e == k, offs[k], 0) for k in range(N_EXPERTS))
        return start + route_t[4 + choice].astype(jnp.int32)

    pos0, pos1 = sorted_row(0), sorted_row(1)
    nt = (2 * t) // tm + N_EXPERTS
    n_valid = (ends[-1] // tm).astype(jnp.int32).reshape(1)
    tile = jnp.minimum(jnp.arange(nt, dtype=jnp.int32), n_valid[0] - 1)
    tile_expert = jnp.sum((tile[:, None] >= (ends // tm)[None, :]).astype(jnp.int32), axis=1)
    xs = dispatch_rows(h.reshape(t, d), pos0, pos1, nt * tm)
    ys = grouped_ffn(xs, tile_expert, n_valid, w1, w3, w2, tm, 2)
    yg = return_rows(ys, jnp.concatenate([pos0, pos1])).reshape(2, b, l, d // 2)
    return combine_final(x, yg, route, gate, final_g, _row_tile(l, ROW_TM))


def _in_weights(w_in):
    o1 = SSD_INNER
    o2 = o1 + XBC_WIDTH
    o3 = o2 + 2 * SSD_HEADS
    w_main = jnp.concatenate([w_in[:, :o2], w_in[:, o3:]], axis=1).astype(BF16)
    w_dt = jnp.pad(w_in[:, o2:o3], ((0, 0), (0, LANES - 2 * SSD_HEADS))).astype(BF16)
    return w_main, w_dt


def kernel(x, c, ctx, c_ctx, w_mod, b_mod, norm1_g, norm2_g, w_in, b_gate, ssd_conv_w, ssd_conv_b, ssd_dt_bias, ssd_a_log, ssd_d, ssd_norm_g, w_ssd_out, sc_conv_w, w_sc_out, w_o, ffn_w1, ffn_w3, ffn_w2, router_w, moe_w1, moe_w3, moe_w2, final_g):
    b, l, d = x.shape
    lc = ctx.shape[1]
    depth = w_mod.shape[0]
    assert depth % 2 == 0, "the final norm is fused into the routed channel mixer of the (odd) last layer"
    cc = jnp.zeros((16, d), F32).at[:b].set(c).at[b].set(c_ctx)
    mod = modulation(cc, w_mod, b_mod)
    zeros_state = jnp.zeros((b, SSD_GROUPS, 2, SSD_STATE, GROUP_W), F32)
    nctx = b * lc
    ctx = ctx.reshape(1, nctx, d)
    tmc = _row_tile(nctx, ROW_TM, lc)
    tmx = _row_tile(l, ROW_TM, GRID_W)

    def per_seq(t):
        return t.reshape(b, lc, t.shape[-1])

    for i in range(depth):
        last = i == depth - 1
        mx = mod[i, :b].reshape(b, N_MOD, 1, d)
        mc = mod[i, b].reshape(1, N_MOD, 1, d)
        w_main, w_dt = _in_weights(w_in[i])
        ssd_p = (ssd_conv_w[i], ssd_conv_b[i], ssd_dt_bias[i], ssd_a_log[i], ssd_d[i])
        out_p = (ssd_norm_g[i], b_gate[i], sc_conv_w[i], w_ssd_out[i].astype(BF16),
                 w_sc_out[i].astype(BF16), w_o[i].astype(BF16))

        if last:
            w_xbc = w_main[:, SSD_INNER:SSD_INNER + XBC_WIDTH]
            proj_c, dt_c = in_proj(ctx, norm1_g[i], mc[:, 0], mc[:, 1], w_xbc, w_dt,
                                   _row_tile(nctx, PROJ_TM), PROJ_TN_XBC)
            _, s_ctx = ssd_mixer(per_seq(proj_c), 0, per_seq(dt_c), *ssd_p, zeros_state, SSD_Q)
        else:
            proj_c, dt_c = in_proj(ctx, norm1_g[i], mc[:, 0], mc[:, 1], w_main, w_dt,
                                   _row_tile(nctx, PROJ_TM), PROJ_TN)
            y_c, s_ctx = ssd_mixer(per_seq(proj_c), SSD_INNER, per_seq(dt_c), *ssd_p, zeros_state, SSD_Q)
            ctx = mixer_out(y_c.reshape(1, nctx, SSD_INNER), proj_c, ctx, mc[:, 2], *out_p, tmc, lc)

        proj_x, dt_x = in_proj(x, norm1_g[i], mx[:, 0], mx[:, 1], w_main, w_dt, _row_tile(l, PROJ_TM), PROJ_TN)
        y_x, _ = ssd_mixer(proj_x, SSD_INNER, dt_x, *ssd_p, s_ctx, SSD_Q)
        x = mixer_out(y_x, proj_x, x, mx[:, 2], *out_p, tmx, GRID_W)

        j = i // 2
        if i % 2 == 0:
            w1, w3, w2 = ffn_w1[j].astype(BF16), ffn_w3[j].astype(BF16), ffn_w2[j].astype(BF16)
            x = ffn_dense(x, norm2_g[i], mx[:, 3], mx[:, 4], mx[:, 5], w1, w3, w2, _row_tile(l, FFN_TM))
            if not last:
                ctx = ffn_dense(ctx, norm2_g[i], mc[:, 3], mc[:, 4], mc[:, 5], w1, w3, w2, tmc)
        else:
            assert last, "the routed channel mixer is fused with the final norm"
            w1, w3, w2 = moe_w1[j].astype(BF16), moe_w3[j].astype(BF16), moe_w2[j].astype(BF16)
            x = moe_block(x, norm2_g[i], mx[:, 3], mx[:, 4], mx[:, 5], router_w[j], w1, w3, w2, final_g)
    return x
```

```python
import functools

import numpy as np
import jax
import jax.numpy as jnp
from jax import lax
from jax.experimental import pallas as pl
from jax.experimental.pallas import tpu as pltpu
from jax.experimental.pallas import tpu_sc as plsc

F32 = jnp.float32
BF16 = jnp.bfloat16

D_MODEL = 1024
GRID_W = 64
SSD_INNER = 2048
SSD_HEADS = 32
SSD_GROUPS = 4
SSD_HPG = 8
SSD_HEAD_DIM = 64
SSD_STATE = 128
GROUP_W = SSD_HPG * SSD_HEAD_DIM
XBC_WIDTH = SSD_INNER + 2 * SSD_GROUPS * SSD_STATE
SC_WIDTH = 1024
N_MOD = 6
N_EXPERTS = 8
EPS = 1e-6
LOG2E = 1.4426950408889634

LANES = 128
SSD_Q = 128
MOE_TM = 512
ROW_SUBTILES = 2
PROJ_TM = 1024
PROJ_TN = 2560
PROJ_TN_XBC = 1024
ROW_TM = 512
FFN_TM = 1024


def _row_tile(rows, want, multiple=1):
    t = multiple * max(1, min(want, rows) // multiple)
    assert rows % t == 0, (rows, t)
    return t
VMEM_LIMIT = 56 * 1024 * 1024


def _dot(a, b):
    return jnp.dot(a, b, preferred_element_type=F32)


def _sigmoid(v):
    return 1.0 / (1.0 + jnp.exp(-v))


def _silu(v):
    return v * _sigmoid(v)


def _sigmoid_t(v):
    return 0.5 + 0.5 * jnp.tanh(0.5 * v)


def _silu_t(v):
    hv = 0.5 * v
    return hv + hv * jnp.tanh(hv)


def _split2(a):
    hi = a.astype(BF16)
    lo = (a - hi.astype(F32)).astype(BF16)
    return hi, lo


def _split3(a):
    hi = a.astype(BF16)
    r = a - hi.astype(F32)
    mid = r.astype(BF16)
    lo = (r - mid.astype(F32)).astype(BF16)
    return hi, mid, lo


def _norm_mod(x, g, shift, scale):
    ms = jnp.mean(x * x, axis=-1, keepdims=True)
    return (x * lax.rsqrt(ms + EPS) * g) * (1.0 + scale) + shift


def _cparams(sem, vmem=VMEM_LIMIT):
    return pltpu.CompilerParams(dimension_semantics=sem, vmem_limit_bytes=vmem)


def _mod_kernel(c_ref, w_ref, b_ref, o_ref):
    a_hi, a_lo = _split2(_silu(c_ref[...]))
    w_hi, w_lo = _split2(w_ref[0])
    o_ref[0] = _dot(a_hi, w_hi) + _dot(a_lo, w_hi) + _dot(a_hi, w_lo) + b_ref[0]


def modulation(cc, w_mod, b_mod):
    depth, d, n = w_mod.shape
    tn = 1536
    return pl.pallas_call(
        _mod_kernel,
        grid=(depth, n // tn),
        in_specs=[pl.BlockSpec((16, d), lambda i, j: (0, 0)),
                  pl.BlockSpec((1, d, tn), lambda i, j: (i, 0, j)),
                  pl.BlockSpec((1, 1, tn), lambda i, j: (i, 0, j))],
        out_specs=pl.BlockSpec((1, 16, tn), lambda i, j: (i, 0, j)),
        out_shape=jax.ShapeDtypeStruct((depth, 16, n), F32),
        compiler_params=_cparams(("parallel", "parallel")),
        name="modulation",
    )(cc, w_mod, b_mod.reshape(depth, 1, n))


def _inproj_kernel(x_ref, g_ref, sh_ref, sc_ref, w_ref, wdt_ref, o_ref, dt_ref, h_ref):
    @pl.when(pl.program_id(2) == 0)
    def _():
        hb = _norm_mod(x_ref[0], g_ref[...], sh_ref[0], sc_ref[0]).astype(BF16)
        h_ref[...] = hb
        dt_ref[0] = _dot(hb, wdt_ref[...])

    o_ref[0] = _dot(h_ref[...], w_ref[...]).astype(o_ref.dtype)


def in_proj(x, g, shift, scale, w, wdt, tm, tn):
    b, l, d = x.shape
    n = w.shape[1]
    return pl.pallas_call(
        _inproj_kernel,
        grid=(b, l // tm, n // tn),
        in_specs=[pl.BlockSpec((1, tm, d), lambda i, m, j: (i, m, 0)),
                  pl.BlockSpec((1, d), lambda i, m, j: (0, 0)),
                  pl.BlockSpec((1, 1, d), lambda i, m, j: (i, 0, 0)),
                  pl.BlockSpec((1, 1, d), lambda i, m, j: (i, 0, 0)),
                  pl.BlockSpec((d, tn), lambda i, m, j: (0, j)),
                  pl.BlockSpec((d, LANES), lambda i, m, j: (0, 0))],
        out_specs=[pl.BlockSpec((1, tm, tn), lambda i, m, j: (i, m, j)),
                   pl.BlockSpec((1, tm, LANES), lambda i, m, j: (i, m, 0))],
        out_shape=[jax.ShapeDtypeStruct((b, l, n), BF16),
                   jax.ShapeDtypeStruct((b, l, LANES), F32)],
        scratch_shapes=[pltpu.VMEM((tm, d), BF16)],
        compiler_params=_cparams(("parallel", "parallel", "arbitrary")),
        name="in_proj",
    )(x, g.reshape(1, d), shift, scale, w, wdt)


def _dt_kernel(raw_ref, bias_ref, a_ref, pc_ref, ph_ref, plo_ref, col_ref, row_ref, sp_ref, *, q, ch):
    row = lax.broadcasted_iota(jnp.int32, (q, q), 0)
    col = lax.broadcasted_iota(jnp.int32, (q, q), 1)
    tri_l = jnp.where(col <= row, 1.0, 0.0).astype(BF16)
    tri_u = jnp.where(col >= row, 1.0, 0.0).astype(BF16)
    lane = lax.broadcasted_iota(jnp.int32, (q, LANES), 1)
    fwd = lane < SSD_HEADS
    ph, plo = ph_ref[...], plo_ref[...]

    def place_split(t):
        t_hi, t_lo = _split2(t)
        return (_dot(t_hi, ph) + _dot(t_lo, plo)).astype(BF16)

    for k in range(ch):
        v = raw_ref[0, k * q:(k + 1) * q, :] + bias_ref[...]
        dt = jnp.maximum(v, 0.0) + jnp.log1p(jnp.exp(-jnp.abs(v)))
        d1, d2, d3 = _split3(dt * a_ref[...])
        cs_f = _dot(tri_l, d1) + _dot(tri_l, d2) + _dot(tri_l, d3)
        cs_b = _dot(tri_u, d1) + _dot(tri_u, d2) + _dot(tri_u, d3)
        cs = jnp.where(fwd, cs_f, cs_b)
        tot = jnp.where(fwd[0:1], cs_f[q - 1:q, :], cs_b[0:1, :])
        sl = slice(k * q, (k + 1) * q)
        cs2 = cs * LOG2E
        c1, c2, c3 = _split3(cs2)
        pc = pc_ref[...]
        col_ref[0, sl, :] = _dot(c1, pc) + _dot(c2, pc) + _dot(c3, pc)
        r_t = (cs2 - jnp.log(dt) * LOG2E).T
        dt_t = dt.T
        for g in range(SSD_GROUPS):
            lo = g * SSD_HPG
            dt_f = dt_t[lo:lo + 8, :]
            dt_b = dt_t[SSD_HEADS + lo:SSD_HEADS + lo + 8, :]
            row_ref[0, g, k, 0:8, :] = r_t[lo:lo + 8, :]
            row_ref[0, g, k, 8:16, :] = r_t[SSD_HEADS + lo:SSD_HEADS + lo + 8, :]
            row_ref[0, g, k, 16:24, :] = jnp.log(dt_f + dt_b) * LOG2E
        sp_ref[0, sl, 0:LANES] = place_split(dt * jnp.exp(tot - cs))
        sp_ref[0, sl, LANES:2 * LANES] = place_split(jnp.exp(cs))


def _placements():
    pc = np.zeros((LANES, LANES), np.float32)
    ph = np.zeros((LANES, LANES), np.float32)
    plo = np.zeros((LANES, LANES), np.float32)
    for d in range(2):
        for g in range(SSD_GROUPS):
            for j in range(SSD_HPG):
                src = d * SSD_HEADS + g * SSD_HPG + j
                pc[src, g * 16 + d * 8 + j] = 1.0
                ph[src, (d * SSD_GROUPS + g) * 16 + j] = 1.0
                plo[src, (d * SSD_GROUPS + g) * 16 + 8 + j] = 1.0
    return jnp.asarray(pc, BF16), jnp.asarray(ph, BF16), jnp.asarray(plo, BF16)


def dt_prep(raw, bias, a, q):
    b, l, _ = raw.shape
    nc = l // q
    ch = min(8, nc)
    gn = SSD_GROUPS
    vspec = pl.BlockSpec((1, LANES), lambda i, c: (0, 0))
    pspec = pl.BlockSpec((LANES, LANES), lambda i, c: (0, 0))
    return pl.pallas_call(
        functools.partial(_dt_kernel, q=q, ch=ch),
        grid=(b, nc // ch),
        in_specs=[pl.BlockSpec((1, ch * q, LANES), lambda i, c: (i, c, 0)), vspec, vspec, pspec, pspec, pspec],
        out_specs=[pl.BlockSpec((1, ch * q, LANES), lambda i, c: (i, c, 0)),
                   pl.BlockSpec((1, gn, ch, 24, q), lambda i, c: (i, 0, c, 0, 0)),
                   pl.BlockSpec((1, ch * q, 2 * LANES), lambda i, c: (i, c, 0))],
        out_shape=[jax.ShapeDtypeStruct((b, l, LANES), F32),
                   jax.ShapeDtypeStruct((b, gn, nc, 24, q), F32),
                   jax.ShapeDtypeStruct((b, l, 2 * LANES), BF16)],
        compiler_params=_cparams(("parallel", "parallel")),
        name="dt_prep",
    )(raw, bias, a, *_placements())


def _ssd_kernel(xp_ref, bp_ref, cp_ref, cwx_ref, cwb_ref, cwc_ref, cbx_ref, cbb_ref, cbc_ref,
                col_ref, row_ref, sp_ref, e2f_ref, e2b_ref, dexp_ref, s0_ref,
                y_ref, sfin_ref,
                xs_ref, cc_ref, bt_ref, xwf_ref, sbe_ref, edge_ref, sf_ref, sb_ref, yo_ref, *, l, q):
    nc = l // q
    sr = lax.broadcasted_iota(jnp.int32, (q, q + 32), 0)
    sc = lax.broadcasted_iota(jnp.int32, (q, q + 32), 1)
    shift_prev = jnp.where((sc == sr - 1) | ((sr == 0) & (sc == q + 15)), 1.0, 0.0).astype(BF16)
    shift_next = jnp.where(((sc == sr + 1) & (sc < q)) | ((sr == q - 1) & (sc == q + 16)), 1.0, 0.0).astype(BF16)

    def conv_silu(srcs, w, b, k):
        r0 = pl.multiple_of(k * q, q)
        p0 = pl.multiple_of(jnp.maximum(r0 - 16, 0), 16)
        n0 = pl.multiple_of(jnp.minimum(r0 + q, l - 16), 16)

        def rows_at(start, n):
            parts = [r[0, pl.ds(start, n), ls] for r, ls in srcs]
            return parts[0] if len(parts) == 1 else jnp.concatenate(parts, axis=1)

        blk = rows_at(r0, q)
        before = rows_at(p0, 16)
        after = rows_at(n0, 16)
        before = jnp.where(k > 0, before, jnp.zeros_like(before))
        after = jnp.where(k < nc - 1, after, jnp.zeros_like(after))
        stacked = jnp.concatenate([blk, before, after], axis=0)
        v = (w[0:1] * _dot(shift_prev, stacked) + w[1:2] * blk.astype(F32)
             + w[2:3] * _dot(shift_next, stacked) + b)
        return _silu_t(v)

    all_lanes = slice(None)

    def prep(k):
        r0 = pl.multiple_of(k * q, q)
        x = conv_silu(((xp_ref, all_lanes),), cwx_ref[...], cbx_ref[...], k)
        xs_ref[pl.ds(r0, q), :] = x.astype(BF16)
        bc = conv_silu(((bp_ref, all_lanes), (cp_ref, all_lanes)),
                       jnp.concatenate([cwb_ref[...], cwc_ref[...]], axis=1),
                       jnp.concatenate([cbb_ref[...], cbc_ref[...]], axis=1), k)
        b_t = bc[:, :SSD_STATE].T.astype(BF16)
        bt_ref[k] = b_t
        cc_ref[pl.ds(r0, q), :] = bc[:, SSD_STATE:].astype(BF16)
        sp_w = sp_ref[0, pl.ds(r0, q), 0:LANES]
        xwf_ref[pl.ds(r0, q), :] = (x * _dot(sp_w, e2f_ref[0, 0])).astype(BF16)
        xw_b = (x * _dot(sp_w, e2b_ref[0, 0])).astype(BF16)
        e_last = sp_ref[0, pl.ds(pl.multiple_of(r0 + q - 16, 16), 16), LANES:2 * LANES]
        e_first = sp_ref[0, pl.ds(r0, 16), LANES:2 * LANES]
        edge_ref[k] = _dot(e_last, e2f_ref[0, 0])[15:16, :]
        edge_b = _dot(e_first, e2b_ref[0, 0])[0:1, :]
        return b_t, xw_b, edge_b

    sb_ref[...] = s0_ref[0, 0, 1]

    def bstep(i, carry):
        c = nc - 1 - i
        b_t, xw_b, edge_b = prep(c)
        sbe_ref[c] = sb_ref[...].astype(BF16)
        sb_ref[...] = sb_ref[...] * edge_b + _dot(b_t, xw_b)
        return carry

    lax.fori_loop(0, nc, bstep, 0, unroll=16)
    sfin_ref[0, 0, 1] = sb_ref[...]

    sf_ref[...] = s0_ref[0, 0, 0]
    li = lax.broadcasted_iota(jnp.int32, (q, q), 0)
    si = lax.broadcasted_iota(jnp.int32, (q, q), 1)
    lower = si <= li
    eye = si == li
    left = lax.broadcasted_iota(jnp.int32, (q, LANES), 1) < SSD_HEAD_DIM
    fu = yo_ref.shape[0]
    lane0 = 16 * pl.program_id(1)

    def chunk(c, slot):
        r0 = pl.multiple_of(c * q, q)
        cc = cc_ref[pl.ds(r0, q), :]
        sp_e = sp_ref[0, pl.ds(r0, q), LANES:2 * LANES]
        yo_ref[slot] = (_dot(sp_e, e2f_ref[0, 0]) * _dot(cc, sf_ref[...].astype(BF16))
                        + _dot(sp_e, e2b_ref[0, 0]) * _dot(cc, sbe_ref[c]))
        sf_ref[...] = sf_ref[...] * edge_ref[c] + _dot(bt_ref[c], xwf_ref[pl.ds(r0, q), :])
        g = _dot(cc, bt_ref[c])
        col = pltpu.roll(col_ref[0, pl.ds(r0, q), :], LANES - lane0, 1)
        row = row_ref[0, 0, c]
        for j in range(SSD_HPG // 2):
            lanes = slice(j * LANES, (j + 1) * LANES)
            xf = xs_ref[pl.ds(r0, q), lanes].astype(F32)
            x_diag = jnp.concatenate([jnp.where(left, xf, 0.0), jnp.where(left, 0.0, xf)], axis=0).astype(BF16)
            ms = []
            for h in (2 * j, 2 * j + 1):
                cs_l = jnp.take_along_axis(col, jnp.where(lower, h, 8 + h), axis=1, mode="promise_in_bounds")
                arg = cs_l - jnp.where(lower, row[h:h + 1, :], row[8 + h:9 + h, :])
                p = jnp.exp2(jnp.where(eye, row[16 + h:17 + h, :], arg))
                ms.append((g * p).astype(BF16))
            y = _dot(jnp.concatenate(ms, axis=1), x_diag) + yo_ref[slot, :, lanes] + dexp_ref[0, :, lanes] * xf
            y_ref[0, pl.ds(r0, q), lanes] = y.astype(y_ref.dtype)

    def fstep(i, carry):
        for slot in range(fu):
            chunk(fu * i + slot, slot)
        return carry

    lax.fori_loop(0, nc // fu, fstep, 0)
    sfin_ref[0, 0, 0] = sf_ref[...]


def _expanders():
    e = np.zeros((2, SSD_GROUPS, LANES, GROUP_W), np.float32)
    for d in range(2):
        for g in range(SSD_GROUPS):
            for s in range(2):
                for j in range(SSD_HPG):
                    e[d, g, (d * SSD_GROUPS + g) * 16 + s * 8 + j, j * SSD_HEAD_DIM:(j + 1) * SSD_HEAD_DIM] = 1.0
    return jnp.asarray(e, BF16)


def ssd_mixer(proj, xbc_col0, dt_raw, conv_w, conv_b, dt_bias, a_log, d_skip, s0, q):
    b, l, _ = proj.shape
    assert l % q == 0 and q % 16 == 0, (l, q)
    nc = l // q
    gn = SSD_GROUPS
    bias = jnp.zeros((1, LANES), F32).at[0, :2 * SSD_HEADS].set(dt_bias.reshape(-1))
    a = jnp.zeros((1, LANES), F32).at[0, :2 * SSD_HEADS].set(-jnp.exp(a_log.reshape(-1)))
    colg, rowg, sp = dt_prep(dt_raw, bias, a, q)
    dexp = jnp.repeat(d_skip.astype(F32), SSD_HEAD_DIM).reshape(gn, 1, GROUP_W)
    e2 = _expanders()

    cw = conv_w.astype(F32)
    cb = conv_b.astype(F32).reshape(1, -1)
    nb = SSD_INNER
    cwx, cwb, cwc = cw[:, :nb], cw[:, nb:nb + gn * SSD_STATE], cw[:, nb + gn * SSD_STATE:]
    cbx, cbb, cbc = cb[:, :nb], cb[:, nb:nb + gn * SSD_STATE], cb[:, nb + gn * SSD_STATE:]

    fu = max(u for u in (16, 8, 4, 2, 1) if nc % u == 0)
    xo = xbc_col0 // GROUP_W
    bo = (xbc_col0 + SSD_INNER) // SSD_STATE
    co = bo + gn
    st_spec = pl.BlockSpec((1, 1, 2, SSD_STATE, GROUP_W), lambda i, g: (i, g, 0, 0, 0))
    y, sfin = pl.pallas_call(
        functools.partial(_ssd_kernel, l=l, q=q),
        grid=(b, gn),
        in_specs=[pl.BlockSpec((1, l, GROUP_W), lambda i, g: (i, 0, xo + g)),
                  pl.BlockSpec((1, l, SSD_STATE), lambda i, g: (i, 0, bo + g)),
                  pl.BlockSpec((1, l, SSD_STATE), lambda i, g: (i, 0, co + g)),
                  pl.BlockSpec((3, GROUP_W), lambda i, g: (0, g)),
                  pl.BlockSpec((3, SSD_STATE), lambda i, g: (0, g)),
                  pl.BlockSpec((3, SSD_STATE), lambda i, g: (0, g)),
                  pl.BlockSpec((1, GROUP_W), lambda i, g: (0, g)),
                  pl.BlockSpec((1, SSD_STATE), lambda i, g: (0, g)),
                  pl.BlockSpec((1, SSD_STATE), lambda i, g: (0, g)),
                  pl.BlockSpec((1, l, LANES), lambda i, g: (i, 0, 0)),
                  pl.BlockSpec((1, 1, nc, 24, q), lambda i, g: (i, g, 0, 0, 0)),
                  pl.BlockSpec((1, l, 2 * LANES), lambda i, g: (i, 0, 0)),
                  pl.BlockSpec((1, 1, LANES, GROUP_W), lambda i, g: (0, g, 0, 0)),
                  pl.BlockSpec((1, 1, LANES, GROUP_W), lambda i, g: (1, g, 0, 0)),
                  pl.BlockSpec((1, 1, GROUP_W), lambda i, g: (g, 0, 0)),
                  st_spec],
        out_specs=[pl.BlockSpec((1, l, GROUP_W), lambda i, g: (i, 0, g)), st_spec],
        out_shape=[jax.ShapeDtypeStruct((b, l, SSD_INNER), BF16),
                   jax.ShapeDtypeStruct((b, gn, 2, SSD_STATE, GROUP_W), F32)],
        scratch_shapes=[pltpu.VMEM((l, GROUP_W), BF16),
                        pltpu.VMEM((l, SSD_STATE), BF16),
                        pltpu.VMEM((nc, SSD_STATE, q), BF16),
                        pltpu.VMEM((l, GROUP_W), BF16),
                        pltpu.VMEM((nc, SSD_STATE, GROUP_W), BF16),
                        pltpu.VMEM((nc, 1, GROUP_W), F32),
                        pltpu.VMEM((SSD_STATE, GROUP_W), F32),
                        pltpu.VMEM((SSD_STATE, GROUP_W), F32),
                        pltpu.VMEM((fu, q, GROUP_W), F32)],
        compiler_params=_cparams(("parallel", "parallel")),
        name="ssd",
    )(proj, proj, proj, cwx, cwb, cwc, cbx, cbb, cbc, colg, rowg, sp, e2, e2, dexp, s0)
    return y, sfin


def _mixout_kernel(y_ref, z_ref, gb_ref, gc_ref, hv_ref, g0_ref, g1_ref, x_ref, gate_ref,
                   ng_ref, bg_ref, scw_ref, wssd_ref, wsc_ref, wo_ref, o_ref, *, tm, period):
    ns = 2 if (tm // 2) % period == 0 else 1
    ts = tm // ns
    pos = lax.broadcasted_iota(jnp.int32, (ts, 1), 0) % period
    w = scw_ref[...]
    bg = bg_ref[...]
    for s in range(ns):
        r = slice(s * ts, (s + 1) * ts)
        yz = y_ref[0, r, :].astype(F32) * _silu_t(z_ref[0, r, :].astype(F32))
        ms = jnp.mean(yz * yz, axis=-1, keepdims=True)
        yn = (yz * lax.rsqrt(ms + EPS) * ng_ref[...]).astype(BF16)
        y_ssd = _dot(yn, wssd_ref[...])

        u = gc_ref[0, r, :].astype(F32) * hv_ref[0, r, :].astype(F32)
        u_prev = jnp.where(pos == 0, 0.0, pltpu.roll(u, 1, 0))
        u_next = jnp.where(pos == period - 1, 0.0, pltpu.roll(u, ts - 1, 0))
        v = w[0:1] * u_prev + w[1:2] * u + w[2:3] * u_next
        y_sc = _dot((gb_ref[0, r, :].astype(F32) * v).astype(BF16), wsc_ref[...])

        g0 = _sigmoid_t(g0_ref[0, r, :].astype(F32) + bg[:, :D_MODEL])
        g1 = _sigmoid_t(g1_ref[0, r, :].astype(F32) + bg[:, D_MODEL:])
        out = _dot((g0 * y_ssd + g1 * y_sc).astype(BF16), wo_ref[...])
        o_ref[0, r, :] = x_ref[0, r, :] + gate_ref[0] * out


def mixer_out(y, proj, x, gate, norm_g, b_gate, sc_conv_w, w_ssd, w_sc, w_o, tm, period):
    b, l, d = x.shape
    pc = lambda k: pl.BlockSpec((1, tm, d), lambda i, m, k=k: (i, m, k))
    full = lambda shp: pl.BlockSpec(shp, lambda i, m: (0,) * len(shp))
    return pl.pallas_call(
        functools.partial(_mixout_kernel, tm=tm, period=period),
        grid=(b, l // tm),
        in_specs=[pl.BlockSpec((1, tm, SSD_INNER), lambda i, m: (i, m, 0)),
                  pl.BlockSpec((1, tm, SSD_INNER), lambda i, m: (i, m, 0)),
                  pc(5), pc(6), pc(7), pc(8), pc(9),
                  pl.BlockSpec((1, tm, d), lambda i, m: (i, m, 0)),
                  pl.BlockSpec((1, 1, d), lambda i, m: (i, 0, 0)),
                  full((1, SSD_INNER)), full((1, 2 * d)), full((3, SC_WIDTH)),
                  full((SSD_INNER, d)), full((SC_WIDTH, d)), full((d, d))],
        out_specs=pl.BlockSpec((1, tm, d), lambda i, m: (i, m, 0)),
        out_shape=jax.ShapeDtypeStruct((b, l, d), F32),
        compiler_params=_cparams(("parallel", "parallel")),
        name="mixer_out",
    )(y, proj, proj, proj, proj, proj, proj, x, gate,
      norm_g.reshape(1, -1), b_gate.reshape(1, -1), sc_conv_w, w_ssd, w_sc, w_o)


def _ffn_kernel(x_ref, g_ref, sh_ref, sc_ref, gate_ref, w1_ref, w3_ref, w2_ref, o_ref, *, nf):
    fw = w1_ref.shape[1] // nf
    ts = x_ref.shape[1] // ROW_SUBTILES
    for s in range(ROW_SUBTILES):
        r = slice(s * ts, (s + 1) * ts)
        x = x_ref[0, r, :]
        hb = _norm_mod(x, g_ref[...], sh_ref[0], sc_ref[0]).astype(BF16)
        acc = None
        for k in range(nf):
            a = _dot(hb, w1_ref[:, k * fw:(k + 1) * fw])
            bb = _dot(hb, w3_ref[:, k * fw:(k + 1) * fw])
            part = _dot((_silu_t(a) * bb).astype(BF16), w2_ref[k * fw:(k + 1) * fw, :])
            acc = part if acc is None else acc + part
        o_ref[0, r, :] = x + gate_ref[0] * acc


def ffn_dense(x, g, shift, scale, gate, w1, w3, w2, tm):
    b, l, d = x.shape
    f = w1.shape[1]
    vec = pl.BlockSpec((1, 1, d), lambda i, m: (i, 0, 0))
    const = lambda shp: pl.BlockSpec(shp, lambda i, m: (0, 0), pipeline_mode=pl.Buffered(1))
    return pl.pallas_call(
        functools.partial(_ffn_kernel, nf=2),
        grid=(b, l // tm),
        in_specs=[pl.BlockSpec((1, tm, d), lambda i, m: (i, m, 0)),
                  pl.BlockSpec((1, d), lambda i, m: (0, 0)),
                  vec, vec, vec, const((d, f)), const((d, f)), const((f, d))],
        out_specs=pl.BlockSpec((1, tm, d), lambda i, m: (i, m, 0)),
        out_shape=jax.ShapeDtypeStruct((b, l, d), F32),
        compiler_params=_cparams(("parallel", "parallel")),
        name="ffn_dense",
    )(x, g.reshape(1, d), shift, scale, gate, w1, w3, w2)


def _router_kernel(x_ref, g_ref, sh_ref, sc_ref, rw_ref, h_ref, route_ref, route_t_ref, cnt_ref, run_ref, *, tm):
    @pl.when((pl.program_id(0) == 0) & (pl.program_id(1) == 0))
    def _():
        run_ref[...] = jnp.zeros_like(run_ref)

    h = _norm_mod(x_ref[0], g_ref[...], sh_ref[0], sc_ref[0])
    h_hi, h_lo = _split2(h)
    h_ref[0] = h
    w_hi, w_lo = _split2(rw_ref[...])
    logits = _dot(h_hi, w_hi) + _dot(h_lo, w_hi) + _dot(h_hi, w_lo)
    lane = lax.broadcasted_iota(jnp.int32, (tm, LANES), 1)
    ninf = float("-inf")
    lg = jnp.where(lane < N_EXPERTS, logits, ninf)
    m1 = jnp.max(lg, axis=1, keepdims=True)
    i1 = jnp.min(jnp.where(lg == m1, lane, LANES), axis=1, keepdims=True)
    lg2 = jnp.where(lane == i1, ninf, lg)
    m2 = jnp.max(lg2, axis=1, keepdims=True)
    i2 = jnp.min(jnp.where(lg2 == m2, lane, LANES), axis=1, keepdims=True)
    e2 = jnp.exp(m2 - m1)
    den = 1.0 + e2
    sel1 = jnp.where(lane == i1, 1.0, 0.0)
    sel2 = jnp.where(lane == i2, 1.0, 0.0)
    cnt = sel1 + sel2
    r = lax.broadcasted_iota(jnp.int32, (tm, tm), 0)
    c = lax.broadcasted_iota(jnp.int32, (tm, tm), 1)
    tri = jnp.where(c < r, 1.0, 0.0).astype(BF16)
    base = _dot(tri, cnt.astype(BF16)) + run_ref[0:1, :]
    r1 = jnp.sum(sel1 * base, axis=1, keepdims=True)
    r2 = jnp.sum(sel2 * base, axis=1, keepdims=True)
    vals = (i1.astype(F32), i2.astype(F32), 1.0 / den, e2 / den, r1, r2)
    out = jnp.zeros((tm, LANES), F32)
    for k, v in enumerate(vals):
        out = jnp.where(lane == k, v, out)
    route_ref[0] = out
    route_t_ref[...] = out.T
    new_run = run_ref[...] + jnp.sum(cnt, axis=0, keepdims=True)
    run_ref[...] = new_run
    cnt_ref[...] = new_run


def router(x, g, shift, scale, router_w, tm):
    b, l, d = x.shape
    rw = jnp.zeros((d, LANES), F32).at[:, :N_EXPERTS].set(router_w)
    vec = pl.BlockSpec((1, 1, d), lambda i, m: (i, 0, 0))
    mt = l // tm
    return pl.pallas_call(
        functools.partial(_router_kernel, tm=tm),
        grid=(b, l // tm),
        in_specs=[pl.BlockSpec((1, tm, d), lambda i, m: (i, m, 0)),
                  pl.BlockSpec((1, d), lambda i, m: (0, 0)),
                  vec, vec, pl.BlockSpec((d, LANES), lambda i, m: (0, 0))],
        out_specs=[pl.BlockSpec((1, tm, d), lambda i, m: (i, m, 0)),
                   pl.BlockSpec((1, tm, LANES), lambda i, m: (i, m, 0)),
                   pl.BlockSpec((LANES, tm), lambda i, m: (0, i * mt + m)),
                   pl.BlockSpec((8, LANES), lambda i, m: (0, 0))],
        out_shape=[jax.ShapeDtypeStruct((b, l, d), F32),
                   jax.ShapeDtypeStruct((b, l, LANES), F32),
                   jax.ShapeDtypeStruct((LANES, b * l), F32),
                   jax.ShapeDtypeStruct((8, LANES), F32)],
        scratch_shapes=[pltpu.VMEM((8, LANES), F32)],
        compiler_params=_cparams(("arbitrary", "arbitrary")),
        name="router",
    )(x, g.reshape(1, d), shift, scale, rw)


def _pack_bf16_pairs(y):
    k = y.shape[1] // 2
    bits = lax.bitcast_convert_type(y.astype(BF16).astype(F32), jnp.uint32)
    return bits[:, :k] | (bits[:, k:] >> 16)


def _unpack_bf16_pairs(p):
    hi = lax.bitcast_convert_type(p & jnp.uint32(0xFFFF0000), F32)
    lo = lax.bitcast_convert_type(p << 16, F32)
    return jnp.concatenate([hi, lo], axis=1)


def _gffn_kernel(te_ref, nv_ref, x_ref, w1_ref, w3_ref, w2_ref, o_ref, acc_ref, *, nf):
    i = pl.program_id(0)
    f = pl.program_id(1)

    @pl.when(i < nv_ref[0])
    def _():
        ts = x_ref.shape[0] // ROW_SUBTILES
        parts = []
        for s in range(ROW_SUBTILES):
            x = x_ref[s * ts:(s + 1) * ts, :].astype(BF16)
            a = _dot(x, w1_ref[0])
            bb = _dot(x, w3_ref[0])
            parts.append(_dot((_silu_t(a) * bb).astype(BF16), w2_ref[0]))
        part = jnp.concatenate(parts, axis=0)

        @pl.when(f == 0)
        def _():
            acc_ref[...] = part

        @pl.when(f > 0)
        def _():
            acc_ref[...] += part

        @pl.when(f == nf - 1)
        def _():
            o_ref[...] = _pack_bf16_pairs(acc_ref[...])


def grouped_ffn(xs, tile_expert, n_valid, w1, w3, w2, tm, nf):
    rows, d = xs.shape
    nt = rows // tm
    f = w1.shape[2]
    fw = f // nf

    def tile(i, nv):
        return jnp.minimum(i, nv[0] - 1)

    def fchunk(i, k, nv):
        return jnp.where(i < nv[0], k, nf - 1)

    grid_spec = pltpu.PrefetchScalarGridSpec(
        num_scalar_prefetch=2,
        grid=(nt, nf),
        in_specs=[pl.BlockSpec((tm, d), lambda i, k, te, nv: (tile(i, nv), 0)),
                  pl.BlockSpec((1, d, fw), lambda i, k, te, nv: (te[i], 0, fchunk(i, k, nv))),
                  pl.BlockSpec((1, d, fw), lambda i, k, te, nv: (te[i], 0, fchunk(i, k, nv))),
                  pl.BlockSpec((1, fw, d), lambda i, k, te, nv: (te[i], fchunk(i, k, nv), 0))],
        out_specs=pl.BlockSpec((tm, d // 2), lambda i, k, te, nv: (tile(i, nv), 0)),
        scratch_shapes=[pltpu.VMEM((tm, d), F32)],
    )
    return pl.pallas_call(
        functools.partial(_gffn_kernel, nf=nf),
        grid_spec=grid_spec,
        out_shape=jax.ShapeDtypeStruct((rows, d // 2), jnp.uint32),
        compiler_params=_cparams(("arbitrary", "arbitrary")),
        name="grouped_ffn",
    )(tile_expert, n_valid, xs, w1, w3, w2)


def _combine_kernel(x_ref, y0_ref, y1_ref, route_ref, gate_ref, fg_ref, o_ref):
    r = route_ref[0]
    moe = r[:, 2:3] * _unpack_bf16_pairs(y0_ref[0, 0]) + r[:, 3:4] * _unpack_bf16_pairs(y1_ref[0, 0])
    xn = x_ref[0] + gate_ref[0] * moe
    ms = jnp.mean(xn * xn, axis=-1, keepdims=True)
    o_ref[0] = xn * lax.rsqrt(ms + EPS) * fg_ref[...]


def combine_final(x, yg, route, gate, final_g, tm):
    b, l, d = x.shape
    return pl.pallas_call(
        _combine_kernel,
        grid=(b, l // tm),
        in_specs=[pl.BlockSpec((1, tm, d), lambda i, m: (i, m, 0)),
                  pl.BlockSpec((1, 1, tm, d // 2), lambda i, m: (0, i, m, 0)),
                  pl.BlockSpec((1, 1, tm, d // 2), lambda i, m: (1, i, m, 0)),
                  pl.BlockSpec((1, tm, LANES), lambda i, m: (i, m, 0)),
                  pl.BlockSpec((1, 1, d), lambda i, m: (i, 0, 0)),
                  pl.BlockSpec((1, d), lambda i, m: (0, 0))],
        out_specs=pl.BlockSpec((1, tm, d), lambda i, m: (i, m, 0)),
        out_shape=jax.ShapeDtypeStruct((b, l, d), F32),
        compiler_params=_cparams(("parallel", "parallel")),
        name="combine_final",
    )(x, yg, yg, route, gate, final_g.reshape(1, d))


SC_CORES = 2
SC_SUBCORES = 16
SC_WORKERS = SC_CORES * SC_SUBCORES
SC_STREAM_BYTES = 256 * 1024
SC_STREAM_ROWS = 128


def _sc_rows(per_worker, d, dtype):
    return min(SC_STREAM_ROWS, SC_STREAM_BYTES // (d * jnp.dtype(dtype).itemsize), per_worker)


def _sc_mesh():
    return plsc.VectorSubcoreMesh(core_axis_name="c", subcore_axis_name="s",
                                  num_cores=SC_CORES, num_subcores=SC_SUBCORES)


def dispatch_rows(h, pos0, pos1, n_rows):
    t, d = h.shape
    per_w = t // SC_WORKERS
    ch = _sc_rows(per_w, d, h.dtype)
    assert t % SC_WORKERS == 0 and per_w % ch == 0 and ch % 8 == 0, (t, ch)

    @functools.partial(
        pl.kernel, mesh=_sc_mesh(),
        out_type=jax.ShapeDtypeStruct((n_rows, d), h.dtype),
        scratch_types=[pltpu.VMEM((ch,), jnp.int32), pltpu.VMEM((ch,), jnp.int32),
                       pltpu.VMEM((ch, d), h.dtype), pltpu.SemaphoreType.DMA],
        name="moe_dispatch")
    def scatter(h_hbm, p0_hbm, p1_hbm, out_hbm, i0_v, i1_v, rows_v, sem):
        base = (lax.axis_index("s") * SC_CORES + lax.axis_index("c")) * per_w

        @pl.loop(0, per_w // ch)
        def _(j):
            off = base + j * ch
            pltpu.sync_copy(h_hbm.at[pl.ds(off, ch)], rows_v)
            pltpu.sync_copy(p0_hbm.at[pl.ds(off, ch)], i0_v)
            pltpu.sync_copy(p1_hbm.at[pl.ds(off, ch)], i1_v)
            pltpu.async_copy(rows_v, out_hbm.at[i0_v], sem).wait()
            pltpu.async_copy(rows_v, out_hbm.at[i1_v], sem).wait()

    return scatter(h, pos0, pos1)


def return_rows(ys, idx):
    n = idx.shape[0]
    d = ys.shape[1]
    per_w = n // SC_WORKERS
    ch = _sc_rows(per_w, d, ys.dtype)
    assert n % SC_WORKERS == 0 and per_w % ch == 0 and ch % 8 == 0, (n, ch)

    @functools.partial(
        pl.kernel, mesh=_sc_mesh(),
        out_type=jax.ShapeDtypeStruct((n, d), ys.dtype),
        scratch_types=[pltpu.VMEM((ch,), jnp.int32), pltpu.VMEM((ch, d), ys.dtype), pltpu.SemaphoreType.DMA],
        name="moe_return")
    def gather(ys_hbm, idx_hbm, out_hbm, idx_v, rows_v, sem):
        base = (lax.axis_index("s") * SC_CORES + lax.axis_index("c")) * per_w

        @pl.loop(0, per_w // ch)
        def _(j):
            off = base + j * ch
            pltpu.sync_copy(idx_hbm.at[pl.ds(off, ch)], idx_v)
            pltpu.async_copy(ys_hbm.at[idx_v], rows_v, sem).wait()
            pltpu.sync_copy(rows_v, out_hbm.at[pl.ds(off, ch)])

    return gather(ys, idx)


def moe_block(x, g, shift, scale, gate, router_w, w1, w3, w2, final_g):
    b, l, d = x.shape
    t = b * l
    tm = MOE_TM
    h, route, route_t, counts = router(x, g, shift, scale, router_w, _row_tile(l, ROW_TM))
    cnt = counts[0, :N_EXPERTS].astype(jnp.int32)
    gs = ((cnt + tm - 1) // tm) * tm
    ends = jnp.cumsum(gs)
    offs = ends - gs

    def sorted_row(choice):
        e = route_t[choice].astype(jnp.int32)
        start = sum(jnp.where(e == k, offs[k], 0) for k in range(N_EXPERTS))
        return start + route_t[4 + choice].astype(jnp.int32)

    pos0, pos1 = sorted_row(0), sorted_row(1)
    nt = (2 * t) // tm + N_EXPERTS
    n_valid = (ends[-1] // tm).astype(jnp.int32).reshape(1)
    tile = jnp.minimum(jnp.arange(nt, dtype=jnp.int32), n_valid[0] - 1)
    tile_expert = jnp.sum((tile[:, None] >= (ends // tm)[None, :]).astype(jnp.int32), axis=1)
    xs = dispatch_rows(h.reshape(t, d), pos0, pos1, nt * tm)
    ys = grouped_ffn(xs, tile_expert, n_valid, w1, w3, w2, tm, 2)
    yg = return_rows(ys, jnp.concatenate([pos0, pos1])).reshape(2, b, l, d // 2)
    return combine_final(x, yg, route, gate, final_g, _row_tile(l, 2 * ROW_TM))


def _in_weights(w_in):
    o1 = SSD_INNER
    o2 = o1 + XBC_WIDTH
    o3 = o2 + 2 * SSD_HEADS
    w_main = jnp.concatenate([w_in[:, :o2], w_in[:, o3:]], axis=1).astype(BF16)
    w_dt = jnp.pad(w_in[:, o2:o3], ((0, 0), (0, LANES - 2 * SSD_HEADS))).astype(BF16)
    return w_main, w_dt


def kernel(x, c, ctx, c_ctx, w_mod, b_mod, norm1_g, norm2_g, w_in, b_gate, ssd_conv_w, ssd_conv_b, ssd_dt_bias, ssd_a_log, ssd_d, ssd_norm_g, w_ssd_out, sc_conv_w, w_sc_out, w_o, ffn_w1, ffn_w3, ffn_w2, router_w, moe_w1, moe_w3, moe_w2, final_g):
    b, l, d = x.shape
    lc = ctx.shape[1]
    depth = w_mod.shape[0]
    assert depth % 2 == 0, "the final norm is fused into the routed channel mixer of the (odd) last layer"
    cc = jnp.zeros((16, d), F32).at[:b].set(c).at[b].set(c_ctx)
    mod = modulation(cc, w_mod, b_mod)
    zeros_state = jnp.zeros((b, SSD_GROUPS, 2, SSD_STATE, GROUP_W), F32)
    nctx = b * lc
    ctx = ctx.reshape(1, nctx, d)
    tmc = _row_tile(nctx, ROW_TM, lc)
    tmx = _row_tile(l, ROW_TM, GRID_W)

    def per_seq(t):
        return t.reshape(b, lc, t.shape[-1])

    for i in range(depth):
        last = i == depth - 1
        mx = mod[i, :b].reshape(b, N_MOD, 1, d)
        mc = mod[i, b].reshape(1, N_MOD, 1, d)
        w_main, w_dt = _in_weights(w_in[i])
        ssd_p = (ssd_conv_w[i], ssd_conv_b[i], ssd_dt_bias[i], ssd_a_log[i], ssd_d[i])
        out_p = (ssd_norm_g[i], b_gate[i], sc_conv_w[i], w_ssd_out[i].astype(BF16),
                 w_sc_out[i].astype(BF16), w_o[i].astype(BF16))

        if last:
            w_xbc = w_main[:, SSD_INNER:SSD_INNER + XBC_WIDTH]
            proj_c, dt_c = in_proj(ctx, norm1_g[i], mc[:, 0], mc[:, 1], w_xbc, w_dt,
                                   _row_tile(nctx, PROJ_TM), PROJ_TN_XBC)
            _, s_ctx = ssd_mixer(per_seq(proj_c), 0, per_seq(dt_c), *ssd_p, zeros_state, SSD_Q)
        else:
            proj_c, dt_c = in_proj(ctx, norm1_g[i], mc[:, 0], mc[:, 1], w_main, w_dt,
                                   _row_tile(nctx, PROJ_TM), PROJ_TN)
            y_c, s_ctx = ssd_mixer(per_seq(proj_c), SSD_INNER, per_seq(dt_c), *ssd_p, zeros_state, SSD_Q)
            ctx = mixer_out(y_c.reshape(1, nctx, SSD_INNER), proj_c, ctx, mc[:, 2], *out_p, tmc, lc)

        proj_x, dt_x = in_proj(x, norm1_g[i], mx[:, 0], mx[:, 1], w_main, w_dt, _row_tile(l, PROJ_TM), PROJ_TN)
        y_x, _ = ssd_mixer(proj_x, SSD_INNER, dt_x, *ssd_p, s_ctx, SSD_Q)
        x = mixer_out(y_x, proj_x, x, mx[:, 2], *out_p, tmx, GRID_W)

        j = i // 2
        if i % 2 == 0:
            w1, w3, w2 = ffn_w1[j].astype(BF16), ffn_w3[j].astype(BF16), ffn_w2[j].astype(BF16)
            x = ffn_dense(x, norm2_g[i], mx[:, 3], mx[:, 4], mx[:, 5], w1, w3, w2, _row_tile(l, FFN_TM))
            if not last:
                ctx = ffn_dense(ctx, norm2_g[i], mc[:, 3], mc[:, 4], mc[:, 5], w1, w3, w2, tmc)
        else:
            assert last, "the routed channel mixer is fused with the final norm"
            w1, w3, w2 = moe_w1[j].astype(BF16), moe_w3[j].astype(BF16), moe_w2[j].astype(BF16)
            x = moe_block(x, norm2_g[i], mx[:, 3], mx[:, 4], mx[:, 5], router_w[j], w1, w3, w2, final_g)
    return x
```

```python
import functools

import numpy as np
import jax
import jax.numpy as jnp
from jax import lax
from jax.experimental import pallas as pl
from jax.experimental.pallas import tpu as pltpu
from jax.experimental.pallas import tpu_sc as plsc

F32 = jnp.float32
BF16 = jnp.bfloat16

D_MODEL = 1024
GRID_W = 64
SSD_INNER = 2048
SSD_HEADS = 32
SSD_GROUPS = 4
SSD_HPG = 8
SSD_HEAD_DIM = 64
SSD_STATE = 128
GROUP_W = SSD_HPG * SSD_HEAD_DIM
XBC_WIDTH = SSD_INNER + 2 * SSD_GROUPS * SSD_STATE
SC_WIDTH = 1024
N_MOD = 6
N_EXPERTS = 8
EPS = 1e-6
LOG2E = 1.4426950408889634

LANES = 128
SSD_Q = 128
MOE_TM = 512
ROW_SUBTILES = 2
PROJ_TM = 1024
PROJ_TN = 2560
PROJ_TN_XBC = 1024
ROW_TM = 512
FFN_TM = 1024
VMEM_LIMIT = 56 * 1024 * 1024


def _row_tile(rows, want, multiple=1):
    t = multiple * max(1, min(want, rows) // multiple)
    assert rows % t == 0, (rows, t)
    return t


def _dot(a, b):
    return jnp.dot(a, b, preferred_element_type=F32)


def _sigmoid(v):
    return 1.0 / (1.0 + jnp.exp(-v))


def _silu(v):
    return v * _sigmoid(v)


def _sigmoid_t(v):
    return 0.5 + 0.5 * jnp.tanh(0.5 * v)


def _silu_t(v):
    hv = 0.5 * v
    return hv + hv * jnp.tanh(hv)


def _split2(a):
    hi = a.astype(BF16)
    lo = (a - hi.astype(F32)).astype(BF16)
    return hi, lo


def _split3(a):
    hi = a.astype(BF16)
    r = a - hi.astype(F32)
    mid = r.astype(BF16)
    lo = (r - mid.astype(F32)).astype(BF16)
    return hi, mid, lo


def _norm_mod(x, g, shift, scale):
    ms = jnp.mean(x * x, axis=-1, keepdims=True)
    return (x * lax.rsqrt(ms + EPS) * g) * (1.0 + scale) + shift


def _cparams(sem, vmem=VMEM_LIMIT):
    return pltpu.CompilerParams(dimension_semantics=sem, vmem_limit_bytes=vmem)


def _mod_kernel(c_ref, w_ref, b_ref, o_ref):
    a_hi, a_lo = _split2(_silu(c_ref[...]))
    w_hi, w_lo = _split2(w_ref[0])
    o_ref[0] = _dot(a_hi, w_hi) + _dot(a_lo, w_hi) + _dot(a_hi, w_lo) + b_ref[0]


def modulation(cc, w_mod, b_mod):
    depth, d, n = w_mod.shape
    tn = 1536
    return pl.pallas_call(
        _mod_kernel,
        grid=(depth, n // tn),
        in_specs=[pl.BlockSpec((16, d), lambda i, j: (0, 0)),
                  pl.BlockSpec((1, d, tn), lambda i, j: (i, 0, j)),
                  pl.BlockSpec((1, 1, tn), lambda i, j: (i, 0, j))],
        out_specs=pl.BlockSpec((1, 16, tn), lambda i, j: (i, 0, j)),
        out_shape=jax.ShapeDtypeStruct((depth, 16, n), F32),
        compiler_params=_cparams(("parallel", "parallel")),
        name="modulation",
    )(cc, w_mod, b_mod.reshape(depth, 1, n))


def _inproj_kernel(x_ref, g_ref, sh_ref, sc_ref, w_ref, wdt_ref, o_ref, dt_ref, h_ref):
    @pl.when(pl.program_id(2) == 0)
    def _():
        hb = _norm_mod(x_ref[0], g_ref[...], sh_ref[0], sc_ref[0]).astype(BF16)
        h_ref[...] = hb
        dt_ref[0] = _dot(hb, wdt_ref[...])

    o_ref[0] = _dot(h_ref[...], w_ref[...]).astype(o_ref.dtype)


def in_proj(x, g, shift, scale, w, wdt, tm, tn):
    b, l, d = x.shape
    n = w.shape[1]
    return pl.pallas_call(
        _inproj_kernel,
        grid=(b, l // tm, n // tn),
        in_specs=[pl.BlockSpec((1, tm, d), lambda i, m, j: (i, m, 0)),
                  pl.BlockSpec((1, d), lambda i, m, j: (0, 0)),
                  pl.BlockSpec((1, 1, d), lambda i, m, j: (i, 0, 0)),
                  pl.BlockSpec((1, 1, d), lambda i, m, j: (i, 0, 0)),
                  pl.BlockSpec((d, tn), lambda i, m, j: (0, j)),
                  pl.BlockSpec((d, LANES), lambda i, m, j: (0, 0))],
        out_specs=[pl.BlockSpec((1, tm, tn), lambda i, m, j: (i, m, j)),
                   pl.BlockSpec((1, tm, LANES), lambda i, m, j: (i, m, 0))],
        out_shape=[jax.ShapeDtypeStruct((b, l, n), BF16),
                   jax.ShapeDtypeStruct((b, l, LANES), F32)],
        scratch_shapes=[pltpu.VMEM((tm, d), BF16)],
        compiler_params=_cparams(("parallel", "parallel", "arbitrary")),
        name="in_proj",
    )(x, g.reshape(1, d), shift, scale, w, wdt)


def _dt_kernel(raw_ref, bias_ref, a_ref, pc_ref, ph_ref, plo_ref, col_ref, row_ref, sp_ref, *, q, ch):
    row = lax.broadcasted_iota(jnp.int32, (q, q), 0)
    col = lax.broadcasted_iota(jnp.int32, (q, q), 1)
    tri_l = jnp.where(col <= row, 1.0, 0.0).astype(BF16)
    tri_u = jnp.where(col >= row, 1.0, 0.0).astype(BF16)
    lane = lax.broadcasted_iota(jnp.int32, (q, LANES), 1)
    fwd = lane < SSD_HEADS
    ph, plo = ph_ref[...], plo_ref[...]

    def place_split(t):
        t_hi, t_lo = _split2(t)
        return (_dot(t_hi, ph) + _dot(t_lo, plo)).astype(BF16)

    for k in range(ch):
        v = raw_ref[0, k * q:(k + 1) * q, :] + bias_ref[...]
        dt = jnp.maximum(v, 0.0) + jnp.log1p(jnp.exp(-jnp.abs(v)))
        d1, d2, d3 = _split3(dt * a_ref[...])
        cs_f = _dot(tri_l, d1) + _dot(tri_l, d2) + _dot(tri_l, d3)
        cs_b = _dot(tri_u, d1) + _dot(tri_u, d2) + _dot(tri_u, d3)
        cs = jnp.where(fwd, cs_f, cs_b)
        tot = jnp.where(fwd[0:1], cs_f[q - 1:q, :], cs_b[0:1, :])
        sl = slice(k * q, (k + 1) * q)
        cs2 = cs * LOG2E
        c1, c2, c3 = _split3(cs2)
        pc = pc_ref[...]
        col_ref[0, sl, :] = _dot(c1, pc) + _dot(c2, pc) + _dot(c3, pc)
        r_t = (cs2 - jnp.log(dt) * LOG2E).T
        dt_t = dt.T
        for g in range(SSD_GROUPS):
            lo = g * SSD_HPG
            dt_f = dt_t[lo:lo + 8, :]
            dt_b = dt_t[SSD_HEADS + lo:SSD_HEADS + lo + 8, :]
            row_ref[0, g, k, 0:8, :] = r_t[lo:lo + 8, :]
            row_ref[0, g, k, 8:16, :] = r_t[SSD_HEADS + lo:SSD_HEADS + lo + 8, :]
            row_ref[0, g, k, 16:24, :] = jnp.log(dt_f + dt_b) * LOG2E
        sp_ref[0, sl, 0:LANES] = place_split(dt * jnp.exp(tot - cs))
        sp_ref[0, sl, LANES:2 * LANES] = place_split(jnp.exp(cs))


def _placements():
    pc = np.zeros((LANES, LANES), np.float32)
    ph = np.zeros((LANES, LANES), np.float32)
    plo = np.zeros((LANES, LANES), np.float32)
    for d in range(2):
        for g in range(SSD_GROUPS):
            for j in range(SSD_HPG):
                src = d * SSD_HEADS + g * SSD_HPG + j
                pc[src, g * 16 + d * 8 + j] = 1.0
                ph[src, (d * SSD_GROUPS + g) * 16 + j] = 1.0
                plo[src, (d * SSD_GROUPS + g) * 16 + 8 + j] = 1.0
    return jnp.asarray(pc, BF16), jnp.asarray(ph, BF16), jnp.asarray(plo, BF16)


def dt_prep(raw, bias, a, q):
    b, l, _ = raw.shape
    nc = l // q
    ch = min(8, nc)
    gn = SSD_GROUPS
    vspec = pl.BlockSpec((1, LANES), lambda i, c: (0, 0))
    pspec = pl.BlockSpec((LANES, LANES), lambda i, c: (0, 0))
    return pl.pallas_call(
        functools.partial(_dt_kernel, q=q, ch=ch),
        grid=(b, nc // ch),
        in_specs=[pl.BlockSpec((1, ch * q, LANES), lambda i, c: (i, c, 0)), vspec, vspec, pspec, pspec, pspec],
        out_specs=[pl.BlockSpec((1, ch * q, LANES), lambda i, c: (i, c, 0)),
                   pl.BlockSpec((1, gn, ch, 24, q), lambda i, c: (i, 0, c, 0, 0)),
                   pl.BlockSpec((1, ch * q, 2 * LANES), lambda i, c: (i, c, 0))],
        out_shape=[jax.ShapeDtypeStruct((b, l, LANES), F32),
                   jax.ShapeDtypeStruct((b, gn, nc, 24, q), F32),
                   jax.ShapeDtypeStruct((b, l, 2 * LANES), BF16)],
        compiler_params=_cparams(("parallel", "parallel")),
        name="dt_prep",
    )(raw, bias, a, *_placements())


def _ssd_kernel(xp_ref, bp_ref, cp_ref, cwx_ref, cwb_ref, cwc_ref, cbx_ref, cbb_ref, cbc_ref,
                col_ref, row_ref, sp_ref, e2f_ref, e2b_ref, dexp_ref, s0_ref,
                y_ref, sfin_ref,
                xs_ref, cc_ref, bt_ref, xwf_ref, sbe_ref, edge_ref, sf_ref, sb_ref, yo_ref, *, l, q):
    nc = l // q
    sr = lax.broadcasted_iota(jnp.int32, (q, q + 32), 0)
    sc = lax.broadcasted_iota(jnp.int32, (q, q + 32), 1)
    shift_prev = jnp.where((sc == sr - 1) | ((sr == 0) & (sc == q + 15)), 1.0, 0.0).astype(BF16)
    shift_next = jnp.where(((sc == sr + 1) & (sc < q)) | ((sr == q - 1) & (sc == q + 16)), 1.0, 0.0).astype(BF16)

    def conv_silu(srcs, w, b, k):
        r0 = pl.multiple_of(k * q, q)
        p0 = pl.multiple_of(jnp.maximum(r0 - 16, 0), 16)
        n0 = pl.multiple_of(jnp.minimum(r0 + q, l - 16), 16)

        def rows_at(start, n):
            parts = [r[0, pl.ds(start, n), ls] for r, ls in srcs]
            return parts[0] if len(parts) == 1 else jnp.concatenate(parts, axis=1)

        blk = rows_at(r0, q)
        before = rows_at(p0, 16)
        after = rows_at(n0, 16)
        before = jnp.where(k > 0, before, jnp.zeros_like(before))
        after = jnp.where(k < nc - 1, after, jnp.zeros_like(after))
        stacked = jnp.concatenate([blk, before, after], axis=0)
        v = (w[0:1] * _dot(shift_prev, stacked) + w[1:2] * blk.astype(F32)
             + w[2:3] * _dot(shift_next, stacked) + b)
        return _silu_t(v)

    all_lanes = slice(None)

    def prep(k):
        r0 = pl.multiple_of(k * q, q)
        x = conv_silu(((xp_ref, all_lanes),), cwx_ref[...], cbx_ref[...], k)
        xs_ref[pl.ds(r0, q), :] = x.astype(BF16)
        bc = conv_silu(((bp_ref, all_lanes), (cp_ref, all_lanes)),
                       jnp.concatenate([cwb_ref[...], cwc_ref[...]], axis=1),
                       jnp.concatenate([cbb_ref[...], cbc_ref[...]], axis=1), k)
        b_t = bc[:, :SSD_STATE].T.astype(BF16)
        bt_ref[k] = b_t
        cc_ref[pl.ds(r0, q), :] = bc[:, SSD_STATE:].astype(BF16)
        sp_w = sp_ref[0, pl.ds(r0, q), 0:LANES]
        xwf_ref[pl.ds(r0, q), :] = (x * _dot(sp_w, e2f_ref[0, 0])).astype(BF16)
        xw_b = (x * _dot(sp_w, e2b_ref[0, 0])).astype(BF16)
        e_last = sp_ref[0, pl.ds(pl.multiple_of(r0 + q - 16, 16), 16), LANES:2 * LANES]
        e_first = sp_ref[0, pl.ds(r0, 16), LANES:2 * LANES]
        edge_ref[k] = _dot(e_last, e2f_ref[0, 0])[15:16, :]
        edge_b = _dot(e_first, e2b_ref[0, 0])[0:1, :]
        return b_t, xw_b, edge_b

    sb_ref[...] = s0_ref[0, 0, 1]

    def bstep(i, carry):
        c = nc - 1 - i
        b_t, xw_b, edge_b = prep(c)
        sbe_ref[c] = sb_ref[...].astype(BF16)
        sb_ref[...] = sb_ref[...] * edge_b + _dot(b_t, xw_b)
        return carry

    lax.fori_loop(0, nc, bstep, 0, unroll=16)
    sfin_ref[0, 0, 1] = sb_ref[...]

    sf_ref[...] = s0_ref[0, 0, 0]
    li = lax.broadcasted_iota(jnp.int32, (q, q), 0)
    si = lax.broadcasted_iota(jnp.int32, (q, q), 1)
    lower = si <= li
    eye = si == li
    left = lax.broadcasted_iota(jnp.int32, (q, LANES), 1) < SSD_HEAD_DIM
    fu = yo_ref.shape[0]
    lane0 = 16 * pl.program_id(1)

    def chunk(c, slot):
        r0 = pl.multiple_of(c * q, q)
        cc = cc_ref[pl.ds(r0, q), :]
        sp_e = sp_ref[0, pl.ds(r0, q), LANES:2 * LANES]
        yo_ref[slot] = (_dot(sp_e, e2f_ref[0, 0]) * _dot(cc, sf_ref[...].astype(BF16))
                        + _dot(sp_e, e2b_ref[0, 0]) * _dot(cc, sbe_ref[c]))
        sf_ref[...] = sf_ref[...] * edge_ref[c] + _dot(bt_ref[c], xwf_ref[pl.ds(r0, q), :])
        g = _dot(cc, bt_ref[c])
        col = pltpu.roll(col_ref[0, pl.ds(r0, q), :], LANES - lane0, 1)
        row = row_ref[0, 0, c]
        for j in range(SSD_HPG // 2):
            lanes = slice(j * LANES, (j + 1) * LANES)
            xf = xs_ref[pl.ds(r0, q), lanes].astype(F32)
            x_diag = jnp.concatenate([jnp.where(left, xf, 0.0), jnp.where(left, 0.0, xf)], axis=0).astype(BF16)
            ms = []
            for h in (2 * j, 2 * j + 1):
                cs_l = jnp.take_along_axis(col, jnp.where(lower, h, 8 + h), axis=1, mode="promise_in_bounds")
                arg = cs_l - jnp.where(lower, row[h:h + 1, :], row[8 + h:9 + h, :])
                p = jnp.exp2(jnp.where(eye, row[16 + h:17 + h, :], arg))
                ms.append((g * p).astype(BF16))
            y = _dot(jnp.concatenate(ms, axis=1), x_diag) + yo_ref[slot, :, lanes] + dexp_ref[0, :, lanes] * xf
            y_ref[0, pl.ds(r0, q), lanes] = y.astype(y_ref.dtype)

    def fstep(i, carry):
        for slot in range(fu):
            chunk(fu * i + slot, slot)
        return carry

    lax.fori_loop(0, nc // fu, fstep, 0)
    sfin_ref[0, 0, 0] = sf_ref[...]


def _expanders():
    e = np.zeros((2, SSD_GROUPS, LANES, GROUP_W), np.float32)
    for d in range(2):
        for g in range(SSD_GROUPS):
            for s in range(2):
                for j in range(SSD_HPG):
                    e[d, g, (d * SSD_GROUPS + g) * 16 + s * 8 + j, j * SSD_HEAD_DIM:(j + 1) * SSD_HEAD_DIM] = 1.0
    return jnp.asarray(e, BF16)


def ssd_mixer(proj, xbc_col0, dt_raw, conv_w, conv_b, dt_bias, a_log, d_skip, s0, q):
    b, l, _ = proj.shape
    assert l % q == 0 and q % 16 == 0, (l, q)
    nc = l // q
    gn = SSD_GROUPS
    bias = jnp.zeros((1, LANES), F32).at[0, :2 * SSD_HEADS].set(dt_bias.reshape(-1))
    a = jnp.zeros((1, LANES), F32).at[0, :2 * SSD_HEADS].set(-jnp.exp(a_log.reshape(-1)))
    colg, rowg, sp = dt_prep(dt_raw, bias, a, q)
    dexp = jnp.repeat(d_skip.astype(F32), SSD_HEAD_DIM).reshape(gn, 1, GROUP_W)
    e2 = _expanders()

    cw = conv_w.astype(F32)
    cb = conv_b.astype(F32).reshape(1, -1)
    nb = SSD_INNER
    cwx, cwb, cwc = cw[:, :nb], cw[:, nb:nb + gn * SSD_STATE], cw[:, nb + gn * SSD_STATE:]
    cbx, cbb, cbc = cb[:, :nb], cb[:, nb:nb + gn * SSD_STATE], cb[:, nb + gn * SSD_STATE:]

    fu = max(u for u in (16, 8, 4, 2, 1) if nc % u == 0)
    xo = xbc_col0 // GROUP_W
    bo = (xbc_col0 + SSD_INNER) // SSD_STATE
    co = bo + gn
    st_spec = pl.BlockSpec((1, 1, 2, SSD_STATE, GROUP_W), lambda i, g: (i, g, 0, 0, 0))
    y, sfin = pl.pallas_call(
        functools.partial(_ssd_kernel, l=l, q=q),
        grid=(b, gn),
        in_specs=[pl.BlockSpec((1, l, GROUP_W), lambda i, g: (i, 0, xo + g)),
                  pl.BlockSpec((1, l, SSD_STATE), lambda i, g: (i, 0, bo + g)),
                  pl.BlockSpec((1, l, SSD_STATE), lambda i, g: (i, 0, co + g)),
                  pl.BlockSpec((3, GROUP_W), lambda i, g: (0, g)),
                  pl.BlockSpec((3, SSD_STATE), lambda i, g: (0, g)),
                  pl.BlockSpec((3, SSD_STATE), lambda i, g: (0, g)),
                  pl.BlockSpec((1, GROUP_W), lambda i, g: (0, g)),
                  pl.BlockSpec((1, SSD_STATE), lambda i, g: (0, g)),
                  pl.BlockSpec((1, SSD_STATE), lambda i, g: (0, g)),
                  pl.BlockSpec((1, l, LANES), lambda i, g: (i, 0, 0)),
                  pl.BlockSpec((1, 1, nc, 24, q), lambda i, g: (i, g, 0, 0, 0)),
                  pl.BlockSpec((1, l, 2 * LANES), lambda i, g: (i, 0, 0)),
                  pl.BlockSpec((1, 1, LANES, GROUP_W), lambda i, g: (0, g, 0, 0)),
                  pl.BlockSpec((1, 1, LANES, GROUP_W), lambda i, g: (1, g, 0, 0)),
                  pl.BlockSpec((1, 1, GROUP_W), lambda i, g: (g, 0, 0)),
                  st_spec],
        out_specs=[pl.BlockSpec((1, l, GROUP_W), lambda i, g: (i, 0, g)), st_spec],
        out_shape=[jax.ShapeDtypeStruct((b, l, SSD_INNER), BF16),
                   jax.ShapeDtypeStruct((b, gn, 2, SSD_STATE, GROUP_W), F32)],
        scratch_shapes=[pltpu.VMEM((l, GROUP_W), BF16),
                        pltpu.VMEM((l, SSD_STATE), BF16),
                        pltpu.VMEM((nc, SSD_STATE, q), BF16),
                        pltpu.VMEM((l, GROUP_W), BF16),
                        pltpu.VMEM((nc, SSD_STATE, GROUP_W), BF16),
                        pltpu.VMEM((nc, 1, GROUP_W), F32),
                        pltpu.VMEM((SSD_STATE, GROUP_W), F32),
                        pltpu.VMEM((SSD_STATE, GROUP_W), F32),
                        pltpu.VMEM((fu, q, GROUP_W), F32)],
        compiler_params=_cparams(("parallel", "parallel")),
        name="ssd",
    )(proj, proj, proj, cwx, cwb, cwc, cbx, cbb, cbc, colg, rowg, sp, e2, e2, dexp, s0)
    return y, sfin


def _mixout_kernel(y_ref, z_ref, gb_ref, gc_ref, hv_ref, g0_ref, g1_ref, x_ref, gate_ref,
                   ng_ref, bg_ref, scw_ref, wssd_ref, wsc_ref, wo_ref, o_ref, *, tm, period):
    ns = 2 if (tm // 2) % period == 0 else 1
    ts = tm // ns
    pos = lax.broadcasted_iota(jnp.int32, (ts, 1), 0) % period
    w = scw_ref[...]
    bg = bg_ref[...]
    for s in range(ns):
        r = slice(s * ts, (s + 1) * ts)
        yz = y_ref[0, r, :].astype(F32) * _silu_t(z_ref[0, r, :].astype(F32))
        ms = jnp.mean(yz * yz, axis=-1, keepdims=True)
        yn = (yz * lax.rsqrt(ms + EPS) * ng_ref[...]).astype(BF16)
        y_ssd = _dot(yn, wssd_ref[...])

        u = gc_ref[0, r, :].astype(F32) * hv_ref[0, r, :].astype(F32)
        u_prev = jnp.where(pos == 0, 0.0, pltpu.roll(u, 1, 0))
        u_next = jnp.where(pos == period - 1, 0.0, pltpu.roll(u, ts - 1, 0))
        v = w[0:1] * u_prev + w[1:2] * u + w[2:3] * u_next
        y_sc = _dot((gb_ref[0, r, :].astype(F32) * v).astype(BF16), wsc_ref[...])

        g0 = _sigmoid_t(g0_ref[0, r, :].astype(F32) + bg[:, :D_MODEL])
        g1 = _sigmoid_t(g1_ref[0, r, :].astype(F32) + bg[:, D_MODEL:])
        out = _dot((g0 * y_ssd + g1 * y_sc).astype(BF16), wo_ref[...])
        o_ref[0, r, :] = x_ref[0, r, :] + gate_ref[0] * out


def mixer_out(y, proj, x, gate, norm_g, b_gate, sc_conv_w, w_ssd, w_sc, w_o, tm, period):
    b, l, d = x.shape
    pc = lambda k: pl.BlockSpec((1, tm, d), lambda i, m, k=k: (i, m, k))
    full = lambda shp: pl.BlockSpec(shp, lambda i, m: (0,) * len(shp))
    return pl.pallas_call(
        functools.partial(_mixout_kernel, tm=tm, period=period),
        grid=(b, l // tm),
        in_specs=[pl.BlockSpec((1, tm, SSD_INNER), lambda i, m: (i, m, 0)),
                  pl.BlockSpec((1, tm, SSD_INNER), lambda i, m: (i, m, 0)),
                  pc(5), pc(6), pc(7), pc(8), pc(9),
                  pl.BlockSpec((1, tm, d), lambda i, m: (i, m, 0)),
                  pl.BlockSpec((1, 1, d), lambda i, m: (i, 0, 0)),
                  full((1, SSD_INNER)), full((1, 2 * d)), full((3, SC_WIDTH)),
                  full((SSD_INNER, d)), full((SC_WIDTH, d)), full((d, d))],
        out_specs=pl.BlockSpec((1, tm, d), lambda i, m: (i, m, 0)),
        out_shape=jax.ShapeDtypeStruct((b, l, d), F32),
        compiler_params=_cparams(("parallel", "parallel")),
        name="mixer_out",
    )(y, proj, proj, proj, proj, proj, proj, x, gate,
      norm_g.reshape(1, -1), b_gate.reshape(1, -1), sc_conv_w, w_ssd, w_sc, w_o)


def _ffn_kernel(x_ref, g_ref, sh_ref, sc_ref, gate_ref, w1_ref, w3_ref, w2_ref, o_ref, *, nf):
    fw = w1_ref.shape[1] // nf
    ts = x_ref.shape[1] // ROW_SUBTILES
    for s in range(ROW_SUBTILES):
        r = slice(s * ts, (s + 1) * ts)
        x = x_ref[0, r, :]
        hb = _norm_mod(x, g_ref[...], sh_ref[0], sc_ref[0]).astype(BF16)
        acc = None
        for k in range(nf):
            a = _dot(hb, w1_ref[:, k * fw:(k + 1) * fw])
            bb = _dot(hb, w3_ref[:, k * fw:(k + 1) * fw])
            part = _dot((_silu_t(a) * bb).astype(BF16), w2_ref[k * fw:(k + 1) * fw, :])
            acc = part if acc is None else acc + part
        o_ref[0, r, :] = x + gate_ref[0] * acc


def ffn_dense(x, g, shift, scale, gate, w1, w3, w2, tm):
    b, l, d = x.shape
    f = w1.shape[1]
    vec = pl.BlockSpec((1, 1, d), lambda i, m: (i, 0, 0))
    const = lambda shp: pl.BlockSpec(shp, lambda i, m: (0, 0), pipeline_mode=pl.Buffered(1))
    return pl.pallas_call(
        functools.partial(_ffn_kernel, nf=2),
        grid=(b, l // tm),
        in_specs=[pl.BlockSpec((1, tm, d), lambda i, m: (i, m, 0)),
                  pl.BlockSpec((1, d), lambda i, m: (0, 0)),
                  vec, vec, vec, const((d, f)), const((d, f)), const((f, d))],
        out_specs=pl.BlockSpec((1, tm, d), lambda i, m: (i, m, 0)),
        out_shape=jax.ShapeDtypeStruct((b, l, d), F32),
        compiler_params=_cparams(("parallel", "parallel")),
        name="ffn_dense",
    )(x, g.reshape(1, d), shift, scale, gate, w1, w3, w2)


def _router_kernel(x_ref, g_ref, sh_ref, sc_ref, rw_ref, h_ref, route_ref, route_t_ref, cnt_ref, run_ref, *, tm):
    @pl.when((pl.program_id(0) == 0) & (pl.program_id(1) == 0))
    def _():
        run_ref[...] = jnp.zeros_like(run_ref)

    h = _norm_mod(x_ref[0], g_ref[...], sh_ref[0], sc_ref[0])
    h_hi, h_lo = _split2(h)
    h_ref[0] = h
    w_hi, w_lo = _split2(rw_ref[...])
    logits = _dot(h_hi, w_hi) + _dot(h_lo, w_hi) + _dot(h_hi, w_lo)
    lane = lax.broadcasted_iota(jnp.int32, (tm, LANES), 1)
    ninf = float("-inf")
    lg = jnp.where(lane < N_EXPERTS, logits, ninf)
    m1 = jnp.max(lg, axis=1, keepdims=True)
    i1 = jnp.min(jnp.where(lg == m1, lane, LANES), axis=1, keepdims=True)
    lg2 = jnp.where(lane == i1, ninf, lg)
    m2 = jnp.max(lg2, axis=1, keepdims=True)
    i2 = jnp.min(jnp.where(lg2 == m2, lane, LANES), axis=1, keepdims=True)
    e2 = jnp.exp(m2 - m1)
    den = 1.0 + e2
    sel1 = jnp.where(lane == i1, 1.0, 0.0)
    sel2 = jnp.where(lane == i2, 1.0, 0.0)
    cnt = sel1 + sel2
    r = lax.broadcasted_iota(jnp.int32, (tm, tm), 0)
    c = lax.broadcasted_iota(jnp.int32, (tm, tm), 1)
    tri = jnp.where(c < r, 1.0, 0.0).astype(BF16)
    base = _dot(tri, cnt.astype(BF16)) + run_ref[0:1, :]
    r1 = jnp.sum(sel1 * base, axis=1, keepdims=True)
    r2 = jnp.sum(sel2 * base, axis=1, keepdims=True)
    vals = (i1.astype(F32), i2.astype(F32), 1.0 / den, e2 / den, r1, r2)
    out = jnp.zeros((tm, LANES), F32)
    for k, v in enumerate(vals):
        out = jnp.where(lane == k, v, out)
    route_ref[0] = out
    route_t_ref[...] = out.T
    new_run = run_ref[...] + jnp.sum(cnt, axis=0, keepdims=True)
    run_ref[...] = new_run
    cnt_ref[...] = new_run


def router(x, g, shift, scale, router_w, tm):
    b, l, d = x.shape
    rw = jnp.zeros((d, LANES), F32).at[:, :N_EXPERTS].set(router_w)
    vec = pl.BlockSpec((1, 1, d), lambda i, m: (i, 0, 0))
    mt = l // tm
    return pl.pallas_call(
        functools.partial(_router_kernel, tm=tm),
        grid=(b, l // tm),
        in_specs=[pl.BlockSpec((1, tm, d), lambda i, m: (i, m, 0)),
                  pl.BlockSpec((1, d), lambda i, m: (0, 0)),
                  vec, vec, pl.BlockSpec((d, LANES), lambda i, m: (0, 0))],
        out_specs=[pl.BlockSpec((1, tm, d), lambda i, m: (i, m, 0)),
                   pl.BlockSpec((1, tm, LANES), lambda i, m: (i, m, 0)),
                   pl.BlockSpec((LANES, tm), lambda i, m: (0, i * mt + m)),
                   pl.BlockSpec((8, LANES), lambda i, m: (0, 0))],
        out_shape=[jax.ShapeDtypeStruct((b, l, d), F32),
                   jax.ShapeDtypeStruct((b, l, LANES), F32),
                   jax.ShapeDtypeStruct((LANES, b * l), F32),
                   jax.ShapeDtypeStruct((8, LANES), F32)],
        scratch_shapes=[pltpu.VMEM((8, LANES), F32)],
        compiler_params=_cparams(("arbitrary", "arbitrary")),
        name="router",
    )(x, g.reshape(1, d), shift, scale, rw)


def _pack_bf16_pairs(y):
    k = y.shape[1] // 2
    bits = lax.bitcast_convert_type(y.astype(BF16).astype(F32), jnp.uint32)
    return bits[:, :k] | (bits[:, k:] >> 16)


def _unpack_bf16_pairs(p):
    hi = lax.bitcast_convert_type(p & jnp.uint32(0xFFFF0000), F32)
    lo = lax.bitcast_convert_type(p << 16, F32)
    return jnp.concatenate([hi, lo], axis=1)


def _gffn_kernel(te_ref, nv_ref, x_ref, w1_ref, w3_ref, w2_ref, o_ref, acc_ref, *, nf):
    i = pl.program_id(0)
    f = pl.program_id(1)

    @pl.when(i < nv_ref[0])
    def _():
        ts = x_ref.shape[0] // ROW_SUBTILES
        parts = []
        for s in range(ROW_SUBTILES):
            x = x_ref[s * ts:(s + 1) * ts, :].astype(BF16)
            a = _dot(x, w1_ref[0])
            bb = _dot(x, w3_ref[0])
            parts.append(_dot((_silu_t(a) * bb).astype(BF16), w2_ref[0]))
        part = jnp.concatenate(parts, axis=0)

        @pl.when(f == 0)
        def _():
            acc_ref[...] = part

        @pl.when(f > 0)
        def _():
            acc_ref[...] += part

        @pl.when(f == nf - 1)
        def _():
            o_ref[...] = _pack_bf16_pairs(acc_ref[...])


def grouped_ffn(xs, tile_expert, n_valid, w1, w3, w2, tm, nf):
    rows, d = xs.shape
    nt = rows // tm
    f = w1.shape[2]
    fw = f // nf

    def tile(i, nv):
        return jnp.minimum(i, nv[0] - 1)

    def fchunk(i, k, nv):
        return jnp.where(i < nv[0], k, nf - 1)

    grid_spec = pltpu.PrefetchScalarGridSpec(
        num_scalar_prefetch=2,
        grid=(nt, nf),
        in_specs=[pl.BlockSpec((tm, d), lambda i, k, te, nv: (tile(i, nv), 0)),
                  pl.BlockSpec((1, d, fw), lambda i, k, te, nv: (te[i], 0, fchunk(i, k, nv))),
                  pl.BlockSpec((1, d, fw), lambda i, k, te, nv: (te[i], 0, fchunk(i, k, nv))),
                  pl.BlockSpec((1, fw, d), lambda i, k, te, nv: (te[i], fchunk(i, k, nv), 0))],
        out_specs=pl.BlockSpec((tm, d // 2), lambda i, k, te, nv: (tile(i, nv), 0)),
        scratch_shapes=[pltpu.VMEM((tm, d), F32)],
    )
    return pl.pallas_call(
        functools.partial(_gffn_kernel, nf=nf),
        grid_spec=grid_spec,
        out_shape=jax.ShapeDtypeStruct((rows, d // 2), jnp.uint32),
        compiler_params=_cparams(("arbitrary", "arbitrary")),
        name="grouped_ffn",
    )(tile_expert, n_valid, xs, w1, w3, w2)


def _combine_kernel(x_ref, y0_ref, y1_ref, route_ref, gate_ref, fg_ref, o_ref):
    r = route_ref[0]
    moe = r[:, 2:3] * _unpack_bf16_pairs(y0_ref[0, 0]) + r[:, 3:4] * _unpack_bf16_pairs(y1_ref[0, 0])
    xn = x_ref[0] + gate_ref[0] * moe
    ms = jnp.mean(xn * xn, axis=-1, keepdims=True)
    o_ref[0] = xn * lax.rsqrt(ms + EPS) * fg_ref[...]


def combine_final(x, yg, route, gate, final_g, tm):
    b, l, d = x.shape
    return pl.pallas_call(
        _combine_kernel,
        grid=(b, l // tm),
        in_specs=[pl.BlockSpec((1, tm, d), lambda i, m: (i, m, 0)),
                  pl.BlockSpec((1, 1, tm, d // 2), lambda i, m: (0, i, m, 0)),
                  pl.BlockSpec((1, 1, tm, d // 2), lambda i, m: (1, i, m, 0)),
                  pl.BlockSpec((1, tm, LANES), lambda i, m: (i, m, 0)),
                  pl.BlockSpec((1, 1, d), lambda i, m: (i, 0, 0)),
                  pl.BlockSpec((1, d), lambda i, m: (0, 0))],
        out_specs=pl.BlockSpec((1, tm, d), lambda i, m: (i, m, 0)),
        out_shape=jax.ShapeDtypeStruct((b, l, d), F32),
        compiler_params=_cparams(("parallel", "parallel")),
        name="combine_final",
    )(x, yg, yg, route, gate, final_g.reshape(1, d))


SC_CORES = 2
SC_SUBCORES = 16
SC_WORKERS = SC_CORES * SC_SUBCORES
SC_STREAM_BYTES = 256 * 1024
SC_STREAM_ROWS = 128


def _sc_rows(per_worker, d, dtype):
    return min(SC_STREAM_ROWS, SC_STREAM_BYTES // (d * jnp.dtype(dtype).itemsize), per_worker)


def _sc_mesh():
    return plsc.VectorSubcoreMesh(core_axis_name="c", subcore_axis_name="s",
                                  num_cores=SC_CORES, num_subcores=SC_SUBCORES)


def dispatch_rows(h, pos0, pos1, n_rows):
    t, d = h.shape
    per_w = t // SC_WORKERS
    ch = _sc_rows(per_w, d, h.dtype)
    assert t % SC_WORKERS == 0 and per_w % ch == 0 and ch % 8 == 0, (t, ch)

    @functools.partial(
        pl.kernel, mesh=_sc_mesh(),
        out_type=jax.ShapeDtypeStruct((n_rows, d), h.dtype),
        scratch_types=[pltpu.VMEM((ch,), jnp.int32), pltpu.VMEM((ch,), jnp.int32),
                       pltpu.VMEM((ch, d), h.dtype), pltpu.SemaphoreType.DMA],
        name="moe_dispatch")
    def scatter(h_hbm, p0_hbm, p1_hbm, out_hbm, i0_v, i1_v, rows_v, sem):
        base = (lax.axis_index("s") * SC_CORES + lax.axis_index("c")) * per_w

        @pl.loop(0, per_w // ch)
        def _(j):
            off = base + j * ch
            pltpu.sync_copy(h_hbm.at[pl.ds(off, ch)], rows_v)
            pltpu.sync_copy(p0_hbm.at[pl.ds(off, ch)], i0_v)
            pltpu.sync_copy(p1_hbm.at[pl.ds(off, ch)], i1_v)
            pltpu.async_copy(rows_v, out_hbm.at[i0_v], sem).wait()
            pltpu.async_copy(rows_v, out_hbm.at[i1_v], sem).wait()

    return scatter(h, pos0, pos1)


def return_rows(ys, idx):
    n = idx.shape[0]
    d = ys.shape[1]
    per_w = n // SC_WORKERS
    ch = _sc_rows(per_w, d, ys.dtype)
    assert n % SC_WORKERS == 0 and per_w % ch == 0 and ch % 8 == 0, (n, ch)

    @functools.partial(
        pl.kernel, mesh=_sc_mesh(),
        out_type=jax.ShapeDtypeStruct((n, d), ys.dtype),
        scratch_types=[pltpu.VMEM((ch,), jnp.int32), pltpu.VMEM((ch, d), ys.dtype), pltpu.SemaphoreType.DMA],
        name="moe_return")
    def gather(ys_hbm, idx_hbm, out_hbm, idx_v, rows_v, sem):
        base = (lax.axis_index("s") * SC_CORES + lax.axis_index("c")) * per_w

        @pl.loop(0, per_w // ch)
        def _(j):
            off = base + j * ch
            pltpu.sync_copy(idx_hbm.at[pl.ds(off, ch)], idx_v)
            pltpu.async_copy(ys_hbm.at[idx_v], rows_v, sem).wait()
            pltpu.sync_copy(rows_v, out_hbm.at[pl.ds(off, ch)])

    return gather(ys, idx)


def moe_block(x, g, shift, scale, gate, router_w, w1, w3, w2, final_g):
    b, l, d = x.shape
    t = b * l
    tm = MOE_TM
    h, route, route_t, counts = router(x, g, shift, scale, router_w, _row_tile(l, ROW_TM))
    cnt = counts[0, :N_EXPERTS].astype(jnp.int32)
    gs = ((cnt + tm - 1) // tm) * tm
    ends = jnp.cumsum(gs)
    offs = ends - gs

    def sorted_row(choice):
        e = route_t[choice].astype(jnp.int32)
        start = sum(jnp.where(e == k, offs[k], 0) for k in range(N_EXPERTS))
        return start + route_t[4 + choice].astype(jnp.int32)

    pos0, pos1 = sorted_row(0), sorted_row(1)
    nt = (2 * t) // tm + N_EXPERTS
    n_valid = (ends[-1] // tm).astype(jnp.int32).reshape(1)
    tile = jnp.minimum(jnp.arange(nt, dtype=jnp.int32), n_valid[0] - 1)
    tile_expert = jnp.sum((tile[:, None] >= (ends // tm)[None, :]).astype(jnp.int32), axis=1)
    xs = dispatch_rows(h.reshape(t, d), pos0, pos1, nt * tm)
    ys = grouped_ffn(xs, tile_expert, n_valid, w1, w3, w2, tm, 2)
    yg = return_rows(ys, jnp.concatenate([pos0, pos1])).reshape(2, b, l, d // 2)
    return combine_final(x, yg, route, gate, final_g, _row_tile(l, 2 * ROW_TM))


def _in_weights(w_in):
    o1 = SSD_INNER
    o2 = o1 + XBC_WIDTH
    o3 = o2 + 2 * SSD_HEADS
    w_main = jnp.concatenate([w_in[:, :o2], w_in[:, o3:]], axis=1).astype(BF16)
    w_dt = jnp.pad(w_in[:, o2:o3], ((0, 0), (0, LANES - 2 * SSD_HEADS))).astype(BF16)
    return w_main, w_dt


def kernel(x, c, ctx, c_ctx, w_mod, b_mod, norm1_g, norm2_g, w_in, b_gate, ssd_conv_w, ssd_conv_b, ssd_dt_bias, ssd_a_log, ssd_d, ssd_norm_g, w_ssd_out, sc_conv_w, w_sc_out, w_o, ffn_w1, ffn_w3, ffn_w2, router_w, moe_w1, moe_w3, moe_w2, final_g):
    b, l, d = x.shape
    lc = ctx.shape[1]
    depth = w_mod.shape[0]
    assert depth % 2 == 0, "the final norm is fused into the routed channel mixer of the (odd) last layer"
    cc = jnp.zeros((16, d), F32).at[:b].set(c).at[b].set(c_ctx)
    mod = modulation(cc, w_mod, b_mod)
    zeros_state = jnp.zeros((b, SSD_GROUPS, 2, SSD_STATE, GROUP_W), F32)
    nctx = b * lc
    ctx = ctx.reshape(1, nctx, d)
    tmc = _row_tile(nctx, ROW_TM, lc)
    tmx = _row_tile(l, ROW_TM, GRID_W)

    def per_seq(t):
        return t.reshape(b, lc, t.shape[-1])

    for i in range(depth):
        last = i == depth - 1
        mx = mod[i, :b].reshape(b, N_MOD, 1, d)
        mc = mod[i, b].reshape(1, N_MOD, 1, d)
        w_main, w_dt = _in_weights(w_in[i])
        ssd_p = (ssd_conv_w[i], ssd_conv_b[i], ssd_dt_bias[i], ssd_a_log[i], ssd_d[i])
        out_p = (ssd_norm_g[i], b_gate[i], sc_conv_w[i], w_ssd_out[i].astype(BF16),
                 w_sc_out[i].astype(BF16), w_o[i].astype(BF16))

        if last:
            w_xbc = w_main[:, SSD_INNER:SSD_INNER + XBC_WIDTH]
            proj_c, dt_c = in_proj(ctx, norm1_g[i], mc[:, 0], mc[:, 1], w_xbc, w_dt,
                                   _row_tile(nctx, PROJ_TM), PROJ_TN_XBC)
            _, s_ctx = ssd_mixer(per_seq(proj_c), 0, per_seq(dt_c), *ssd_p, zeros_state, SSD_Q)
        else:
            proj_c, dt_c = in_proj(ctx, norm1_g[i], mc[:, 0], mc[:, 1], w_main, w_dt,
                                   _row_tile(nctx, PROJ_TM), PROJ_TN)
            y_c, s_ctx = ssd_mixer(per_seq(proj_c), SSD_INNER, per_seq(dt_c), *ssd_p, zeros_state, SSD_Q)
            ctx = mixer_out(y_c.reshape(1, nctx, SSD_INNER), proj_c, ctx, mc[:, 2], *out_p, tmc, lc)

        proj_x, dt_x = in_proj(x, norm1_g[i], mx[:, 0], mx[:, 1], w_main, w_dt, _row_tile(l, PROJ_TM), PROJ_TN)
        y_x, _ = ssd_mixer(proj_x, SSD_INNER, dt_x, *ssd_p, s_ctx, SSD_Q)
        x = mixer_out(y_x, proj_x, x, mx[:, 2], *out_p, tmx, GRID_W)

        j = i // 2
        if i % 2 == 0:
            w1, w3, w2 = ffn_w1[j].astype(BF16), ffn_w3[j].astype(BF16), ffn_w2[j].astype(BF16)
            x = ffn_dense(x, norm2_g[i], mx[:, 3], mx[:, 4], mx[:, 5], w1, w3, w2, _row_tile(l, FFN_TM))
            if not last:
                ctx = ffn_dense(ctx, norm2_g[i], mc[:, 3], mc[:, 4], mc[:, 5], w1, w3, w2, tmc)
        else:
            assert last, "the routed channel mixer is fused with the final norm"
            w1, w3, w2 = moe_w1[j].astype(BF16), moe_w3[j].astype(BF16), moe_w2[j].astype(BF16)
            x = moe_block(x, norm2_g[i], mx[:, 3], mx[:, 4], mx[:, 5], router_w[j], w1, w3, w2, final_g)
    return x
```

```python
import functools

import numpy as np
import jax
import jax.numpy as jnp
from jax import lax
from jax.experimental import pallas as pl
from jax.experimental.pallas import tpu as pltpu
from jax.experimental.pallas import tpu_sc as plsc

F32 = jnp.float32
BF16 = jnp.bfloat16

D_MODEL = 1024
GRID_W = 64
SSD_INNER = 2048
SSD_HEADS = 32
SSD_GROUPS = 4
SSD_HPG = 8
SSD_HEAD_DIM = 64
SSD_STATE = 128
GROUP_W = SSD_HPG * SSD_HEAD_DIM
XBC_WIDTH = SSD_INNER + 2 * SSD_GROUPS * SSD_STATE
SC_WIDTH = 1024
N_MOD = 6
N_EXPERTS = 8
EPS = 1e-6
LOG2E = 1.4426950408889634

LANES = 128
SSD_Q = 128
MOE_TM = 512
ROW_SUBTILES = 2
PROJ_TM = 1024
PROJ_TN = 2560
PROJ_TN_XBC = 1024
ROW_TM = 512
FFN_TM = 1024
VMEM_LIMIT = 56 * 1024 * 1024


def _row_tile(rows, want, multiple=1):
    t = multiple * max(1, min(want, rows) // multiple)
    assert rows % t == 0, (rows, t)
    return t


def _dot(a, b):
    return jnp.dot(a, b, preferred_element_type=F32)


def _sigmoid(v):
    return 1.0 / (1.0 + jnp.exp(-v))


def _silu(v):
    return v * _sigmoid(v)


def _sigmoid_t(v):
    return 0.5 + 0.5 * jnp.tanh(0.5 * v)


def _silu_t(v):
    hv = 0.5 * v
    return hv + hv * jnp.tanh(hv)


def _split2(a):
    hi = a.astype(BF16)
    lo = (a - hi.astype(F32)).astype(BF16)
    return hi, lo


def _split3(a):
    hi = a.astype(BF16)
    r = a - hi.astype(F32)
    mid = r.astype(BF16)
    lo = (r - mid.astype(F32)).astype(BF16)
    return hi, mid, lo


def _norm_mod(x, g, shift, scale):
    ms = jnp.mean(x * x, axis=-1, keepdims=True)
    return (x * lax.rsqrt(ms + EPS) * g) * (1.0 + scale) + shift


def _cparams(sem, vmem=VMEM_LIMIT):
    return pltpu.CompilerParams(dimension_semantics=sem, vmem_limit_bytes=vmem)


def _mod_kernel(c_ref, w_ref, b_ref, o_ref):
    a_hi, a_lo = _split2(_silu(c_ref[...]))
    w_hi, w_lo = _split2(w_ref[0])
    o_ref[0] = _dot(a_hi, w_hi) + _dot(a_lo, w_hi) + _dot(a_hi, w_lo) + b_ref[0]


def modulation(cc, w_mod, b_mod):
    depth, d, n = w_mod.shape
    tn = 1536
    return pl.pallas_call(
        _mod_kernel,
        grid=(depth, n // tn),
        in_specs=[pl.BlockSpec((16, d), lambda i, j: (0, 0)),
                  pl.BlockSpec((1, d, tn), lambda i, j: (i, 0, j)),
                  pl.BlockSpec((1, 1, tn), lambda i, j: (i, 0, j))],
        out_specs=pl.BlockSpec((1, 16, tn), lambda i, j: (i, 0, j)),
        out_shape=jax.ShapeDtypeStruct((depth, 16, n), F32),
        compiler_params=_cparams(("parallel", "parallel")),
        name="modulation",
    )(cc, w_mod, b_mod.reshape(depth, 1, n))


def _inproj_kernel(x_ref, g_ref, sh_ref, sc_ref, w_ref, wdt_ref, o_ref, dt_ref, h_ref):
    @pl.when(pl.program_id(2) == 0)
    def _():
        hb = _norm_mod(x_ref[0], g_ref[...], sh_ref[0], sc_ref[0]).astype(BF16)
        h_ref[...] = hb
        dt_ref[0] = _dot(hb, wdt_ref[...])

    o_ref[0] = _dot(h_ref[...], w_ref[...]).astype(o_ref.dtype)


def in_proj(x, g, shift, scale, w, wdt, tm, tn):
    b, l, d = x.shape
    n = w.shape[1]
    return pl.pallas_call(
        _inproj_kernel,
        grid=(b, l // tm, n // tn),
        in_specs=[pl.BlockSpec((1, tm, d), lambda i, m, j: (i, m, 0)),
                  pl.BlockSpec((1, d), lambda i, m, j: (0, 0)),
                  pl.BlockSpec((1, 1, d), lambda i, m, j: (i, 0, 0)),
                  pl.BlockSpec((1, 1, d), lambda i, m, j: (i, 0, 0)),
                  pl.BlockSpec((d, tn), lambda i, m, j: (0, j)),
                  pl.BlockSpec((d, LANES), lambda i, m, j: (0, 0))],
        out_specs=[pl.BlockSpec((1, tm, tn), lambda i, m, j: (i, m, j)),
                   pl.BlockSpec((1, tm, LANES), lambda i, m, j: (i, m, 0))],
        out_shape=[jax.ShapeDtypeStruct((b, l, n), BF16),
                   jax.ShapeDtypeStruct((b, l, LANES), F32)],
        scratch_shapes=[pltpu.VMEM((tm, d), BF16)],
        compiler_params=_cparams(("parallel", "parallel", "arbitrary")),
        name="in_proj",
    )(x, g.reshape(1, d), shift, scale, w, wdt)


def _dt_kernel(raw_ref, bias_ref, a_ref, pc_ref, ph_ref, plo_ref, col_ref, row_ref, sp_ref, *, q, ch):
    row = lax.broadcasted_iota(jnp.int32, (q, q), 0)
    col = lax.broadcasted_iota(jnp.int32, (q, q), 1)
    tri_l = jnp.where(col <= row, 1.0, 0.0).astype(BF16)
    tri_u = jnp.where(col >= row, 1.0, 0.0).astype(BF16)
    lane = lax.broadcasted_iota(jnp.int32, (q, LANES), 1)
    fwd = lane < SSD_HEADS
    ph, plo = ph_ref[...], plo_ref[...]

    def place_split(t):
        t_hi, t_lo = _split2(t)
        return (_dot(t_hi, ph) + _dot(t_lo, plo)).astype(BF16)

    for k in range(ch):
        v = raw_ref[0, k * q:(k + 1) * q, :] + bias_ref[...]
        dt = jnp.maximum(v, 0.0) + jnp.log1p(jnp.exp(-jnp.abs(v)))
        d1, d2, d3 = _split3(dt * a_ref[...])
        cs_f = _dot(tri_l, d1) + _dot(tri_l, d2) + _dot(tri_l, d3)
        cs_b = _dot(tri_u, d1) + _dot(tri_u, d2) + _dot(tri_u, d3)
        cs = jnp.where(fwd, cs_f, cs_b)
        tot = jnp.where(fwd[0:1], cs_f[q - 1:q, :], cs_b[0:1, :])
        sl = slice(k * q, (k + 1) * q)
        cs2 = cs * LOG2E
        c1, c2, c3 = _split3(cs2)
        pc = pc_ref[...]
        col_ref[0, sl, :] = _dot(c1, pc) + _dot(c2, pc) + _dot(c3, pc)
        r_t = (cs2 - jnp.log(dt) * LOG2E).T
        dt_t = dt.T
        for g in range(SSD_GROUPS):
            lo = g * SSD_HPG
            dt_f = dt_t[lo:lo + 8, :]
            dt_b = dt_t[SSD_HEADS + lo:SSD_HEADS + lo + 8, :]
            row_ref[0, g, k, 0:8, :] = r_t[lo:lo + 8, :]
            row_ref[0, g, k, 8:16, :] = r_t[SSD_HEADS + lo:SSD_HEADS + lo + 8, :]
            row_ref[0, g, k, 16:24, :] = jnp.log(dt_f + dt_b) * LOG2E
        sp_ref[0, sl, 0:LANES] = place_split(dt * jnp.exp(tot - cs))
        sp_ref[0, sl, LANES:2 * LANES] = place_split(jnp.exp(cs))


def _placements():
    pc = np.zeros((LANES, LANES), np.float32)
    ph = np.zeros((LANES, LANES), np.float32)
    plo = np.zeros((LANES, LANES), np.float32)
    for d in range(2):
        for g in range(SSD_GROUPS):
            for j in range(SSD_HPG):
                src = d * SSD_HEADS + g * SSD_HPG + j
                pc[src, g * 16 + d * 8 + j] = 1.0
                ph[src, (d * SSD_GROUPS + g) * 16 + j] = 1.0
                plo[src, (d * SSD_GROUPS + g) * 16 + 8 + j] = 1.0
    return jnp.asarray(pc, BF16), jnp.asarray(ph, BF16), jnp.asarray(plo, BF16)


def dt_prep(raw, bias, a, q):
    b, l, _ = raw.shape
    nc = l // q
    ch = min(8, nc)
    gn = SSD_GROUPS
    vspec = pl.BlockSpec((1, LANES), lambda i, c: (0, 0))
    pspec = pl.BlockSpec((LANES, LANES), lambda i, c: (0, 0))
    return pl.pallas_call(
        functools.partial(_dt_kernel, q=q, ch=ch),
        grid=(b, nc // ch),
        in_specs=[pl.BlockSpec((1, ch * q, LANES), lambda i, c: (i, c, 0)), vspec, vspec, pspec, pspec, pspec],
        out_specs=[pl.BlockSpec((1, ch * q, LANES), lambda i, c: (i, c, 0)),
                   pl.BlockSpec((1, gn, ch, 24, q), lambda i, c: (i, 0, c, 0, 0)),
                   pl.BlockSpec((1, ch * q, 2 * LANES), lambda i, c: (i, c, 0))],
        out_shape=[jax.ShapeDtypeStruct((b, l, LANES), F32),
                   jax.ShapeDtypeStruct((b, gn, nc, 24, q), F32),
                   jax.ShapeDtypeStruct((b, l, 2 * LANES), BF16)],
        compiler_params=_cparams(("parallel", "parallel")),
        name="dt_prep",
    )(raw, bias, a, *_placements())


def _ssd_kernel(xp_ref, bp_ref, cp_ref, cwx_ref, cwb_ref, cwc_ref, cbx_ref, cbb_ref, cbc_ref,
                col_ref, row_ref, sp_ref, e2f_ref, e2b_ref, dexp_ref, s0_ref,
                y_ref, sfin_ref,
                xs_ref, cc_ref, bt_ref, xwf_ref, sbe_ref, edge_ref, sf_ref, sb_ref, yo_ref, *, l, q):
    nc = l // q
    sr = lax.broadcasted_iota(jnp.int32, (q, q + 32), 0)
    sc = lax.broadcasted_iota(jnp.int32, (q, q + 32), 1)
    shift_prev = jnp.where((sc == sr - 1) | ((sr == 0) & (sc == q + 15)), 1.0, 0.0).astype(BF16)
    shift_next = jnp.where(((sc == sr + 1) & (sc < q)) | ((sr == q - 1) & (sc == q + 16)), 1.0, 0.0).astype(BF16)

    def conv_silu(srcs, w, b, k):
        r0 = pl.multiple_of(k * q, q)
        p0 = pl.multiple_of(jnp.maximum(r0 - 16, 0), 16)
        n0 = pl.multiple_of(jnp.minimum(r0 + q, l - 16), 16)

        def rows_at(start, n):
            parts = [r[0, pl.ds(start, n), ls] for r, ls in srcs]
            return parts[0] if len(parts) == 1 else jnp.concatenate(parts, axis=1)

        blk = rows_at(r0, q)
        before = rows_at(p0, 16)
        after = rows_at(n0, 16)
        before = jnp.where(k > 0, before, jnp.zeros_like(before))
        after = jnp.where(k < nc - 1, after, jnp.zeros_like(after))
        stacked = jnp.concatenate([blk, before, after], axis=0)
        v = (w[0:1] * _dot(shift_prev, stacked) + w[1:2] * blk.astype(F32)
             + w[2:3] * _dot(shift_next, stacked) + b)
        return _silu_t(v)

    all_lanes = slice(None)

    def prep(k):
        r0 = pl.multiple_of(k * q, q)
        x = conv_silu(((xp_ref, all_lanes),), cwx_ref[...], cbx_ref[...], k)
        xs_ref[pl.ds(r0, q), :] = x.astype(BF16)
        bc = conv_silu(((bp_ref, all_lanes), (cp_ref, all_lanes)),
                       jnp.concatenate([cwb_ref[...], cwc_ref[...]], axis=1),
                       jnp.concatenate([cbb_ref[...], cbc_ref[...]], axis=1), k)
        b_t = bc[:, :SSD_STATE].T.astype(BF16)
        bt_ref[k] = b_t
        cc_ref[pl.ds(r0, q), :] = bc[:, SSD_STATE:].astype(BF16)
        sp_w = sp_ref[0, pl.ds(r0, q), 0:LANES]
        xwf_ref[pl.ds(r0, q), :] = (x * _dot(sp_w, e2f_ref[0, 0])).astype(BF16)
        xw_b = (x * _dot(sp_w, e2b_ref[0, 0])).astype(BF16)
        e_last = sp_ref[0, pl.ds(pl.multiple_of(r0 + q - 16, 16), 16), LANES:2 * LANES]
        e_first = sp_ref[0, pl.ds(r0, 16), LANES:2 * LANES]
        edge_ref[k] = _dot(e_last, e2f_ref[0, 0])[15:16, :]
        edge_b = _dot(e_first, e2b_ref[0, 0])[0:1, :]
        return b_t, xw_b, edge_b

    sb_ref[...] = s0_ref[0, 0, 1]

    def bstep(i, carry):
        c = nc - 1 - i
        b_t, xw_b, edge_b = prep(c)
        sbe_ref[c] = sb_ref[...].astype(BF16)
        sb_ref[...] = sb_ref[...] * edge_b + _dot(b_t, xw_b)
        return carry

    lax.fori_loop(0, nc, bstep, 0, unroll=16)
    sfin_ref[0, 0, 1] = sb_ref[...]

    sf_ref[...] = s0_ref[0, 0, 0]
    li = lax.broadcasted_iota(jnp.int32, (q, q), 0)
    si = lax.broadcasted_iota(jnp.int32, (q, q), 1)
    lower = si <= li
    eye = si == li
    left = lax.broadcasted_iota(jnp.int32, (q, LANES), 1) < SSD_HEAD_DIM
    fu = yo_ref.shape[0]
    lane0 = 16 * pl.program_id(1)

    def chunk(c, slot):
        r0 = pl.multiple_of(c * q, q)
        cc = cc_ref[pl.ds(r0, q), :]
        sp_e = sp_ref[0, pl.ds(r0, q), LANES:2 * LANES]
        yo_ref[slot] = (_dot(sp_e, e2f_ref[0, 0]) * _dot(cc, sf_ref[...].astype(BF16))
                        + _dot(sp_e, e2b_ref[0, 0]) * _dot(cc, sbe_ref[c]))
        sf_ref[...] = sf_ref[...] * edge_ref[c] + _dot(bt_ref[c], xwf_ref[pl.ds(r0, q), :])
        g = _dot(cc, bt_ref[c])
        col = pltpu.roll(col_ref[0, pl.ds(r0, q), :], LANES - lane0, 1)
        row = row_ref[0, 0, c]
        for j in range(SSD_HPG // 2):
            lanes = slice(j * LANES, (j + 1) * LANES)
            xf = xs_ref[pl.ds(r0, q), lanes].astype(F32)
            x_diag = jnp.concatenate([jnp.where(left, xf, 0.0), jnp.where(left, 0.0, xf)], axis=0).astype(BF16)
            ms = []
            for h in (2 * j, 2 * j + 1):
                cs_l = jnp.take_along_axis(col, jnp.where(lower, h, 8 + h), axis=1, mode="promise_in_bounds")
                arg = cs_l - jnp.where(lower, row[h:h + 1, :], row[8 + h:9 + h, :])
                p = jnp.exp2(jnp.where(eye, row[16 + h:17 + h, :], arg))
                ms.append((g * p).astype(BF16))
            y = _dot(jnp.concatenate(ms, axis=1), x_diag) + yo_ref[slot, :, lanes] + dexp_ref[0, :, lanes] * xf
            y_ref[0, pl.ds(r0, q), lanes] = y.astype(y_ref.dtype)

    def fstep(i, carry):
        for slot in range(fu):
            chunk(fu * i + slot, slot)
        return carry

    lax.fori_loop(0, nc // fu, fstep, 0)
    sfin_ref[0, 0, 0] = sf_ref[...]


def _expanders():
    e = np.zeros((2, SSD_GROUPS, LANES, GROUP_W), np.float32)
    for d in range(2):
        for g in range(SSD_GROUPS):
            for s in range(2):
                for j in range(SSD_HPG):
                    e[d, g, (d * SSD_GROUPS + g) * 16 + s * 8 + j, j * SSD_HEAD_DIM:(j + 1) * SSD_HEAD_DIM] = 1.0
    return jnp.asarray(e, BF16)


def ssd_mixer(proj, xbc_col0, dt_raw, conv_w, conv_b, dt_bias, a_log, d_skip, s0, q):
    b, l, _ = proj.shape
    assert l % q == 0 and q % 16 == 0, (l, q)
    nc = l // q
    gn = SSD_GROUPS
    bias = jnp.zeros((1, LANES), F32).at[0, :2 * SSD_HEADS].set(dt_bias.reshape(-1))
    a = jnp.zeros((1, LANES), F32).at[0, :2 * SSD_HEADS].set(-jnp.exp(a_log.reshape(-1)))
    colg, rowg, sp = dt_prep(dt_raw, bias, a, q)
    dexp = jnp.repeat(d_skip.astype(F32), SSD_HEAD_DIM).reshape(gn, 1, GROUP_W)
    e2 = _expanders()

    cw = conv_w.astype(F32)
    cb = conv_b.astype(F32).reshape(1, -1)
    nb = SSD_INNER
    cwx, cwb, cwc = cw[:, :nb], cw[:, nb:nb + gn * SSD_STATE], cw[:, nb + gn * SSD_STATE:]
    cbx, cbb, cbc = cb[:, :nb], cb[:, nb:nb + gn * SSD_STATE], cb[:, nb + gn * SSD_STATE:]

    fu = max(u for u in (16, 8, 4, 2, 1) if nc % u == 0)
    xo = xbc_col0 // GROUP_W
    bo = (xbc_col0 + SSD_INNER) // SSD_STATE
    co = bo + gn
    st_spec = pl.BlockSpec((1, 1, 2, SSD_STATE, GROUP_W), lambda i, g: (i, g, 0, 0, 0))
    y, sfin = pl.pallas_call(
        functools.partial(_ssd_kernel, l=l, q=q),
        grid=(b, gn),
        in_specs=[pl.BlockSpec((1, l, GROUP_W), lambda i, g: (i, 0, xo + g)),
                  pl.BlockSpec((1, l, SSD_STATE), lambda i, g: (i, 0, bo + g)),
                  pl.BlockSpec((1, l, SSD_STATE), lambda i, g: (i, 0, co + g)),
                  pl.BlockSpec((3, GROUP_W), lambda i, g: (0, g)),
                  pl.BlockSpec((3, SSD_STATE), lambda i, g: (0, g)),
                  pl.BlockSpec((3, SSD_STATE), lambda i, g: (0, g)),
                  pl.BlockSpec((1, GROUP_W), lambda i, g: (0, g)),
                  pl.BlockSpec((1, SSD_STATE), lambda i, g: (0, g)),
                  pl.BlockSpec((1, SSD_STATE), lambda i, g: (0, g)),
                  pl.BlockSpec((1, l, LANES), lambda i, g: (i, 0, 0)),
                  pl.BlockSpec((1, 1, nc, 24, q), lambda i, g: (i, g, 0, 0, 0)),
                  pl.BlockSpec((1, l, 2 * LANES), lambda i, g: (i, 0, 0)),
                  pl.BlockSpec((1, 1, LANES, GROUP_W), lambda i, g: (0, g, 0, 0)),
                  pl.BlockSpec((1, 1, LANES, GROUP_W), lambda i, g: (1, g, 0, 0)),
                  pl.BlockSpec((1, 1, GROUP_W), lambda i, g: (g, 0, 0)),
                  st_spec],
        out_specs=[pl.BlockSpec((1, l, GROUP_W), lambda i, g: (i, 0, g)), st_spec],
        out_shape=[jax.ShapeDtypeStruct((b, l, SSD_INNER), BF16),
                   jax.ShapeDtypeStruct((b, gn, 2, SSD_STATE, GROUP_W), F32)],
        scratch_shapes=[pltpu.VMEM((l, GROUP_W), BF16),
                        pltpu.VMEM((l, SSD_STATE), BF16),
                        pltpu.VMEM((nc, SSD_STATE, q), BF16),
                        pltpu.VMEM((l, GROUP_W), BF16),
                        pltpu.VMEM((nc, SSD_STATE, GROUP_W), BF16),
                        pltpu.VMEM((nc, 1, GROUP_W), F32),
                        pltpu.VMEM((SSD_STATE, GROUP_W), F32),
                        pltpu.VMEM((SSD_STATE, GROUP_W), F32),
                        pltpu.VMEM((fu, q, GROUP_W), F32)],
        compiler_params=_cparams(("parallel", "parallel")),
        name="ssd",
    )(proj, proj, proj, cwx, cwb, cwc, cbx, cbb, cbc, colg, rowg, sp, e2, e2, dexp, s0)
    return y, sfin


def _mixout_kernel(y_ref, z_ref, gb_ref, gc_ref, hv_ref, g0_ref, g1_ref, x_ref, gate_ref,
                   ng_ref, bg_ref, scw_ref, wssd_ref, wsc_ref, wo_ref, o_ref, *, tm, period):
    ns = 2 if (tm // 2) % period == 0 else 1
    ts = tm // ns
    pos = lax.broadcasted_iota(jnp.int32, (ts, 1), 0) % period
    w = scw_ref[...]
    bg = bg_ref[...]
    for s in range(ns):
        r = slice(s * ts, (s + 1) * ts)
        yz = y_ref[0, r, :].astype(F32) * _silu_t(z_ref[0, r, :].astype(F32))
        ms = jnp.mean(yz * yz, axis=-1, keepdims=True)
        yn = (yz * lax.rsqrt(ms + EPS) * ng_ref[...]).astype(BF16)
        y_ssd = _dot(yn, wssd_ref[...])

        u = gc_ref[0, r, :].astype(F32) * hv_ref[0, r, :].astype(F32)
        u_prev = jnp.where(pos == 0, 0.0, pltpu.roll(u, 1, 0))
        u_next = jnp.where(pos == period - 1, 0.0, pltpu.roll(u, ts - 1, 0))
        v = w[0:1] * u_prev + w[1:2] * u + w[2:3] * u_next
        y_sc = _dot((gb_ref[0, r, :].astype(F32) * v).astype(BF16), wsc_ref[...])

        g0 = _sigmoid_t(g0_ref[0, r, :].astype(F32) + bg[:, :D_MODEL])
        g1 = _sigmoid_t(g1_ref[0, r, :].astype(F32) + bg[:, D_MODEL:])
        out = _dot((g0 * y_ssd + g1 * y_sc).astype(BF16), wo_ref[...])
        o_ref[0, r, :] = x_ref[0, r, :] + gate_ref[0] * out


def mixer_out(y, proj, x, gate, norm_g, b_gate, sc_conv_w, w_ssd, w_sc, w_o, tm, period):
    b, l, d = x.shape
    pc = lambda k: pl.BlockSpec((1, tm, d), lambda i, m, k=k: (i, m, k))
    full = lambda shp: pl.BlockSpec(shp, lambda i, m: (0,) * len(shp))
    return pl.pallas_call(
        functools.partial(_mixout_kernel, tm=tm, period=period),
        grid=(b, l // tm),
        in_specs=[pl.BlockSpec((1, tm, SSD_INNER), lambda i, m: (i, m, 0)),
                  pl.BlockSpec((1, tm, SSD_INNER), lambda i, m: (i, m, 0)),
                  pc(5), pc(6), pc(7), pc(8), pc(9),
                  pl.BlockSpec((1, tm, d), lambda i, m: (i, m, 0)),
                  pl.BlockSpec((1, 1, d), lambda i, m: (i, 0, 0)),
                  full((1, SSD_INNER)), full((1, 2 * d)), full((3, SC_WIDTH)),
                  full((SSD_INNER, d)), full((SC_WIDTH, d)), full((d, d))],
        out_specs=pl.BlockSpec((1, tm, d), lambda i, m: (i, m, 0)),
        out_shape=jax.ShapeDtypeStruct((b, l, d), F32),
        compiler_params=_cparams(("parallel", "parallel")),
        name="mixer_out",
    )(y, proj, proj, proj, proj, proj, proj, x, gate,
      norm_g.reshape(1, -1), b_gate.reshape(1, -1), sc_conv_w, w_ssd, w_sc, w_o)


def _ffn_kernel(x_ref, g_ref, sh_ref, sc_ref, gate_ref, w1_ref, w3_ref, w2_ref, o_ref, *, nf):
    fw = w1_ref.shape[1] // nf
    ts = x_ref.shape[1] // ROW_SUBTILES
    for s in range(ROW_SUBTILES):
        r = slice(s * ts, (s + 1) * ts)
        x = x_ref[0, r, :]
        hb = _norm_mod(x, g_ref[...], sh_ref[0], sc_ref[0]).astype(BF16)
        acc = None
        for k in range(nf):
            a = _dot(hb, w1_ref[:, k * fw:(k + 1) * fw])
            bb = _dot(hb, w3_ref[:, k * fw:(k + 1) * fw])
            part = _dot((_silu_t(a) * bb).astype(BF16), w2_ref[k * fw:(k + 1) * fw, :])
            acc = part if acc is None else acc + part
        o_ref[0, r, :] = x + gate_ref[0] * acc


def ffn_dense(x, g, shift, scale, gate, w1, w3, w2, tm):
    b, l, d = x.shape
    f = w1.shape[1]
    vec = pl.BlockSpec((1, 1, d), lambda i, m: (i, 0, 0))
    const = lambda shp: pl.BlockSpec(shp, lambda i, m: (0, 0), pipeline_mode=pl.Buffered(1))
    return pl.pallas_call(
        functools.partial(_ffn_kernel, nf=2),
        grid=(b, l // tm),
        in_specs=[pl.BlockSpec((1, tm, d), lambda i, m: (i, m, 0)),
                  pl.BlockSpec((1, d), lambda i, m: (0, 0)),
                  vec, vec, vec, const((d, f)), const((d, f)), const((f, d))],
        out_specs=pl.BlockSpec((1, tm, d), lambda i, m: (i, m, 0)),
        out_shape=jax.ShapeDtypeStruct((b, l, d), F32),
        compiler_params=_cparams(("parallel", "parallel")),
        name="ffn_dense",
    )(x, g.reshape(1, d), shift, scale, gate, w1, w3, w2)


def _router_kernel(x_ref, g_ref, sh_ref, sc_ref, rw_ref, h_ref, route_ref, route_t_ref, cnt_ref, run_ref, *, tm):
    @pl.when((pl.program_id(0) == 0) & (pl.program_id(1) == 0))
    def _():
        run_ref[...] = jnp.zeros_like(run_ref)

    h = _norm_mod(x_ref[0], g_ref[...], sh_ref[0], sc_ref[0])
    h_hi, h_lo = _split2(h)
    h_ref[0] = h
    w_hi, w_lo = _split2(rw_ref[...])
    logits = _dot(h_hi, w_hi) + _dot(h_lo, w_hi) + _dot(h_hi, w_lo)
    lane = lax.broadcasted_iota(jnp.int32, (tm, LANES), 1)
    ninf = float("-inf")
    lg = jnp.where(lane < N_EXPERTS, logits, ninf)
    m1 = jnp.max(lg, axis=1, keepdims=True)
    i1 = jnp.min(jnp.where(lg == m1, lane, LANES), axis=1, keepdims=True)
    lg2 = jnp.where(lane == i1, ninf, lg)
    m2 = jnp.max(lg2, axis=1, keepdims=True)
    i2 = jnp.min(jnp.where(lg2 == m2, lane, LANES), axis=1, keepdims=True)
    e2 = jnp.exp(m2 - m1)
    den = 1.0 + e2
    sel1 = jnp.where(lane == i1, 1.0, 0.0)
    sel2 = jnp.where(lane == i2, 1.0, 0.0)
    cnt = sel1 + sel2
    r = lax.broadcasted_iota(jnp.int32, (tm, tm), 0)
    c = lax.broadcasted_iota(jnp.int32, (tm, tm), 1)
    tri = jnp.where(c < r, 1.0, 0.0).astype(BF16)
    base = _dot(tri, cnt.astype(BF16)) + run_ref[0:1, :]
    r1 = jnp.sum(sel1 * base, axis=1, keepdims=True)
    r2 = jnp.sum(sel2 * base, axis=1, keepdims=True)
    vals = (i1.astype(F32), i2.astype(F32), 1.0 / den, e2 / den, r1, r2)
    out = jnp.zeros((tm, LANES), F32)
    for k, v in enumerate(vals):
        out = jnp.where(lane == k, v, out)
    route_ref[0] = out
    route_t_ref[...] = out.T
    new_run = run_ref[...] + jnp.sum(cnt, axis=0, keepdims=True)
    run_ref[...] = new_run
    cnt_ref[...] = new_run


def router(x, g, shift, scale, router_w, tm):
    b, l, d = x.shape
    rw = jnp.zeros((d, LANES), F32).at[:, :N_EXPERTS].set(router_w)
    vec = pl.BlockSpec((1, 1, d), lambda i, m: (i, 0, 0))
    mt = l // tm
    return pl.pallas_call(
        functools.partial(_router_kernel, tm=tm),
        grid=(b, l // tm),
        in_specs=[pl.BlockSpec((1, tm, d), lambda i, m: (i, m, 0)),
                  pl.BlockSpec((1, d), lambda i, m: (0, 0)),
                  vec, vec, pl.BlockSpec((d, LANES), lambda i, m: (0, 0))],
        out_specs=[pl.BlockSpec((1, tm, d), lambda i, m: (i, m, 0)),
                   pl.BlockSpec((1, tm, LANES), lambda i, m: (i, m, 0)),
                   pl.BlockSpec((LANES, tm), lambda i, m: (0, i * mt + m)),
                   pl.BlockSpec((8, LANES), lambda i, m: (0, 0))],
        out_shape=[jax.ShapeDtypeStruct((b, l, d), F32),
                   jax.ShapeDtypeStruct((b, l, LANES), F32),
                   jax.ShapeDtypeStruct((LANES, b * l), F32),
                   jax.ShapeDtypeStruct((8, LANES), F32)],
        scratch_shapes=[pltpu.VMEM((8, LANES), F32)],
        compiler_params=_cparams(("arbitrary", "arbitrary")),
        name="router",
    )(x, g.reshape(1, d), shift, scale, rw)


def _pack_bf16_pairs(y):
    k = y.shape[1] // 2
    bits = lax.bitcast_convert_type(y.astype(BF16).astype(F32), jnp.uint32)
    return bits[:, :k] | (bits[:, k:] >> 16)


def _unpack_bf16_pairs(p):
    hi = lax.bitcast_convert_type(p & jnp.uint32(0xFFFF0000), F32)
    lo = lax.bitcast_convert_type(p << 16, F32)
    return jnp.concatenate([hi, lo], axis=1)


def _gffn_kernel(te_ref, nv_ref, x_ref, w1_ref, w3_ref, w2_ref, o_ref, acc_ref, *, nf):
    i = pl.program_id(0)
    f = pl.program_id(1)

    @pl.when(i < nv_ref[0])
    def _():
        ts = x_ref.shape[0] // ROW_SUBTILES
        parts = []
        for s in range(ROW_SUBTILES):
            x = x_ref[s * ts:(s + 1) * ts, :].astype(BF16)
            a = _dot(x, w1_ref[0])
            bb = _dot(x, w3_ref[0])
            parts.append(_dot((_silu_t(a) * bb).astype(BF16), w2_ref[0]))
        part = jnp.concatenate(parts, axis=0)

        @pl.when(f == 0)
        def _():
            acc_ref[...] = part

        if nf > 2:
            @pl.when((f > 0) & (f < nf - 1))
            def _():
                acc_ref[...] += part

        @pl.when(f == nf - 1)
        def _():
            o_ref[...] = _pack_bf16_pairs(part if nf == 1 else acc_ref[...] + part)


def grouped_ffn(xs, tile_expert, n_valid, w1, w3, w2, tm, nf):
    rows, d = xs.shape
    nt = rows // tm
    f = w1.shape[2]
    fw = f // nf

    def tile(i, nv):
        return jnp.minimum(i, nv[0] - 1)

    def fchunk(i, k, nv):
        return jnp.where(i < nv[0], k, nf - 1)

    grid_spec = pltpu.PrefetchScalarGridSpec(
        num_scalar_prefetch=2,
        grid=(nt, nf),
        in_specs=[pl.BlockSpec((tm, d), lambda i, k, te, nv: (tile(i, nv), 0)),
                  pl.BlockSpec((1, d, fw), lambda i, k, te, nv: (te[i], 0, fchunk(i, k, nv))),
                  pl.BlockSpec((1, d, fw), lambda i, k, te, nv: (te[i], 0, fchunk(i, k, nv))),
                  pl.BlockSpec((1, fw, d), lambda i, k, te, nv: (te[i], fchunk(i, k, nv), 0))],
        out_specs=pl.BlockSpec((tm, d // 2), lambda i, k, te, nv: (tile(i, nv), 0)),
        scratch_shapes=[pltpu.VMEM((tm, d), F32)],
    )
    return pl.pallas_call(
        functools.partial(_gffn_kernel, nf=nf),
        grid_spec=grid_spec,
        out_shape=jax.ShapeDtypeStruct((rows, d // 2), jnp.uint32),
        compiler_params=_cparams(("arbitrary", "arbitrary")),
        name="grouped_ffn",
    )(tile_expert, n_valid, xs, w1, w3, w2)


def _combine_kernel(x_ref, y0_ref, y1_ref, route_ref, gate_ref, fg_ref, o_ref):
    r = route_ref[0]
    moe = r[:, 2:3] * _unpack_bf16_pairs(y0_ref[0, 0]) + r[:, 3:4] * _unpack_bf16_pairs(y1_ref[0, 0])
    xn = x_ref[0] + gate_ref[0] * moe
    ms = jnp.mean(xn * xn, axis=-1, keepdims=True)
    o_ref[0] = xn * lax.rsqrt(ms + EPS) * fg_ref[...]


def combine_final(x, yg, route, gate, final_g, tm):
    b, l, d = x.shape
    return pl.pallas_call(
        _combine_kernel,
        grid=(b, l // tm),
        in_specs=[pl.BlockSpec((1, tm, d), lambda i, m: (i, m, 0)),
                  pl.BlockSpec((1, 1, tm, d // 2), lambda i, m: (0, i, m, 0)),
                  pl.BlockSpec((1, 1, tm, d // 2), lambda i, m: (1, i, m, 0)),
                  pl.BlockSpec((1, tm, LANES), lambda i, m: (i, m, 0)),
                  pl.BlockSpec((1, 1, d), lambda i, m: (i, 0, 0)),
                  pl.BlockSpec((1, d), lambda i, m: (0, 0))],
        out_specs=pl.BlockSpec((1, tm, d), lambda i, m: (i, m, 0)),
        out_shape=jax.ShapeDtypeStruct((b, l, d), F32),
        compiler_params=_cparams(("parallel", "parallel")),
        name="combine_final",
    )(x, yg, yg, route, gate, final_g.reshape(1, d))


SC_CORES = 2
SC_SUBCORES = 16
SC_WORKERS = SC_CORES * SC_SUBCORES
SC_STREAM_BYTES = 256 * 1024
SC_STREAM_ROWS = 128


def _sc_rows(per_worker, d, dtype):
    return min(SC_STREAM_ROWS, SC_STREAM_BYTES // (d * jnp.dtype(dtype).itemsize), per_worker)


def _sc_mesh():
    return plsc.VectorSubcoreMesh(core_axis_name="c", subcore_axis_name="s",
                                  num_cores=SC_CORES, num_subcores=SC_SUBCORES)


def dispatch_rows(h, pos0, pos1, n_rows):
    t, d = h.shape
    per_w = t // SC_WORKERS
    ch = _sc_rows(per_w, d, h.dtype)
    assert t % SC_WORKERS == 0 and per_w % ch == 0 and ch % 8 == 0, (t, ch)

    @functools.partial(
        pl.kernel, mesh=_sc_mesh(),
        out_type=jax.ShapeDtypeStruct((n_rows, d), h.dtype),
        scratch_types=[pltpu.VMEM((ch,), jnp.int32), pltpu.VMEM((ch,), jnp.int32),
                       pltpu.VMEM((ch, d), h.dtype), pltpu.SemaphoreType.DMA],
        name="moe_dispatch")
    def scatter(h_hbm, p0_hbm, p1_hbm, out_hbm, i0_v, i1_v, rows_v, sem):
        base = (lax.axis_index("s") * SC_CORES + lax.axis_index("c")) * per_w

        @pl.loop(0, per_w // ch)
        def _(j):
            off = base + j * ch
            pltpu.sync_copy(h_hbm.at[pl.ds(off, ch)], rows_v)
            pltpu.sync_copy(p0_hbm.at[pl.ds(off, ch)], i0_v)
            pltpu.sync_copy(p1_hbm.at[pl.ds(off, ch)], i1_v)
            pltpu.async_copy(rows_v, out_hbm.at[i0_v], sem).wait()
            pltpu.async_copy(rows_v, out_hbm.at[i1_v], sem).wait()

    return scatter(h, pos0, pos1)


def return_rows(ys, idx):
    n = idx.shape[0]
    d = ys.shape[1]
    per_w = n // SC_WORKERS
    ch = _sc_rows(per_w, d, ys.dtype)
    assert n % SC_WORKERS == 0 and per_w % ch == 0 and ch % 8 == 0, (n, ch)

    @functools.partial(
        pl.kernel, mesh=_sc_mesh(),
        out_type=jax.ShapeDtypeStruct((n, d), ys.dtype),
        scratch_types=[pltpu.VMEM((ch,), jnp.int32), pltpu.VMEM((ch, d), ys.dtype), pltpu.SemaphoreType.DMA],
        name="moe_return")
    def gather(ys_hbm, idx_hbm, out_hbm, idx_v, rows_v, sem):
        base = (lax.axis_index("s") * SC_CORES + lax.axis_index("c")) * per_w

        @pl.loop(0, per_w // ch)
        def _(j):
            off = base + j * ch
            pltpu.sync_copy(idx_hbm.at[pl.ds(off, ch)], idx_v)
            pltpu.async_copy(ys_hbm.at[idx_v], rows_v, sem).wait()
            pltpu.sync_copy(rows_v, out_hbm.at[pl.ds(off, ch)])

    return gather(ys, idx)


def moe_block(x, g, shift, scale, gate, router_w, w1, w3, w2, final_g):
    b, l, d = x.shape
    t = b * l
    tm = MOE_TM
    h, route, route_t, counts = router(x, g, shift, scale, router_w, _row_tile(l, ROW_TM))
    cnt = counts[0, :N_EXPERTS].astype(jnp.int32)
    gs = ((cnt + tm - 1) // tm) * tm
    ends = jnp.cumsum(gs)
    offs = ends - gs

    def sorted_row(choice):
        e = route_t[choice].astype(jnp.int32)
        start = sum(jnp.where(e == k, offs[k], 0) for k in range(N_EXPERTS))
        return start + route_t[4 + choice].astype(jnp.int32)

    pos0, pos1 = sorted_row(0), sorted_row(1)
    nt = (2 * t) // tm + N_EXPERTS
    n_valid = (ends[-1] // tm).astype(jnp.int32).reshape(1)
    tile = jnp.minimum(jnp.arange(nt, dtype=jnp.int32), n_valid[0] - 1)
    tile_expert = jnp.sum((tile[:, None] >= (ends // tm)[None, :]).astype(jnp.int32), axis=1)
    xs = dispatch_rows(h.reshape(t, d), pos0, pos1, nt * tm)
    ys = grouped_ffn(xs, tile_expert, n_valid, w1, w3, w2, tm, 2)
    yg = return_rows(ys, jnp.concatenate([pos0, pos1])).reshape(2, b, l, d // 2)
    return combine_final(x, yg, route, gate, final_g, _row_tile(l, 2 * ROW_TM))


def _in_weights(w_in):
    o1 = SSD_INNER
    o2 = o1 + XBC_WIDTH
    o3 = o2 + 2 * SSD_HEADS
    w_main = jnp.concatenate([w_in[:, :o2], w_in[:, o3:]], axis=1).astype(BF16)
    w_dt = jnp.pad(w_in[:, o2:o3], ((0, 0), (0, LANES - 2 * SSD_HEADS))).astype(BF16)
    return w_main, w_dt


def kernel(x, c, ctx, c_ctx, w_mod, b_mod, norm1_g, norm2_g, w_in, b_gate, ssd_conv_w, ssd_conv_b, ssd_dt_bias, ssd_a_log, ssd_d, ssd_norm_g, w_ssd_out, sc_conv_w, w_sc_out, w_o, ffn_w1, ffn_w3, ffn_w2, router_w, moe_w1, moe_w3, moe_w2, final_g):
    b, l, d = x.shape
    lc = ctx.shape[1]
    depth = w_mod.shape[0]
    assert depth % 2 == 0, "the final norm is fused into the routed channel mixer of the (odd) last layer"
    cc = jnp.zeros((16, d), F32).at[:b].set(c).at[b].set(c_ctx)
    mod = modulation(cc, w_mod, b_mod)
    zeros_state = jnp.zeros((b, SSD_GROUPS, 2, SSD_STATE, GROUP_W), F32)
    nctx = b * lc
    ctx = ctx.reshape(1, nctx, d)
    tmc = _row_tile(nctx, ROW_TM, lc)
    tmx = _row_tile(l, ROW_TM, GRID_W)

    def per_seq(t):
        return t.reshape(b, lc, t.shape[-1])

    for i in range(depth):
        last = i == depth - 1
        mx = mod[i, :b].reshape(b, N_MOD, 1, d)
        mc = mod[i, b].reshape(1, N_MOD, 1, d)
        w_main, w_dt = _in_weights(w_in[i])
        ssd_p = (ssd_conv_w[i], ssd_conv_b[i], ssd_dt_bias[i], ssd_a_log[i], ssd_d[i])
        out_p = (ssd_norm_g[i], b_gate[i], sc_conv_w[i], w_ssd_out[i].astype(BF16),
                 w_sc_out[i].astype(BF16), w_o[i].astype(BF16))

        if last:
            w_xbc = w_main[:, SSD_INNER:SSD_INNER + XBC_WIDTH]
            proj_c, dt_c = in_proj(ctx, norm1_g[i], mc[:, 0], mc[:, 1], w_xbc, w_dt,
                                   _row_tile(nctx, PROJ_TM), PROJ_TN_XBC)
            _, s_ctx = ssd_mixer(per_seq(proj_c), 0, per_seq(dt_c), *ssd_p, zeros_state, SSD_Q)
        else:
            proj_c, dt_c = in_proj(ctx, norm1_g[i], mc[:, 0], mc[:, 1], w_main, w_dt,
                                   _row_tile(nctx, PROJ_TM), PROJ_TN)
            y_c, s_ctx = ssd_mixer(per_seq(proj_c), SSD_INNER, per_seq(dt_c), *ssd_p, zeros_state, SSD_Q)
            ctx = mixer_out(y_c.reshape(1, nctx, SSD_INNER), proj_c, ctx, mc[:, 2], *out_p, tmc, lc)

        proj_x, dt_x = in_proj(x, norm1_g[i], mx[:, 0], mx[:, 1], w_main, w_dt, _row_tile(l, PROJ_TM), PROJ_TN)
        y_x, _ = ssd_mixer(proj_x, SSD_INNER, dt_x, *ssd_p, s_ctx, SSD_Q)
        x = mixer_out(y_x, proj_x, x, mx[:, 2], *out_p, tmx, GRID_W)

        j = i // 2
        if i % 2 == 0:
            w1, w3, w2 = ffn_w1[j].astype(BF16), ffn_w3[j].astype(BF16), ffn_w2[j].astype(BF16)
            x = ffn_dense(x, norm2_g[i], mx[:, 3], mx[:, 4], mx[:, 5], w1, w3, w2, _row_tile(l, FFN_TM))
            if not last:
                ctx = ffn_dense(ctx, norm2_g[i], mc[:, 3], mc[:, 4], mc[:, 5], w1, w3, w2, tmc)
        else:
            assert last, "the routed channel mixer is fused with the final norm"
            w1, w3, w2 = moe_w1[j].astype(BF16), moe_w3[j].astype(BF16), moe_w2[j].astype(BF16)
            x = moe_block(x, norm2_g[i], mx[:, 3], mx[:, 4], mx[:, 5], router_w[j], w1, w3, w2, final_g)
    return x
```

```python
import functools

import numpy as np
import jax
import jax.numpy as jnp
from jax import lax
from jax.experimental import pallas as pl
from jax.experimental.pallas import tpu as pltpu
from jax.experimental.pallas import tpu_sc as plsc

F32 = jnp.float32
BF16 = jnp.bfloat16

D_MODEL = 1024
GRID_W = 64
SSD_INNER = 2048
SSD_HEADS = 32
SSD_GROUPS = 4
SSD_HPG = 8
SSD_HEAD_DIM = 64
SSD_STATE = 128
GROUP_W = SSD_HPG * SSD_HEAD_DIM
XBC_WIDTH = SSD_INNER + 2 * SSD_GROUPS * SSD_STATE
SC_WIDTH = 1024
N_MOD = 6
N_EXPERTS = 8
EPS = 1e-6
LOG2E = 1.4426950408889634

LANES = 128
SSD_Q = 128
MOE_TM = 512
ROW_SUBTILES = 2
PROJ_TM = 1024
PROJ_TN = 2560
PROJ_TN_XBC = 1024
ROW_TM = 512
FFN_TM = 1024
VMEM_LIMIT = 56 * 1024 * 1024


def _row_tile(rows, want, multiple=1):
    t = multiple * max(1, min(want, rows) // multiple)
    assert rows % t == 0, (rows, t)
    return t


def _dot(a, b):
    return jnp.dot(a, b, preferred_element_type=F32)


def _sigmoid(v):
    return 1.0 / (1.0 + jnp.exp(-v))


def _silu(v):
    return v * _sigmoid(v)


def _sigmoid_t(v):
    return 0.5 + 0.5 * jnp.tanh(0.5 * v)


def _silu_t(v):
    hv = 0.5 * v
    return hv + hv * jnp.tanh(hv)


def _split2(a):
    hi = a.astype(BF16)
    lo = (a - hi.astype(F32)).astype(BF16)
    return hi, lo


def _split3(a):
    hi = a.astype(BF16)
    r = a - hi.astype(F32)
    mid = r.astype(BF16)
    lo = (r - mid.astype(F32)).astype(BF16)
    return hi, mid, lo


def _norm_mod(x, g, shift, scale):
    ms = jnp.mean(x * x, axis=-1, keepdims=True)
    return (x * lax.rsqrt(ms + EPS) * g) * (1.0 + scale) + shift


def _cparams(sem, vmem=VMEM_LIMIT):
    return pltpu.CompilerParams(dimension_semantics=sem, vmem_limit_bytes=vmem)


def _mod_kernel(c_ref, w_ref, b_ref, o_ref):
    a_hi, a_lo = _split2(_silu(c_ref[...]))
    w_hi, w_lo = _split2(w_ref[0])
    o_ref[0] = _dot(a_hi, w_hi) + _dot(a_lo, w_hi) + _dot(a_hi, w_lo) + b_ref[0]


def modulation(cc, w_mod, b_mod):
    depth, d, n = w_mod.shape
    tn = 1536
    return pl.pallas_call(
        _mod_kernel,
        grid=(depth, n // tn),
        in_specs=[pl.BlockSpec((16, d), lambda i, j: (0, 0)),
                  pl.BlockSpec((1, d, tn), lambda i, j: (i, 0, j)),
                  pl.BlockSpec((1, 1, tn), lambda i, j: (i, 0, j))],
        out_specs=pl.BlockSpec((1, 16, tn), lambda i, j: (i, 0, j)),
        out_shape=jax.ShapeDtypeStruct((depth, 16, n), F32),
        compiler_params=_cparams(("parallel", "parallel")),
        name="modulation",
    )(cc, w_mod, b_mod.reshape(depth, 1, n))


def _inproj_kernel(x_ref, g_ref, sh_ref, sc_ref, w_ref, wdt_ref, o_ref, dt_ref, h_ref):
    @pl.when(pl.program_id(2) == 0)
    def _():
        hb = _norm_mod(x_ref[0], g_ref[...], sh_ref[0], sc_ref[0]).astype(BF16)
        h_ref[...] = hb
        dt_ref[0] = _dot(hb, wdt_ref[...])

    o_ref[0] = _dot(h_ref[...], w_ref[...]).astype(o_ref.dtype)


def in_proj(x, g, shift, scale, w, wdt, tm, tn):
    b, l, d = x.shape
    n = w.shape[1]
    return pl.pallas_call(
        _inproj_kernel,
        grid=(b, l // tm, n // tn),
        in_specs=[pl.BlockSpec((1, tm, d), lambda i, m, j: (i, m, 0)),
                  pl.BlockSpec((1, d), lambda i, m, j: (0, 0)),
                  pl.BlockSpec((1, 1, d), lambda i, m, j: (i, 0, 0)),
                  pl.BlockSpec((1, 1, d), lambda i, m, j: (i, 0, 0)),
                  pl.BlockSpec((d, tn), lambda i, m, j: (0, j)),
                  pl.BlockSpec((d, LANES), lambda i, m, j: (0, 0))],
        out_specs=[pl.BlockSpec((1, tm, tn), lambda i, m, j: (i, m, j)),
                   pl.BlockSpec((1, tm, LANES), lambda i, m, j: (i, m, 0))],
        out_shape=[jax.ShapeDtypeStruct((b, l, n), BF16),
                   jax.ShapeDtypeStruct((b, l, LANES), F32)],
        scratch_shapes=[pltpu.VMEM((tm, d), BF16)],
        compiler_params=_cparams(("parallel", "parallel", "arbitrary")),
        name="in_proj",
    )(x, g.reshape(1, d), shift, scale, w, wdt)


def _dt_kernel(raw_ref, bias_ref, a_ref, pc_ref, ph_ref, plo_ref, col_ref, row_ref, sp_ref, *, q, ch):
    row = lax.broadcasted_iota(jnp.int32, (q, q), 0)
    col = lax.broadcasted_iota(jnp.int32, (q, q), 1)
    tri_l = jnp.where(col <= row, 1.0, 0.0).astype(BF16)
    tri_u = jnp.where(col >= row, 1.0, 0.0).astype(BF16)
    lane = lax.broadcasted_iota(jnp.int32, (q, LANES), 1)
    fwd = lane < SSD_HEADS
    ph, plo = ph_ref[...], plo_ref[...]

    def place_split(t):
        t_hi, t_lo = _split2(t)
        return (_dot(t_hi, ph) + _dot(t_lo, plo)).astype(BF16)

    v = raw_ref[0] + bias_ref[...]
    dt = jnp.maximum(v, 0.0) + jnp.log1p(jnp.exp(-jnp.abs(v)))
    d1, d2, d3 = _split3(dt * a_ref[...])
    cs_parts, tot_parts = [], []
    for k in range(ch):
        sl = slice(k * q, (k + 1) * q)
        cs_f = _dot(tri_l, d1[sl]) + _dot(tri_l, d2[sl]) + _dot(tri_l, d3[sl])
        cs_b = _dot(tri_u, d1[sl]) + _dot(tri_u, d2[sl]) + _dot(tri_u, d3[sl])
        cs_parts.append(jnp.where(fwd, cs_f, cs_b))
        tot = jnp.where(fwd[0:1], cs_f[q - 1:q, :], cs_b[0:1, :])
        tot_parts.append(jnp.broadcast_to(tot, (q, LANES)))
    cs = jnp.concatenate(cs_parts, axis=0)
    tot = jnp.concatenate(tot_parts, axis=0)
    cs2 = cs * LOG2E
    c1, c2, c3 = _split3(cs2)
    pc = pc_ref[...]
    col_ref[0] = _dot(c1, pc) + _dot(c2, pc) + _dot(c3, pc)
    r = cs2 - jnp.log(dt) * LOG2E
    for k in range(ch):
        sl = slice(k * q, (k + 1) * q)
        r_t = r[sl].T
        dt_t = dt[sl].T
        for g in range(SSD_GROUPS):
            lo = g * SSD_HPG
            dt_f = dt_t[lo:lo + 8, :]
            dt_b = dt_t[SSD_HEADS + lo:SSD_HEADS + lo + 8, :]
            row_ref[0, g, k, 0:8, :] = r_t[lo:lo + 8, :]
            row_ref[0, g, k, 8:16, :] = r_t[SSD_HEADS + lo:SSD_HEADS + lo + 8, :]
            row_ref[0, g, k, 16:24, :] = jnp.log(dt_f + dt_b) * LOG2E
    sp_ref[0, :, 0:LANES] = place_split(dt * jnp.exp(tot - cs))
    sp_ref[0, :, LANES:2 * LANES] = place_split(jnp.exp(cs))


def _placements():
    pc = np.zeros((LANES, LANES), np.float32)
    ph = np.zeros((LANES, LANES), np.float32)
    plo = np.zeros((LANES, LANES), np.float32)
    for d in range(2):
        for g in range(SSD_GROUPS):
            for j in range(SSD_HPG):
                src = d * SSD_HEADS + g * SSD_HPG + j
                pc[src, g * 16 + d * 8 + j] = 1.0
                ph[src, (d * SSD_GROUPS + g) * 16 + j] = 1.0
                plo[src, (d * SSD_GROUPS + g) * 16 + 8 + j] = 1.0
    return jnp.asarray(pc, BF16), jnp.asarray(ph, BF16), jnp.asarray(plo, BF16)


def dt_prep(raw, bias, a, q):
    b, l, _ = raw.shape
    nc = l // q
    ch = min(8, nc)
    gn = SSD_GROUPS
    vspec = pl.BlockSpec((1, LANES), lambda i, c: (0, 0))
    pspec = pl.BlockSpec((LANES, LANES), lambda i, c: (0, 0))
    return pl.pallas_call(
        functools.partial(_dt_kernel, q=q, ch=ch),
        grid=(b, nc // ch),
        in_specs=[pl.BlockSpec((1, ch * q, LANES), lambda i, c: (i, c, 0)), vspec, vspec, pspec, pspec, pspec],
        out_specs=[pl.BlockSpec((1, ch * q, LANES), lambda i, c: (i, c, 0)),
                   pl.BlockSpec((1, gn, ch, 24, q), lambda i, c: (i, 0, c, 0, 0)),
                   pl.BlockSpec((1, ch * q, 2 * LANES), lambda i, c: (i, c, 0))],
        out_shape=[jax.ShapeDtypeStruct((b, l, LANES), F32),
                   jax.ShapeDtypeStruct((b, gn, nc, 24, q), F32),
                   jax.ShapeDtypeStruct((b, l, 2 * LANES), BF16)],
        compiler_params=_cparams(("parallel", "parallel")),
        name="dt_prep",
    )(raw, bias, a, *_placements())


def _ssd_kernel(xp_ref, bp_ref, cp_ref, cwx_ref, cwb_ref, cwc_ref, cbx_ref, cbb_ref, cbc_ref,
                col_ref, row_ref, sp_ref, e2f_ref, e2b_ref, dexp_ref, s0_ref,
                y_ref, sfin_ref,
                xs_ref, cc_ref, bt_ref, xwf_ref, sbe_ref, edge_ref, sf_ref, sb_ref, yo_ref, *, l, q):
    nc = l // q
    sr = lax.broadcasted_iota(jnp.int32, (q, q + 32), 0)
    sc = lax.broadcasted_iota(jnp.int32, (q, q + 32), 1)
    shift_prev = jnp.where((sc == sr - 1) | ((sr == 0) & (sc == q + 15)), 1.0, 0.0).astype(BF16)
    shift_next = jnp.where(((sc == sr + 1) & (sc < q)) | ((sr == q - 1) & (sc == q + 16)), 1.0, 0.0).astype(BF16)

    def conv_taps(srcs, k):
        r0 = pl.multiple_of(k * q, q)
        p0 = pl.multiple_of(jnp.maximum(r0 - 16, 0), 16)
        n0 = pl.multiple_of(jnp.minimum(r0 + q, l - 16), 16)

        def rows_at(start, n):
            parts = [r[0, pl.ds(start, n), :] for r in srcs]
            return parts[0] if len(parts) == 1 else jnp.concatenate(parts, axis=1)

        blk = rows_at(r0, q)
        before = rows_at(p0, 16)
        after = rows_at(n0, 16)
        before = jnp.where(k > 0, before, jnp.zeros_like(before))
        after = jnp.where(k < nc - 1, after, jnp.zeros_like(after))
        stacked = jnp.concatenate([blk, before, after], axis=0)
        return _dot(shift_prev, stacked), blk, _dot(shift_next, stacked)

    def conv_silu(taps, w, b):
        return _silu_t(w[0:1] * taps[0] + w[1:2] * taps[1].astype(F32) + w[2:3] * taps[2] + b)

    sb_ref[...] = s0_ref[0, 0, 1]
    bu = max(u for u in (8, 4, 2, 1) if nc % u == 0)
    w_bc = jnp.concatenate([cwb_ref[...], cwc_ref[...]], axis=1)
    b_bc = jnp.concatenate([cbb_ref[...], cbc_ref[...]], axis=1)

    def btrip(i, carry):
        ks = [nc - 1 - (bu * i + u) for u in range(bu)]
        rs = [pl.multiple_of(k * q, q) for k in ks]
        x_taps = [conv_taps((xp_ref,), k) for k in ks]
        bc_taps = [conv_taps((bp_ref, cp_ref), k) for k in ks]
        sp_w = [sp_ref[0, pl.ds(r0, q), 0:LANES] for r0 in rs]
        w_f = [_dot(s, e2f_ref[0, 0]) for s in sp_w]
        w_b = [_dot(s, e2b_ref[0, 0]) for s in sp_w]
        edge_b = []
        for k, r0 in zip(ks, rs):
            e_last = sp_ref[0, pl.ds(pl.multiple_of(r0 + q - 16, 16), 16), LANES:2 * LANES]
            e_first = sp_ref[0, pl.ds(r0, 16), LANES:2 * LANES]
            edge_ref[k] = _dot(e_last, e2f_ref[0, 0])[15:16, :]
            edge_b.append(_dot(e_first, e2b_ref[0, 0])[0:1, :])
        xs = [conv_silu(t, cwx_ref[...], cbx_ref[...]) for t in x_taps]
        bcs = [conv_silu(t, w_bc, b_bc) for t in bc_taps]
        b_ts, xw_bs = [], []
        for u, (k, r0) in enumerate(zip(ks, rs)):
            xs_ref[pl.ds(r0, q), :] = xs[u].astype(BF16)
            b_t = bcs[u][:, :SSD_STATE].T.astype(BF16)
            bt_ref[k] = b_t
            b_ts.append(b_t)
            cc_ref[pl.ds(r0, q), :] = bcs[u][:, SSD_STATE:].astype(BF16)
            xwf_ref[pl.ds(r0, q), :] = (xs[u] * w_f[u]).astype(BF16)
            xw_bs.append((xs[u] * w_b[u]).astype(BF16))
        d_s = [_dot(b_ts[u], xw_bs[u]) for u in range(bu)]
        for u, k in enumerate(ks):
            sbe_ref[k] = sb_ref[...].astype(BF16)
            sb_ref[...] = sb_ref[...] * edge_b[u] + d_s[u]
        return carry

    lax.fori_loop(0, nc // bu, btrip, 0)
    sfin_ref[0, 0, 1] = sb_ref[...]

    sf_ref[...] = s0_ref[0, 0, 0]
    li = lax.broadcasted_iota(jnp.int32, (q, q), 0)
    si = lax.broadcasted_iota(jnp.int32, (q, q), 1)
    lower = si <= li
    eye = si == li
    left = lax.broadcasted_iota(jnp.int32, (q, LANES), 1) < SSD_HEAD_DIM
    fu = yo_ref.shape[0]
    lane0 = 16 * pl.program_id(1)

    su = max(u for u in (4, 2, 1) if nc % u == 0)

    def strip(i, carry):
        cs_ = [su * i + u for u in range(su)]
        rs = [pl.multiple_of(c * q, q) for c in cs_]
        ccs = [cc_ref[pl.ds(r0, q), :] for r0 in rs]
        sp_es = [sp_ref[0, pl.ds(r0, q), LANES:2 * LANES] for r0 in rs]
        y_bs = [_dot(s, e2b_ref[0, 0]) * _dot(ccs[u], sbe_ref[cs_[u]]) for u, s in enumerate(sp_es)]
        e_fs = [_dot(s, e2f_ref[0, 0]) for s in sp_es]
        d_ss = [_dot(bt_ref[c], xwf_ref[pl.ds(rs[u], q), :]) for u, c in enumerate(cs_)]
        for u, c in enumerate(cs_):
            y_off = e_fs[u] * _dot(ccs[u], sf_ref[...].astype(BF16)) + y_bs[u]
            y_ref[0, pl.ds(rs[u], q), :] = y_off.astype(y_ref.dtype)
            sf_ref[...] = sf_ref[...] * edge_ref[c] + d_ss[u]
        return carry

    lax.fori_loop(0, nc // su, strip, 0)

    def chunk(c, slot):
        r0 = pl.multiple_of(c * q, q)
        cc = cc_ref[pl.ds(r0, q), :]
        g = _dot(cc, bt_ref[c])
        col = pltpu.roll(col_ref[0, pl.ds(r0, q), :], LANES - lane0, 1)
        row = row_ref[0, 0, c]
        for j in range(SSD_HPG // 2):
            lanes = slice(j * LANES, (j + 1) * LANES)
            xf = xs_ref[pl.ds(r0, q), lanes].astype(F32)
            x_diag = jnp.concatenate([jnp.where(left, xf, 0.0), jnp.where(left, 0.0, xf)], axis=0).astype(BF16)
            ms = []
            for h in (2 * j, 2 * j + 1):
                cs_l = jnp.take_along_axis(col, jnp.where(lower, h, 8 + h), axis=1, mode="promise_in_bounds")
                arg = cs_l - jnp.where(lower, row[h:h + 1, :], row[8 + h:9 + h, :])
                p = jnp.exp2(jnp.where(eye, row[16 + h:17 + h, :], arg))
                ms.append((g * p).astype(BF16))
            y = (_dot(jnp.concatenate(ms, axis=1), x_diag) + y_ref[0, pl.ds(r0, q), lanes].astype(F32)
                 + dexp_ref[0, :, lanes] * xf)
            y_ref[0, pl.ds(r0, q), lanes] = y.astype(y_ref.dtype)

    def fstep(i, carry):
        for slot in range(fu):
            chunk(fu * i + slot, slot)
        return carry

    lax.fori_loop(0, nc // fu, fstep, 0)
    sfin_ref[0, 0, 0] = sf_ref[...]


def _expanders():
    e = np.zeros((2, SSD_GROUPS, LANES, GROUP_W), np.float32)
    for d in range(2):
        for g in range(SSD_GROUPS):
            for s in range(2):
                for j in range(SSD_HPG):
                    e[d, g, (d * SSD_GROUPS + g) * 16 + s * 8 + j, j * SSD_HEAD_DIM:(j + 1) * SSD_HEAD_DIM] = 1.0
    return jnp.asarray(e, BF16)


def ssd_mixer(proj, xbc_col0, dt_raw, conv_w, conv_b, dt_bias, a_log, d_skip, s0, q):
    b, l, _ = proj.shape
    assert l % q == 0 and q % 16 == 0, (l, q)
    nc = l // q
    gn = SSD_GROUPS
    bias = jnp.zeros((1, LANES), F32).at[0, :2 * SSD_HEADS].set(dt_bias.reshape(-1))
    a = jnp.zeros((1, LANES), F32).at[0, :2 * SSD_HEADS].set(-jnp.exp(a_log.reshape(-1)))
    colg, rowg, sp = dt_prep(dt_raw, bias, a, q)
    dexp = jnp.repeat(d_skip.astype(F32), SSD_HEAD_DIM).reshape(gn, 1, GROUP_W)
    e2 = _expanders()

    cw = conv_w.astype(F32)
    cb = conv_b.astype(F32).reshape(1, -1)
    nb = SSD_INNER
    cwx, cwb, cwc = cw[:, :nb], cw[:, nb:nb + gn * SSD_STATE], cw[:, nb + gn * SSD_STATE:]
    cbx, cbb, cbc = cb[:, :nb], cb[:, nb:nb + gn * SSD_STATE], cb[:, nb + gn * SSD_STATE:]

    fu = max(u for u in (16, 8, 4, 2, 1) if nc % u == 0)
    xo = xbc_col0 // GROUP_W
    bo = (xbc_col0 + SSD_INNER) // SSD_STATE
    co = bo + gn
    st_spec = pl.BlockSpec((1, 1, 2, SSD_STATE, GROUP_W), lambda i, g: (i, g, 0, 0, 0))
    y, sfin = pl.pallas_call(
        functools.partial(_ssd_kernel, l=l, q=q),
        grid=(b, gn),
        in_specs=[pl.BlockSpec((1, l, GROUP_W), lambda i, g: (i, 0, xo + g)),
                  pl.BlockSpec((1, l, SSD_STATE), lambda i, g: (i, 0, bo + g)),
                  pl.BlockSpec((1, l, SSD_STATE), lambda i, g: (i, 0, co + g)),
                  pl.BlockSpec((3, GROUP_W), lambda i, g: (0, g)),
                  pl.BlockSpec((3, SSD_STATE), lambda i, g: (0, g)),
                  pl.BlockSpec((3, SSD_STATE), lambda i, g: (0, g)),
                  pl.BlockSpec((1, GROUP_W), lambda i, g: (0, g)),
                  pl.BlockSpec((1, SSD_STATE), lambda i, g: (0, g)),
                  pl.BlockSpec((1, SSD_STATE), lambda i, g: (0, g)),
                  pl.BlockSpec((1, l, LANES), lambda i, g: (i, 0, 0)),
                  pl.BlockSpec((1, 1, nc, 24, q), lambda i, g: (i, g, 0, 0, 0)),
                  pl.BlockSpec((1, l, 2 * LANES), lambda i, g: (i, 0, 0)),
                  pl.BlockSpec((1, 1, LANES, GROUP_W), lambda i, g: (0, g, 0, 0)),
                  pl.BlockSpec((1, 1, LANES, GROUP_W), lambda i, g: (1, g, 0, 0)),
                  pl.BlockSpec((1, 1, GROUP_W), lambda i, g: (g, 0, 0)),
                  st_spec],
        out_specs=[pl.BlockSpec((1, l, GROUP_W), lambda i, g: (i, 0, g)), st_spec],
        out_shape=[jax.ShapeDtypeStruct((b, l, SSD_INNER), BF16),
                   jax.ShapeDtypeStruct((b, gn, 2, SSD_STATE, GROUP_W), F32)],
        scratch_shapes=[pltpu.VMEM((l, GROUP_W), BF16),
                        pltpu.VMEM((l, SSD_STATE), BF16),
                        pltpu.VMEM((nc, SSD_STATE, q), BF16),
                        pltpu.VMEM((l, GROUP_W), BF16),
                        pltpu.VMEM((nc, SSD_STATE, GROUP_W), BF16),
                        pltpu.VMEM((nc, 1, GROUP_W), F32),
                        pltpu.VMEM((SSD_STATE, GROUP_W), F32),
                        pltpu.VMEM((SSD_STATE, GROUP_W), F32),
                        pltpu.VMEM((fu, 8, LANES), F32)],
        compiler_params=_cparams(("parallel", "parallel")),
        name="ssd",
    )(proj, proj, proj, cwx, cwb, cwc, cbx, cbb, cbc, colg, rowg, sp, e2, e2, dexp, s0)
    return y, sfin


def _mixout_kernel(y_ref, z_ref, gb_ref, gc_ref, hv_ref, g0_ref, g1_ref, x_ref, gate_ref,
                   ng_ref, bg_ref, scw_ref, wssd_ref, wsc_ref, wo_ref, o_ref, *, tm, period):
    ns = 2 if (tm // 2) % period == 0 else 1
    ts = tm // ns
    pos = lax.broadcasted_iota(jnp.int32, (ts, 1), 0) % period
    w = scw_ref[...]
    bg = bg_ref[...]
    for s in range(ns):
        r = slice(s * ts, (s + 1) * ts)
        yz = y_ref[0, r, :].astype(F32) * _silu_t(z_ref[0, r, :].astype(F32))
        ms = jnp.mean(yz * yz, axis=-1, keepdims=True)
        yn = (yz * lax.rsqrt(ms + EPS) * ng_ref[...]).astype(BF16)
        y_ssd = _dot(yn, wssd_ref[...])

        u = gc_ref[0, r, :].astype(F32) * hv_ref[0, r, :].astype(F32)
        u_prev = jnp.where(pos == 0, 0.0, pltpu.roll(u, 1, 0))
        u_next = jnp.where(pos == period - 1, 0.0, pltpu.roll(u, ts - 1, 0))
        v = w[0:1] * u_prev + w[1:2] * u + w[2:3] * u_next
        y_sc = _dot((gb_ref[0, r, :].astype(F32) * v).astype(BF16), wsc_ref[...])

        g0 = _sigmoid_t(g0_ref[0, r, :].astype(F32) + bg[:, :D_MODEL])
        g1 = _sigmoid_t(g1_ref[0, r, :].astype(F32) + bg[:, D_MODEL:])
        out = _dot((g0 * y_ssd + g1 * y_sc).astype(BF16), wo_ref[...])
        o_ref[0, r, :] = x_ref[0, r, :] + gate_ref[0] * out


def mixer_out(y, proj, x, gate, norm_g, b_gate, sc_conv_w, w_ssd, w_sc, w_o, tm, period):
    b, l, d = x.shape
    pc = lambda k: pl.BlockSpec((1, tm, d), lambda i, m, k=k: (i, m, k))
    full = lambda shp: pl.BlockSpec(shp, lambda i, m: (0,) * len(shp))
    return pl.pallas_call(
        functools.partial(_mixout_kernel, tm=tm, period=period),
        grid=(b, l // tm),
        in_specs=[pl.BlockSpec((1, tm, SSD_INNER), lambda i, m: (i, m, 0)),
                  pl.BlockSpec((1, tm, SSD_INNER), lambda i, m: (i, m, 0)),
                  pc(5), pc(6), pc(7), pc(8), pc(9),
                  pl.BlockSpec((1, tm, d), lambda i, m: (i, m, 0)),
                  pl.BlockSpec((1, 1, d), lambda i, m: (i, 0, 0)),
                  full((1, SSD_INNER)), full((1, 2 * d)), full((3, SC_WIDTH)),
                  full((SSD_INNER, d)), full((SC_WIDTH, d)), full((d, d))],
        out_specs=pl.BlockSpec((1, tm, d), lambda i, m: (i, m, 0)),
        out_shape=jax.ShapeDtypeStruct((b, l, d), F32),
        compiler_params=_cparams(("parallel", "parallel")),
        name="mixer_out",
    )(y, proj, proj, proj, proj, proj, proj, x, gate,
      norm_g.reshape(1, -1), b_gate.reshape(1, -1), sc_conv_w, w_ssd, w_sc, w_o)


def _ffn_kernel(x_ref, g_ref, sh_ref, sc_ref, gate_ref, w1_ref, w3_ref, w2_ref, o_ref, *, nf):
    fw = w1_ref.shape[1] // nf
    ts = x_ref.shape[1] // ROW_SUBTILES
    for s in range(ROW_SUBTILES):
        r = slice(s * ts, (s + 1) * ts)
        x = x_ref[0, r, :]
        hb = _norm_mod(x, g_ref[...], sh_ref[0], sc_ref[0]).astype(BF16)
        acc = None
        for k in range(nf):
            a = _dot(hb, w1_ref[:, k * fw:(k + 1) * fw])
            bb = _dot(hb, w3_ref[:, k * fw:(k + 1) * fw])
            part = _dot((_silu_t(a) * bb).astype(BF16), w2_ref[k * fw:(k + 1) * fw, :])
            acc = part if acc is None else acc + part
        o_ref[0, r, :] = x + gate_ref[0] * acc


def ffn_dense(x, g, shift, scale, gate, w1, w3, w2, tm):
    b, l, d = x.shape
    f = w1.shape[1]
    vec = pl.BlockSpec((1, 1, d), lambda i, m: (i, 0, 0))
    const = lambda shp: pl.BlockSpec(shp, lambda i, m: (0, 0), pipeline_mode=pl.Buffered(1))
    return pl.pallas_call(
        functools.partial(_ffn_kernel, nf=2),
        grid=(b, l // tm),
        in_specs=[pl.BlockSpec((1, tm, d), lambda i, m: (i, m, 0)),
                  pl.BlockSpec((1, d), lambda i, m: (0, 0)),
                  vec, vec, vec, const((d, f)), const((d, f)), const((f, d))],
        out_specs=pl.BlockSpec((1, tm, d), lambda i, m: (i, m, 0)),
        out_shape=jax.ShapeDtypeStruct((b, l, d), F32),
        compiler_params=_cparams(("parallel", "parallel")),
        name="ffn_dense",
    )(x, g.reshape(1, d), shift, scale, gate, w1, w3, w2)


def _router_kernel(x_ref, g_ref, sh_ref, sc_ref, rw_ref, h_ref, route_ref, route_t_ref, cnt_ref, run_ref, *, tm):
    @pl.when((pl.program_id(0) == 0) & (pl.program_id(1) == 0))
    def _():
        run_ref[...] = jnp.zeros_like(run_ref)

    h = _norm_mod(x_ref[0], g_ref[...], sh_ref[0], sc_ref[0])
    h_hi, h_lo = _split2(h)
    h_ref[0] = h
    w_hi, w_lo = _split2(rw_ref[...])
    logits = _dot(h_hi, w_hi) + _dot(h_lo, w_hi) + _dot(h_hi, w_lo)
    lane = lax.broadcasted_iota(jnp.int32, (tm, LANES), 1)
    ninf = float("-inf")
    lg = jnp.where(lane < N_EXPERTS, logits, ninf)
    m1 = jnp.max(lg, axis=1, keepdims=True)
    i1 = jnp.min(jnp.where(lg == m1, lane, LANES), axis=1, keepdims=True)
    lg2 = jnp.where(lane == i1, ninf, lg)
    m2 = jnp.max(lg2, axis=1, keepdims=True)
    i2 = jnp.min(jnp.where(lg2 == m2, lane, LANES), axis=1, keepdims=True)
    e2 = jnp.exp(m2 - m1)
    den = 1.0 + e2
    sel1 = jnp.where(lane == i1, 1.0, 0.0)
    sel2 = jnp.where(lane == i2, 1.0, 0.0)
    cnt = sel1 + sel2
    r = lax.broadcasted_iota(jnp.int32, (tm, tm), 0)
    c = lax.broadcasted_iota(jnp.int32, (tm, tm), 1)
    tri = jnp.where(c < r, 1.0, 0.0).astype(BF16)
    base = _dot(tri, cnt.astype(BF16)) + run_ref[0:1, :]
    r1 = jnp.sum(sel1 * base, axis=1, keepdims=True)
    r2 = jnp.sum(sel2 * base, axis=1, keepdims=True)
    vals = (i1.astype(F32), i2.astype(F32), 1.0 / den, e2 / den, r1, r2)
    out = jnp.zeros((tm, LANES), F32)
    for k, v in enumerate(vals):
        out = jnp.where(lane == k, v, out)
    route_ref[0] = out
    route_t_ref[...] = out.T
    new_run = run_ref[...] + jnp.sum(cnt, axis=0, keepdims=True)
    run_ref[...] = new_run
    cnt_ref[...] = new_run


def router(x, g, shift, scale, router_w, tm):
    b, l, d = x.shape
    rw = jnp.zeros((d, LANES), F32).at[:, :N_EXPERTS].set(router_w)
    vec = pl.BlockSpec((1, 1, d), lambda i, m: (i, 0, 0))
    mt = l // tm
    return pl.pallas_call(
        functools.partial(_router_kernel, tm=tm),
        grid=(b, l // tm),
        in_specs=[pl.BlockSpec((1, tm, d), lambda i, m: (i, m, 0)),
                  pl.BlockSpec((1, d), lambda i, m: (0, 0)),
                  vec, vec, pl.BlockSpec((d, LANES), lambda i, m: (0, 0))],
        out_specs=[pl.BlockSpec((1, tm, d), lambda i, m: (i, m, 0)),
                   pl.BlockSpec((1, tm, LANES), lambda i, m: (i, m, 0)),
                   pl.BlockSpec((LANES, tm), lambda i, m: (0, i * mt + m)),
                   pl.BlockSpec((8, LANES), lambda i, m: (0, 0))],
        out_shape=[jax.ShapeDtypeStruct((b, l, d), F32),
                   jax.ShapeDtypeStruct((b, l, LANES), F32),
                   jax.ShapeDtypeStruct((LANES, b * l), F32),
                   jax.ShapeDtypeStruct((8, LANES), F32)],
        scratch_shapes=[pltpu.VMEM((8, LANES), F32)],
        compiler_params=_cparams(("arbitrary", "arbitrary")),
        name="router",
    )(x, g.reshape(1, d), shift, scale, rw)


def _pack_bf16_pairs(y):
    k = y.shape[1] // 2
    bits = lax.bitcast_convert_type(y.astype(BF16).astype(F32), jnp.uint32)
    return bits[:, :k] | (bits[:, k:] >> 16)


def _unpack_bf16_pairs(p):
    hi = lax.bitcast_convert_type(p & jnp.uint32(0xFFFF0000), F32)
    lo = lax.bitcast_convert_type(p << 16, F32)
    return jnp.concatenate([hi, lo], axis=1)


def _gffn_kernel(te_ref, nv_ref, x_ref, w1_ref, w3_ref, w2_ref, o_ref, acc_ref, *, nf):
    i = pl.program_id(0)
    f = pl.program_id(1)

    @pl.when(i < nv_ref[0])
    def _():
        ts = x_ref.shape[0] // ROW_SUBTILES
        parts = []
        for s in range(ROW_SUBTILES):
            x = x_ref[s * ts:(s + 1) * ts, :].astype(BF16)
            a = _dot(x, w1_ref[0])
            bb = _dot(x, w3_ref[0])
            parts.append(_dot((_silu_t(a) * bb).astype(BF16), w2_ref[0]))
        part = jnp.concatenate(parts, axis=0)

        @pl.when(f == 0)
        def _():
            acc_ref[...] = part

        if nf > 2:
            @pl.when((f > 0) & (f < nf - 1))
            def _():
                acc_ref[...] += part

        @pl.when(f == nf - 1)
        def _():
            o_ref[...] = _pack_bf16_pairs(part if nf == 1 else acc_ref[...] + part)


def grouped_ffn(xs, tile_expert, n_valid, w1, w3, w2, tm, nf):
    rows, d = xs.shape
    nt = rows // tm
    f = w1.shape[2]
    fw = f // nf

    def tile(i, nv):
        return jnp.minimum(i, nv[0] - 1)

    def fchunk(i, k, nv):
        return jnp.where(i < nv[0], k, nf - 1)

    grid_spec = pltpu.PrefetchScalarGridSpec(
        num_scalar_prefetch=2,
        grid=(nt, nf),
        in_specs=[pl.BlockSpec((tm, d), lambda i, k, te, nv: (tile(i, nv), 0)),
                  pl.BlockSpec((1, d, fw), lambda i, k, te, nv: (te[i], 0, fchunk(i, k, nv))),
                  pl.BlockSpec((1, d, fw), lambda i, k, te, nv: (te[i], 0, fchunk(i, k, nv))),
                  pl.BlockSpec((1, fw, d), lambda i, k, te, nv: (te[i], fchunk(i, k, nv), 0))],
        out_specs=pl.BlockSpec((tm, d // 2), lambda i, k, te, nv: (tile(i, nv), 0)),
        scratch_shapes=[pltpu.VMEM((tm, d), F32)],
    )
    return pl.pallas_call(
        functools.partial(_gffn_kernel, nf=nf),
        grid_spec=grid_spec,
        out_shape=jax.ShapeDtypeStruct((rows, d // 2), jnp.uint32),
        compiler_params=_cparams(("arbitrary", "arbitrary")),
        name="grouped_ffn",
    )(tile_expert, n_valid, xs, w1, w3, w2)


def _combine_kernel(x_ref, y0_ref, y1_ref, route_ref, gate_ref, fg_ref, o_ref):
    r = route_ref[0]
    moe = r[:, 2:3] * _unpack_bf16_pairs(y0_ref[0, 0]) + r[:, 3:4] * _unpack_bf16_pairs(y1_ref[0, 0])
    xn = x_ref[0] + gate_ref[0] * moe
    ms = jnp.mean(xn * xn, axis=-1, keepdims=True)
    o_ref[0] = xn * lax.rsqrt(ms + EPS) * fg_ref[...]


def combine_final(x, yg, route, gate, final_g, tm):
    b, l, d = x.shape
    return pl.pallas_call(
        _combine_kernel,
        grid=(b, l // tm),
        in_specs=[pl.BlockSpec((1, tm, d), lambda i, m: (i, m, 0)),
                  pl.BlockSpec((1, 1, tm, d // 2), lambda i, m: (0, i, m, 0)),
                  pl.BlockSpec((1, 1, tm, d // 2), lambda i, m: (1, i, m, 0)),
                  pl.BlockSpec((1, tm, LANES), lambda i, m: (i, m, 0)),
                  pl.BlockSpec((1, 1, d), lambda i, m: (i, 0, 0)),
                  pl.BlockSpec((1, d), lambda i, m: (0, 0))],
        out_specs=pl.BlockSpec((1, tm, d), lambda i, m: (i, m, 0)),
        out_shape=jax.ShapeDtypeStruct((b, l, d), F32),
        compiler_params=_cparams(("parallel", "parallel")),
        name="combine_final",
    )(x, yg, yg, route, gate, final_g.reshape(1, d))


SC_CORES = 2
SC_SUBCORES = 16
SC_WORKERS = SC_CORES * SC_SUBCORES
SC_STREAM_BYTES = 256 * 1024
SC_STREAM_ROWS = 128


def _sc_rows(per_worker, d, dtype):
    return min(SC_STREAM_ROWS, SC_STREAM_BYTES // (d * jnp.dtype(dtype).itemsize), per_worker)


def _sc_mesh():
    return plsc.VectorSubcoreMesh(core_axis_name="c", subcore_axis_name="s",
                                  num_cores=SC_CORES, num_subcores=SC_SUBCORES)


def dispatch_rows(h, pos0, pos1, n_rows):
    t, d = h.shape
    per_w = t // SC_WORKERS
    ch = _sc_rows(per_w, d, h.dtype)
    assert t % SC_WORKERS == 0 and per_w % ch == 0 and ch % 8 == 0, (t, ch)

    @functools.partial(
        pl.kernel, mesh=_sc_mesh(),
        out_type=jax.ShapeDtypeStruct((n_rows, d), h.dtype),
        scratch_types=[pltpu.VMEM((ch,), jnp.int32), pltpu.VMEM((ch,), jnp.int32),
                       pltpu.VMEM((ch, d), h.dtype), pltpu.SemaphoreType.DMA],
        name="moe_dispatch")
    def scatter(h_hbm, p0_hbm, p1_hbm, out_hbm, i0_v, i1_v, rows_v, sem):
        base = (lax.axis_index("s") * SC_CORES + lax.axis_index("c")) * per_w

        @pl.loop(0, per_w // ch)
        def _(j):
            off = base + j * ch
            pltpu.sync_copy(h_hbm.at[pl.ds(off, ch)], rows_v)
            pltpu.sync_copy(p0_hbm.at[pl.ds(off, ch)], i0_v)
            pltpu.sync_copy(p1_hbm.at[pl.ds(off, ch)], i1_v)
            pltpu.async_copy(rows_v, out_hbm.at[i0_v], sem).wait()
            pltpu.async_copy(rows_v, out_hbm.at[i1_v], sem).wait()

    return scatter(h, pos0, pos1)


def return_rows(ys, idx):
    n = idx.shape[0]
    d = ys.shape[1]
    per_w = n // SC_WORKERS
    ch = _sc_rows(per_w, d, ys.dtype)
    assert n % SC_WORKERS == 0 and per_w % ch == 0 and ch % 8 == 0, (n, ch)

    @functools.partial(
        pl.kernel, mesh=_sc_mesh(),
        out_type=jax.ShapeDtypeStruct((n, d), ys.dtype),
        scratch_types=[pltpu.VMEM((ch,), jnp.int32), pltpu.VMEM((ch, d), ys.dtype), pltpu.SemaphoreType.DMA],
        name="moe_return")
    def gather(ys_hbm, idx_hbm, out_hbm, idx_v, rows_v, sem):
        base = (lax.axis_index("s") * SC_CORES + lax.axis_index("c")) * per_w

        @pl.loop(0, per_w // ch)
        def _(j):
            off = base + j * ch
            pltpu.sync_copy(idx_hbm.at[pl.ds(off, ch)], idx_v)
            pltpu.async_copy(ys_hbm.at[idx_v], rows_v, sem).wait()
            pltpu.sync_copy(rows_v, out_hbm.at[pl.ds(off, ch)])

    return gather(ys, idx)


def moe_block(x, g, shift, scale, gate, router_w, w1, w3, w2, final_g):
    b, l, d = x.shape
    t = b * l
    tm = MOE_TM
    h, route, route_t, counts = router(x, g, shift, scale, router_w, _row_tile(l, ROW_TM))
    cnt = counts[0, :N_EXPERTS].astype(jnp.int32)
    gs = ((cnt + tm - 1) // tm) * tm
    ends = jnp.cumsum(gs)
    offs = ends - gs

    def sorted_row(choice):
        e = route_t[choice].astype(jnp.int32)
        start = sum(jnp.where(e == k, offs[k], 0) for k in range(N_EXPERTS))
        return start + route_t[4 + choice].astype(jnp.int32)

    pos0, pos1 = sorted_row(0), sorted_row(1)
    nt = (2 * t) // tm + N_EXPERTS
    n_valid = (ends[-1] // tm).astype(jnp.int32).reshape(1)
    tile = jnp.minimum(jnp.arange(nt, dtype=jnp.int32), n_valid[0] - 1)
    tile_expert = jnp.sum((tile[:, None] >= (ends // tm)[None, :]).astype(jnp.int32), axis=1)
    xs = dispatch_rows(h.reshape(t, d), pos0, pos1, nt * tm)
    ys = grouped_ffn(xs, tile_expert, n_valid, w1, w3, w2, tm, 2)
    yg = return_rows(ys, jnp.concatenate([pos0, pos1])).reshape(2, b, l, d // 2)
    return combine_final(x, yg, route, gate, final_g, _row_tile(l, 2 * ROW_TM))


def _in_weights(w_in):
    o1 = SSD_INNER
    o2 = o1 + XBC_WIDTH
    o3 = o2 + 2 * SSD_HEADS
    w_main = jnp.concatenate([w_in[:, :o2], w_in[:, o3:]], axis=1).astype(BF16)
    w_dt = jnp.pad(w_in[:, o2:o3], ((0, 0), (0, LANES - 2 * SSD_HEADS))).astype(BF16)
    return w_main, w_dt


def kernel(x, c, ctx, c_ctx, w_mod, b_mod, norm1_g, norm2_g, w_in, b_gate, ssd_conv_w, ssd_conv_b, ssd_dt_bias, ssd_a_log, ssd_d, ssd_norm_g, w_ssd_out, sc_conv_w, w_sc_out, w_o, ffn_w1, ffn_w3, ffn_w2, router_w, moe_w1, moe_w3, moe_w2, final_g):
    b, l, d = x.shape
    lc = ctx.shape[1]
    depth = w_mod.shape[0]
    assert depth % 2 == 0, "the final norm is fused into the routed channel mixer of the (odd) last layer"
    cc = jnp.zeros((16, d), F32).at[:b].set(c).at[b].set(c_ctx)
    mod = modulation(cc, w_mod, b_mod)
    zeros_state = jnp.zeros((b, SSD_GROUPS, 2, SSD_STATE, GROUP_W), F32)
    nctx = b * lc
    ctx = ctx.reshape(1, nctx, d)
    tmc = _row_tile(nctx, ROW_TM, lc)
    tmx = _row_tile(l, ROW_TM, GRID_W)

    def per_seq(t):
        return t.reshape(b, lc, t.shape[-1])

    for i in range(depth):
        last = i == depth - 1
        mx = mod[i, :b].reshape(b, N_MOD, 1, d)
        mc = mod[i, b].reshape(1, N_MOD, 1, d)
        w_main, w_dt = _in_weights(w_in[i])
        ssd_p = (ssd_conv_w[i], ssd_conv_b[i], ssd_dt_bias[i], ssd_a_log[i], ssd_d[i])
        out_p = (ssd_norm_g[i], b_gate[i], sc_conv_w[i], w_ssd_out[i].astype(BF16),
                 w_sc_out[i].astype(BF16), w_o[i].astype(BF16))

        if last:
            w_xbc = w_main[:, SSD_INNER:SSD_INNER + XBC_WIDTH]
            proj_c, dt_c = in_proj(ctx, norm1_g[i], mc[:, 0], mc[:, 1], w_xbc, w_dt,
                                   _row_tile(nctx, PROJ_TM), PROJ_TN_XBC)
            _, s_ctx = ssd_mixer(per_seq(proj_c), 0, per_seq(dt_c), *ssd_p, zeros_state, SSD_Q)
        else:
            proj_c, dt_c = in_proj(ctx, norm1_g[i], mc[:, 0], mc[:, 1], w_main, w_dt,
                                   _row_tile(nctx, PROJ_TM), PROJ_TN)
            y_c, s_ctx = ssd_mixer(per_seq(proj_c), SSD_INNER, per_seq(dt_c), *ssd_p, zeros_state, SSD_Q)
            ctx = mixer_out(y_c.reshape(1, nctx, SSD_INNER), proj_c, ctx, mc[:, 2], *out_p, tmc, lc)

        proj_x, dt_x = in_proj(x, norm1_g[i], mx[:, 0], mx[:, 1], w_main, w_dt, _row_tile(l, PROJ_TM), PROJ_TN)
        y_x, _ = ssd_mixer(proj_x, SSD_INNER, dt_x, *ssd_p, s_ctx, SSD_Q)
        x = mixer_out(y_x, proj_x, x, mx[:, 2], *out_p, tmx, GRID_W)

        j = i // 2
        if i % 2 == 0:
            w1, w3, w2 = ffn_w1[j].astype(BF16), ffn_w3[j].astype(BF16), ffn_w2[j].astype(BF16)
            x = ffn_dense(x, norm2_g[i], mx[:, 3], mx[:, 4], mx[:, 5], w1, w3, w2, _row_tile(l, FFN_TM))
            if not last:
                ctx = ffn_dense(ctx, norm2_g[i], mc[:, 3], mc[:, 4], mc[:, 5], w1, w3, w2, tmc)
        else:
            assert last, "the routed channel mixer is fused with the final norm"
            w1, w3, w2 = moe_w1[j].astype(BF16), moe_w3[j].astype(BF16), moe_w2[j].astype(BF16)
            x = moe_block(x, norm2_g[i], mx[:, 3], mx[:, 4], mx[:, 5], router_w[j], w1, w3, w2, final_g)
    return x
```

```python
import functools

import numpy as np
import jax
import jax.numpy as jnp
from jax import lax
from jax.experimental import pallas as pl
from jax.experimental.pallas import tpu as pltpu
from jax.experimental.pallas import tpu_sc as plsc

F32 = jnp.float32
BF16 = jnp.bfloat16

D_MODEL = 1024
GRID_W = 64
SSD_INNER = 2048
SSD_HEADS = 32
SSD_GROUPS = 4
SSD_HPG = 8
SSD_HEAD_DIM = 64
SSD_STATE = 128
GROUP_W = SSD_HPG * SSD_HEAD_DIM
XBC_WIDTH = SSD_INNER + 2 * SSD_GROUPS * SSD_STATE
SC_WIDTH = 1024
N_MOD = 6
N_EXPERTS = 8
EPS = 1e-6
LOG2E = 1.4426950408889634

LANES = 128
SSD_Q = 128
MOE_TM = 512
ROW_SUBTILES = 2
PROJ_TM = 1024
PROJ_TN = 2560
PROJ_TN_XBC = 1024
ROW_TM = 512
FFN_TM = 1024
VMEM_LIMIT = 56 * 1024 * 1024


def _row_tile(rows, want, multiple=1):
    t = multiple * max(1, min(want, rows) // multiple)
    assert rows % t == 0, (rows, t)
    return t


def _dot(a, b):
    return jnp.dot(a, b, preferred_element_type=F32)


def _sigmoid(v):
    return 1.0 / (1.0 + jnp.exp(-v))


def _silu(v):
    return v * _sigmoid(v)


def _sigmoid_t(v):
    return 0.5 + 0.5 * jnp.tanh(0.5 * v)


def _silu_t(v):
    hv = 0.5 * v
    return hv + hv * jnp.tanh(hv)


def _split2(a):
    hi = a.astype(BF16)
    lo = (a - hi.astype(F32)).astype(BF16)
    return hi, lo


def _split3(a):
    hi = a.astype(BF16)
    r = a - hi.astype(F32)
    mid = r.astype(BF16)
    lo = (r - mid.astype(F32)).astype(BF16)
    return hi, mid, lo


def _norm_mod(x, g, shift, scale):
    ms = jnp.mean(x * x, axis=-1, keepdims=True)
    return (x * lax.rsqrt(ms + EPS) * g) * (1.0 + scale) + shift


def _cparams(sem, vmem=VMEM_LIMIT):
    return pltpu.CompilerParams(dimension_semantics=sem, vmem_limit_bytes=vmem)


def _mod_kernel(c_ref, w_ref, b_ref, o_ref):
    a_hi, a_lo = _split2(_silu(c_ref[...]))
    w_hi, w_lo = _split2(w_ref[0])
    o_ref[0] = _dot(a_hi, w_hi) + _dot(a_lo, w_hi) + _dot(a_hi, w_lo) + b_ref[0]


def modulation(cc, w_mod, b_mod):
    depth, d, n = w_mod.shape
    tn = 1536
    return pl.pallas_call(
        _mod_kernel,
        grid=(depth, n // tn),
        in_specs=[pl.BlockSpec((16, d), lambda i, j: (0, 0)),
                  pl.BlockSpec((1, d, tn), lambda i, j: (i, 0, j)),
                  pl.BlockSpec((1, 1, tn), lambda i, j: (i, 0, j))],
        out_specs=pl.BlockSpec((1, 16, tn), lambda i, j: (i, 0, j)),
        out_shape=jax.ShapeDtypeStruct((depth, 16, n), F32),
        compiler_params=_cparams(("parallel", "parallel")),
        name="modulation",
    )(cc, w_mod, b_mod.reshape(depth, 1, n))


def _inproj_kernel(x_ref, g_ref, sh_ref, sc_ref, w_ref, wdt_ref, o_ref, dt_ref, h_ref):
    @pl.when(pl.program_id(2) == 0)
    def _():
        hb = _norm_mod(x_ref[0], g_ref[...], sh_ref[0], sc_ref[0]).astype(BF16)
        h_ref[...] = hb
        dt_ref[0] = _dot(hb, wdt_ref[...])

    o_ref[0] = _dot(h_ref[...], w_ref[...]).astype(o_ref.dtype)


def in_proj(x, g, shift, scale, w, wdt, tm, tn):
    b, l, d = x.shape
    n = w.shape[1]
    return pl.pallas_call(
        _inproj_kernel,
        grid=(b, l // tm, n // tn),
        in_specs=[pl.BlockSpec((1, tm, d), lambda i, m, j: (i, m, 0)),
                  pl.BlockSpec((1, d), lambda i, m, j: (0, 0)),
                  pl.BlockSpec((1, 1, d), lambda i, m, j: (i, 0, 0)),
                  pl.BlockSpec((1, 1, d), lambda i, m, j: (i, 0, 0)),
                  pl.BlockSpec((d, tn), lambda i, m, j: (0, j)),
                  pl.BlockSpec((d, LANES), lambda i, m, j: (0, 0))],
        out_specs=[pl.BlockSpec((1, tm, tn), lambda i, m, j: (i, m, j)),
                   pl.BlockSpec((1, tm, LANES), lambda i, m, j: (i, m, 0))],
        out_shape=[jax.ShapeDtypeStruct((b, l, n), BF16),
                   jax.ShapeDtypeStruct((b, l, LANES), F32)],
        scratch_shapes=[pltpu.VMEM((tm, d), BF16)],
        compiler_params=_cparams(("parallel", "parallel", "arbitrary")),
        name="in_proj",
    )(x, g.reshape(1, d), shift, scale, w, wdt)


def _dt_kernel(raw_ref, bias_ref, a_ref, pc_ref, ph_ref, plo_ref, col_ref, row_ref, sp_ref, *, q, ch):
    row = lax.broadcasted_iota(jnp.int32, (q, q), 0)
    col = lax.broadcasted_iota(jnp.int32, (q, q), 1)
    tri_l = jnp.where(col <= row, 1.0, 0.0).astype(BF16)
    tri_u = jnp.where(col >= row, 1.0, 0.0).astype(BF16)
    lane = lax.broadcasted_iota(jnp.int32, (q, LANES), 1)
    fwd = lane < SSD_HEADS
    ph, plo = ph_ref[...], plo_ref[...]

    def place_split(t):
        t_hi, t_lo = _split2(t)
        return (_dot(t_hi, ph) + _dot(t_lo, plo)).astype(BF16)

    v = raw_ref[0] + bias_ref[...]
    dt = jnp.maximum(v, 0.0) + jnp.log1p(jnp.exp(-jnp.abs(v)))
    d1, d2, d3 = _split3(dt * a_ref[...])
    cs_parts, tot_parts = [], []
    for k in range(ch):
        sl = slice(k * q, (k + 1) * q)
        cs_f = _dot(tri_l, d1[sl]) + _dot(tri_l, d2[sl]) + _dot(tri_l, d3[sl])
        cs_b = _dot(tri_u, d1[sl]) + _dot(tri_u, d2[sl]) + _dot(tri_u, d3[sl])
        cs_parts.append(jnp.where(fwd, cs_f, cs_b))
        tot = jnp.where(fwd[0:1], cs_f[q - 1:q, :], cs_b[0:1, :])
        tot_parts.append(jnp.broadcast_to(tot, (q, LANES)))
    cs = jnp.concatenate(cs_parts, axis=0)
    tot = jnp.concatenate(tot_parts, axis=0)
    cs2 = cs * LOG2E
    c1, c2, c3 = _split3(cs2)
    pc = pc_ref[...]
    col_ref[0] = _dot(c1, pc) + _dot(c2, pc) + _dot(c3, pc)
    r = cs2 - jnp.log(dt) * LOG2E
    for k in range(ch):
        sl = slice(k * q, (k + 1) * q)
        r_t = r[sl].T
        dt_t = dt[sl].T
        for g in range(SSD_GROUPS):
            lo = g * SSD_HPG
            dt_f = dt_t[lo:lo + 8, :]
            dt_b = dt_t[SSD_HEADS + lo:SSD_HEADS + lo + 8, :]
            row_ref[0, g, k, 0:8, :] = r_t[lo:lo + 8, :]
            row_ref[0, g, k, 8:16, :] = r_t[SSD_HEADS + lo:SSD_HEADS + lo + 8, :]
            row_ref[0, g, k, 16:24, :] = jnp.log(dt_f + dt_b) * LOG2E
    sp_ref[0, :, 0:LANES] = place_split(dt * jnp.exp(tot - cs))
    sp_ref[0, :, LANES:2 * LANES] = place_split(jnp.exp(cs))


def _placements():
    pc = np.zeros((LANES, LANES), np.float32)
    ph = np.zeros((LANES, LANES), np.float32)
    plo = np.zeros((LANES, LANES), np.float32)
    for d in range(2):
        for g in range(SSD_GROUPS):
            for j in range(SSD_HPG):
                src = d * SSD_HEADS + g * SSD_HPG + j
                pc[src, g * 16 + d * 8 + j] = 1.0
                ph[src, (d * SSD_GROUPS + g) * 16 + j] = 1.0
                plo[src, (d * SSD_GROUPS + g) * 16 + 8 + j] = 1.0
    return jnp.asarray(pc, BF16), jnp.asarray(ph, BF16), jnp.asarray(plo, BF16)


def dt_prep(raw, bias, a, q):
    b, l, _ = raw.shape
    nc = l // q
    ch = min(8, nc)
    gn = SSD_GROUPS
    vspec = pl.BlockSpec((1, LANES), lambda i, c: (0, 0))
    pspec = pl.BlockSpec((LANES, LANES), lambda i, c: (0, 0))
    return pl.pallas_call(
        functools.partial(_dt_kernel, q=q, ch=ch),
        grid=(b, nc // ch),
        in_specs=[pl.BlockSpec((1, ch * q, LANES), lambda i, c: (i, c, 0)), vspec, vspec, pspec, pspec, pspec],
        out_specs=[pl.BlockSpec((1, ch * q, LANES), lambda i, c: (i, c, 0)),
                   pl.BlockSpec((1, gn, ch, 24, q), lambda i, c: (i, 0, c, 0, 0)),
                   pl.BlockSpec((1, ch * q, 2 * LANES), lambda i, c: (i, c, 0))],
        out_shape=[jax.ShapeDtypeStruct((b, l, LANES), F32),
                   jax.ShapeDtypeStruct((b, gn, nc, 24, q), F32),
                   jax.ShapeDtypeStruct((b, l, 2 * LANES), BF16)],
        compiler_params=_cparams(("parallel", "parallel")),
        name="dt_prep",
    )(raw, bias, a, *_placements())


def _ssd_kernel(xp_ref, bp_ref, cp_ref, cwx_ref, cwb_ref, cwc_ref, cbx_ref, cbb_ref, cbc_ref,
                col_ref, row_ref, sp_ref, e2f_ref, e2b_ref, dexp_ref, s0_ref,
                y_ref, sfin_ref,
                xs_ref, cc_ref, bt_ref, xwf_ref, sbe_ref, edge_ref, sf_ref, sb_ref, yo_ref, *, l, q):
    nc = l // q
    sr = lax.broadcasted_iota(jnp.int32, (q, q + 32), 0)
    sc = lax.broadcasted_iota(jnp.int32, (q, q + 32), 1)
    shift_prev = jnp.where((sc == sr - 1) | ((sr == 0) & (sc == q + 15)), 1.0, 0.0).astype(BF16)
    shift_next = jnp.where(((sc == sr + 1) & (sc < q)) | ((sr == q - 1) & (sc == q + 16)), 1.0, 0.0).astype(BF16)

    def conv_taps(srcs, k):
        r0 = pl.multiple_of(k * q, q)
        p0 = pl.multiple_of(jnp.maximum(r0 - 16, 0), 16)
        n0 = pl.multiple_of(jnp.minimum(r0 + q, l - 16), 16)

        def rows_at(start, n):
            parts = [r[0, pl.ds(start, n), :] for r in srcs]
            return parts[0] if len(parts) == 1 else jnp.concatenate(parts, axis=1)

        blk = rows_at(r0, q)
        before = rows_at(p0, 16)
        after = rows_at(n0, 16)
        before = jnp.where(k > 0, before, jnp.zeros_like(before))
        after = jnp.where(k < nc - 1, after, jnp.zeros_like(after))
        stacked = jnp.concatenate([blk, before, after], axis=0)
        return _dot(shift_prev, stacked), blk, _dot(shift_next, stacked)

    def conv_silu(taps, w, b):
        return _silu_t(w[0:1] * taps[0] + w[1:2] * taps[1].astype(F32) + w[2:3] * taps[2] + b)

    sb_ref[...] = s0_ref[0, 0, 1]
    bu = max(u for u in (8, 4, 2, 1) if nc % u == 0)
    w_bc = jnp.concatenate([cwb_ref[...], cwc_ref[...]], axis=1)
    b_bc = jnp.concatenate([cbb_ref[...], cbc_ref[...]], axis=1)

    def btrip(i, carry):
        ks = [nc - 1 - (bu * i + u) for u in range(bu)]
        rs = [pl.multiple_of(k * q, q) for k in ks]
        x_taps = [conv_taps((xp_ref,), k) for k in ks]
        bc_taps = [conv_taps((bp_ref, cp_ref), k) for k in ks]
        sp_w = [sp_ref[0, pl.ds(r0, q), 0:LANES] for r0 in rs]
        w_f = [_dot(s, e2f_ref[0, 0]) for s in sp_w]
        w_b = [_dot(s, e2b_ref[0, 0]) for s in sp_w]
        edge_b = []
        for k, r0 in zip(ks, rs):
            e_last = sp_ref[0, pl.ds(pl.multiple_of(r0 + q - 16, 16), 16), LANES:2 * LANES]
            e_first = sp_ref[0, pl.ds(r0, 16), LANES:2 * LANES]
            edge_ref[k] = _dot(e_last, e2f_ref[0, 0])[15:16, :]
            edge_b.append(_dot(e_first, e2b_ref[0, 0])[0:1, :])
        xs = [conv_silu(t, cwx_ref[...], cbx_ref[...]) for t in x_taps]
        bcs = [conv_silu(t, w_bc, b_bc) for t in bc_taps]
        b_ts, xw_bs = [], []
        for u, (k, r0) in enumerate(zip(ks, rs)):
            xs_ref[pl.ds(r0, q), :] = xs[u].astype(BF16)
            b_t = bcs[u][:, :SSD_STATE].T.astype(BF16)
            bt_ref[k] = b_t
            b_ts.append(b_t)
            cc_ref[pl.ds(r0, q), :] = bcs[u][:, SSD_STATE:].astype(BF16)
            xwf_ref[pl.ds(r0, q), :] = (xs[u] * w_f[u]).astype(BF16)
            xw_bs.append((xs[u] * w_b[u]).astype(BF16))
        d_s = [_dot(b_ts[u], xw_bs[u]) for u in range(bu)]
        for u, k in enumerate(ks):
            sbe_ref[k] = sb_ref[...].astype(BF16)
            sb_ref[...] = sb_ref[...] * edge_b[u] + d_s[u]
        return carry

    lax.fori_loop(0, nc // bu, btrip, 0)
    sfin_ref[0, 0, 1] = sb_ref[...]

    sf_ref[...] = s0_ref[0, 0, 0]
    li = lax.broadcasted_iota(jnp.int32, (q, q), 0)
    si = lax.broadcasted_iota(jnp.int32, (q, q), 1)
    lower = si <= li
    eye = si == li
    left = lax.broadcasted_iota(jnp.int32, (q, LANES), 1) < SSD_HEAD_DIM
    fu = yo_ref.shape[0]
    lane0 = 16 * pl.program_id(1)

    def chunk(c, slot):
        r0 = pl.multiple_of(c * q, q)
        cc = cc_ref[pl.ds(r0, q), :]
        sp_e = sp_ref[0, pl.ds(r0, q), LANES:2 * LANES]
        yo_ref[slot] = (_dot(sp_e, e2f_ref[0, 0]) * _dot(cc, sf_ref[...].astype(BF16))
                        + _dot(sp_e, e2b_ref[0, 0]) * _dot(cc, sbe_ref[c]))
        sf_ref[...] = sf_ref[...] * edge_ref[c] + _dot(bt_ref[c], xwf_ref[pl.ds(r0, q), :])
        g = _dot(cc, bt_ref[c])
        col = pltpu.roll(col_ref[0, pl.ds(r0, q), :], LANES - lane0, 1)
        row = row_ref[0, 0, c]
        for j in range(SSD_HPG // 2):
            lanes = slice(j * LANES, (j + 1) * LANES)
            ms = []
            for h in (2 * j, 2 * j + 1):
                cs_l = jnp.take_along_axis(col, jnp.where(lower, h, 8 + h), axis=1, mode="promise_in_bounds")
                arg = cs_l - jnp.where(lower, row[h:h + 1, :], row[8 + h:9 + h, :])
                p = jnp.exp2(jnp.where(eye, row[16 + h:17 + h, :], arg))
                ms.append((g * p).astype(BF16))
            xf = xs_ref[pl.ds(r0, q), lanes].astype(F32)
            x_diag = jnp.concatenate([jnp.where(left, xf, 0.0), jnp.where(left, 0.0, xf)], axis=0).astype(BF16)
            y = _dot(jnp.concatenate(ms, axis=1), x_diag) + yo_ref[slot, :, lanes] + dexp_ref[0, :, lanes] * xf
            y_ref[0, pl.ds(r0, q), lanes] = y.astype(y_ref.dtype)

    def fstep(i, carry):
        for slot in range(fu):
            chunk(fu * i + slot, slot)
        return carry

    lax.fori_loop(0, nc // fu, fstep, 0)
    sfin_ref[0, 0, 0] = sf_ref[...]


def _expanders():
    e = np.zeros((2, SSD_GROUPS, LANES, GROUP_W), np.float32)
    for d in range(2):
        for g in range(SSD_GROUPS):
            for s in range(2):
                for j in range(SSD_HPG):
                    e[d, g, (d * SSD_GROUPS + g) * 16 + s * 8 + j, j * SSD_HEAD_DIM:(j + 1) * SSD_HEAD_DIM] = 1.0
    return jnp.asarray(e, BF16)


def ssd_mixer(proj, xbc_col0, dt_raw, conv_w, conv_b, dt_bias, a_log, d_skip, s0, q):
    b, l, _ = proj.shape
    assert l % q == 0 and q % 16 == 0, (l, q)
    nc = l // q
    gn = SSD_GROUPS
    bias = jnp.zeros((1, LANES), F32).at[0, :2 * SSD_HEADS].set(dt_bias.reshape(-1))
    a = jnp.zeros((1, LANES), F32).at[0, :2 * SSD_HEADS].set(-jnp.exp(a_log.reshape(-1)))
    colg, rowg, sp = dt_prep(dt_raw, bias, a, q)
    dexp = jnp.repeat(d_skip.astype(F32), SSD_HEAD_DIM).reshape(gn, 1, GROUP_W)
    e2 = _expanders()

    cw = conv_w.astype(F32)
    cb = conv_b.astype(F32).reshape(1, -1)
    nb = SSD_INNER
    cwx, cwb, cwc = cw[:, :nb], cw[:, nb:nb + gn * SSD_STATE], cw[:, nb + gn * SSD_STATE:]
    cbx, cbb, cbc = cb[:, :nb], cb[:, nb:nb + gn * SSD_STATE], cb[:, nb + gn * SSD_STATE:]

    fu = max(u for u in (16, 8, 4, 2, 1) if nc % u == 0)
    xo = xbc_col0 // GROUP_W
    bo = (xbc_col0 + SSD_INNER) // SSD_STATE
    co = bo + gn
    st_spec = pl.BlockSpec((1, 1, 2, SSD_STATE, GROUP_W), lambda i, g: (i, g, 0, 0, 0))
    y, sfin = pl.pallas_call(
        functools.partial(_ssd_kernel, l=l, q=q),
        grid=(b, gn),
        in_specs=[pl.BlockSpec((1, l, GROUP_W), lambda i, g: (i, 0, xo + g)),
                  pl.BlockSpec((1, l, SSD_STATE), lambda i, g: (i, 0, bo + g)),
                  pl.BlockSpec((1, l, SSD_STATE), lambda i, g: (i, 0, co + g)),
                  pl.BlockSpec((3, GROUP_W), lambda i, g: (0, g)),
                  pl.BlockSpec((3, SSD_STATE), lambda i, g: (0, g)),
                  pl.BlockSpec((3, SSD_STATE), lambda i, g: (0, g)),
                  pl.BlockSpec((1, GROUP_W), lambda i, g: (0, g)),
                  pl.BlockSpec((1, SSD_STATE), lambda i, g: (0, g)),
                  pl.BlockSpec((1, SSD_STATE), lambda i, g: (0, g)),
                  pl.BlockSpec((1, l, LANES), lambda i, g: (i, 0, 0)),
                  pl.BlockSpec((1, 1, nc, 24, q), lambda i, g: (i, g, 0, 0, 0)),
                  pl.BlockSpec((1, l, 2 * LANES), lambda i, g: (i, 0, 0)),
                  pl.BlockSpec((1, 1, LANES, GROUP_W), lambda i, g: (0, g, 0, 0)),
                  pl.BlockSpec((1, 1, LANES, GROUP_W), lambda i, g: (1, g, 0, 0)),
                  pl.BlockSpec((1, 1, GROUP_W), lambda i, g: (g, 0, 0)),
                  st_spec],
        out_specs=[pl.BlockSpec((1, l, GROUP_W), lambda i, g: (i, 0, g)), st_spec],
        out_shape=[jax.ShapeDtypeStruct((b, l, SSD_INNER), BF16),
                   jax.ShapeDtypeStruct((b, gn, 2, SSD_STATE, GROUP_W), F32)],
        scratch_shapes=[pltpu.VMEM((l, GROUP_W), BF16),
                        pltpu.VMEM((l, SSD_STATE), BF16),
                        pltpu.VMEM((nc, SSD_STATE, q), BF16),
                        pltpu.VMEM((l, GROUP_W), BF16),
                        pltpu.VMEM((nc, SSD_STATE, GROUP_W), BF16),
                        pltpu.VMEM((nc, 1, GROUP_W), F32),
                        pltpu.VMEM((SSD_STATE, GROUP_W), F32),
                        pltpu.VMEM((SSD_STATE, GROUP_W), F32),
                        pltpu.VMEM((fu, q, GROUP_W), F32)],
        compiler_params=_cparams(("parallel", "parallel")),
        name="ssd",
    )(proj, proj, proj, cwx, cwb, cwc, cbx, cbb, cbc, colg, rowg, sp, e2, e2, dexp, s0)
    return y, sfin


def _mixout_kernel(y_ref, z_ref, gb_ref, gc_ref, hv_ref, g0_ref, g1_ref, x_ref, gate_ref,
                   ng_ref, bg_ref, scw_ref, wssd_ref, wsc_ref, wo_ref, o_ref, *, tm, period):
    ns = 2 if (tm // 2) % period == 0 else 1
    ts = tm // ns
    pos = lax.broadcasted_iota(jnp.int32, (ts, 1), 0) % period
    w = scw_ref[...]
    bg = bg_ref[...]
    for s in range(ns):
        r = slice(s * ts, (s + 1) * ts)
        yz = y_ref[0, r, :].astype(F32) * _silu_t(z_ref[0, r, :].astype(F32))
        ms = jnp.mean(yz * yz, axis=-1, keepdims=True)
        yn = (yz * lax.rsqrt(ms + EPS) * ng_ref[...]).astype(BF16)
        y_ssd = _dot(yn, wssd_ref[...])

        u = gc_ref[0, r, :].astype(F32) * hv_ref[0, r, :].astype(F32)
        u_prev = jnp.where(pos == 0, 0.0, pltpu.roll(u, 1, 0))
        u_next = jnp.where(pos == period - 1, 0.0, pltpu.roll(u, ts - 1, 0))
        v = w[0:1] * u_prev + w[1:2] * u + w[2:3] * u_next
        y_sc = _dot((gb_ref[0, r, :].astype(F32) * v).astype(BF16), wsc_ref[...])

        g0 = _sigmoid_t(g0_ref[0, r, :].astype(F32) + bg[:, :D_MODEL])
        g1 = _sigmoid_t(g1_ref[0, r, :].astype(F32) + bg[:, D_MODEL:])
        out = _dot((g0 * y_ssd + g1 * y_sc).astype(BF16), wo_ref[...])
        o_ref[0, r, :] = x_ref[0, r, :] + gate_ref[0] * out


def mixer_out(y, proj, x, gate, norm_g, b_gate, sc_conv_w, w_ssd, w_sc, w_o, tm, period):
    b, l, d = x.shape
    pc = lambda k: pl.BlockSpec((1, tm, d), lambda i, m, k=k: (i, m, k))
    full = lambda shp: pl.BlockSpec(shp, lambda i, m: (0,) * len(shp))
    return pl.pallas_call(
        functools.partial(_mixout_kernel, tm=tm, period=period),
        grid=(b, l // tm),
        in_specs=[pl.BlockSpec((1, tm, SSD_INNER), lambda i, m: (i, m, 0)),
                  pl.BlockSpec((1, tm, SSD_INNER), lambda i, m: (i, m, 0)),
                  pc(5), pc(6), pc(7), pc(8), pc(9),
                  pl.BlockSpec((1, tm, d), lambda i, m: (i, m, 0)),
                  pl.BlockSpec((1, 1, d), lambda i, m: (i, 0, 0)),
                  full((1, SSD_INNER)), full((1, 2 * d)), full((3, SC_WIDTH)),
                  full((SSD_INNER, d)), full((SC_WIDTH, d)), full((d, d))],
        out_specs=pl.BlockSpec((1, tm, d), lambda i, m: (i, m, 0)),
        out_shape=jax.ShapeDtypeStruct((b, l, d), F32),
        compiler_params=_cparams(("parallel", "parallel")),
        name="mixer_out",
    )(y, proj, proj, proj, proj, proj, proj, x, gate,
      norm_g.reshape(1, -1), b_gate.reshape(1, -1), sc_conv_w, w_ssd, w_sc, w_o)


def _ffn_kernel(x_ref, g_ref, sh_ref, sc_ref, gate_ref, w1_ref, w3_ref, w2_ref, o_ref, *, nf):
    fw = w1_ref.shape[1] // nf
    ts = x_ref.shape[1] // ROW_SUBTILES
    for s in range(ROW_SUBTILES):
        r = slice(s * ts, (s + 1) * ts)
        x = x_ref[0, r, :]
        hb = _norm_mod(x, g_ref[...], sh_ref[0], sc_ref[0]).astype(BF16)
        acc = None
        for k in range(nf):
            a = _dot(hb, w1_ref[:, k * fw:(k + 1) * fw])
            bb = _dot(hb, w3_ref[:, k * fw:(k + 1) * fw])
            part = _dot((_silu_t(a) * bb).astype(BF16), w2_ref[k * fw:(k + 1) * fw, :])
            acc = part if acc is None else acc + part
        o_ref[0, r, :] = x + gate_ref[0] * acc


def ffn_dense(x, g, shift, scale, gate, w1, w3, w2, tm):
    b, l, d = x.shape
    f = w1.shape[1]
    vec = pl.BlockSpec((1, 1, d), lambda i, m: (i, 0, 0))
    const = lambda shp: pl.BlockSpec(shp, lambda i, m: (0, 0), pipeline_mode=pl.Buffered(1))
    return pl.pallas_call(
        functools.partial(_ffn_kernel, nf=2),
        grid=(b, l // tm),
        in_specs=[pl.BlockSpec((1, tm, d), lambda i, m: (i, m, 0)),
                  pl.BlockSpec((1, d), lambda i, m: (0, 0)),
                  vec, vec, vec, const((d, f)), const((d, f)), const((f, d))],
        out_specs=pl.BlockSpec((1, tm, d), lambda i, m: (i, m, 0)),
        out_shape=jax.ShapeDtypeStruct((b, l, d), F32),
        compiler_params=_cparams(("parallel", "parallel")),
        name="ffn_dense",
    )(x, g.reshape(1, d), shift, scale, gate, w1, w3, w2)


def _router_kernel(x_ref, g_ref, sh_ref, sc_ref, rw_ref, h_ref, route_ref, route_t_ref, cnt_ref, run_ref, *, tm):
    @pl.when((pl.program_id(0) == 0) & (pl.program_id(1) == 0))
    def _():
        run_ref[...] = jnp.zeros_like(run_ref)

    h = _norm_mod(x_ref[0], g_ref[...], sh_ref[0], sc_ref[0])
    h_hi, h_lo = _split2(h)
    h_ref[0] = h
    w_hi, w_lo = _split2(rw_ref[...])
    logits = _dot(h_hi, w_hi) + _dot(h_lo, w_hi) + _dot(h_hi, w_lo)
    lane = lax.broadcasted_iota(jnp.int32, (tm, LANES), 1)
    ninf = float("-inf")
    lg = jnp.where(lane < N_EXPERTS, logits, ninf)
    m1 = jnp.max(lg, axis=1, keepdims=True)
    i1 = jnp.min(jnp.where(lg == m1, lane, LANES), axis=1, keepdims=True)
    lg2 = jnp.where(lane == i1, ninf, lg)
    m2 = jnp.max(lg2, axis=1, keepdims=True)
    i2 = jnp.min(jnp.where(lg2 == m2, lane, LANES), axis=1, keepdims=True)
    e2 = jnp.exp(m2 - m1)
    den = 1.0 + e2
    sel1 = jnp.where(lane == i1, 1.0, 0.0)
    sel2 = jnp.where(lane == i2, 1.0, 0.0)
    cnt = sel1 + sel2
    r = lax.broadcasted_iota(jnp.int32, (tm, tm), 0)
    c = lax.broadcasted_iota(jnp.int32, (tm, tm), 1)
    tri = jnp.where(c < r, 1.0, 0.0).astype(BF16)
    base = _dot(tri, cnt.astype(BF16)) + run_ref[0:1, :]
    r1 = jnp.sum(sel1 * base, axis=1, keepdims=True)
    r2 = jnp.sum(sel2 * base, axis=1, keepdims=True)
    vals = (i1.astype(F32), i2.astype(F32), 1.0 / den, e2 / den, r1, r2)
    out = jnp.zeros((tm, LANES), F32)
    for k, v in enumerate(vals):
        out = jnp.where(lane == k, v, out)
    route_ref[0] = out
    route_t_ref[...] = out.T
    new_run = run_ref[...] + jnp.sum(cnt, axis=0, keepdims=True)
    run_ref[...] = new_run
    cnt_ref[...] = new_run


def router(x, g, shift, scale, router_w, tm):
    b, l, d = x.shape
    rw = jnp.zeros((d, LANES), F32).at[:, :N_EXPERTS].set(router_w)
    vec = pl.BlockSpec((1, 1, d), lambda i, m: (i, 0, 0))
    mt = l // tm
    return pl.pallas_call(
        functools.partial(_router_kernel, tm=tm),
        grid=(b, l // tm),
        in_specs=[pl.BlockSpec((1, tm, d), lambda i, m: (i, m, 0)),
                  pl.BlockSpec((1, d), lambda i, m: (0, 0)),
                  vec, vec, pl.BlockSpec((d, LANES), lambda i, m: (0, 0))],
        out_specs=[pl.BlockSpec((1, tm, d), lambda i, m: (i, m, 0)),
                   pl.BlockSpec((1, tm, LANES), lambda i, m: (i, m, 0)),
                   pl.BlockSpec((LANES, tm), lambda i, m: (0, i * mt + m)),
                   pl.BlockSpec((8, LANES), lambda i, m: (0, 0))],
        out_shape=[jax.ShapeDtypeStruct((b, l, d), F32),
                   jax.ShapeDtypeStruct((b, l, LANES), F32),
                   jax.ShapeDtypeStruct((LANES, b * l), F32),
                   jax.ShapeDtypeStruct((8, LANES), F32)],
        scratch_shapes=[pltpu.VMEM((8, LANES), F32)],
        compiler_params=_cparams(("arbitrary", "arbitrary")),
        name="router",
    )(x, g.reshape(1, d), shift, scale, rw)


def _pack_bf16_pairs(y):
    k = y.shape[1] // 2
    bits = lax.bitcast_convert_type(y.astype(BF16).astype(F32), jnp.uint32)
    return bits[:, :k] | (bits[:, k:] >> 16)


def _unpack_bf16_pairs(p):
    hi = lax.bitcast_convert_type(p & jnp.uint32(0xFFFF0000), F32)
    lo = lax.bitcast_convert_type(p << 16, F32)
    return jnp.concatenate([hi, lo], axis=1)


def _gffn_kernel(te_ref, nv_ref, x_ref, w1_ref, w3_ref, w2_ref, o_ref, acc_ref, *, nf):
    i = pl.program_id(0)
    f = pl.program_id(1)

    @pl.when(i < nv_ref[0])
    def _():
        ts = x_ref.shape[0] // ROW_SUBTILES
        parts = []
        for s in range(ROW_SUBTILES):
            x = x_ref[s * ts:(s + 1) * ts, :].astype(BF16)
            a = _dot(x, w1_ref[0])
            bb = _dot(x, w3_ref[0])
            parts.append(_dot((_silu_t(a) * bb).astype(BF16), w2_ref[0]))
        part = jnp.concatenate(parts, axis=0)

        @pl.when(f == 0)
        def _():
            acc_ref[...] = part

        if nf > 2:
            @pl.when((f > 0) & (f < nf - 1))
            def _():
                acc_ref[...] += part

        @pl.when(f == nf - 1)
        def _():
            o_ref[...] = _pack_bf16_pairs(part if nf == 1 else acc_ref[...] + part)


def grouped_ffn(xs, tile_expert, n_valid, w1, w3, w2, tm, nf):
    rows, d = xs.shape
    nt = rows // tm
    f = w1.shape[2]
    fw = f // nf

    def tile(i, nv):
        return jnp.minimum(i, nv[0] - 1)

    def fchunk(i, k, nv):
        return jnp.where(i < nv[0], k, nf - 1)

    grid_spec = pltpu.PrefetchScalarGridSpec(
        num_scalar_prefetch=2,
        grid=(nt, nf),
        in_specs=[pl.BlockSpec((tm, d), lambda i, k, te, nv: (tile(i, nv), 0)),
                  pl.BlockSpec((1, d, fw), lambda i, k, te, nv: (te[i], 0, fchunk(i, k, nv))),
                  pl.BlockSpec((1, d, fw), lambda i, k, te, nv: (te[i], 0, fchunk(i, k, nv))),
                  pl.BlockSpec((1, fw, d), lambda i, k, te, nv: (te[i], fchunk(i, k, nv), 0))],
        out_specs=pl.BlockSpec((tm, d // 2), lambda i, k, te, nv: (tile(i, nv), 0)),
        scratch_shapes=[pltpu.VMEM((tm, d), F32)],
    )
    return pl.pallas_call(
        functools.partial(_gffn_kernel, nf=nf),
        grid_spec=grid_spec,
        out_shape=jax.ShapeDtypeStruct((rows, d // 2), jnp.uint32),
        compiler_params=_cparams(("arbitrary", "arbitrary")),
        name="grouped_ffn",
    )(tile_expert, n_valid, xs, w1, w3, w2)


def _combine_kernel(x_ref, y0_ref, y1_ref, route_ref, gate_ref, fg_ref, o_ref):
    r = route_ref[0]
    moe = r[:, 2:3] * _unpack_bf16_pairs(y0_ref[0, 0]) + r[:, 3:4] * _unpack_bf16_pairs(y1_ref[0, 0])
    xn = x_ref[0] + gate_ref[0] * moe
    ms = jnp.mean(xn * xn, axis=-1, keepdims=True)
    o_ref[0] = xn * lax.rsqrt(ms + EPS) * fg_ref[...]


def combine_final(x, yg, route, gate, final_g, tm):
    b, l, d = x.shape
    return pl.pallas_call(
        _combine_kernel,
        grid=(b, l // tm),
        in_specs=[pl.BlockSpec((1, tm, d), lambda i, m: (i, m, 0)),
                  pl.BlockSpec((1, 1, tm, d // 2), lambda i, m: (0, i, m, 0)),
                  pl.BlockSpec((1, 1, tm, d // 2), lambda i, m: (1, i, m, 0)),
                  pl.BlockSpec((1, tm, LANES), lambda i, m: (i, m, 0)),
                  pl.BlockSpec((1, 1, d), lambda i, m: (i, 0, 0)),
                  pl.BlockSpec((1, d), lambda i, m: (0, 0))],
        out_specs=pl.BlockSpec((1, tm, d), lambda i, m: (i, m, 0)),
        out_shape=jax.ShapeDtypeStruct((b, l, d), F32),
        compiler_params=_cparams(("parallel", "parallel")),
        name="combine_final",
    )(x, yg, yg, route, gate, final_g.reshape(1, d))


SC_CORES = 2
SC_SUBCORES = 16
SC_WORKERS = SC_CORES * SC_SUBCORES
SC_STREAM_BYTES = 256 * 1024
SC_STREAM_ROWS = 128


def _sc_rows(per_worker, d, dtype):
    return min(SC_STREAM_ROWS, SC_STREAM_BYTES // (d * jnp.dtype(dtype).itemsize), per_worker)


def _sc_mesh():
    return plsc.VectorSubcoreMesh(core_axis_name="c", subcore_axis_name="s",
                                  num_cores=SC_CORES, num_subcores=SC_SUBCORES)


def dispatch_rows(h, pos0, pos1, n_rows):
    t, d = h.shape
    per_w = t // SC_WORKERS
    ch = _sc_rows(per_w, d, h.dtype)
    assert t % SC_WORKERS == 0 and per_w % ch == 0 and ch % 8 == 0, (t, ch)

    @functools.partial(
        pl.kernel, mesh=_sc_mesh(),
        out_type=jax.ShapeDtypeStruct((n_rows, d), h.dtype),
        scratch_types=[pltpu.VMEM((ch,), jnp.int32), pltpu.VMEM((ch,), jnp.int32),
                       pltpu.VMEM((ch, d), h.dtype), pltpu.SemaphoreType.DMA],
        name="moe_dispatch")
    def scatter(h_hbm, p0_hbm, p1_hbm, out_hbm, i0_v, i1_v, rows_v, sem):
        base = (lax.axis_index("s") * SC_CORES + lax.axis_index("c")) * per_w

        @pl.loop(0, per_w // ch)
        def _(j):
            off = base + j * ch
            pltpu.sync_copy(h_hbm.at[pl.ds(off, ch)], rows_v)
            pltpu.sync_copy(p0_hbm.at[pl.ds(off, ch)], i0_v)
            pltpu.sync_copy(p1_hbm.at[pl.ds(off, ch)], i1_v)
            pltpu.async_copy(rows_v, out_hbm.at[i0_v], sem).wait()
            pltpu.async_copy(rows_v, out_hbm.at[i1_v], sem).wait()

    return scatter(h, pos0, pos1)


def return_rows(ys, idx):
    n = idx.shape[0]
    d = ys.shape[1]
    per_w = n // SC_WORKERS
    ch = _sc_rows(per_w, d, ys.dtype)
    assert n % SC_WORKERS == 0 and per_w % ch == 0 and ch % 8 == 0, (n, ch)

    @functools.partial(
        pl.kernel, mesh=_sc_mesh(),
        out_type=jax.ShapeDtypeStruct((n, d), ys.dtype),
        scratch_types=[pltpu.VMEM((ch,), jnp.int32), pltpu.VMEM((ch, d), ys.dtype), pltpu.SemaphoreType.DMA],
        name="moe_return")
    def gather(ys_hbm, idx_hbm, out_hbm, idx_v, rows_v, sem):
        base = (lax.axis_index("s") * SC_CORES + lax.axis_index("c")) * per_w

        @pl.loop(0, per_w // ch)
        def _(j):
            off = base + j * ch
            pltpu.sync_copy(idx_hbm.at[pl.ds(off, ch)], idx_v)
            pltpu.async_copy(ys_hbm.at[idx_v], rows_v, sem).wait()
            pltpu.sync_copy(rows_v, out_hbm.at[pl.ds(off, ch)])

    return gather(ys, idx)


def moe_block(x, g, shift, scale, gate, router_w, w1, w3, w2, final_g):
    b, l, d = x.shape
    t = b * l
    tm = MOE_TM
    h, route, route_t, counts = router(x, g, shift, scale, router_w, _row_tile(l, ROW_TM))
    cnt = counts[0, :N_EXPERTS].astype(jnp.int32)
    gs = ((cnt + tm - 1) // tm) * tm
    ends = jnp.cumsum(gs)
    offs = ends - gs

    def sorted_row(choice):
        e = route_t[choice].astype(jnp.int32)
        start = sum(jnp.where(e == k, offs[k], 0) for k in range(N_EXPERTS))
        return start + route_t[4 + choice].astype(jnp.int32)

    pos0, pos1 = sorted_row(0), sorted_row(1)
    nt = (2 * t) // tm + N_EXPERTS
    n_valid = (ends[-1] // tm).astype(jnp.int32).reshape(1)
    tile = jnp.minimum(jnp.arange(nt, dtype=jnp.int32), n_valid[0] - 1)
    tile_expert = jnp.sum((tile[:, None] >= (ends // tm)[None, :]).astype(jnp.int32), axis=1)
    xs = dispatch_rows(h.reshape(t, d), pos0, pos1, nt * tm)
    ys = grouped_ffn(xs, tile_expert, n_valid, w1, w3, w2, tm, 2)
    yg = return_rows(ys, jnp.concatenate([pos0, pos1])).reshape(2, b, l, d // 2)
    return combine_final(x, yg, route, gate, final_g, _row_tile(l, 2 * ROW_TM))


def _in_weights(w_in):
    o1 = SSD_INNER
    o2 = o1 + XBC_WIDTH
    o3 = o2 + 2 * SSD_HEADS
    w_main = jnp.concatenate([w_in[:, :o2], w_in[:, o3:]], axis=1).astype(BF16)
    w_dt = jnp.pad(w_in[:, o2:o3], ((0, 0), (0, LANES - 2 * SSD_HEADS))).astype(BF16)
    return w_main, w_dt


def kernel(x, c, ctx, c_ctx, w_mod, b_mod, norm1_g, norm2_g, w_in, b_gate, ssd_conv_w, ssd_conv_b, ssd_dt_bias, ssd_a_log, ssd_d, ssd_norm_g, w_ssd_out, sc_conv_w, w_sc_out, w_o, ffn_w1, ffn_w3, ffn_w2, router_w, moe_w1, moe_w3, moe_w2, final_g):
    b, l, d = x.shape
    lc = ctx.shape[1]
    depth = w_mod.shape[0]
    assert depth % 2 == 0, "the final norm is fused into the routed channel mixer of the (odd) last layer"
    cc = jnp.zeros((16, d), F32).at[:b].set(c).at[b].set(c_ctx)
    mod = modulation(cc, w_mod, b_mod)
    zeros_state = jnp.zeros((b, SSD_GROUPS, 2, SSD_STATE, GROUP_W), F32)
    nctx = b * lc
    ctx = ctx.reshape(1, nctx, d)
    tmc = _row_tile(nctx, ROW_TM, lc)
    tmx = _row_tile(l, ROW_TM, GRID_W)

    def per_seq(t):
        return t.reshape(b, lc, t.shape[-1])

    for i in range(depth):
        last = i == depth - 1
        mx = mod[i, :b].reshape(b, N_MOD, 1, d)
        mc = mod[i, b].reshape(1, N_MOD, 1, d)
        w_main, w_dt = _in_weights(w_in[i])
        ssd_p = (ssd_conv_w[i], ssd_conv_b[i], ssd_dt_bias[i], ssd_a_log[i], ssd_d[i])
        out_p = (ssd_norm_g[i], b_gate[i], sc_conv_w[i], w_ssd_out[i].astype(BF16),
                 w_sc_out[i].astype(BF16), w_o[i].astype(BF16))

        if last:
            w_xbc = w_main[:, SSD_INNER:SSD_INNER + XBC_WIDTH]
            proj_c, dt_c = in_proj(ctx, norm1_g[i], mc[:, 0], mc[:, 1], w_xbc, w_dt,
                                   _row_tile(nctx, PROJ_TM), PROJ_TN_XBC)
            _, s_ctx = ssd_mixer(per_seq(proj_c), 0, per_seq(dt_c), *ssd_p, zeros_state, SSD_Q)
        else:
            proj_c, dt_c = in_proj(ctx, norm1_g[i], mc[:, 0], mc[:, 1], w_main, w_dt,
                                   _row_tile(nctx, PROJ_TM), PROJ_TN)
            y_c, s_ctx = ssd_mixer(per_seq(proj_c), SSD_INNER, per_seq(dt_c), *ssd_p, zeros_state, SSD_Q)
            ctx = mixer_out(y_c.reshape(1, nctx, SSD_INNER), proj_c, ctx, mc[:, 2], *out_p, tmc, lc)

        proj_x, dt_x = in_proj(x, norm1_g[i], mx[:, 0], mx[:, 1], w_main, w_dt, _row_tile(l, PROJ_TM), PROJ_TN)
        y_x, _ = ssd_mixer(proj_x, SSD_INNER, dt_x, *ssd_p, s_ctx, SSD_Q)
        x = mixer_out(y_x, proj_x, x, mx[:, 2], *out_p, tmx, GRID_W)

        j = i // 2
        if i % 2 == 0:
            w1, w3, w2 = ffn_w1[j].astype(BF16), ffn_w3[j].astype(BF16), ffn_w2[j].astype(BF16)
            x = ffn_dense(x, norm2_g[i], mx[:, 3], mx[:, 4], mx[:, 5], w1, w3, w2, _row_tile(l, FFN_TM))
            if not last:
                ctx = ffn_dense(ctx, norm2_g[i], mc[:, 3], mc[:, 4], mc[:, 5], w1, w3, w2, tmc)
        else:
            assert last, "the routed channel mixer is fused with the final norm"
            w1, w3, w2 = moe_w1[j].astype(BF16), moe_w3[j].astype(BF16), moe_w2[j].astype(BF16)
            x = moe_block(x, norm2_g[i], mx[:, 3], mx[:, 4], mx[:, 5], router_w[j], w1, w3, w2, final_g)
    return x
```
